```python
import jax, jax.numpy as jnp
from jax import lax
import numpy as np

D_MODEL = 1024
BATCH = 8
SEQ = 2048
DEPTH = 1

CHUNK = 128
N_SGU_GROUPS = 8
SGU_WIDTH = D_MODEL
SGU_GROUP = SGU_WIDTH // N_SGU_GROUPS
CONV_WIDTH = D_MODEL
CONV_K = 3
PROJ_WIDTHS = (SGU_WIDTH, SGU_WIDTH, CONV_WIDTH, CONV_WIDTH, CONV_WIDTH, D_MODEL, D_MODEL)
PROJ_TOTAL = sum(PROJ_WIDTHS)
PROJ_SPLITS = tuple(int(i) for i in np.cumsum(PROJ_WIDTHS)[:-1])
N_EXPERT_GROUPS = 8
EXPERTS_PER_GROUP = 8
N_EXPERTS = N_EXPERT_GROUPS * EXPERTS_PER_GROUP
TOP_K = 2
D_EXPERT = D_MODEL // 2
MOE_BLOCK = 128
EPS = 1e-6

kernel_name = "hybrid_sgu_shortconv_hiermoe_encoder"


def _rmsnorm(x, g):
    x32 = x.astype(jnp.float32)
    y = x32 * lax.rsqrt(jnp.mean(x32 * x32, axis=-1, keepdims=True) + EPS)
    return (y * g.astype(jnp.float32)).astype(x.dtype)


def _group_layernorm(v, g):
    v32 = v.astype(jnp.float32)
    mu = jnp.mean(v32, axis=-1, keepdims=True)
    d = v32 - mu
    var = jnp.mean(d * d, axis=-1, keepdims=True)
    return (d * lax.rsqrt(var + EPS) * g.astype(jnp.float32)).astype(v.dtype)


def _spatial_gate(u, v, w_s, b_s, g_ln):
    b, s, _ = v.shape
    nc = s // CHUNK
    vg = v.reshape(b, nc, CHUNK, N_SGU_GROUPS, SGU_GROUP)
    vg = _group_layernorm(vg, g_ln.reshape(N_SGU_GROUPS, SGU_GROUP))
    z = jnp.einsum('gqp,bnpgc->bnqgc', w_s, vg) + b_s.T[None, None, :, :, None]
    return u * z.reshape(b, s, SGU_WIDTH)


def _short_conv(z, w):
    s = z.shape[1]
    half = CONV_K // 2
    zp = jnp.pad(z, ((0, 0), (half, half), (0, 0)))
    out = w[0] * zp[:, 0:s]
    for k in range(1, CONV_K):
        out = out + w[k] * zp[:, k:k + s]
    return out


def _hier_moe(h, w_rg, w_re, w_gate, w_up, w_down):
    t = h.shape[0]
    g_logits = (h @ w_rg).astype(jnp.float32)
    g_prob = jax.nn.softmax(g_logits, axis=-1)
    g_idx = jnp.argmax(g_logits, axis=-1).astype(jnp.int32)
    p_g = jnp.take_along_axis(g_prob, g_idx[:, None], axis=1)
    e_logits = (h @ w_re).astype(jnp.float32).reshape(t, N_EXPERT_GROUPS, EXPERTS_PER_GROUP)
    e_logits = jnp.take_along_axis(e_logits, g_idx[:, None, None], axis=1)[:, 0]
    top_l, top_i = lax.top_k(e_logits, TOP_K)
    gate = p_g * jax.nn.softmax(top_l, axis=-1)

    eid = (g_idx[:, None] * EXPERTS_PER_GROUP + top_i).reshape(-1).astype(jnp.int32)
    tok = jnp.repeat(jnp.arange(t, dtype=jnp.int32), TOP_K)
    wt = gate.reshape(-1)
    order = jnp.argsort(eid)
    eid_s, tok_s, wt_s = eid[order], tok[order], wt[order]

    counts = jnp.bincount(eid, length=N_EXPERTS).astype(jnp.int32)
    padded = ((counts + MOE_BLOCK - 1) // MOE_BLOCK) * MOE_BLOCK
    start = jnp.cumsum(counts) - counts
    pend = jnp.cumsum(padded)
    pstart = pend - padded
    n_assign = t * TOP_K
    n_rows = -(-(n_assign + N_EXPERTS * (MOE_BLOCK - 1)) // MOE_BLOCK) * MOE_BLOCK
    n_blocks = n_rows // MOE_BLOCK

    dest = pstart[eid_s] + jnp.arange(n_assign, dtype=jnp.int32) - start[eid_s]
    row_tok = jnp.zeros((n_rows,), jnp.int32).at[dest].set(tok_s)
    row_w = jnp.zeros((n_rows,), h.dtype).at[dest].set(wt_s.astype(h.dtype))
    blk_start = jnp.arange(n_blocks, dtype=jnp.int32) * MOE_BLOCK
    blk_e = jnp.minimum(jnp.searchsorted(pend, blk_start, side='right'), N_EXPERTS - 1).astype(jnp.int32)

    xb = h[row_tok].reshape(n_blocks, MOE_BLOCK, h.shape[-1])

    def expert_block(args):
        xe, e = args
        hid = jax.nn.silu(xe @ w_gate[e]) * (xe @ w_up[e])
        return hid @ w_down[e]

    yb = lax.map(expert_block, (xb, blk_e))
    y_rows = yb.reshape(n_rows, -1) * row_w[:, None]
    return jnp.zeros_like(h).at[row_tok].add(y_rows)


def setup_inputs(seed: int = 0) -> dict:
    key = jax.random.key(seed)
    ks = jax.random.split(key, 16)
    f32 = jnp.float32
    nrm = lambda k, shape, scale: (jax.random.normal(k, shape, f32) * scale).astype(f32)
    x = jax.random.normal(ks[0], (BATCH, SEQ, D_MODEL), f32)
    g_mix = 1.0 + nrm(ks[1], (DEPTH, D_MODEL), 0.02)
    w_in = nrm(ks[2], (DEPTH, D_MODEL, PROJ_TOTAL), D_MODEL ** -0.5)
    w_s = nrm(ks[3], (DEPTH, N_SGU_GROUPS, CHUNK, CHUNK), CHUNK ** -0.5)
    b_s = 1.0 + nrm(ks[4], (DEPTH, N_SGU_GROUPS, CHUNK), 0.02)
    g_sgu = 1.0 + nrm(ks[5], (DEPTH, SGU_WIDTH), 0.02)
    w_conv = nrm(ks[6], (DEPTH, CONV_K, CONV_WIDTH), CONV_K ** -0.5)
    w_out = nrm(ks[7], (DEPTH, D_MODEL, D_MODEL), D_MODEL ** -0.5)
    g_ffn = 1.0 + nrm(ks[8], (DEPTH, D_MODEL), 0.02)
    w_router_group = nrm(ks[9], (DEPTH, D_MODEL, N_EXPERT_GROUPS), D_MODEL ** -0.5)
    w_router_expert = nrm(ks[10], (DEPTH, D_MODEL, N_EXPERTS), D_MODEL ** -0.5)
    w_gate = nrm(ks[11], (DEPTH, N_EXPERTS, D_MODEL, D_EXPERT), D_MODEL ** -0.5)
    w_up = nrm(ks[12], (DEPTH, N_EXPERTS, D_MODEL, D_EXPERT), D_MODEL ** -0.5)
    w_down = nrm(ks[13], (DEPTH, N_EXPERTS, D_EXPERT, D_MODEL), D_EXPERT ** -0.5)
    g_final = 1.0 + nrm(ks[14], (D_MODEL,), 0.02)
    return {"x": x, "g_mix": g_mix, "w_in": w_in, "w_s": w_s, "b_s": b_s,
            "g_sgu": g_sgu, "w_conv": w_conv, "w_out": w_out, "g_ffn": g_ffn,
            "w_router_group": w_router_group, "w_router_expert": w_router_expert,
            "w_gate": w_gate, "w_up": w_up, "w_down": w_down, "g_final": g_final}


def reference(x, g_mix, w_in, w_s, b_s, g_sgu, w_conv, w_out, g_ffn,
              w_router_group, w_router_expert, w_gate, w_up, w_down, g_final):
    for l in range(DEPTH):
        h = _rmsnorm(x, g_mix[l])
        proj = h @ w_in[l]
        u, v, cb, cc, xc, ga, gb = jnp.split(proj, PROJ_SPLITS, axis=-1)
        y_a = _spatial_gate(jax.nn.gelu(u), jax.nn.gelu(v), w_s[l], b_s[l], g_sgu[l])
        y_b = cb * _short_conv(cc * xc, w_conv[l])
        merged = jax.nn.sigmoid(ga) * y_a + jax.nn.sigmoid(gb) * y_b
        x = x + merged @ w_out[l]
        h2 = _rmsnorm(x, g_ffn[l]).reshape(-1, D_MODEL)
        y_moe = _hier_moe(h2, w_router_group[l], w_router_expert[l],
                          w_gate[l], w_up[l], w_down[l])
        x = x + y_moe.reshape(x.shape)
    return _rmsnorm(x, g_final)
```

```python
import functools

import jax
import jax.numpy as jnp
from jax import lax
from jax.experimental import pallas as pl
from jax.experimental.pallas import tpu as pltpu

F32 = jnp.float32
BF16 = jnp.bfloat16
I32 = jnp.int32

EPS = 1e-6
LANES = 128
SUBLANES = 8
CHUNK = 128
N_GROUPS = 8
EXPERTS_PER_GROUP = 8
TOP_K = 2
CONV_K = 3
N_BRANCH = 7

MIX_ROWS = 256
ROUTE_TOKENS = 512
MOVE_TOKENS = 256
EXPERT_ROWS = 128
VMEM_LIMIT = 56 * 1024 * 1024


def _rms(x, g):
    return x * lax.rsqrt(jnp.mean(x * x, axis=-1, keepdims=True) + EPS) * g


def _sigmoid(x):
    return 0.5 * (1.0 + jnp.tanh(0.5 * x))


def _gelu_tanh(x):
    c = 0.7978845608028654
    return x * (0.5 * (1.0 + jnp.tanh(c * (x + 0.044715 * (x * x * x)))))


def _rows_to_tiles(dst_ref, val, rows):
    nt = val.shape[1] // LANES
    for c in range(nt):
        dst_ref[pl.ds(c, rows, stride=nt), :] = val[:, c * LANES:(c + 1) * LANES]


def _tiles_to_rows(src_ref, rows, nt):
    return jnp.concatenate([src_ref[pl.ds(c, rows, stride=nt), :] for c in range(nt)], axis=1)


def _mixer_kernel(x_ref, xp_ref, xn_ref, gmix_ref, win_ref, ws_ref, bias_ref, gsgu_ref, wconv_ref,
                  wout_ref, gffn_ref, wr_ref, x1_ref, h2t_ref, lt_ref, vg_ref, z_ref, acc_ref):
    ts, d = x_ref.shape[1], x_ref.shape[2]
    s = pl.program_id(1)
    ns = pl.num_programs(1)
    gw = d // N_GROUPS
    nc = ts // CHUNK

    x = x_ref[0]
    gmix = gmix_ref[...]
    h = _rms(x, gmix).astype(BF16)

    def proj(j, n=1):
        return jnp.dot(h, win_ref[:, j * d:(j + n) * d], preferred_element_type=F32)

    gv = _gelu_tanh(proj(1))
    for g in range(N_GROUPS):
        cs = slice(g * gw, (g + 1) * gw)
        blk = gv[:, cs]
        mu = jnp.mean(blk, axis=-1, keepdims=True)
        dv = blk - mu
        var = jnp.mean(dv * dv, axis=-1, keepdims=True)
        vg_ref[:, cs] = (dv * lax.rsqrt(var + EPS) * gsgu_ref[:, cs]).astype(BF16)
    for g in range(N_GROUPS):
        cs = slice(g * gw, (g + 1) * gw)
        vcat = jnp.concatenate([vg_ref[n * CHUNK:(n + 1) * CHUNK, cs] for n in range(nc)], axis=1)
        zg = jnp.dot(ws_ref[g], vcat, preferred_element_type=F32)
        for n in range(nc):
            z_ref[n * CHUNK:(n + 1) * CHUNK, cs] = zg[:, n * gw:(n + 1) * gw]
    u = _gelu_tanh(proj(0))
    ga = _sigmoid(proj(5))
    for n in range(nc):
        rs = slice(n * CHUNK, (n + 1) * CHUNK)
        acc_ref[rs, :] = ga[rs] * (u[rs] * (z_ref[rs, :] + bias_ref[...]))

    cx = proj(3, 2)
    z2 = cx[:, :d] * cx[:, d:]
    xh = jnp.concatenate([xp_ref[0], xn_ref[0]], axis=0)
    hh = _rms(xh, gmix).astype(BF16)
    cxh = jnp.dot(hh, win_ref[:, 3 * d:5 * d], preferred_element_type=F32)
    z2h = cxh[:, :d] * cxh[:, d:]
    prev = jnp.where(s > 0, z2h[SUBLANES - 1:SUBLANES, :], 0.0)
    nxt = jnp.where(s < ns - 1, z2h[SUBLANES:SUBLANES + 1, :], 0.0)
    row = lax.broadcasted_iota(I32, (ts, d), 0)
    zm1 = jnp.where(row == 0, prev, pltpu.roll(z2, 1, 0))
    zp1 = jnp.where(row == ts - 1, nxt, pltpu.roll(z2, ts - 1, 0))
    conv = wconv_ref[0:1, :] * zm1 + wconv_ref[1:2, :] * z2 + wconv_ref[2:3, :] * zp1
    cb = proj(2)
    gb = _sigmoid(proj(6))
    merged = acc_ref[...] + gb * (cb * conv)

    x1 = x + jnp.dot(merged.astype(BF16), wout_ref[...], preferred_element_type=F32)
    x1_ref[0] = x1

    h2 = _rms(x1, gffn_ref[...]).astype(BF16)
    lt_ref[...] = lax.dot_general(wr_ref[...], h2, (((1,), (1,)), ((), ())),
                                  preferred_element_type=F32)
    _rows_to_tiles(h2t_ref, h2.astype(F32), ts)


def _mixer(x, g_mix, w_in_b, w_s_b, bias_full, g_sgu, w_conv, w_out_b, g_ffn, wr_t):
    b, s, d = x.shape
    ts = MIX_ROWS
    ns = s // ts
    t = b * s
    nt = d // LANES
    hb = ts // SUBLANES
    last_hb = s // SUBLANES - 1

    const = lambda *shape: pl.BlockSpec(shape, lambda bi, si: (0,) * len(shape))
    in_specs = [
        pl.BlockSpec((1, ts, d), lambda bi, si: (bi, si, 0)),
        pl.BlockSpec((1, SUBLANES, d), lambda bi, si: (bi, jnp.maximum(si * hb - 1, 0), 0)),
        pl.BlockSpec((1, SUBLANES, d), lambda bi, si: (bi, jnp.minimum((si + 1) * hb, last_hb), 0)),
        const(1, d),
        pl.BlockSpec((d, N_BRANCH * d), lambda bi, si: (0, 0), pipeline_mode=pl.Buffered(1)),
        const(N_GROUPS, CHUNK, CHUNK),
        const(CHUNK, d),
        const(1, d),
        const(CONV_K, d),
        const(d, d),
        const(1, d),
        const(LANES, d),
    ]
    out_specs = [
        pl.BlockSpec((1, ts, d), lambda bi, si: (bi, si, 0)),
        pl.BlockSpec((ts * nt, LANES), lambda bi, si: (bi * ns + si, 0)),
        pl.BlockSpec((LANES, ts), lambda bi, si: (0, bi * ns + si)),
    ]
    out_shape = [
        jax.ShapeDtypeStruct((b, s, d), F32),
        jax.ShapeDtypeStruct((t * nt, LANES), F32),
        jax.ShapeDtypeStruct((LANES, t), F32),
    ]
    return pl.pallas_call(
        _mixer_kernel,
        grid=(b, ns),
        in_specs=in_specs,
        out_specs=out_specs,
        out_shape=out_shape,
        scratch_shapes=[pltpu.VMEM((ts, d), BF16), pltpu.VMEM((ts, d), F32), pltpu.VMEM((ts, d), F32)],
        compiler_params=pltpu.CompilerParams(
            dimension_semantics=("arbitrary", "arbitrary"), vmem_limit_bytes=VMEM_LIMIT),
        name="mixer",
    )(x, x, x, g_mix, w_in_b, w_s_b, bias_full, g_sgu, w_conv, w_out_b, g_ffn, wr_t)


def _route_kernel(lt_ref, info_ref, gt_ref, cnt_ref, carry_ref):
    tb = lt_ref.shape[1]
    ne = N_GROUPS * EXPERTS_PER_GROUP

    @pl.when(pl.program_id(0) == 0)
    def _():
        carry_ref[...] = jnp.zeros_like(carry_ref)

    row8 = lax.broadcasted_iota(I32, (SUBLANES, tb), 0)
    gl = lt_ref[0:N_GROUPS, :]
    gmax = jnp.max(gl, axis=0, keepdims=True)
    gidx = jnp.min(jnp.where(gl == gmax, row8, N_GROUPS), axis=0, keepdims=True)
    pg = 1.0 / jnp.sum(jnp.exp(gl - gmax), axis=0, keepdims=True)

    sel = jnp.zeros((EXPERTS_PER_GROUP, tb), F32)
    for g in range(N_GROUPS):
        lo = N_GROUPS + g * EXPERTS_PER_GROUP
        sel = jnp.where(gidx == g, lt_ref[lo:lo + EXPERTS_PER_GROUP, :], sel)
    m1 = jnp.max(sel, axis=0, keepdims=True)
    i1 = jnp.min(jnp.where(sel == m1, row8, EXPERTS_PER_GROUP), axis=0, keepdims=True)
    sel2 = jnp.where(row8 == i1, -jnp.inf, sel)
    m2 = jnp.max(sel2, axis=0, keepdims=True)
    i2 = jnp.min(jnp.where(sel2 == m2, row8, EXPERTS_PER_GROUP), axis=0, keepdims=True)
    e2 = jnp.exp(m2 - m1)
    den = 1.0 + e2
    gate0 = pg * (1.0 / den)
    gate1 = pg * (e2 / den)
    eid0 = gidx * EXPERTS_PER_GROUP + i1
    eid1 = gidx * EXPERTS_PER_GROUP + i2

    rowe = lax.broadcasted_iota(I32, (ne, tb), 0)
    hit0 = rowe == eid0
    hit1 = rowe == eid1
    onehot = jnp.where(hit0 | hit1, 1.0, 0.0)
    before = (lax.broadcasted_iota(I32, (tb, tb), 0) < lax.broadcasted_iota(I32, (tb, tb), 1))
    prefix = jnp.dot(onehot.astype(BF16), jnp.where(before, 1.0, 0.0).astype(BF16),
                     preferred_element_type=F32)
    base = prefix + carry_ref[:, 0:1]
    rank0 = jnp.sum(jnp.where(hit0, base, 0.0), axis=0, keepdims=True).astype(I32)
    rank1 = jnp.sum(jnp.where(hit1, base, 0.0), axis=0, keepdims=True).astype(I32)
    carry_ref[...] = carry_ref[...] + jnp.sum(onehot, axis=1, keepdims=True)
    cnt_ref[...] = carry_ref[...].astype(I32)

    info_ref[...] = jnp.where(row8 == 0, eid0, jnp.where(row8 == 1, eid1,
                              jnp.where(row8 == 2, rank0, jnp.where(row8 == 3, rank1, 0))))
    rowl = lax.broadcasted_iota(I32, (LANES, tb), 0)
    gates = jnp.where(rowl == 0, gate0, jnp.where(rowl == 1, gate1, 0.0))
    gt_ref[...] = gates.T


def _route(logits_t):
    t = logits_t.shape[1]
    tb = ROUTE_TOKENS
    ne = N_GROUPS * EXPERTS_PER_GROUP
    return pl.pallas_call(
        _route_kernel,
        grid=(t // tb,),
        in_specs=[pl.BlockSpec((LANES, tb), lambda i: (0, i))],
        out_specs=[
            pl.BlockSpec((SUBLANES, tb), lambda i: (0, i)),
            pl.BlockSpec((tb, LANES), lambda i: (i, 0)),
            pl.BlockSpec((ne, LANES), lambda i: (0, 0)),
        ],
        out_shape=[
            jax.ShapeDtypeStruct((SUBLANES, t), I32),
            jax.ShapeDtypeStruct((t, LANES), F32),
            jax.ShapeDtypeStruct((ne, LANES), I32),
        ],
        scratch_shapes=[pltpu.VMEM((ne, LANES), F32)],
        compiler_params=pltpu.CompilerParams(dimension_semantics=("arbitrary",)),
        name="route",
    )(logits_t)


def _row_copy(src_ref, src_row, dst_ref, dst_row, nt, sem):
    return pltpu.make_async_copy(
        src_ref.at[pl.ds(pl.multiple_of(src_row * nt, nt), nt), :],
        dst_ref.at[pl.ds(pl.multiple_of(dst_row * nt, nt), nt), :],
        sem)


def _dispatch_kernel(ps_ref, info_ref, h2_ref, xs_in_ref, xs_ref, sem):
    del xs_in_ref
    td = info_ref.shape[1]
    nt = h2_ref.shape[0] // td

    def issue(t, c):
        for k in range(TOP_K):
            dst = ps_ref[info_ref[k, t]] + info_ref[TOP_K + k, t]
            _row_copy(h2_ref, t, xs_ref, dst, nt, sem).start()
        return c

    lax.fori_loop(0, td, issue, 0)

    def drain(t, c):
        for k in range(TOP_K):
            _row_copy(h2_ref, 0, xs_ref, 0, nt, sem).wait()
        return c

    lax.fori_loop(0, td, drain, 0)


def _dispatch(pstart, info, h2t, xs_zero, nt):
    t = info.shape[1]
    td = MOVE_TOKENS
    grid_spec = pltpu.PrefetchScalarGridSpec(
        num_scalar_prefetch=1,
        grid=(t // td,),
        in_specs=[
            pl.BlockSpec((SUBLANES, td), lambda i, ps: (0, i), memory_space=pltpu.SMEM),
            pl.BlockSpec((td * nt, LANES), lambda i, ps: (i, 0)),
            pl.BlockSpec(memory_space=pl.ANY),
        ],
        out_specs=pl.BlockSpec(memory_space=pl.ANY),
        scratch_shapes=[pltpu.SemaphoreType.DMA],
    )
    return pl.pallas_call(
        _dispatch_kernel,
        grid_spec=grid_spec,
        out_shape=jax.ShapeDtypeStruct(xs_zero.shape, xs_zero.dtype),
        input_output_aliases={3: 0},
        compiler_params=pltpu.CompilerParams(dimension_semantics=("arbitrary",)),
        name="dispatch",
    )(pstart, info, h2t, xs_zero)


def _experts_kernel(be_ref, nu_ref, xs_ref, wg_ref, wu_ref, wd_ref, y_ref, wgu_b, wd_b):
    j = pl.program_id(0)
    d, de = wg_ref.shape[1], wg_ref.shape[2]
    nt = d // LANES
    bs = xs_ref.shape[0] // nt
    e = be_ref[j]
    e_prev = be_ref[jnp.maximum(j - 1, 0)]

    @pl.when((j == 0) | (e != e_prev))
    def _():
        wgu_b[:, :de] = wg_ref[0].astype(BF16)
        wgu_b[:, de:] = wu_ref[0].astype(BF16)
        wd_b[...] = wd_ref[0].astype(BF16)

    @pl.when(j < nu_ref[0])
    def _():
        xb = _tiles_to_rows(xs_ref, bs, nt).astype(BF16)
        gu = jnp.dot(xb, wgu_b[...], preferred_element_type=F32)
        gate = gu[:, :de]
        hid = (gate * _sigmoid(gate)) * gu[:, de:]
        y = jnp.dot(hid.astype(BF16), wd_b[...], preferred_element_type=F32)
        _rows_to_tiles(y_ref, y, bs)

    @pl.when(j >= nu_ref[0])
    def _():
        y_ref[...] = jnp.zeros_like(y_ref)


def _experts(blk_e, n_used, xs, w_gate, w_up, w_down):
    ne, d, de = w_gate.shape
    nt = d // LANES
    bs = EXPERT_ROWS
    nb = xs.shape[0] // (bs * nt)
    grid_spec = pltpu.PrefetchScalarGridSpec(
        num_scalar_prefetch=2,
        grid=(nb,),
        in_specs=[
            pl.BlockSpec((bs * nt, LANES), lambda j, be, nu: (j, 0)),
            pl.BlockSpec((1, d, de), lambda j, be, nu: (be[j], 0, 0)),
            pl.BlockSpec((1, d, de), lambda j, be, nu: (be[j], 0, 0)),
            pl.BlockSpec((1, de, d), lambda j, be, nu: (be[j], 0, 0)),
        ],
        out_specs=pl.BlockSpec((bs * nt, LANES), lambda j, be, nu: (j, 0)),
        scratch_shapes=[pltpu.VMEM((d, 2 * de), BF16), pltpu.VMEM((de, d), BF16)],
    )
    return pl.pallas_call(
        _experts_kernel,
        grid_spec=grid_spec,
        out_shape=jax.ShapeDtypeStruct(xs.shape, F32),
        compiler_params=pltpu.CompilerParams(
            dimension_semantics=("arbitrary",), vmem_limit_bytes=VMEM_LIMIT),
        name="experts",
    )(blk_e, n_used, xs, w_gate, w_up, w_down)


def _combine_kernel(ps_ref, info_ref, x1_ref, gt_ref, y_ref, gfin_ref, o_ref, y0_buf, y1_buf, sem):
    tc, d = x1_ref.shape
    nt = d // LANES
    bufs = (y0_buf, y1_buf)

    def issue(t, c):
        for k in range(TOP_K):
            src = ps_ref[info_ref[k, t]] + info_ref[TOP_K + k, t]
            _row_copy(y_ref, src, bufs[k], t, nt, sem).start()
        return c

    lax.fori_loop(0, tc, issue, 0)

    def drain(t, c):
        for k in range(TOP_K):
            _row_copy(y_ref, 0, bufs[k], 0, nt, sem).wait()
        return c

    lax.fori_loop(0, tc, drain, 0)

    y0 = _tiles_to_rows(y0_buf, tc, nt)
    y1 = _tiles_to_rows(y1_buf, tc, nt)
    gates = gt_ref[...]
    xo = x1_ref[...] + (gates[:, 0:1] * y0 + gates[:, 1:2] * y1)
    o_ref[...] = _rms(xo, gfin_ref[...])


def _combine(pstart, info, x1, gates_t, y, g_final):
    t, d = x1.shape
    nt = d // LANES
    tc = MOVE_TOKENS
    grid_spec = pltpu.PrefetchScalarGridSpec(
        num_scalar_prefetch=1,
        grid=(t // tc,),
        in_specs=[
            pl.BlockSpec((SUBLANES, tc), lambda i, ps: (0, i), memory_space=pltpu.SMEM),
            pl.BlockSpec((tc, d), lambda i, ps: (i, 0)),
            pl.BlockSpec((tc, LANES), lambda i, ps: (i, 0)),
            pl.BlockSpec(memory_space=pl.ANY),
            pl.BlockSpec((1, d), lambda i, ps: (0, 0)),
        ],
        out_specs=pl.BlockSpec((tc, d), lambda i, ps: (i, 0)),
        scratch_shapes=[pltpu.VMEM((tc * nt, LANES), F32), pltpu.VMEM((tc * nt, LANES), F32),
                        pltpu.SemaphoreType.DMA],
    )
    return pl.pallas_call(
        _combine_kernel,
        grid_spec=grid_spec,
        out_shape=jax.ShapeDtypeStruct((t, d), F32),
        compiler_params=pltpu.CompilerParams(dimension_semantics=("arbitrary",)),
        name="combine",
    )(pstart, info, x1, gates_t, y, g_final)


def _layer(x, g_mix, w_in, w_s, b_s, g_sgu, w_conv, w_out, g_ffn, w_rg, w_re, w_gate, w_up, w_down):
    b, s, d = x.shape
    t = b * s
    nt = d // LANES
    ne = w_gate.shape[0]
    bs = EXPERT_ROWS

    bias_full = jnp.repeat(b_s.T, d // N_GROUPS, axis=1)
    wr_t = jnp.concatenate([w_rg, w_re], axis=1).T
    wr_t = jnp.pad(wr_t, ((0, LANES - wr_t.shape[0]), (0, 0))).astype(BF16)
    x1, h2t, logits_t = _mixer(
        x, g_mix.reshape(1, d), w_in.astype(BF16), w_s.astype(BF16), bias_full, g_sgu.reshape(1, d),
        w_conv, w_out.astype(BF16), g_ffn.reshape(1, d), wr_t)

    info, gates_t, counts = _route(logits_t)

    counts = counts[:, 0]
    padded = (counts + bs - 1) // bs * bs
    pend = jnp.cumsum(padded)
    pstart = (pend - padded).astype(I32)
    n_rows = -(-(t * TOP_K + ne * (bs - 1)) // bs) * bs
    nb = n_rows // bs
    blk_e = jnp.minimum(jnp.searchsorted(pend, jnp.arange(nb, dtype=I32) * bs, side='right'),
                        ne - 1).astype(I32)
    n_used = (pend[-1:] // bs).astype(I32)

    xs = _dispatch(pstart, info, h2t, jnp.zeros((n_rows * nt, LANES), F32), nt)
    y = _experts(blk_e, n_used, xs, w_gate, w_up, w_down)
    return x1.reshape(t, d), info, gates_t, pstart, y


def kernel(x, g_mix, w_in, w_s, b_s, g_sgu, w_conv, w_out, g_ffn, w_router_group, w_router_expert,
           w_gate, w_up, w_down, g_final):
    b, s, d = x.shape
    depth = g_mix.shape[0]
    assert depth == 1, "the final RMSNorm is fused into the last layer's combine"
    assert s % MIX_ROWS == 0 and MIX_ROWS % CHUNK == 0 and d % LANES == 0
    assert (b * s) % ROUTE_TOKENS == 0 and (b * s) % MOVE_TOKENS == 0
    l = 0
    x1, info, gates_t, pstart, y = _layer(
        x, g_mix[l], w_in[l], w_s[l], b_s[l], g_sgu[l], w_conv[l], w_out[l], g_ffn[l],
        w_router_group[l], w_router_expert[l], w_gate[l], w_up[l], w_down[l])
    out = _combine(pstart, info, x1, gates_t, y, g_final.reshape(1, d))
    return out.reshape(b, s, d)
```

```python
import functools

import jax
import jax.numpy as jnp
from jax import lax
from jax.experimental import pallas as pl
from jax.experimental.pallas import tpu as pltpu

F32 = jnp.float32
BF16 = jnp.bfloat16
I32 = jnp.int32

EPS = 1e-6
LANES = 128
SUBLANES = 8
CHUNK = 128
N_GROUPS = 8
EXPERTS_PER_GROUP = 8
TOP_K = 2
CONV_K = 3
N_BRANCH = 7

MIX_ROWS = 256
ROUTE_TOKENS = 512
PLACE_TOKENS = 2048
MOVE_TOKENS = 256
MOVE_UNROLL = 8
EXPERT_ROWS = 128
VMEM_LIMIT = 56 * 1024 * 1024


def _rms(x, g):
    return x * lax.rsqrt(jnp.mean(x * x, axis=-1, keepdims=True) + EPS) * g


def _sigmoid(x):
    return 0.5 * (1.0 + jnp.tanh(0.5 * x))


def _gelu_tanh(x):
    c = 0.7978845608028654
    return x * (0.5 * (1.0 + jnp.tanh(c * (x + 0.044715 * (x * x * x)))))


def _rows_to_tiles(dst_ref, val, rows):
    nt = val.shape[1] // LANES
    for c in range(nt):
        dst_ref[pl.ds(c, rows, stride=nt), :] = val[:, c * LANES:(c + 1) * LANES]


def _tiles_to_rows(src_ref, rows, nt):
    return jnp.concatenate([src_ref[pl.ds(c, rows, stride=nt), :] for c in range(nt)], axis=1)


def _mixer_kernel(x_ref, xp_ref, xn_ref, gmix_ref, win_ref, ws_ref, bias_ref, gsgu_ref, wconv_ref,
                  wout_ref, gffn_ref, wr_ref, x1_ref, h2t_ref, lt_ref, vg_ref, z_ref, acc_ref):
    ts, d = x_ref.shape[1], x_ref.shape[2]
    s = pl.program_id(1)
    ns = pl.num_programs(1)
    gw = d // N_GROUPS
    nc = ts // CHUNK

    x = x_ref[0]
    gmix = gmix_ref[...]
    h = _rms(x, gmix).astype(BF16)

    def proj(j, n=1):
        return jnp.dot(h, win_ref[:, j * d:(j + n) * d], preferred_element_type=F32)

    gv = _gelu_tanh(proj(1))
    for g in range(N_GROUPS):
        cs = slice(g * gw, (g + 1) * gw)
        blk = gv[:, cs]
        mu = jnp.mean(blk, axis=-1, keepdims=True)
        dv = blk - mu
        var = jnp.mean(dv * dv, axis=-1, keepdims=True)
        vg_ref[:, cs] = (dv * lax.rsqrt(var + EPS) * gsgu_ref[:, cs]).astype(BF16)
    for g in range(N_GROUPS):
        cs = slice(g * gw, (g + 1) * gw)
        vcat = jnp.concatenate([vg_ref[n * CHUNK:(n + 1) * CHUNK, cs] for n in range(nc)], axis=1)
        zg = jnp.dot(ws_ref[g], vcat, preferred_element_type=F32)
        for n in range(nc):
            z_ref[n * CHUNK:(n + 1) * CHUNK, cs] = zg[:, n * gw:(n + 1) * gw]
    u = _gelu_tanh(proj(0))
    ga = _sigmoid(proj(5))
    for n in range(nc):
        rs = slice(n * CHUNK, (n + 1) * CHUNK)
        acc_ref[rs, :] = ga[rs] * (u[rs] * (z_ref[rs, :] + bias_ref[...]))

    cx = proj(3, 2)
    z2 = cx[:, :d] * cx[:, d:]
    xh = jnp.concatenate([xp_ref[0], xn_ref[0]], axis=0)
    hh = _rms(xh, gmix).astype(BF16)
    cxh = jnp.dot(hh, win_ref[:, 3 * d:5 * d], preferred_element_type=F32)
    z2h = cxh[:, :d] * cxh[:, d:]
    prev = jnp.where(s > 0, z2h[SUBLANES - 1:SUBLANES, :], 0.0)
    nxt = jnp.where(s < ns - 1, z2h[SUBLANES:SUBLANES + 1, :], 0.0)
    row = lax.broadcasted_iota(I32, (ts, d), 0)
    zm1 = jnp.where(row == 0, prev, pltpu.roll(z2, 1, 0))
    zp1 = jnp.where(row == ts - 1, nxt, pltpu.roll(z2, ts - 1, 0))
    conv = wconv_ref[0:1, :] * zm1 + wconv_ref[1:2, :] * z2 + wconv_ref[2:3, :] * zp1
    cb = proj(2)
    gb = _sigmoid(proj(6))
    merged = acc_ref[...] + gb * (cb * conv)

    x1 = x + jnp.dot(merged.astype(BF16), wout_ref[...], preferred_element_type=F32)
    x1_ref[0] = x1

    h2 = _rms(x1, gffn_ref[...]).astype(BF16)
    lt_ref[...] = lax.dot_general(wr_ref[...], h2, (((1,), (1,)), ((), ())),
                                  preferred_element_type=F32)
    _rows_to_tiles(h2t_ref, h2.astype(F32), ts)


def _mixer(x, g_mix, w_in_b, w_s_b, bias_full, g_sgu, w_conv, w_out_b, g_ffn, wr_t):
    b, s, d = x.shape
    ts = MIX_ROWS
    ns = s // ts
    t = b * s
    nt = d // LANES
    hb = ts // SUBLANES
    last_hb = s // SUBLANES - 1

    const = lambda *shape: pl.BlockSpec(shape, lambda bi, si: (0,) * len(shape))
    in_specs = [
        pl.BlockSpec((1, ts, d), lambda bi, si: (bi, si, 0)),
        pl.BlockSpec((1, SUBLANES, d), lambda bi, si: (bi, jnp.maximum(si * hb - 1, 0), 0)),
        pl.BlockSpec((1, SUBLANES, d), lambda bi, si: (bi, jnp.minimum((si + 1) * hb, last_hb), 0)),
        const(1, d),
        pl.BlockSpec((d, N_BRANCH * d), lambda bi, si: (0, 0), pipeline_mode=pl.Buffered(1)),
        const(N_GROUPS, CHUNK, CHUNK),
        const(CHUNK, d),
        const(1, d),
        const(CONV_K, d),
        const(d, d),
        const(1, d),
        const(LANES, d),
    ]
    out_specs = [
        pl.BlockSpec((1, ts, d), lambda bi, si: (bi, si, 0)),
        pl.BlockSpec((ts * nt, LANES), lambda bi, si: (bi * ns + si, 0)),
        pl.BlockSpec((LANES, ts), lambda bi, si: (0, bi * ns + si)),
    ]
    out_shape = [
        jax.ShapeDtypeStruct((b, s, d), F32),
        jax.ShapeDtypeStruct((t * nt, LANES), F32),
        jax.ShapeDtypeStruct((LANES, t), F32),
    ]
    return pl.pallas_call(
        _mixer_kernel,
        grid=(b, ns),
        in_specs=in_specs,
        out_specs=out_specs,
        out_shape=out_shape,
        scratch_shapes=[pltpu.VMEM((ts, d), BF16), pltpu.VMEM((ts, d), F32), pltpu.VMEM((ts, d), F32)],
        compiler_params=pltpu.CompilerParams(
            dimension_semantics=("arbitrary", "arbitrary"), vmem_limit_bytes=VMEM_LIMIT),
        name="mixer",
    )(x, x, x, g_mix, w_in_b, w_s_b, bias_full, g_sgu, w_conv, w_out_b, g_ffn, wr_t)


def _route_kernel(lt_ref, info_ref, gt_ref, cnt_ref, carry_ref):
    tb = lt_ref.shape[1]
    ne = N_GROUPS * EXPERTS_PER_GROUP

    @pl.when(pl.program_id(0) == 0)
    def _():
        carry_ref[...] = jnp.zeros_like(carry_ref)

    row8 = lax.broadcasted_iota(I32, (SUBLANES, tb), 0)
    gl = lt_ref[0:N_GROUPS, :]
    gmax = jnp.max(gl, axis=0, keepdims=True)
    gidx = jnp.min(jnp.where(gl == gmax, row8, N_GROUPS), axis=0, keepdims=True)
    pg = 1.0 / jnp.sum(jnp.exp(gl - gmax), axis=0, keepdims=True)

    sel = jnp.zeros((EXPERTS_PER_GROUP, tb), F32)
    for g in range(N_GROUPS):
        lo = N_GROUPS + g * EXPERTS_PER_GROUP
        sel = jnp.where(gidx == g, lt_ref[lo:lo + EXPERTS_PER_GROUP, :], sel)
    m1 = jnp.max(sel, axis=0, keepdims=True)
    i1 = jnp.min(jnp.where(sel == m1, row8, EXPERTS_PER_GROUP), axis=0, keepdims=True)
    sel2 = jnp.where(row8 == i1, -jnp.inf, sel)
    m2 = jnp.max(sel2, axis=0, keepdims=True)
    i2 = jnp.min(jnp.where(sel2 == m2, row8, EXPERTS_PER_GROUP), axis=0, keepdims=True)
    e2 = jnp.exp(m2 - m1)
    den = 1.0 + e2
    gate0 = pg * (1.0 / den)
    gate1 = pg * (e2 / den)
    eid0 = gidx * EXPERTS_PER_GROUP + i1
    eid1 = gidx * EXPERTS_PER_GROUP + i2

    rowe = lax.broadcasted_iota(I32, (ne, tb), 0)
    hit0 = rowe == eid0
    hit1 = rowe == eid1
    onehot = jnp.where(hit0 | hit1, 1.0, 0.0)
    before = (lax.broadcasted_iota(I32, (tb, tb), 0) < lax.broadcasted_iota(I32, (tb, tb), 1))
    prefix = jnp.dot(onehot.astype(BF16), jnp.where(before, 1.0, 0.0).astype(BF16),
                     preferred_element_type=F32)
    base = prefix + carry_ref[:, 0:1]
    rank0 = jnp.sum(jnp.where(hit0, base, 0.0), axis=0, keepdims=True).astype(I32)
    rank1 = jnp.sum(jnp.where(hit1, base, 0.0), axis=0, keepdims=True).astype(I32)
    carry_ref[...] = carry_ref[...] + jnp.sum(onehot, axis=1, keepdims=True)
    cnt_ref[...] = carry_ref[...].astype(I32)

    info_ref[...] = jnp.where(row8 == 0, eid0, jnp.where(row8 == 1, eid1,
                              jnp.where(row8 == 2, rank0, jnp.where(row8 == 3, rank1, 0))))
    rowl = lax.broadcasted_iota(I32, (LANES, tb), 0)
    gates = jnp.where(rowl == 0, gate0, jnp.where(rowl == 1, gate1, 0.0))
    gt_ref[...] = gates.T


def _route(logits_t):
    t = logits_t.shape[1]
    tb = ROUTE_TOKENS
    ne = N_GROUPS * EXPERTS_PER_GROUP
    return pl.pallas_call(
        _route_kernel,
        grid=(t // tb,),
        in_specs=[pl.BlockSpec((LANES, tb), lambda i: (0, i))],
        out_specs=[
            pl.BlockSpec((SUBLANES, tb), lambda i: (0, i)),
            pl.BlockSpec((tb, LANES), lambda i: (i, 0)),
            pl.BlockSpec((ne, LANES), lambda i: (0, 0)),
        ],
        out_shape=[
            jax.ShapeDtypeStruct((SUBLANES, t), I32),
            jax.ShapeDtypeStruct((t, LANES), F32),
            jax.ShapeDtypeStruct((ne, LANES), I32),
        ],
        scratch_shapes=[pltpu.VMEM((ne, LANES), F32)],
        compiler_params=pltpu.CompilerParams(dimension_semantics=("arbitrary",)),
        name="route",
    )(logits_t)


def _place_kernel(info_ref, ps_ref, dest_ref):
    tb = info_ref.shape[1]
    ne = ps_ref.shape[0]
    rowe = lax.broadcasted_iota(I32, (ne, tb), 0)
    row8 = lax.broadcasted_iota(I32, (SUBLANES, tb), 0)
    ps = ps_ref[:, 0:1]
    dest = jnp.zeros((SUBLANES, tb), I32)
    for k in range(TOP_K):
        start = jnp.sum(jnp.where(rowe == info_ref[k:k + 1, :], ps, 0), axis=0, keepdims=True)
        dest = jnp.where(row8 == k, start + info_ref[TOP_K + k:TOP_K + k + 1, :], dest)
    dest_ref[...] = dest


def _place(info, pstart):
    t = info.shape[1]
    tb = PLACE_TOKENS
    ne = pstart.shape[0]
    return pl.pallas_call(
        _place_kernel,
        grid=(t // tb,),
        in_specs=[pl.BlockSpec((SUBLANES, tb), lambda i: (0, i)),
                  pl.BlockSpec((ne, LANES), lambda i: (0, 0))],
        out_specs=pl.BlockSpec((SUBLANES, tb), lambda i: (0, i)),
        out_shape=jax.ShapeDtypeStruct((SUBLANES, t), I32),
        compiler_params=pltpu.CompilerParams(dimension_semantics=("arbitrary",)),
        name="place",
    )(info, jnp.broadcast_to(pstart[:, None], (ne, LANES)))


def _row_copy(src_ref, src_row, dst_ref, dst_row, nt, sem):
    return pltpu.make_async_copy(
        src_ref.at[pl.ds(pl.multiple_of(src_row * nt, nt), nt), :],
        dst_ref.at[pl.ds(pl.multiple_of(dst_row * nt, nt), nt), :],
        sem)


def _rows_wait(ref, n_rows, nt, sem):
    pltpu.make_async_copy(ref.at[pl.ds(0, n_rows * nt), :], ref.at[pl.ds(0, n_rows * nt), :], sem).wait()


def _for_each_assignment(dest_ref, n_tok, start_copy):
    def group(g, c):
        t0 = g * MOVE_UNROLL
        rows = [[dest_ref[k * n_tok + t0 + u] for k in range(TOP_K)] for u in range(MOVE_UNROLL)]
        for u in range(MOVE_UNROLL):
            for k in range(TOP_K):
                start_copy(k, t0 + u, rows[u][k])
        return c

    lax.fori_loop(0, n_tok // MOVE_UNROLL, group, 0)


def _dispatch_kernel(nt, td, dest_ref, h2_ref, xs_in_ref, xs_ref, sem):
    del xs_in_ref
    i = pl.program_id(0)
    base = i * td
    _for_each_assignment(
        dest_ref, td,
        lambda k, t, row: _row_copy(h2_ref, base + t, xs_ref, row, nt, sem).start(priority=k))

    @pl.when(i > 0)
    def _():
        _rows_wait(xs_ref, TOP_K * td, nt, sem)

    @pl.when(i == pl.num_programs(0) - 1)
    def _():
        _rows_wait(xs_ref, TOP_K * td, nt, sem)


def _dispatch(dest, h2t, xs_zero):
    td = MOVE_TOKENS
    t = dest.shape[0] // TOP_K
    return pl.pallas_call(
        functools.partial(_dispatch_kernel, h2t.shape[0] // t, td),
        grid=(t // td,),
        in_specs=[
            pl.BlockSpec((TOP_K * td,), lambda i: (i,), memory_space=pltpu.SMEM),
            pl.BlockSpec(memory_space=pl.ANY),
            pl.BlockSpec(memory_space=pl.ANY),
        ],
        out_specs=pl.BlockSpec(memory_space=pl.ANY),
        out_shape=jax.ShapeDtypeStruct(xs_zero.shape, xs_zero.dtype),
        scratch_shapes=[pltpu.SemaphoreType.DMA],
        input_output_aliases={2: 0},
        compiler_params=pltpu.CompilerParams(dimension_semantics=("arbitrary",)),
        name="dispatch",
    )(dest, h2t, xs_zero)


def _experts_kernel(be_ref, nx_ref, nu_ref, xs_ref, wg_hbm, wu_hbm, wd_hbm, y_ref,
                    wg_st, wu_st, wd_st, wgu_b, wd_b, slot_ref, sems):
    j = pl.program_id(0)
    d, de = wg_st.shape[1], wg_st.shape[2]
    nt = d // LANES
    bs = xs_ref.shape[0] // nt
    e = be_ref[j]
    used = j < nu_ref[0]
    first = (j == 0) | (e != be_ref[jnp.maximum(j - 1, 0)])

    def weight_copies(ex, slot):
        return (pltpu.make_async_copy(wg_hbm.at[ex], wg_st.at[slot], sems.at[slot]),
                pltpu.make_async_copy(wu_hbm.at[ex], wu_st.at[slot], sems.at[slot]),
                pltpu.make_async_copy(wd_hbm.at[ex], wd_st.at[slot], sems.at[slot]))

    @pl.when(j == 0)
    def _():
        slot_ref[0] = 0
        for c in weight_copies(e, 0):
            c.start()

    @pl.when(first & used)
    def _():
        slot = slot_ref[0]
        for c in weight_copies(e, slot):
            c.wait()
        nx = nx_ref[j]

        @pl.when(nx >= 0)
        def _():
            for c in weight_copies(nx, 1 - slot):
                c.start()

        wgu_b[:, :de] = wg_st[slot].astype(BF16)
        wgu_b[:, de:] = wu_st[slot].astype(BF16)
        wd_b[...] = wd_st[slot].astype(BF16)
        slot_ref[0] = 1 - slot

    @pl.when(used)
    def _():
        xb = _tiles_to_rows(xs_ref, bs, nt).astype(BF16)
        gu = jnp.dot(xb, wgu_b[...], preferred_element_type=F32)
        gate = gu[:, :de]
        hid = (gate * _sigmoid(gate)) * gu[:, de:]
        y = jnp.dot(hid.astype(BF16), wd_b[...], preferred_element_type=F32)
        _rows_to_tiles(y_ref, y, bs)

    @pl.when(jnp.logical_not(used))
    def _():
        y_ref[...] = jnp.zeros_like(y_ref)


def _experts(blk_e, nxt_e, n_used, xs, w_gate, w_up, w_down):
    ne, d, de = w_gate.shape
    nt = d // LANES
    bs = EXPERT_ROWS
    nb = xs.shape[0] // (bs * nt)
    blk = lambda j, be, nx, nu: (jnp.minimum(j, nu[0] - 1), 0)
    grid_spec = pltpu.PrefetchScalarGridSpec(
        num_scalar_prefetch=3,
        grid=(nb,),
        in_specs=[
            pl.BlockSpec((bs * nt, LANES), blk),
            pl.BlockSpec(memory_space=pl.ANY),
            pl.BlockSpec(memory_space=pl.ANY),
            pl.BlockSpec(memory_space=pl.ANY),
        ],
        out_specs=pl.BlockSpec((bs * nt, LANES), lambda j, be, nx, nu: (j, 0)),
        scratch_shapes=[pltpu.VMEM((2, d, de), F32), pltpu.VMEM((2, d, de), F32),
                        pltpu.VMEM((2, de, d), F32),
                        pltpu.VMEM((d, 2 * de), BF16), pltpu.VMEM((de, d), BF16),
                        pltpu.SMEM((1,), I32), pltpu.SemaphoreType.DMA((2,))],
    )
    return pl.pallas_call(
        _experts_kernel,
        grid_spec=grid_spec,
        out_shape=jax.ShapeDtypeStruct(xs.shape, F32),
        compiler_params=pltpu.CompilerParams(
            dimension_semantics=("arbitrary",), vmem_limit_bytes=VMEM_LIMIT),
        name="experts",
    )(blk_e, nxt_e, n_used, xs, w_gate, w_up, w_down)


def _combine_kernel(dcur_ref, dnxt_ref, x1_ref, gt_ref, y_ref, gfin_ref, o_ref, ybuf, sems):
    tc, d = x1_ref.shape
    nt = d // LANES
    i = pl.program_id(0)
    n = pl.num_programs(0)

    def issue_tile(dest_ref, slot):
        _for_each_assignment(
            dest_ref, tc,
            lambda k, t, row: _row_copy(y_ref, row, ybuf.at[slot, k], t, nt,
                                        sems.at[slot]).start(priority=k))

    @pl.when(i == 0)
    def _():
        issue_tile(dcur_ref, 0)

    @pl.when(i + 1 < n)
    def _():
        issue_tile(dnxt_ref, (i + 1) % 2)

    slot = i % 2
    _rows_wait(y_ref, TOP_K * tc, nt, sems.at[slot])
    y0 = _tiles_to_rows(ybuf.at[slot, 0], tc, nt)
    y1 = _tiles_to_rows(ybuf.at[slot, 1], tc, nt)
    gates = gt_ref[...]
    xo = x1_ref[...] + (gates[:, 0:1] * y0 + gates[:, 1:2] * y1)
    o_ref[...] = _rms(xo, gfin_ref[...])


def _combine(dest, x1, gates_t, y, g_final):
    t, d = x1.shape
    nt = d // LANES
    tc = MOVE_TOKENS
    last = t // tc - 1
    return pl.pallas_call(
        _combine_kernel,
        grid=(t // tc,),
        in_specs=[
            pl.BlockSpec((TOP_K * tc,), lambda i: (i,), memory_space=pltpu.SMEM),
            pl.BlockSpec((TOP_K * tc,), lambda i: (jnp.minimum(i + 1, last),),
                         memory_space=pltpu.SMEM),
            pl.BlockSpec((tc, d), lambda i: (i, 0)),
            pl.BlockSpec((tc, LANES), lambda i: (i, 0)),
            pl.BlockSpec(memory_space=pl.ANY),
            pl.BlockSpec((1, d), lambda i: (0, 0)),
        ],
        out_specs=pl.BlockSpec((tc, d), lambda i: (i, 0)),
        out_shape=jax.ShapeDtypeStruct((t, d), F32),
        scratch_shapes=[pltpu.VMEM((2, TOP_K, tc * nt, LANES), F32), pltpu.SemaphoreType.DMA((2,))],
        compiler_params=pltpu.CompilerParams(dimension_semantics=("arbitrary",)),
        name="combine",
    )(dest, dest, x1, gates_t, y, g_final)


def _layer(x, g_mix, w_in, w_s, b_s, g_sgu, w_conv, w_out, g_ffn, w_rg, w_re, w_gate, w_up, w_down):
    b, s, d = x.shape
    t = b * s
    nt = d // LANES
    ne = w_gate.shape[0]
    bs = EXPERT_ROWS

    bias_full = jnp.repeat(b_s.T, d // N_GROUPS, axis=1)
    wr_t = jnp.concatenate([w_rg, w_re], axis=1).T
    wr_t = jnp.pad(wr_t, ((0, LANES - wr_t.shape[0]), (0, 0))).astype(BF16)
    x1, h2t, logits_t = _mixer(
        x, g_mix.reshape(1, d), w_in.astype(BF16), w_s.astype(BF16), bias_full, g_sgu.reshape(1, d),
        w_conv, w_out.astype(BF16), g_ffn.reshape(1, d), wr_t)

    info, gates_t, counts = _route(logits_t)

    counts = counts[:, 0]
    padded = (counts + bs - 1) // bs * bs
    pend = jnp.cumsum(padded)
    pstart = (pend - padded).astype(I32)
    n_rows = -(-(t * TOP_K + ne * (bs - 1)) // bs) * bs
    nb = n_rows // bs
    first_row = jnp.arange(nb, dtype=I32) * bs
    blk_e = jnp.minimum(jnp.sum(pend[None, :] <= first_row[:, None], axis=1), ne - 1).astype(I32)
    n_used = (pend[-1:] // bs).astype(I32)
    end_blk = pend[blk_e] // bs
    nxt_e = jnp.where(end_blk < n_used[0], blk_e[jnp.minimum(end_blk, nb - 1)], -1).astype(I32)

    dest = _place(info, pstart)[:TOP_K].reshape(TOP_K, t // MOVE_TOKENS, MOVE_TOKENS)
    dest = dest.transpose(1, 0, 2).reshape(-1)
    xs = _dispatch(dest, h2t, jnp.zeros((n_rows * nt, LANES), F32))
    y = _experts(blk_e, nxt_e, n_used, xs, w_gate, w_up, w_down)
    return x1.reshape(t, d), dest, gates_t, y


def kernel(x, g_mix, w_in, w_s, b_s, g_sgu, w_conv, w_out, g_ffn, w_router_group, w_router_expert,
           w_gate, w_up, w_down, g_final):
    b, s, d = x.shape
    depth = g_mix.shape[0]
    assert depth == 1, "the final RMSNorm is fused into the last layer's combine"
    assert s % MIX_ROWS == 0 and MIX_ROWS % CHUNK == 0 and d % LANES == 0
    assert (b * s) % ROUTE_TOKENS == 0 and (b * s) % MOVE_TOKENS == 0
    l = 0
    x1, dest, gates_t, y = _layer(
        x, g_mix[l], w_in[l], w_s[l], b_s[l], g_sgu[l], w_conv[l], w_out[l], g_ffn[l],
        w_router_group[l], w_router_expert[l], w_gate[l], w_up[l], w_down[l])
    out = _combine(dest, x1, gates_t, y, g_final.reshape(1, d))
    return out.reshape(b, s, d)
```

```python
import functools

import jax
import jax.numpy as jnp
from jax import lax
from jax.experimental import pallas as pl
from jax.experimental.pallas import tpu as pltpu

F32 = jnp.float32
BF16 = jnp.bfloat16
I32 = jnp.int32

EPS = 1e-6
LANES = 128
SUBLANES = 8
CHUNK = 128
N_GROUPS = 8
EXPERTS_PER_GROUP = 8
TOP_K = 2
CONV_K = 3
N_BRANCH = 7

MIX_ROWS = 256
ROUTE_TOKENS = 512
PLACE_TOKENS = 2048
DISPATCH_TOKENS = 1024
COMBINE_TOKENS = 256
MOVE_UNROLL = 8
WEIGHT_DMA_PRIORITY = 1
EXPERT_ROWS = 128
VMEM_LIMIT = 56 * 1024 * 1024


def _rms(x, g):
    return x * lax.rsqrt(jnp.mean(x * x, axis=-1, keepdims=True) + EPS) * g


def _sigmoid(x):
    return 0.5 * (1.0 + jnp.tanh(0.5 * x))


def _gelu_tanh(x):
    c = 0.7978845608028654
    return x * (0.5 * (1.0 + jnp.tanh(c * (x + 0.044715 * (x * x * x)))))


def _rows_to_tiles(dst_ref, val, rows):
    nt = val.shape[1] // LANES
    for c in range(nt):
        dst_ref[pl.ds(c, rows, stride=nt), :] = val[:, c * LANES:(c + 1) * LANES]


def _tiles_to_rows(src_ref, rows, nt):
    return jnp.concatenate([src_ref[pl.ds(c, rows, stride=nt), :] for c in range(nt)], axis=1)


def _mixer_kernel(x_ref, xp_ref, xn_ref, gmix_ref, win_ref, ws_ref, bias_ref, gsgu_ref, wconv_ref,
                  wout_ref, gffn_ref, wr_ref, x1_ref, h2t_ref, lt_ref, vg_ref, z_ref, acc_ref):
    ts, d = x_ref.shape[1], x_ref.shape[2]
    s = pl.program_id(1)
    ns = pl.num_programs(1)
    gw = d // N_GROUPS
    nc = ts // CHUNK

    x = x_ref[0]
    gmix = gmix_ref[...]
    h = _rms(x, gmix).astype(BF16)

    def proj(j, n=1):
        return jnp.dot(h, win_ref[:, j * d:(j + n) * d], preferred_element_type=F32)

    gv = _gelu_tanh(proj(1))
    for g in range(N_GROUPS):
        cs = slice(g * gw, (g + 1) * gw)
        blk = gv[:, cs]
        mu = jnp.mean(blk, axis=-1, keepdims=True)
        dv = blk - mu
        var = jnp.mean(dv * dv, axis=-1, keepdims=True)
        vg_ref[:, cs] = (dv * lax.rsqrt(var + EPS) * gsgu_ref[:, cs]).astype(BF16)
    for g in range(N_GROUPS):
        cs = slice(g * gw, (g + 1) * gw)
        vcat = jnp.concatenate([vg_ref[n * CHUNK:(n + 1) * CHUNK, cs] for n in range(nc)], axis=1)
        zg = jnp.dot(ws_ref[g], vcat, preferred_element_type=F32)
        for n in range(nc):
            z_ref[n * CHUNK:(n + 1) * CHUNK, cs] = zg[:, n * gw:(n + 1) * gw]
    u = _gelu_tanh(proj(0))
    ga = _sigmoid(proj(5))
    for n in range(nc):
        rs = slice(n * CHUNK, (n + 1) * CHUNK)
        acc_ref[rs, :] = ga[rs] * (u[rs] * (z_ref[rs, :] + bias_ref[...]))

    cx = proj(3, 2)
    z2 = cx[:, :d] * cx[:, d:]
    xh = jnp.concatenate([xp_ref[0], xn_ref[0]], axis=0)
    hh = _rms(xh, gmix).astype(BF16)
    cxh = jnp.dot(hh, win_ref[:, 3 * d:5 * d], preferred_element_type=F32)
    z2h = cxh[:, :d] * cxh[:, d:]
    prev = jnp.where(s > 0, z2h[SUBLANES - 1:SUBLANES, :], 0.0)
    nxt = jnp.where(s < ns - 1, z2h[SUBLANES:SUBLANES + 1, :], 0.0)
    row = lax.broadcasted_iota(I32, (ts, d), 0)
    zm1 = jnp.where(row == 0, prev, pltpu.roll(z2, 1, 0))
    zp1 = jnp.where(row == ts - 1, nxt, pltpu.roll(z2, ts - 1, 0))
    conv = wconv_ref[0:1, :] * zm1 + wconv_ref[1:2, :] * z2 + wconv_ref[2:3, :] * zp1
    cb = proj(2)
    gb = _sigmoid(proj(6))
    merged = acc_ref[...] + gb * (cb * conv)

    x1 = x + jnp.dot(merged.astype(BF16), wout_ref[...], preferred_element_type=F32)
    x1_ref[0] = x1

    h2 = _rms(x1, gffn_ref[...]).astype(BF16)
    lt_ref[...] = lax.dot_general(wr_ref[...], h2, (((1,), (1,)), ((), ())),
                                  preferred_element_type=F32)
    _rows_to_tiles(h2t_ref, h2.astype(F32), ts)


def _mixer(x, g_mix, w_in_b, w_s_b, bias_full, g_sgu, w_conv, w_out_b, g_ffn, wr_t):
    b, s, d = x.shape
    ts = MIX_ROWS
    ns = s // ts
    t = b * s
    nt = d // LANES
    hb = ts // SUBLANES
    last_hb = s // SUBLANES - 1

    const = lambda *shape: pl.BlockSpec(shape, lambda bi, si: (0,) * len(shape))
    in_specs = [
        pl.BlockSpec((1, ts, d), lambda bi, si: (bi, si, 0)),
        pl.BlockSpec((1, SUBLANES, d), lambda bi, si: (bi, jnp.maximum(si * hb - 1, 0), 0)),
        pl.BlockSpec((1, SUBLANES, d), lambda bi, si: (bi, jnp.minimum((si + 1) * hb, last_hb), 0)),
        const(1, d),
        pl.BlockSpec((d, N_BRANCH * d), lambda bi, si: (0, 0), pipeline_mode=pl.Buffered(1)),
        const(N_GROUPS, CHUNK, CHUNK),
        const(CHUNK, d),
        const(1, d),
        const(CONV_K, d),
        const(d, d),
        const(1, d),
        const(LANES, d),
    ]
    out_specs = [
        pl.BlockSpec((1, ts, d), lambda bi, si: (bi, si, 0)),
        pl.BlockSpec((ts * nt, LANES), lambda bi, si: (bi * ns + si, 0)),
        pl.BlockSpec((LANES, ts), lambda bi, si: (0, bi * ns + si)),
    ]
    out_shape = [
        jax.ShapeDtypeStruct((b, s, d), F32),
        jax.ShapeDtypeStruct((t * nt, LANES), F32),
        jax.ShapeDtypeStruct((LANES, t), F32),
    ]
    return pl.pallas_call(
        _mixer_kernel,
        grid=(b, ns),
        in_specs=in_specs,
        out_specs=out_specs,
        out_shape=out_shape,
        scratch_shapes=[pltpu.VMEM((ts, d), BF16), pltpu.VMEM((ts, d), F32), pltpu.VMEM((ts, d), F32)],
        compiler_params=pltpu.CompilerParams(
            dimension_semantics=("arbitrary", "arbitrary"), vmem_limit_bytes=VMEM_LIMIT),
        name="mixer",
    )(x, x, x, g_mix, w_in_b, w_s_b, bias_full, g_sgu, w_conv, w_out_b, g_ffn, wr_t)


def _route_kernel(lt_ref, info_ref, gt_ref, cnt_ref, carry_ref):
    tb = lt_ref.shape[1]
    ne = N_GROUPS * EXPERTS_PER_GROUP

    @pl.when(pl.program_id(0) == 0)
    def _():
        carry_ref[...] = jnp.zeros_like(carry_ref)

    row8 = lax.broadcasted_iota(I32, (SUBLANES, tb), 0)
    gl = lt_ref[0:N_GROUPS, :]
    gmax = jnp.max(gl, axis=0, keepdims=True)
    gidx = jnp.min(jnp.where(gl == gmax, row8, N_GROUPS), axis=0, keepdims=True)
    pg = 1.0 / jnp.sum(jnp.exp(gl - gmax), axis=0, keepdims=True)

    sel = jnp.zeros((EXPERTS_PER_GROUP, tb), F32)
    for g in range(N_GROUPS):
        lo = N_GROUPS + g * EXPERTS_PER_GROUP
        sel = jnp.where(gidx == g, lt_ref[lo:lo + EXPERTS_PER_GROUP, :], sel)
    m1 = jnp.max(sel, axis=0, keepdims=True)
    i1 = jnp.min(jnp.where(sel == m1, row8, EXPERTS_PER_GROUP), axis=0, keepdims=True)
    sel2 = jnp.where(row8 == i1, -jnp.inf, sel)
    m2 = jnp.max(sel2, axis=0, keepdims=True)
    i2 = jnp.min(jnp.where(sel2 == m2, row8, EXPERTS_PER_GROUP), axis=0, keepdims=True)
    e2 = jnp.exp(m2 - m1)
    den = 1.0 + e2
    gate0 = pg * (1.0 / den)
    gate1 = pg * (e2 / den)
    eid0 = gidx * EXPERTS_PER_GROUP + i1
    eid1 = gidx * EXPERTS_PER_GROUP + i2

    rowe = lax.broadcasted_iota(I32, (ne, tb), 0)
    hit0 = rowe == eid0
    hit1 = rowe == eid1
    onehot = jnp.where(hit0 | hit1, 1.0, 0.0)
    before = (lax.broadcasted_iota(I32, (tb, tb), 0) < lax.broadcasted_iota(I32, (tb, tb), 1))
    prefix = jnp.dot(onehot.astype(BF16), jnp.where(before, 1.0, 0.0).astype(BF16),
                     preferred_element_type=F32)
    base = prefix + carry_ref[:, 0:1]
    rank0 = jnp.sum(jnp.where(hit0, base, 0.0), axis=0, keepdims=True).astype(I32)
    rank1 = jnp.sum(jnp.where(hit1, base, 0.0), axis=0, keepdims=True).astype(I32)
    carry_ref[...] = carry_ref[...] + jnp.sum(onehot, axis=1, keepdims=True)
    cnt_ref[...] = carry_ref[...].astype(I32)

    info_ref[...] = jnp.where(row8 == 0, eid0, jnp.where(row8 == 1, eid1,
                              jnp.where(row8 == 2, rank0, jnp.where(row8 == 3, rank1, 0))))
    rowl = lax.broadcasted_iota(I32, (LANES, tb), 0)
    gates = jnp.where(rowl == 0, gate0, jnp.where(rowl == 1, gate1, 0.0))
    gt_ref[...] = gates.T


def _route(logits_t):
    t = logits_t.shape[1]
    tb = ROUTE_TOKENS
    ne = N_GROUPS * EXPERTS_PER_GROUP
    return pl.pallas_call(
        _route_kernel,
        grid=(t // tb,),
        in_specs=[pl.BlockSpec((LANES, tb), lambda i: (0, i))],
        out_specs=[
            pl.BlockSpec((SUBLANES, tb), lambda i: (0, i)),
            pl.BlockSpec((tb, LANES), lambda i: (i, 0)),
            pl.BlockSpec((ne, LANES), lambda i: (0, 0)),
        ],
        out_shape=[
            jax.ShapeDtypeStruct((SUBLANES, t), I32),
            jax.ShapeDtypeStruct((t, LANES), F32),
            jax.ShapeDtypeStruct((ne, LANES), I32),
        ],
        scratch_shapes=[pltpu.VMEM((ne, LANES), F32)],
        compiler_params=pltpu.CompilerParams(dimension_semantics=("arbitrary",)),
        name="route",
    )(logits_t)


def _place_kernel(info_ref, ps_ref, dest_ref):
    tb = info_ref.shape[1]
    ne = ps_ref.shape[0]
    rowe = lax.broadcasted_iota(I32, (ne, tb), 0)
    row8 = lax.broadcasted_iota(I32, (SUBLANES, tb), 0)
    ps = ps_ref[:, 0:1]
    dest = jnp.zeros((SUBLANES, tb), I32)
    for k in range(TOP_K):
        start = jnp.sum(jnp.where(rowe == info_ref[k:k + 1, :], ps, 0), axis=0, keepdims=True)
        dest = jnp.where(row8 == k, start + info_ref[TOP_K + k:TOP_K + k + 1, :], dest)
    dest_ref[...] = dest


def _place(info, pstart):
    t = info.shape[1]
    tb = PLACE_TOKENS
    ne = pstart.shape[0]
    return pl.pallas_call(
        _place_kernel,
        grid=(t // tb,),
        in_specs=[pl.BlockSpec((SUBLANES, tb), lambda i: (0, i)),
                  pl.BlockSpec((ne, LANES), lambda i: (0, 0))],
        out_specs=pl.BlockSpec((SUBLANES, tb), lambda i: (0, i)),
        out_shape=jax.ShapeDtypeStruct((SUBLANES, t), I32),
        compiler_params=pltpu.CompilerParams(dimension_semantics=("arbitrary",)),
        name="place",
    )(info, jnp.broadcast_to(pstart[:, None], (ne, LANES)))


def _row_copy(src_ref, src_row, dst_ref, dst_row, nt, sem):
    return pltpu.make_async_copy(
        src_ref.at[pl.ds(pl.multiple_of(src_row * nt, nt), nt), :],
        dst_ref.at[pl.ds(pl.multiple_of(dst_row * nt, nt), nt), :],
        sem)


def _rows_wait(ref, n_rows, nt, sem):
    pltpu.make_async_copy(ref.at[pl.ds(0, n_rows * nt), :], ref.at[pl.ds(0, n_rows * nt), :], sem).wait()


def _for_each_assignment(dest_ref, n_tok, start_copy):
    def group(g, c):
        t0 = g * MOVE_UNROLL
        rows = [[dest_ref[k * n_tok + t0 + u] for k in range(TOP_K)] for u in range(MOVE_UNROLL)]
        for u in range(MOVE_UNROLL):
            for k in range(TOP_K):
                start_copy(k, t0 + u, rows[u][k])
        return c

    lax.fori_loop(0, n_tok // MOVE_UNROLL, group, 0)


def _dispatch_kernel(nt, dest_ref, h2_ref, xs_in_ref, xs_ref, sem):
    del xs_in_ref
    td = h2_ref.shape[0] // nt
    _for_each_assignment(
        dest_ref, td,
        lambda k, t, row: _row_copy(h2_ref, t, xs_ref, row, nt, sem).start(priority=k))
    _rows_wait(xs_ref, TOP_K * td, nt, sem)


def _dispatch(dest, h2t, xs_zero):
    td = DISPATCH_TOKENS
    t = dest.shape[0] // TOP_K
    nt = h2t.shape[0] // t
    return pl.pallas_call(
        functools.partial(_dispatch_kernel, nt),
        grid=(t // td,),
        in_specs=[
            pl.BlockSpec((TOP_K * td,), lambda i: (i,), memory_space=pltpu.SMEM),
            pl.BlockSpec((td * nt, LANES), lambda i: (i, 0)),
            pl.BlockSpec(memory_space=pl.ANY),
        ],
        out_specs=pl.BlockSpec(memory_space=pl.ANY),
        out_shape=jax.ShapeDtypeStruct(xs_zero.shape, xs_zero.dtype),
        scratch_shapes=[pltpu.SemaphoreType.DMA],
        input_output_aliases={2: 0},
        compiler_params=pltpu.CompilerParams(dimension_semantics=("arbitrary",)),
        name="dispatch",
    )(dest, h2t, xs_zero)


def _experts_kernel(be_ref, nx_ref, nu_ref, xs_ref, wg_hbm, wu_hbm, wd_hbm, y_ref,
                    wg_st, wu_st, wd_st, wgu_b, wd_b, slot_ref, sems):
    j = pl.program_id(0)
    d, de = wg_st.shape[1], wg_st.shape[2]
    nt = d // LANES
    bs = xs_ref.shape[0] // nt
    e = be_ref[j]
    used = j < nu_ref[0]
    first = (j == 0) | (e != be_ref[jnp.maximum(j - 1, 0)])

    def weight_copies(ex, slot):
        return (pltpu.make_async_copy(wg_hbm.at[ex], wg_st.at[slot], sems.at[slot]),
                pltpu.make_async_copy(wu_hbm.at[ex], wu_st.at[slot], sems.at[slot]),
                pltpu.make_async_copy(wd_hbm.at[ex], wd_st.at[slot], sems.at[slot]))

    @pl.when(j == 0)
    def _():
        slot_ref[0] = 0
        for c in weight_copies(e, 0):
            c.start(priority=WEIGHT_DMA_PRIORITY)

    @pl.when(first & used)
    def _():
        slot = slot_ref[0]
        for c in weight_copies(e, slot):
            c.wait()
        nx = nx_ref[j]

        @pl.when(nx >= 0)
        def _():
            for c in weight_copies(nx, 1 - slot):
                c.start(priority=WEIGHT_DMA_PRIORITY)

        wgu_b[:, :de] = wg_st[slot].astype(BF16)
        wgu_b[:, de:] = wu_st[slot].astype(BF16)
        wd_b[...] = wd_st[slot].astype(BF16)
        slot_ref[0] = 1 - slot

    @pl.when(used)
    def _():
        xb = _tiles_to_rows(xs_ref, bs, nt).astype(BF16)
        gu = jnp.dot(xb, wgu_b[...], preferred_element_type=F32)
        gate = gu[:, :de]
        hid = (gate * _sigmoid(gate)) * gu[:, de:]
        y = jnp.dot(hid.astype(BF16), wd_b[...], preferred_element_type=F32)
        _rows_to_tiles(y_ref, y, bs)

    @pl.when(jnp.logical_not(used))
    def _():
        y_ref[...] = jnp.zeros_like(y_ref)


def _experts(blk_e, nxt_e, n_used, xs, w_gate, w_up, w_down):
    ne, d, de = w_gate.shape
    nt = d // LANES
    bs = EXPERT_ROWS
    nb = xs.shape[0] // (bs * nt)
    blk = lambda j, be, nx, nu: (jnp.minimum(j, nu[0] - 1), 0)
    grid_spec = pltpu.PrefetchScalarGridSpec(
        num_scalar_prefetch=3,
        grid=(nb,),
        in_specs=[
            pl.BlockSpec((bs * nt, LANES), blk),
            pl.BlockSpec(memory_space=pl.ANY),
            pl.BlockSpec(memory_space=pl.ANY),
            pl.BlockSpec(memory_space=pl.ANY),
        ],
        out_specs=pl.BlockSpec((bs * nt, LANES), lambda j, be, nx, nu: (j, 0)),
        scratch_shapes=[pltpu.VMEM((2, d, de), F32), pltpu.VMEM((2, d, de), F32),
                        pltpu.VMEM((2, de, d), F32),
                        pltpu.VMEM((d, 2 * de), BF16), pltpu.VMEM((de, d), BF16),
                        pltpu.SMEM((1,), I32), pltpu.SemaphoreType.DMA((2,))],
    )
    return pl.pallas_call(
        _experts_kernel,
        grid_spec=grid_spec,
        out_shape=jax.ShapeDtypeStruct(xs.shape, F32),
        compiler_params=pltpu.CompilerParams(
            dimension_semantics=("arbitrary",), vmem_limit_bytes=VMEM_LIMIT),
        name="experts",
    )(blk_e, nxt_e, n_used, xs, w_gate, w_up, w_down)


def _combine_kernel(dcur_ref, dnxt_ref, x1_ref, gt_ref, y_ref, gfin_ref, o_ref, ybuf, sems):
    tc, d = x1_ref.shape
    nt = d // LANES
    i = pl.program_id(0)
    n = pl.num_programs(0)

    def issue_tile(dest_ref, slot):
        _for_each_assignment(
            dest_ref, tc,
            lambda k, t, row: _row_copy(y_ref, row, ybuf.at[slot, k], t, nt,
                                        sems.at[slot]).start(priority=k))

    @pl.when(i == 0)
    def _():
        issue_tile(dcur_ref, 0)

    @pl.when(i + 1 < n)
    def _():
        issue_tile(dnxt_ref, (i + 1) % 2)

    slot = i % 2
    _rows_wait(y_ref, TOP_K * tc, nt, sems.at[slot])
    y0 = _tiles_to_rows(ybuf.at[slot, 0], tc, nt)
    y1 = _tiles_to_rows(ybuf.at[slot, 1], tc, nt)
    gates = gt_ref[...]
    xo = x1_ref[...] + (gates[:, 0:1] * y0 + gates[:, 1:2] * y1)
    o_ref[...] = _rms(xo, gfin_ref[...])


def _combine(dest, x1, gates_t, y, g_final):
    t, d = x1.shape
    nt = d // LANES
    tc = COMBINE_TOKENS
    last = t // tc - 1
    return pl.pallas_call(
        _combine_kernel,
        grid=(t // tc,),
        in_specs=[
            pl.BlockSpec((TOP_K * tc,), lambda i: (i,), memory_space=pltpu.SMEM),
            pl.BlockSpec((TOP_K * tc,), lambda i: (jnp.minimum(i + 1, last),),
                         memory_space=pltpu.SMEM),
            pl.BlockSpec((tc, d), lambda i: (i, 0)),
            pl.BlockSpec((tc, LANES), lambda i: (i, 0)),
            pl.BlockSpec(memory_space=pl.ANY),
            pl.BlockSpec((1, d), lambda i: (0, 0)),
        ],
        out_specs=pl.BlockSpec((tc, d), lambda i: (i, 0)),
        out_shape=jax.ShapeDtypeStruct((t, d), F32),
        scratch_shapes=[pltpu.VMEM((2, TOP_K, tc * nt, LANES), F32), pltpu.SemaphoreType.DMA((2,))],
        compiler_params=pltpu.CompilerParams(dimension_semantics=("arbitrary",)),
        name="combine",
    )(dest, dest, x1, gates_t, y, g_final)


def _layer(x, g_mix, w_in, w_s, b_s, g_sgu, w_conv, w_out, g_ffn, w_rg, w_re, w_gate, w_up, w_down):
    b, s, d = x.shape
    t = b * s
    nt = d // LANES
    ne = w_gate.shape[0]
    bs = EXPERT_ROWS

    bias_full = jnp.repeat(b_s.T, d // N_GROUPS, axis=1)
    wr_t = jnp.concatenate([w_rg, w_re], axis=1).T
    wr_t = jnp.pad(wr_t, ((0, LANES - wr_t.shape[0]), (0, 0))).astype(BF16)
    x1, h2t, logits_t = _mixer(
        x, g_mix.reshape(1, d), w_in.astype(BF16), w_s.astype(BF16), bias_full, g_sgu.reshape(1, d),
        w_conv, w_out.astype(BF16), g_ffn.reshape(1, d), wr_t)

    info, gates_t, counts = _route(logits_t)

    counts = counts[:, 0]
    padded = (counts + bs - 1) // bs * bs
    pend = jnp.cumsum(padded)
    pstart = (pend - padded).astype(I32)
    n_rows = -(-(t * TOP_K + ne * (bs - 1)) // bs) * bs
    nb = n_rows // bs
    first_row = jnp.arange(nb, dtype=I32) * bs
    blk_e = jnp.minimum(jnp.sum(pend[None, :] <= first_row[:, None], axis=1), ne - 1).astype(I32)
    n_used = (pend[-1:] // bs).astype(I32)
    end_blk = pend[blk_e] // bs
    nxt_e = jnp.where(end_blk < n_used[0], blk_e[jnp.minimum(end_blk, nb - 1)], -1).astype(I32)

    dest = _place(info, pstart)[:TOP_K]
    tiled = lambda n: dest.reshape(TOP_K, t // n, n).transpose(1, 0, 2).reshape(-1)
    xs = _dispatch(tiled(DISPATCH_TOKENS), h2t, jnp.zeros((n_rows * nt, LANES), F32))
    y = _experts(blk_e, nxt_e, n_used, xs, w_gate, w_up, w_down)
    return x1.reshape(t, d), tiled(COMBINE_TOKENS), gates_t, y


def kernel(x, g_mix, w_in, w_s, b_s, g_sgu, w_conv, w_out, g_ffn, w_router_group, w_router_expert,
           w_gate, w_up, w_down, g_final):
    b, s, d = x.shape
    depth = g_mix.shape[0]
    assert depth == 1, "the final RMSNorm is fused into the last layer's combine"
    assert s % MIX_ROWS == 0 and MIX_ROWS % CHUNK == 0 and d % LANES == 0
    assert all((b * s) % n == 0 for n in (ROUTE_TOKENS, PLACE_TOKENS, DISPATCH_TOKENS, COMBINE_TOKENS))
    l = 0
    x1, dest, gates_t, y = _layer(
        x, g_mix[l], w_in[l], w_s[l], b_s[l], g_sgu[l], w_conv[l], w_out[l], g_ffn[l],
        w_router_group[l], w_router_expert[l], w_gate[l], w_up[l], w_down[l])
    out = _combine(dest, x1, gates_t, y, g_final.reshape(1, d))
    return out.reshape(b, s, d)
```

```python
import functools

import jax
import jax.numpy as jnp
from jax import lax
from jax.experimental import pallas as pl
from jax.experimental.pallas import tpu as pltpu

F32 = jnp.float32
BF16 = jnp.bfloat16
I32 = jnp.int32

EPS = 1e-6
LANES = 128
SUBLANES = 8
CHUNK = 128
N_GROUPS = 8
EXPERTS_PER_GROUP = 8
TOP_K = 2
CONV_K = 3
N_BRANCH = 7

MIX_ROWS = 256
ROUTE_TOKENS = 512
PLACE_TOKENS = 2048
DISPATCH_TOKENS = 1024
COMBINE_TOKENS = 256
MOVE_UNROLL = 8
WEIGHT_DMA_PRIORITY = 1
EXPERT_ROWS = 128
EXPERT_LOOKAHEAD = 3
VMEM_LIMIT = 56 * 1024 * 1024


def _rms(x, g):
    return x * lax.rsqrt(jnp.mean(x * x, axis=-1, keepdims=True) + EPS) * g


def _sigmoid(x):
    return 0.5 * (1.0 + jnp.tanh(0.5 * x))


def _gelu_tanh(x):
    c = 0.7978845608028654
    return x * (0.5 * (1.0 + jnp.tanh(c * (x + 0.044715 * (x * x * x)))))


def _rows_to_tiles(dst_ref, val, rows):
    nt = val.shape[1] // LANES
    for c in range(nt):
        dst_ref[pl.ds(c, rows, stride=nt), :] = val[:, c * LANES:(c + 1) * LANES]


def _tiles_to_rows(src_ref, rows, nt):
    return jnp.concatenate([src_ref[pl.ds(c, rows, stride=nt), :] for c in range(nt)], axis=1)


def _mixer_kernel(x_ref, xp_ref, xn_ref, gmix_ref, win_ref, ws_ref, bias_ref, gsgu_ref, wconv_ref,
                  wout_ref, gffn_ref, wr_ref, x1_ref, h2t_ref, lt_ref, vg_ref, z_ref, acc_ref):
    ts, d = x_ref.shape[1], x_ref.shape[2]
    s = pl.program_id(1)
    ns = pl.num_programs(1)
    gw = d // N_GROUPS
    nc = ts // CHUNK

    x = x_ref[0]
    gmix = gmix_ref[...]
    h = _rms(x, gmix).astype(BF16)

    def proj(j, n=1):
        return jnp.dot(h, win_ref[:, j * d:(j + n) * d], preferred_element_type=F32)

    gv = _gelu_tanh(proj(1))
    for g in range(N_GROUPS):
        cs = slice(g * gw, (g + 1) * gw)
        blk = gv[:, cs]
        mu = jnp.mean(blk, axis=-1, keepdims=True)
        dv = blk - mu
        var = jnp.mean(dv * dv, axis=-1, keepdims=True)
        vg_ref[:, cs] = (dv * lax.rsqrt(var + EPS) * gsgu_ref[:, cs]).astype(BF16)
    for g in range(N_GROUPS):
        cs = slice(g * gw, (g + 1) * gw)
        vcat = jnp.concatenate([vg_ref[n * CHUNK:(n + 1) * CHUNK, cs] for n in range(nc)], axis=1)
        zg = jnp.dot(ws_ref[g], vcat, preferred_element_type=F32)
        for n in range(nc):
            z_ref[n * CHUNK:(n + 1) * CHUNK, cs] = zg[:, n * gw:(n + 1) * gw]
    u = _gelu_tanh(proj(0))
    ga = _sigmoid(proj(5))
    for n in range(nc):
        rs = slice(n * CHUNK, (n + 1) * CHUNK)
        acc_ref[rs, :] = ga[rs] * (u[rs] * (z_ref[rs, :] + bias_ref[...]))

    cx = proj(3, 2)
    z2 = cx[:, :d] * cx[:, d:]
    xh = jnp.concatenate([xp_ref[0], xn_ref[0]], axis=0)
    hh = _rms(xh, gmix).astype(BF16)
    cxh = jnp.dot(hh, win_ref[:, 3 * d:5 * d], preferred_element_type=F32)
    z2h = cxh[:, :d] * cxh[:, d:]
    prev = jnp.where(s > 0, z2h[SUBLANES - 1:SUBLANES, :], 0.0)
    nxt = jnp.where(s < ns - 1, z2h[SUBLANES:SUBLANES + 1, :], 0.0)
    row = lax.broadcasted_iota(I32, (ts, d), 0)
    zm1 = jnp.where(row == 0, prev, pltpu.roll(z2, 1, 0))
    zp1 = jnp.where(row == ts - 1, nxt, pltpu.roll(z2, ts - 1, 0))
    conv = wconv_ref[0:1, :] * zm1 + wconv_ref[1:2, :] * z2 + wconv_ref[2:3, :] * zp1
    cb = proj(2)
    gb = _sigmoid(proj(6))
    merged = acc_ref[...] + gb * (cb * conv)

    x1 = x + jnp.dot(merged.astype(BF16), wout_ref[...], preferred_element_type=F32)
    x1_ref[0] = x1

    h2 = _rms(x1, gffn_ref[...]).astype(BF16)
    lt_ref[...] = lax.dot_general(wr_ref[...], h2, (((1,), (1,)), ((), ())),
                                  preferred_element_type=F32)
    _rows_to_tiles(h2t_ref, h2.astype(F32), ts)


def _mixer(x, g_mix, w_in_b, w_s_b, bias_full, g_sgu, w_conv, w_out_b, g_ffn, wr_t):
    b, s, d = x.shape
    ts = MIX_ROWS
    ns = s // ts
    t = b * s
    nt = d // LANES
    hb = ts // SUBLANES
    last_hb = s // SUBLANES - 1

    const = lambda *shape: pl.BlockSpec(shape, lambda bi, si: (0,) * len(shape))
    in_specs = [
        pl.BlockSpec((1, ts, d), lambda bi, si: (bi, si, 0)),
        pl.BlockSpec((1, SUBLANES, d), lambda bi, si: (bi, jnp.maximum(si * hb - 1, 0), 0)),
        pl.BlockSpec((1, SUBLANES, d), lambda bi, si: (bi, jnp.minimum((si + 1) * hb, last_hb), 0)),
        const(1, d),
        pl.BlockSpec((d, N_BRANCH * d), lambda bi, si: (0, 0), pipeline_mode=pl.Buffered(1)),
        const(N_GROUPS, CHUNK, CHUNK),
        const(CHUNK, d),
        const(1, d),
        const(CONV_K, d),
        const(d, d),
        const(1, d),
        const(LANES, d),
    ]
    out_specs = [
        pl.BlockSpec((1, ts, d), lambda bi, si: (bi, si, 0)),
        pl.BlockSpec((ts * nt, LANES), lambda bi, si: (bi * ns + si, 0)),
        pl.BlockSpec((LANES, ts), lambda bi, si: (0, bi * ns + si)),
    ]
    out_shape = [
        jax.ShapeDtypeStruct((b, s, d), F32),
        jax.ShapeDtypeStruct((t * nt, LANES), F32),
        jax.ShapeDtypeStruct((LANES, t), F32),
    ]
    return pl.pallas_call(
        _mixer_kernel,
        grid=(b, ns),
        in_specs=in_specs,
        out_specs=out_specs,
        out_shape=out_shape,
        scratch_shapes=[pltpu.VMEM((ts, d), BF16), pltpu.VMEM((ts, d), F32), pltpu.VMEM((ts, d), F32)],
        compiler_params=pltpu.CompilerParams(
            dimension_semantics=("arbitrary", "arbitrary"), vmem_limit_bytes=VMEM_LIMIT),
        name="mixer",
    )(x, x, x, g_mix, w_in_b, w_s_b, bias_full, g_sgu, w_conv, w_out_b, g_ffn, wr_t)


def _route_kernel(lt_ref, info_ref, gt_ref, cnt_ref, carry_ref):
    tb = lt_ref.shape[1]
    ne = N_GROUPS * EXPERTS_PER_GROUP

    @pl.when(pl.program_id(0) == 0)
    def _():
        carry_ref[...] = jnp.zeros_like(carry_ref)

    row8 = lax.broadcasted_iota(I32, (SUBLANES, tb), 0)
    gl = lt_ref[0:N_GROUPS, :]
    gmax = jnp.max(gl, axis=0, keepdims=True)
    gidx = jnp.min(jnp.where(gl == gmax, row8, N_GROUPS), axis=0, keepdims=True)
    pg = 1.0 / jnp.sum(jnp.exp(gl - gmax), axis=0, keepdims=True)

    sel = jnp.zeros((EXPERTS_PER_GROUP, tb), F32)
    for g in range(N_GROUPS):
        lo = N_GROUPS + g * EXPERTS_PER_GROUP
        sel = jnp.where(gidx == g, lt_ref[lo:lo + EXPERTS_PER_GROUP, :], sel)
    m1 = jnp.max(sel, axis=0, keepdims=True)
    i1 = jnp.min(jnp.where(sel == m1, row8, EXPERTS_PER_GROUP), axis=0, keepdims=True)
    sel2 = jnp.where(row8 == i1, -jnp.inf, sel)
    m2 = jnp.max(sel2, axis=0, keepdims=True)
    i2 = jnp.min(jnp.where(sel2 == m2, row8, EXPERTS_PER_GROUP), axis=0, keepdims=True)
    e2 = jnp.exp(m2 - m1)
    den = 1.0 + e2
    gate0 = pg * (1.0 / den)
    gate1 = pg * (e2 / den)
    eid0 = gidx * EXPERTS_PER_GROUP + i1
    eid1 = gidx * EXPERTS_PER_GROUP + i2

    rowe = lax.broadcasted_iota(I32, (ne, tb), 0)
    hit0 = rowe == eid0
    hit1 = rowe == eid1
    onehot = jnp.where(hit0 | hit1, 1.0, 0.0)
    before = (lax.broadcasted_iota(I32, (tb, tb), 0) < lax.broadcasted_iota(I32, (tb, tb), 1))
    prefix = jnp.dot(onehot.astype(BF16), jnp.where(before, 1.0, 0.0).astype(BF16),
                     preferred_element_type=F32)
    base = prefix + carry_ref[:, 0:1]
    rank0 = jnp.sum(jnp.where(hit0, base, 0.0), axis=0, keepdims=True).astype(I32)
    rank1 = jnp.sum(jnp.where(hit1, base, 0.0), axis=0, keepdims=True).astype(I32)
    carry_ref[...] = carry_ref[...] + jnp.sum(onehot, axis=1, keepdims=True)
    cnt_ref[...] = carry_ref[...].astype(I32)

    info_ref[...] = jnp.where(row8 == 0, eid0, jnp.where(row8 == 1, eid1,
                              jnp.where(row8 == 2, rank0, jnp.where(row8 == 3, rank1, 0))))
    rowl = lax.broadcasted_iota(I32, (LANES, tb), 0)
    gates = jnp.where(rowl == 0, gate0, jnp.where(rowl == 1, gate1, 0.0))
    gt_ref[...] = gates.T


def _route(logits_t):
    t = logits_t.shape[1]
    tb = ROUTE_TOKENS
    ne = N_GROUPS * EXPERTS_PER_GROUP
    return pl.pallas_call(
        _route_kernel,
        grid=(t // tb,),
        in_specs=[pl.BlockSpec((LANES, tb), lambda i: (0, i))],
        out_specs=[
            pl.BlockSpec((SUBLANES, tb), lambda i: (0, i)),
            pl.BlockSpec((tb, LANES), lambda i: (i, 0)),
            pl.BlockSpec((ne, LANES), lambda i: (0, 0)),
        ],
        out_shape=[
            jax.ShapeDtypeStruct((SUBLANES, t), I32),
            jax.ShapeDtypeStruct((t, LANES), F32),
            jax.ShapeDtypeStruct((ne, LANES), I32),
        ],
        scratch_shapes=[pltpu.VMEM((ne, LANES), F32)],
        compiler_params=pltpu.CompilerParams(dimension_semantics=("arbitrary",)),
        name="route",
    )(logits_t)


def _place_kernel(info_ref, ps_ref, dest_ref):
    tb = info_ref.shape[1]
    ne = ps_ref.shape[0]
    rowe = lax.broadcasted_iota(I32, (ne, tb), 0)
    row8 = lax.broadcasted_iota(I32, (SUBLANES, tb), 0)
    ps = ps_ref[:, 0:1]
    dest = jnp.zeros((SUBLANES, tb), I32)
    for k in range(TOP_K):
        start = jnp.sum(jnp.where(rowe == info_ref[k:k + 1, :], ps, 0), axis=0, keepdims=True)
        dest = jnp.where(row8 == k, start + info_ref[TOP_K + k:TOP_K + k + 1, :], dest)
    dest_ref[...] = dest


def _place(info, pstart):
    t = info.shape[1]
    tb = PLACE_TOKENS
    ne = pstart.shape[0]
    return pl.pallas_call(
        _place_kernel,
        grid=(t // tb,),
        in_specs=[pl.BlockSpec((SUBLANES, tb), lambda i: (0, i)),
                  pl.BlockSpec((ne, LANES), lambda i: (0, 0))],
        out_specs=pl.BlockSpec((SUBLANES, tb), lambda i: (0, i)),
        out_shape=jax.ShapeDtypeStruct((SUBLANES, t), I32),
        compiler_params=pltpu.CompilerParams(dimension_semantics=("arbitrary",)),
        name="place",
    )(info, jnp.broadcast_to(pstart[:, None], (ne, LANES)))


def _row_copy(src_ref, src_row, dst_ref, dst_row, nt, sem):
    return pltpu.make_async_copy(
        src_ref.at[pl.ds(pl.multiple_of(src_row * nt, nt), nt), :],
        dst_ref.at[pl.ds(pl.multiple_of(dst_row * nt, nt), nt), :],
        sem)


def _rows_wait(ref, n_rows, nt, sem):
    pltpu.make_async_copy(ref.at[pl.ds(0, n_rows * nt), :], ref.at[pl.ds(0, n_rows * nt), :], sem).wait()


def _for_each_assignment(dest_ref, n_tok, start_copy):
    def group(g, c):
        t0 = g * MOVE_UNROLL
        rows = [[dest_ref[k * n_tok + t0 + u] for k in range(TOP_K)] for u in range(MOVE_UNROLL)]
        for u in range(MOVE_UNROLL):
            for k in range(TOP_K):
                start_copy(k, t0 + u, rows[u][k])
        return c

    lax.fori_loop(0, n_tok // MOVE_UNROLL, group, 0)


def _dispatch_kernel(nt, dest_ref, h2_ref, xs_in_ref, xs_ref, sem):
    del xs_in_ref
    td = h2_ref.shape[0] // nt
    _for_each_assignment(
        dest_ref, td,
        lambda k, t, row: _row_copy(h2_ref, t, xs_ref, row, nt, sem).start(priority=k))
    _rows_wait(xs_ref, TOP_K * td, nt, sem)


def _dispatch(dest, h2t, xs_zero):
    td = DISPATCH_TOKENS
    t = dest.shape[0] // TOP_K
    nt = h2t.shape[0] // t
    return pl.pallas_call(
        functools.partial(_dispatch_kernel, nt),
        grid=(t // td,),
        in_specs=[
            pl.BlockSpec((TOP_K * td,), lambda i: (i,), memory_space=pltpu.SMEM),
            pl.BlockSpec((td * nt, LANES), lambda i: (i, 0)),
            pl.BlockSpec(memory_space=pl.ANY),
        ],
        out_specs=pl.BlockSpec(memory_space=pl.ANY),
        out_shape=jax.ShapeDtypeStruct(xs_zero.shape, xs_zero.dtype),
        scratch_shapes=[pltpu.SemaphoreType.DMA],
        input_output_aliases={2: 0},
        compiler_params=pltpu.CompilerParams(dimension_semantics=("arbitrary",)),
        name="dispatch",
    )(dest, h2t, xs_zero)


def _experts_kernel(nb, be_ref, nx_ref, nu_ref, xs_hbm, wg_hbm, wu_hbm, wd_hbm, y_hbm,
                    xbuf, ybuf, wg_st, wu_st, wd_st, wgu_b, wd_b, slot_ref, wsems, isems, osems):
    ring = xbuf.shape[0]
    rows = xbuf.shape[1]
    d, de = wg_st.shape[1], wg_st.shape[2]
    nt = d // LANES
    bs = rows // nt
    nu = nu_ref[0]

    def block(ref, q):
        return ref.at[pl.ds(pl.multiple_of(q * rows, rows), rows), :]

    def in_copy(q):
        return pltpu.make_async_copy(block(xs_hbm, q), xbuf.at[q % ring], isems.at[q % ring])

    def out_copy(q):
        return pltpu.make_async_copy(ybuf.at[q % ring], block(y_hbm, q), osems.at[q % ring])

    def weight_copies(ex, slot):
        return (pltpu.make_async_copy(wg_hbm.at[ex], wg_st.at[slot], wsems.at[slot]),
                pltpu.make_async_copy(wu_hbm.at[ex], wu_st.at[slot], wsems.at[slot]),
                pltpu.make_async_copy(wd_hbm.at[ex], wd_st.at[slot], wsems.at[slot]))

    slot_ref[0] = 0
    for c in weight_copies(be_ref[0], 0):
        c.start(priority=WEIGHT_DMA_PRIORITY)
    for q in range(EXPERT_LOOKAHEAD):
        @pl.when(q < nu)
        def _():
            in_copy(q).start()

    def step(j, carry):
        e = be_ref[j]

        @pl.when((j == 0) | (e != be_ref[jnp.maximum(j - 1, 0)]))
        def _():
            slot = slot_ref[0]
            for c in weight_copies(e, slot):
                c.wait()
            nx = nx_ref[j]

            @pl.when(nx >= 0)
            def _():
                for c in weight_copies(nx, 1 - slot):
                    c.start(priority=WEIGHT_DMA_PRIORITY)

            wgu_b[:, :de] = wg_st[slot].astype(BF16)
            wgu_b[:, de:] = wu_st[slot].astype(BF16)
            wd_b[...] = wd_st[slot].astype(BF16)
            slot_ref[0] = 1 - slot

        @pl.when(j + EXPERT_LOOKAHEAD < nu)
        def _():
            in_copy(j + EXPERT_LOOKAHEAD).start()

        in_copy(j).wait()
        xb = _tiles_to_rows(xbuf.at[j % ring], bs, nt).astype(BF16)
        gu = jnp.dot(xb, wgu_b[...], preferred_element_type=F32)
        gate = gu[:, :de]
        hid = (gate * _sigmoid(gate)) * gu[:, de:]
        y = jnp.dot(hid.astype(BF16), wd_b[...], preferred_element_type=F32)

        @pl.when(j >= ring)
        def _():
            out_copy(j - ring).wait()

        _rows_to_tiles(ybuf.at[j % ring], y, bs)
        out_copy(j).start()
        return carry

    lax.fori_loop(0, nu, step, 0)

    for q in range(1, ring + 1):
        @pl.when(nu - q >= 0)
        def _():
            out_copy(nu - q).wait()

    ybuf[0] = jnp.zeros(ybuf.shape[1:], ybuf.dtype)

    def zero_start(q, carry):
        pltpu.make_async_copy(ybuf.at[0], block(y_hbm, q), osems.at[0]).start()
        return carry

    def zero_wait(q, carry):
        pltpu.make_async_copy(ybuf.at[0], block(y_hbm, q), osems.at[0]).wait()
        return carry

    lax.fori_loop(nu, nb, zero_start, 0)
    lax.fori_loop(nu, nb, zero_wait, 0)


def _experts(blk_e, nxt_e, n_used, xs, w_gate, w_up, w_down):
    ne, d, de = w_gate.shape
    nt = d // LANES
    rows = EXPERT_ROWS * nt
    nb = xs.shape[0] // rows
    ring = EXPERT_LOOKAHEAD + 1
    any_spec = pl.BlockSpec(memory_space=pl.ANY)
    grid_spec = pltpu.PrefetchScalarGridSpec(
        num_scalar_prefetch=3,
        grid=(1,),
        in_specs=[any_spec, any_spec, any_spec, any_spec],
        out_specs=any_spec,
        scratch_shapes=[pltpu.VMEM((ring, rows, LANES), F32), pltpu.VMEM((ring, rows, LANES), F32),
                        pltpu.VMEM((2, d, de), F32), pltpu.VMEM((2, d, de), F32),
                        pltpu.VMEM((2, de, d), F32),
                        pltpu.VMEM((d, 2 * de), BF16), pltpu.VMEM((de, d), BF16),
                        pltpu.SMEM((1,), I32), pltpu.SemaphoreType.DMA((2,)),
                        pltpu.SemaphoreType.DMA((ring,)), pltpu.SemaphoreType.DMA((ring,))],
    )
    return pl.pallas_call(
        functools.partial(_experts_kernel, nb),
        grid_spec=grid_spec,
        out_shape=jax.ShapeDtypeStruct(xs.shape, F32),
        compiler_params=pltpu.CompilerParams(
            dimension_semantics=("arbitrary",), vmem_limit_bytes=VMEM_LIMIT),
        name="experts",
    )(blk_e, nxt_e, n_used, xs, w_gate, w_up, w_down)


def _combine_kernel(dcur_ref, dnxt_ref, x1_ref, gt_ref, y_ref, gfin_ref, o_ref, ybuf, sems):
    tc, d = x1_ref.shape
    nt = d // LANES
    i = pl.program_id(0)
    n = pl.num_programs(0)

    def issue_tile(dest_ref, slot):
        _for_each_assignment(
            dest_ref, tc,
            lambda k, t, row: _row_copy(y_ref, row, ybuf.at[slot, k], t, nt,
                                        sems.at[slot]).start(priority=k))

    @pl.when(i == 0)
    def _():
        issue_tile(dcur_ref, 0)

    @pl.when(i + 1 < n)
    def _():
        issue_tile(dnxt_ref, (i + 1) % 2)

    slot = i % 2
    _rows_wait(y_ref, TOP_K * tc, nt, sems.at[slot])
    y0 = _tiles_to_rows(ybuf.at[slot, 0], tc, nt)
    y1 = _tiles_to_rows(ybuf.at[slot, 1], tc, nt)
    gates = gt_ref[...]
    xo = x1_ref[...] + (gates[:, 0:1] * y0 + gates[:, 1:2] * y1)
    o_ref[...] = _rms(xo, gfin_ref[...])


def _combine(dest, x1, gates_t, y, g_final):
    t, d = x1.shape
    nt = d // LANES
    tc = COMBINE_TOKENS
    last = t // tc - 1
    return pl.pallas_call(
        _combine_kernel,
        grid=(t // tc,),
        in_specs=[
            pl.BlockSpec((TOP_K * tc,), lambda i: (i,), memory_space=pltpu.SMEM),
            pl.BlockSpec((TOP_K * tc,), lambda i: (jnp.minimum(i + 1, last),),
                         memory_space=pltpu.SMEM),
            pl.BlockSpec((tc, d), lambda i: (i, 0)),
            pl.BlockSpec((tc, LANES), lambda i: (i, 0)),
            pl.BlockSpec(memory_space=pl.ANY),
            pl.BlockSpec((1, d), lambda i: (0, 0)),
        ],
        out_specs=pl.BlockSpec((tc, d), lambda i: (i, 0)),
        out_shape=jax.ShapeDtypeStruct((t, d), F32),
        scratch_shapes=[pltpu.VMEM((2, TOP_K, tc * nt, LANES), F32), pltpu.SemaphoreType.DMA((2,))],
        compiler_params=pltpu.CompilerParams(dimension_semantics=("arbitrary",)),
        name="combine",
    )(dest, dest, x1, gates_t, y, g_final)


def _layer(x, g_mix, w_in, w_s, b_s, g_sgu, w_conv, w_out, g_ffn, w_rg, w_re, w_gate, w_up, w_down):
    b, s, d = x.shape
    t = b * s
    nt = d // LANES
    ne = w_gate.shape[0]
    bs = EXPERT_ROWS

    bias_full = jnp.repeat(b_s.T, d // N_GROUPS, axis=1)
    wr_t = jnp.concatenate([w_rg, w_re], axis=1).T
    wr_t = jnp.pad(wr_t, ((0, LANES - wr_t.shape[0]), (0, 0))).astype(BF16)
    x1, h2t, logits_t = _mixer(
        x, g_mix.reshape(1, d), w_in.astype(BF16), w_s.astype(BF16), bias_full, g_sgu.reshape(1, d),
        w_conv, w_out.astype(BF16), g_ffn.reshape(1, d), wr_t)

    info, gates_t, counts = _route(logits_t)

    counts = counts[:, 0]
    padded = (counts + bs - 1) // bs * bs
    pend = jnp.cumsum(padded)
    pstart = (pend - padded).astype(I32)
    n_rows = -(-(t * TOP_K + ne * (bs - 1)) // bs) * bs
    nb = n_rows // bs
    first_row = jnp.arange(nb, dtype=I32) * bs
    blk_e = jnp.minimum(jnp.sum(pend[None, :] <= first_row[:, None], axis=1), ne - 1).astype(I32)
    n_used = (pend[-1:] // bs).astype(I32)
    end_blk = pend[blk_e] // bs
    nxt_e = jnp.where(end_blk < n_used[0], blk_e[jnp.minimum(end_blk, nb - 1)], -1).astype(I32)

    dest = _place(info, pstart)[:TOP_K]
    tiled = lambda n: dest.reshape(TOP_K, t // n, n).transpose(1, 0, 2).reshape(-1)
    xs = _dispatch(tiled(DISPATCH_TOKENS), h2t, jnp.zeros((n_rows * nt, LANES), F32))
    y = _experts(blk_e, nxt_e, n_used, xs, w_gate, w_up, w_down)
    return x1.reshape(t, d), tiled(COMBINE_TOKENS), gates_t, y


def kernel(x, g_mix, w_in, w_s, b_s, g_sgu, w_conv, w_out, g_ffn, w_router_group, w_router_expert,
           w_gate, w_up, w_down, g_final):
    b, s, d = x.shape
    depth = g_mix.shape[0]
    assert depth == 1, "the final RMSNorm is fused into the last layer's combine"
    assert s % MIX_ROWS == 0 and MIX_ROWS % CHUNK == 0 and d % LANES == 0
    assert all((b * s) % n == 0 for n in (ROUTE_TOKENS, PLACE_TOKENS, DISPATCH_TOKENS, COMBINE_TOKENS))
    l = 0
    x1, dest, gates_t, y = _layer(
        x, g_mix[l], w_in[l], w_s[l], b_s[l], g_sgu[l], w_conv[l], w_out[l], g_ffn[l],
        w_router_group[l], w_router_expert[l], w_gate[l], w_up[l], w_down[l])
    out = _combine(dest, x1, gates_t, y, g_final.reshape(1, d))
    return out.reshape(b, s, d)
```

```python
import functools

import jax
import jax.numpy as jnp
from jax import lax
from jax.experimental import pallas as pl
from jax.experimental.pallas import tpu as pltpu

F32 = jnp.float32
BF16 = jnp.bfloat16
I32 = jnp.int32

EPS = 1e-6
LANES = 128
SUBLANES = 8
CHUNK = 128
N_GROUPS = 8
EXPERTS_PER_GROUP = 8
TOP_K = 2
CONV_K = 3
N_BRANCH = 7

MIX_ROWS = 512
ROUTE_TOKENS = 512
PLACE_TOKENS = 2048
DISPATCH_TOKENS = 1024
COMBINE_TOKENS = 256
MOVE_UNROLL = 8
WEIGHT_DMA_PRIORITY = 1
EXPERT_ROWS = 128
EXPERT_LOOKAHEAD = 3
VMEM_LIMIT = 56 * 1024 * 1024


def _rms(x, g):
    return x * lax.rsqrt(jnp.mean(x * x, axis=-1, keepdims=True) + EPS) * g


def _sigmoid(x):
    return 0.5 * (1.0 + jnp.tanh(0.5 * x))


def _gelu_tanh(x):
    c = 0.7978845608028654
    return x * (0.5 * (1.0 + jnp.tanh(c * (x + 0.044715 * (x * x * x)))))


def _rows_to_tiles(dst_ref, val, rows):
    nt = val.shape[1] // LANES
    for c in range(nt):
        dst_ref[pl.ds(c, rows, stride=nt), :] = val[:, c * LANES:(c + 1) * LANES]


def _tiles_to_rows(src_ref, rows, nt):
    return jnp.concatenate([src_ref[pl.ds(c, rows, stride=nt), :] for c in range(nt)], axis=1)


def _mixer_kernel(x_ref, xp_ref, xn_ref, gmix_ref, win_ref, ws_ref, bias_ref, gsgu_ref, wconv_ref,
                  wout_ref, gffn_ref, wr_ref, x1_ref, h2t_ref, lt_ref, vg_ref, z_ref, acc_ref):
    ts, d = x_ref.shape[1], x_ref.shape[2]
    s = pl.program_id(1)
    ns = pl.num_programs(1)
    gw = d // N_GROUPS
    nc = ts // CHUNK

    x = x_ref[0]
    gmix = gmix_ref[...]
    h = _rms(x, gmix).astype(BF16)

    def proj(j, n=1):
        return jnp.dot(h, win_ref[:, j * d:(j + n) * d], preferred_element_type=F32)

    gv = _gelu_tanh(proj(1))
    for g in range(N_GROUPS):
        cs = slice(g * gw, (g + 1) * gw)
        blk = gv[:, cs]
        mu = jnp.mean(blk, axis=-1, keepdims=True)
        dv = blk - mu
        var = jnp.mean(dv * dv, axis=-1, keepdims=True)
        vg_ref[:, cs] = (dv * lax.rsqrt(var + EPS) * gsgu_ref[:, cs]).astype(BF16)
    for g in range(N_GROUPS):
        cs = slice(g * gw, (g + 1) * gw)
        vcat = jnp.concatenate([vg_ref[n * CHUNK:(n + 1) * CHUNK, cs] for n in range(nc)], axis=1)
        zg = jnp.dot(ws_ref[g], vcat, preferred_element_type=F32)
        for n in range(nc):
            z_ref[n * CHUNK:(n + 1) * CHUNK, cs] = zg[:, n * gw:(n + 1) * gw]
    u = _gelu_tanh(proj(0))
    ga = _sigmoid(proj(5))
    for n in range(nc):
        rs = slice(n * CHUNK, (n + 1) * CHUNK)
        acc_ref[rs, :] = ga[rs] * (u[rs] * (z_ref[rs, :] + bias_ref[...]))

    cx = proj(3, 2)
    z2 = cx[:, :d] * cx[:, d:]
    xh = jnp.concatenate([xp_ref[0], xn_ref[0]], axis=0)
    hh = _rms(xh, gmix).astype(BF16)
    cxh = jnp.dot(hh, win_ref[:, 3 * d:5 * d], preferred_element_type=F32)
    z2h = cxh[:, :d] * cxh[:, d:]
    prev = jnp.where(s > 0, z2h[SUBLANES - 1:SUBLANES, :], 0.0)
    nxt = jnp.where(s < ns - 1, z2h[SUBLANES:SUBLANES + 1, :], 0.0)
    row = lax.broadcasted_iota(I32, (ts, d), 0)
    zm1 = jnp.where(row == 0, prev, pltpu.roll(z2, 1, 0))
    zp1 = jnp.where(row == ts - 1, nxt, pltpu.roll(z2, ts - 1, 0))
    conv = wconv_ref[0:1, :] * zm1 + wconv_ref[1:2, :] * z2 + wconv_ref[2:3, :] * zp1
    cb = proj(2)
    gb = _sigmoid(proj(6))
    merged = acc_ref[...] + gb * (cb * conv)

    x1 = x + jnp.dot(merged.astype(BF16), wout_ref[...], preferred_element_type=F32)
    x1_ref[0] = x1

    h2 = _rms(x1, gffn_ref[...]).astype(BF16)
    lt_ref[...] = lax.dot_general(wr_ref[...], h2, (((1,), (1,)), ((), ())),
                                  preferred_element_type=F32)
    _rows_to_tiles(h2t_ref, h2.astype(F32), ts)


def _mixer(x, g_mix, w_in_b, w_s_b, bias_full, g_sgu, w_conv, w_out_b, g_ffn, wr_t):
    b, s, d = x.shape
    ts = MIX_ROWS
    ns = s // ts
    t = b * s
    nt = d // LANES
    hb = ts // SUBLANES
    last_hb = s // SUBLANES - 1

    const = lambda *shape: pl.BlockSpec(shape, lambda bi, si: (0,) * len(shape))
    in_specs = [
        pl.BlockSpec((1, ts, d), lambda bi, si: (bi, si, 0)),
        pl.BlockSpec((1, SUBLANES, d), lambda bi, si: (bi, jnp.maximum(si * hb - 1, 0), 0)),
        pl.BlockSpec((1, SUBLANES, d), lambda bi, si: (bi, jnp.minimum((si + 1) * hb, last_hb), 0)),
        const(1, d),
        pl.BlockSpec((d, N_BRANCH * d), lambda bi, si: (0, 0), pipeline_mode=pl.Buffered(1)),
        const(N_GROUPS, CHUNK, CHUNK),
        const(CHUNK, d),
        const(1, d),
        const(CONV_K, d),
        const(d, d),
        const(1, d),
        const(LANES, d),
    ]
    out_specs = [
        pl.BlockSpec((1, ts, d), lambda bi, si: (bi, si, 0)),
        pl.BlockSpec((ts * nt, LANES), lambda bi, si: (bi * ns + si, 0)),
        pl.BlockSpec((LANES, ts), lambda bi, si: (0, bi * ns + si)),
    ]
    out_shape = [
        jax.ShapeDtypeStruct((b, s, d), F32),
        jax.ShapeDtypeStruct((t * nt, LANES), F32),
        jax.ShapeDtypeStruct((LANES, t), F32),
    ]
    return pl.pallas_call(
        _mixer_kernel,
        grid=(b, ns),
        in_specs=in_specs,
        out_specs=out_specs,
        out_shape=out_shape,
        scratch_shapes=[pltpu.VMEM((ts, d), BF16), pltpu.VMEM((ts, d), F32), pltpu.VMEM((ts, d), F32)],
        compiler_params=pltpu.CompilerParams(
            dimension_semantics=("arbitrary", "arbitrary"), vmem_limit_bytes=VMEM_LIMIT),
        name="mixer",
    )(x, x, x, g_mix, w_in_b, w_s_b, bias_full, g_sgu, w_conv, w_out_b, g_ffn, wr_t)


def _route_kernel(lt_ref, info_ref, gt_ref, cnt_ref, carry_ref):
    tb = lt_ref.shape[1]
    ne = N_GROUPS * EXPERTS_PER_GROUP

    @pl.when(pl.program_id(0) == 0)
    def _():
        carry_ref[...] = jnp.zeros_like(carry_ref)

    row8 = lax.broadcasted_iota(I32, (SUBLANES, tb), 0)
    gl = lt_ref[0:N_GROUPS, :]
    gmax = jnp.max(gl, axis=0, keepdims=True)
    gidx = jnp.min(jnp.where(gl == gmax, row8, N_GROUPS), axis=0, keepdims=True)
    pg = 1.0 / jnp.sum(jnp.exp(gl - gmax), axis=0, keepdims=True)

    sel = jnp.zeros((EXPERTS_PER_GROUP, tb), F32)
    for g in range(N_GROUPS):
        lo = N_GROUPS + g * EXPERTS_PER_GROUP
        sel = jnp.where(gidx == g, lt_ref[lo:lo + EXPERTS_PER_GROUP, :], sel)
    m1 = jnp.max(sel, axis=0, keepdims=True)
    i1 = jnp.min(jnp.where(sel == m1, row8, EXPERTS_PER_GROUP), axis=0, keepdims=True)
    sel2 = jnp.where(row8 == i1, -jnp.inf, sel)
    m2 = jnp.max(sel2, axis=0, keepdims=True)
    i2 = jnp.min(jnp.where(sel2 == m2, row8, EXPERTS_PER_GROUP), axis=0, keepdims=True)
    e2 = jnp.exp(m2 - m1)
    den = 1.0 + e2
    gate0 = pg * (1.0 / den)
    gate1 = pg * (e2 / den)
    eid0 = gidx * EXPERTS_PER_GROUP + i1
    eid1 = gidx * EXPERTS_PER_GROUP + i2

    rowe = lax.broadcasted_iota(I32, (ne, tb), 0)
    hit0 = rowe == eid0
    hit1 = rowe == eid1
    onehot = jnp.where(hit0 | hit1, 1.0, 0.0)
    before = (lax.broadcasted_iota(I32, (tb, tb), 0) < lax.broadcasted_iota(I32, (tb, tb), 1))
    prefix = jnp.dot(onehot.astype(BF16), jnp.where(before, 1.0, 0.0).astype(BF16),
                     preferred_element_type=F32)
    base = prefix + carry_ref[:, 0:1]
    rank0 = jnp.sum(jnp.where(hit0, base, 0.0), axis=0, keepdims=True).astype(I32)
    rank1 = jnp.sum(jnp.where(hit1, base, 0.0), axis=0, keepdims=True).astype(I32)
    carry_ref[...] = carry_ref[...] + jnp.sum(onehot, axis=1, keepdims=True)
    cnt_ref[...] = carry_ref[...].astype(I32)

    info_ref[...] = jnp.where(row8 == 0, eid0, jnp.where(row8 == 1, eid1,
                              jnp.where(row8 == 2, rank0, jnp.where(row8 == 3, rank1, 0))))
    rowl = lax.broadcasted_iota(I32, (LANES, tb), 0)
    gates = jnp.where(rowl == 0, gate0, jnp.where(rowl == 1, gate1, 0.0))
    gt_ref[...] = gates.T


def _route(logits_t):
    t = logits_t.shape[1]
    tb = ROUTE_TOKENS
    ne = N_GROUPS * EXPERTS_PER_GROUP
    return pl.pallas_call(
        _route_kernel,
        grid=(t // tb,),
        in_specs=[pl.BlockSpec((LANES, tb), lambda i: (0, i))],
        out_specs=[
            pl.BlockSpec((SUBLANES, tb), lambda i: (0, i)),
            pl.BlockSpec((tb, LANES), lambda i: (i, 0)),
            pl.BlockSpec((ne, LANES), lambda i: (0, 0)),
        ],
        out_shape=[
            jax.ShapeDtypeStruct((SUBLANES, t), I32),
            jax.ShapeDtypeStruct((t, LANES), F32),
            jax.ShapeDtypeStruct((ne, LANES), I32),
        ],
        scratch_shapes=[pltpu.VMEM((ne, LANES), F32)],
        compiler_params=pltpu.CompilerParams(dimension_semantics=("arbitrary",)),
        name="route",
    )(logits_t)


def _plan_kernel(cnt_ref, exp_ref, blk_ref):
    ne = cnt_ref.shape[0]
    nbp = blk_ref.shape[1]
    shift = EXPERT_ROWS.bit_length() - 1
    pad_rows = lambda c: lax.shift_left(lax.shift_right_logical(c + (EXPERT_ROWS - 1), shift), shift)
    cnt = cnt_ref[...]
    padded = pad_rows(cnt)
    padded_lanes = pad_rows(cnt.astype(F32).T[:ne, :ne].astype(I32))
    e_sub = lax.broadcasted_iota(I32, (ne, ne), 0)
    e_lane = lax.broadcasted_iota(I32, (ne, ne), 1)
    pend = jnp.sum(jnp.where(e_lane <= e_sub, padded_lanes, 0), axis=1, keepdims=True)
    pstart = pend - padded[:, 0:1]

    lane = lax.broadcasted_iota(I32, (ne, LANES), 1)
    exp_ref[...] = jnp.where(lane == 0, pstart, jnp.where(lane == 1, cnt, jnp.where(lane == 2, padded, 0)))

    first_row = lax.broadcasted_iota(I32, (ne, nbp), 1) * EXPERT_ROWS
    e_col = lax.broadcasted_iota(I32, (ne, nbp), 0)
    blk_e = jnp.minimum(jnp.sum(jnp.where(pend <= first_row, 1, 0), axis=0, keepdims=True), ne - 1)
    later = jnp.where((e_col > blk_e) & (padded[:, 0:1] > 0), e_col, ne)
    nxt_e = jnp.min(later, axis=0, keepdims=True)
    nxt_e = jnp.where(nxt_e == ne, -1, nxt_e)
    n_used = lax.shift_right_logical(jnp.max(pend, axis=0, keepdims=True), shift)
    row8 = lax.broadcasted_iota(I32, (SUBLANES, nbp), 0)
    blk_ref[...] = jnp.where(row8 == 0, blk_e, jnp.where(row8 == 1, nxt_e, jnp.where(row8 == 2, n_used, 0)))


def _plan(counts, nb):
    ne = counts.shape[0]
    nbp = -(-nb // LANES) * LANES
    return pl.pallas_call(
        _plan_kernel,
        out_shape=[jax.ShapeDtypeStruct((ne, LANES), I32), jax.ShapeDtypeStruct((SUBLANES, nbp), I32)],
        name="plan",
    )(counts)


def _place_kernel(info_ref, ps_ref, dest_ref):
    tb = info_ref.shape[1]
    ne = ps_ref.shape[0]
    rowe = lax.broadcasted_iota(I32, (ne, tb), 0)
    row8 = lax.broadcasted_iota(I32, (SUBLANES, tb), 0)
    ps = ps_ref[:, 0:1]
    dest = jnp.zeros((SUBLANES, tb), I32)
    for k in range(TOP_K):
        start = jnp.sum(jnp.where(rowe == info_ref[k:k + 1, :], ps, 0), axis=0, keepdims=True)
        dest = jnp.where(row8 == k, start + info_ref[TOP_K + k:TOP_K + k + 1, :], dest)
    dest_ref[...] = dest


def _place(info, exp_tbl):
    t = info.shape[1]
    tb = PLACE_TOKENS
    ne = exp_tbl.shape[0]
    return pl.pallas_call(
        _place_kernel,
        grid=(t // tb,),
        in_specs=[pl.BlockSpec((SUBLANES, tb), lambda i: (0, i)),
                  pl.BlockSpec((ne, LANES), lambda i: (0, 0))],
        out_specs=pl.BlockSpec((SUBLANES, tb), lambda i: (0, i)),
        out_shape=jax.ShapeDtypeStruct((SUBLANES, t), I32),
        compiler_params=pltpu.CompilerParams(dimension_semantics=("arbitrary",)),
        name="place",
    )(info, exp_tbl)


def _row_copy(src_ref, src_row, dst_ref, dst_row, nt, sem):
    return pltpu.make_async_copy(
        src_ref.at[pl.ds(pl.multiple_of(src_row * nt, nt), nt), :],
        dst_ref.at[pl.ds(pl.multiple_of(dst_row * nt, nt), nt), :],
        sem)


def _rows_wait(ref, n_rows, nt, sem):
    pltpu.make_async_copy(ref.at[pl.ds(0, n_rows * nt), :], ref.at[pl.ds(0, n_rows * nt), :], sem).wait()


def _for_each_assignment(dest_ref, n_tok, start_copy):
    def group(g, c):
        t0 = g * MOVE_UNROLL
        rows = [[dest_ref[k * n_tok + t0 + u] for k in range(TOP_K)] for u in range(MOVE_UNROLL)]
        for u in range(MOVE_UNROLL):
            for k in range(TOP_K):
                start_copy(k, t0 + u, rows[u][k])
        return c

    lax.fori_loop(0, n_tok // MOVE_UNROLL, group, 0)


def _dispatch_kernel(nt, nb, plan_ref, dest_ref, h2_ref, xs_ref, zbuf, sem, zsem):
    td = h2_ref.shape[0] // nt
    ne = (plan_ref.shape[0] - 1) // 3
    n_used = plan_ref[3 * ne]
    blk_rows = zbuf.shape[0]

    def zero_rows(start_not_wait):
        def fire(c):
            c.start() if start_not_wait else c.wait()

        def expert(e, carry):
            cnt = plan_ref[3 * e + 1]
            row = plan_ref[3 * e] + cnt
            n = plan_ref[3 * e + 2] - cnt
            p = EXPERT_ROWS // 2
            while p >= 1:
                has = (n & p) != 0
                r, sz = row, p

                @pl.when(has)
                def _():
                    fire(pltpu.make_async_copy(
                        zbuf.at[pl.ds(0, sz * nt), :],
                        xs_ref.at[pl.ds(pl.multiple_of(r * nt, nt), sz * nt), :], zsem))

                row = row + jnp.where(has, p, 0)
                p //= 2
            return carry

        lax.fori_loop(0, ne, expert, 0)

        def tail(q, carry):
            fire(pltpu.make_async_copy(
                zbuf, xs_ref.at[pl.ds(pl.multiple_of(q * blk_rows, blk_rows), blk_rows), :], zsem))
            return carry

        lax.fori_loop(n_used, nb, tail, 0)

    @pl.when(pl.program_id(0) == 0)
    def _():
        zbuf[...] = jnp.zeros_like(zbuf)
        zero_rows(True)

    _for_each_assignment(
        dest_ref, td,
        lambda k, t, row: _row_copy(h2_ref, t, xs_ref, row, nt, sem).start(priority=k))
    _rows_wait(xs_ref, TOP_K * td, nt, sem)

    @pl.when(pl.program_id(0) == 0)
    def _():
        zero_rows(False)


def _dispatch(plan, dest, h2t, n_rows):
    td = DISPATCH_TOKENS
    t = dest.shape[0] // TOP_K
    nt = h2t.shape[0] // t
    nb = n_rows // EXPERT_ROWS
    grid_spec = pltpu.PrefetchScalarGridSpec(
        num_scalar_prefetch=1,
        grid=(t // td,),
        in_specs=[
            pl.BlockSpec((TOP_K * td,), lambda i, plan: (i,), memory_space=pltpu.SMEM),
            pl.BlockSpec((td * nt, LANES), lambda i, plan: (i, 0)),
        ],
        out_specs=pl.BlockSpec(memory_space=pl.ANY),
        scratch_shapes=[pltpu.VMEM((EXPERT_ROWS * nt, LANES), h2t.dtype),
                        pltpu.SemaphoreType.DMA, pltpu.SemaphoreType.DMA],
    )
    return pl.pallas_call(
        functools.partial(_dispatch_kernel, nt, nb),
        grid_spec=grid_spec,
        out_shape=jax.ShapeDtypeStruct((n_rows * nt, LANES), h2t.dtype),
        compiler_params=pltpu.CompilerParams(dimension_semantics=("arbitrary",)),
        name="dispatch",
    )(plan, dest, h2t)


def _experts_kernel(nb, be_ref, nx_ref, nu_ref, xs_hbm, wg_hbm, wu_hbm, wd_hbm, y_hbm,
                    xbuf, ybuf, wg_st, wu_st, wd_st, wgu_b, wd_b, slot_ref, wsems, isems, osems):
    ring = xbuf.shape[0]
    rows = xbuf.shape[1]
    d, de = wg_st.shape[1], wg_st.shape[2]
    nt = d // LANES
    bs = rows // nt
    nu = nu_ref[0]

    def block(ref, q):
        return ref.at[pl.ds(pl.multiple_of(q * rows, rows), rows), :]

    def in_copy(q):
        return pltpu.make_async_copy(block(xs_hbm, q), xbuf.at[q % ring], isems.at[q % ring])

    def out_copy(q):
        return pltpu.make_async_copy(ybuf.at[q % ring], block(y_hbm, q), osems.at[q % ring])

    def weight_copies(ex, slot):
        return (pltpu.make_async_copy(wg_hbm.at[ex], wg_st.at[slot], wsems.at[slot]),
                pltpu.make_async_copy(wu_hbm.at[ex], wu_st.at[slot], wsems.at[slot]),
                pltpu.make_async_copy(wd_hbm.at[ex], wd_st.at[slot], wsems.at[slot]))

    slot_ref[0] = 0
    for c in weight_copies(be_ref[0], 0):
        c.start(priority=WEIGHT_DMA_PRIORITY)
    for q in range(EXPERT_LOOKAHEAD):
        @pl.when(q < nu)
        def _():
            in_copy(q).start()

    def step(j, carry):
        e = be_ref[j]

        @pl.when((j == 0) | (e != be_ref[jnp.maximum(j - 1, 0)]))
        def _():
            slot = slot_ref[0]
            for c in weight_copies(e, slot):
                c.wait()
            nx = nx_ref[j]

            @pl.when(nx >= 0)
            def _():
                for c in weight_copies(nx, 1 - slot):
                    c.start(priority=WEIGHT_DMA_PRIORITY)

            wgu_b[:, :de] = wg_st[slot].astype(BF16)
            wgu_b[:, de:] = wu_st[slot].astype(BF16)
            wd_b[...] = wd_st[slot].astype(BF16)
            slot_ref[0] = 1 - slot

        @pl.when(j + EXPERT_LOOKAHEAD < nu)
        def _():
            in_copy(j + EXPERT_LOOKAHEAD).start()

        in_copy(j).wait()
        xb = _tiles_to_rows(xbuf.at[j % ring], bs, nt).astype(BF16)
        gu = jnp.dot(xb, wgu_b[...], preferred_element_type=F32)
        gate = gu[:, :de]
        hid = (gate * _sigmoid(gate)) * gu[:, de:]
        y = jnp.dot(hid.astype(BF16), wd_b[...], preferred_element_type=F32)

        @pl.when(j >= ring)
        def _():
            out_copy(j - ring).wait()

        _rows_to_tiles(ybuf.at[j % ring], y, bs)
        out_copy(j).start()
        return carry

    lax.fori_loop(0, nu, step, 0)

    for q in range(1, ring + 1):
        @pl.when(nu - q >= 0)
        def _():
            out_copy(nu - q).wait()

    ybuf[0] = jnp.zeros(ybuf.shape[1:], ybuf.dtype)

    def zero_start(q, carry):
        pltpu.make_async_copy(ybuf.at[0], block(y_hbm, q), osems.at[0]).start()
        return carry

    def zero_wait(q, carry):
        pltpu.make_async_copy(ybuf.at[0], block(y_hbm, q), osems.at[0]).wait()
        return carry

    lax.fori_loop(nu, nb, zero_start, 0)
    lax.fori_loop(nu, nb, zero_wait, 0)


def _experts(blk_e, nxt_e, n_used, xs, w_gate, w_up, w_down):
    ne, d, de = w_gate.shape
    nt = d // LANES
    rows = EXPERT_ROWS * nt
    nb = xs.shape[0] // rows
    ring = EXPERT_LOOKAHEAD + 1
    any_spec = pl.BlockSpec(memory_space=pl.ANY)
    grid_spec = pltpu.PrefetchScalarGridSpec(
        num_scalar_prefetch=3,
        grid=(1,),
        in_specs=[any_spec, any_spec, any_spec, any_spec],
        out_specs=any_spec,
        scratch_shapes=[pltpu.VMEM((ring, rows, LANES), F32), pltpu.VMEM((ring, rows, LANES), F32),
                        pltpu.VMEM((2, d, de), F32), pltpu.VMEM((2, d, de), F32),
                        pltpu.VMEM((2, de, d), F32),
                        pltpu.VMEM((d, 2 * de), BF16), pltpu.VMEM((de, d), BF16),
                        pltpu.SMEM((1,), I32), pltpu.SemaphoreType.DMA((2,)),
                        pltpu.SemaphoreType.DMA((ring,)), pltpu.SemaphoreType.DMA((ring,))],
    )
    return pl.pallas_call(
        functools.partial(_experts_kernel, nb),
        grid_spec=grid_spec,
        out_shape=jax.ShapeDtypeStruct(xs.shape, F32),
        compiler_params=pltpu.CompilerParams(
            dimension_semantics=("arbitrary",), vmem_limit_bytes=VMEM_LIMIT),
        name="experts",
    )(blk_e, nxt_e, n_used, xs, w_gate, w_up, w_down)


def _combine_kernel(dcur_ref, dnxt_ref, x1_ref, gt_ref, y_ref, gfin_ref, o_ref, ybuf, sems):
    tc, d = x1_ref.shape
    nt = d // LANES
    i = pl.program_id(0)
    n = pl.num_programs(0)

    def issue_tile(dest_ref, slot):
        _for_each_assignment(
            dest_ref, tc,
            lambda k, t, row: _row_copy(y_ref, row, ybuf.at[slot, k], t, nt,
                                        sems.at[slot]).start(priority=k))

    @pl.when(i == 0)
    def _():
        issue_tile(dcur_ref, 0)

    @pl.when(i + 1 < n)
    def _():
        issue_tile(dnxt_ref, (i + 1) % 2)

    slot = i % 2
    _rows_wait(y_ref, TOP_K * tc, nt, sems.at[slot])
    y0 = _tiles_to_rows(ybuf.at[slot, 0], tc, nt)
    y1 = _tiles_to_rows(ybuf.at[slot, 1], tc, nt)
    gates = gt_ref[...]
    xo = x1_ref[...] + (gates[:, 0:1] * y0 + gates[:, 1:2] * y1)
    o_ref[...] = _rms(xo, gfin_ref[...])


def _combine(dest, x1, gates_t, y, g_final):
    t, d = x1.shape
    nt = d // LANES
    tc = COMBINE_TOKENS
    last = t // tc - 1
    return pl.pallas_call(
        _combine_kernel,
        grid=(t // tc,),
        in_specs=[
            pl.BlockSpec((TOP_K * tc,), lambda i: (i,), memory_space=pltpu.SMEM),
            pl.BlockSpec((TOP_K * tc,), lambda i: (jnp.minimum(i + 1, last),),
                         memory_space=pltpu.SMEM),
            pl.BlockSpec((tc, d), lambda i: (i, 0)),
            pl.BlockSpec((tc, LANES), lambda i: (i, 0)),
            pl.BlockSpec(memory_space=pl.ANY),
            pl.BlockSpec((1, d), lambda i: (0, 0)),
        ],
        out_specs=pl.BlockSpec((tc, d), lambda i: (i, 0)),
        out_shape=jax.ShapeDtypeStruct((t, d), F32),
        scratch_shapes=[pltpu.VMEM((2, TOP_K, tc * nt, LANES), F32), pltpu.SemaphoreType.DMA((2,))],
        compiler_params=pltpu.CompilerParams(dimension_semantics=("arbitrary",)),
        name="combine",
    )(dest, dest, x1, gates_t, y, g_final)


def _layer(x, g_mix, w_in, w_s, b_s, g_sgu, w_conv, w_out, g_ffn, w_rg, w_re, w_gate, w_up, w_down):
    b, s, d = x.shape
    t = b * s
    nt = d // LANES
    ne = w_gate.shape[0]
    bs = EXPERT_ROWS

    bias_full = jnp.repeat(b_s.T, d // N_GROUPS, axis=1)
    wr_t = jnp.concatenate([w_rg, w_re], axis=1).T
    wr_t = jnp.pad(wr_t, ((0, LANES - wr_t.shape[0]), (0, 0))).astype(BF16)
    x1, h2t, logits_t = _mixer(
        x, g_mix.reshape(1, d), w_in.astype(BF16), w_s.astype(BF16), bias_full, g_sgu.reshape(1, d),
        w_conv, w_out.astype(BF16), g_ffn.reshape(1, d), wr_t)

    info, gates_t, counts = _route(logits_t)

    n_rows = -(-(t * TOP_K + ne * (bs - 1)) // bs) * bs
    nb = n_rows // bs
    exp_tbl, blk_tbl = _plan(counts, nb)
    blk_e, nxt_e, n_used = blk_tbl[0, :nb], blk_tbl[1, :nb], blk_tbl[2, :1]

    dest = _place(info, exp_tbl)[:TOP_K]
    tiled = lambda n: dest.reshape(TOP_K, t // n, n).transpose(1, 0, 2).reshape(-1)
    plan = jnp.concatenate([exp_tbl[:, :3].reshape(-1), n_used])
    xs = _dispatch(plan, tiled(DISPATCH_TOKENS), h2t, n_rows)
    y = _experts(blk_e, nxt_e, n_used, xs, w_gate, w_up, w_down)
    return x1.reshape(t, d), tiled(COMBINE_TOKENS), gates_t, y


def kernel(x, g_mix, w_in, w_s, b_s, g_sgu, w_conv, w_out, g_ffn, w_router_group, w_router_expert,
           w_gate, w_up, w_down, g_final):
    b, s, d = x.shape
    depth = g_mix.shape[0]
    assert depth == 1, "the final RMSNorm is fused into the last layer's combine"
    assert s % MIX_ROWS == 0 and MIX_ROWS % CHUNK == 0 and d % LANES == 0
    assert EXPERT_ROWS & (EXPERT_ROWS - 1) == 0, "block bookkeeping uses shifts"
    assert all((b * s) % n == 0 for n in (ROUTE_TOKENS, PLACE_TOKENS, DISPATCH_TOKENS, COMBINE_TOKENS))
    l = 0
    x1, dest, gates_t, y = _layer(
        x, g_mix[l], w_in[l], w_s[l], b_s[l], g_sgu[l], w_conv[l], w_out[l], g_ffn[l],
        w_router_group[l], w_router_expert[l], w_gate[l], w_up[l], w_down[l])
    out = _combine(dest, x1, gates_t, y, g_final.reshape(1, d))
    return out.reshape(b, s, d)
```

```python
import functools

import jax
import jax.numpy as jnp
from jax import lax
from jax.experimental import pallas as pl
from jax.experimental.pallas import tpu as pltpu

F32 = jnp.float32
BF16 = jnp.bfloat16
I32 = jnp.int32

EPS = 1e-6
LANES = 128
SUBLANES = 8
CHUNK = 128
N_GROUPS = 8
EXPERTS_PER_GROUP = 8
TOP_K = 2
CONV_K = 3
N_BRANCH = 7

MIX_ROWS = 512
ROUTE_TOKENS = 512
PLACE_TOKENS = 2048
DISPATCH_TOKENS = 1024
COMBINE_TOKENS = 256
MOVE_UNROLL = 8
WEIGHT_DMA_PRIORITY = 1
EXPERT_ROWS = 128
EXPERT_LOOKAHEAD = 3
VMEM_LIMIT = 56 * 1024 * 1024


def _rms(x, g):
    return x * lax.rsqrt(jnp.mean(x * x, axis=-1, keepdims=True) + EPS) * g


def _sigmoid(x):
    return 0.5 * (1.0 + jnp.tanh(0.5 * x))


def _gelu_tanh(x):
    c = 0.7978845608028654
    return x * (0.5 * (1.0 + jnp.tanh(c * (x + 0.044715 * (x * x * x)))))


def _rows_to_tiles(dst_ref, val, rows):
    nt = val.shape[1] // LANES
    for c in range(nt):
        dst_ref[pl.ds(c, rows, stride=nt), :] = val[:, c * LANES:(c + 1) * LANES]


def _tiles_to_rows(src_ref, rows, nt):
    return jnp.concatenate([src_ref[pl.ds(c, rows, stride=nt), :] for c in range(nt)], axis=1)


def _mixer_kernel(x_ref, xp_ref, xn_ref, gmix_ref, win_ref, ws_ref, bias_ref, gsgu_ref, wconv_ref,
                  wout_ref, gffn_ref, wr_ref, x1_ref, h2t_ref, lt_ref, vg_ref, z_ref, acc_ref):
    ts, d = x_ref.shape[1], x_ref.shape[2]
    s = pl.program_id(1)
    ns = pl.num_programs(1)
    gw = d // N_GROUPS
    nc = ts // CHUNK

    x = x_ref[0]
    gmix = gmix_ref[...]
    h = _rms(x, gmix).astype(BF16)

    def proj(j, n=1):
        return jnp.dot(h, win_ref[:, j * d:(j + n) * d], preferred_element_type=F32)

    gv = _gelu_tanh(proj(1))
    for g in range(N_GROUPS):
        cs = slice(g * gw, (g + 1) * gw)
        blk = gv[:, cs]
        mu = jnp.mean(blk, axis=-1, keepdims=True)
        dv = blk - mu
        var = jnp.mean(dv * dv, axis=-1, keepdims=True)
        vg_ref[:, cs] = (dv * lax.rsqrt(var + EPS) * gsgu_ref[:, cs]).astype(BF16)
    for g in range(N_GROUPS):
        cs = slice(g * gw, (g + 1) * gw)
        vcat = jnp.concatenate([vg_ref[n * CHUNK:(n + 1) * CHUNK, cs] for n in range(nc)], axis=1)
        zg = jnp.dot(ws_ref[g], vcat, preferred_element_type=F32)
        for n in range(nc):
            z_ref[n * CHUNK:(n + 1) * CHUNK, cs] = zg[:, n * gw:(n + 1) * gw]
    u = _gelu_tanh(proj(0))
    ga = _sigmoid(proj(5))
    for n in range(nc):
        rs = slice(n * CHUNK, (n + 1) * CHUNK)
        acc_ref[rs, :] = ga[rs] * (u[rs] * (z_ref[rs, :] + bias_ref[...]))

    cx = proj(3, 2)
    z2 = cx[:, :d] * cx[:, d:]
    xh = jnp.concatenate([xp_ref[0], xn_ref[0]], axis=0)
    hh = _rms(xh, gmix).astype(BF16)
    cxh = jnp.dot(hh, win_ref[:, 3 * d:5 * d], preferred_element_type=F32)
    z2h = cxh[:, :d] * cxh[:, d:]
    prev = jnp.where(s > 0, z2h[SUBLANES - 1:SUBLANES, :], 0.0)
    nxt = jnp.where(s < ns - 1, z2h[SUBLANES:SUBLANES + 1, :], 0.0)
    row = lax.broadcasted_iota(I32, (ts, d), 0)
    zm1 = jnp.where(row == 0, prev, pltpu.roll(z2, 1, 0))
    zp1 = jnp.where(row == ts - 1, nxt, pltpu.roll(z2, ts - 1, 0))
    conv = wconv_ref[0:1, :] * zm1 + wconv_ref[1:2, :] * z2 + wconv_ref[2:3, :] * zp1
    cb = proj(2)
    gb = _sigmoid(proj(6))
    merged = acc_ref[...] + gb * (cb * conv)

    x1 = x + jnp.dot(merged.astype(BF16), wout_ref[...], preferred_element_type=F32)
    x1_ref[0] = x1

    h2 = _rms(x1, gffn_ref[...]).astype(BF16)
    lt_ref[...] = lax.dot_general(wr_ref[...], h2, (((1,), (1,)), ((), ())),
                                  preferred_element_type=F32)
    _rows_to_tiles(h2t_ref, h2.astype(F32), ts)


def _mixer(x, g_mix, w_in_b, w_s_b, bias_full, g_sgu, w_conv, w_out_b, g_ffn, wr_t):
    b, s, d = x.shape
    ts = MIX_ROWS
    ns = s // ts
    t = b * s
    nt = d // LANES
    hb = ts // SUBLANES
    last_hb = s // SUBLANES - 1

    const = lambda *shape: pl.BlockSpec(shape, lambda bi, si: (0,) * len(shape))
    in_specs = [
        pl.BlockSpec((1, ts, d), lambda bi, si: (bi, si, 0)),
        pl.BlockSpec((1, SUBLANES, d), lambda bi, si: (bi, jnp.maximum(si * hb - 1, 0), 0)),
        pl.BlockSpec((1, SUBLANES, d), lambda bi, si: (bi, jnp.minimum((si + 1) * hb, last_hb), 0)),
        const(1, d),
        pl.BlockSpec((d, N_BRANCH * d), lambda bi, si: (0, 0), pipeline_mode=pl.Buffered(1)),
        const(N_GROUPS, CHUNK, CHUNK),
        const(CHUNK, d),
        const(1, d),
        const(CONV_K, d),
        const(d, d),
        const(1, d),
        const(LANES, d),
    ]
    out_specs = [
        pl.BlockSpec((1, ts, d), lambda bi, si: (bi, si, 0)),
        pl.BlockSpec((ts * nt, LANES), lambda bi, si: (bi * ns + si, 0)),
        pl.BlockSpec((LANES, ts), lambda bi, si: (0, bi * ns + si)),
    ]
    out_shape = [
        jax.ShapeDtypeStruct((b, s, d), F32),
        jax.ShapeDtypeStruct((t * nt, LANES), F32),
        jax.ShapeDtypeStruct((LANES, t), F32),
    ]
    return pl.pallas_call(
        _mixer_kernel,
        grid=(b, ns),
        in_specs=in_specs,
        out_specs=out_specs,
        out_shape=out_shape,
        scratch_shapes=[pltpu.VMEM((ts, d), BF16), pltpu.VMEM((ts, d), F32), pltpu.VMEM((ts, d), F32)],
        compiler_params=pltpu.CompilerParams(
            dimension_semantics=("arbitrary", "arbitrary"), vmem_limit_bytes=VMEM_LIMIT),
        name="mixer",
    )(x, x, x, g_mix, w_in_b, w_s_b, bias_full, g_sgu, w_conv, w_out_b, g_ffn, wr_t)


def _route_kernel(lt_ref, info_ref, gt_ref, cnt_ref, carry_ref):
    tb = lt_ref.shape[1]
    ne = N_GROUPS * EXPERTS_PER_GROUP

    @pl.when(pl.program_id(0) == 0)
    def _():
        carry_ref[...] = jnp.zeros_like(carry_ref)

    row8 = lax.broadcasted_iota(I32, (SUBLANES, tb), 0)
    gl = lt_ref[0:N_GROUPS, :]
    gmax = jnp.max(gl, axis=0, keepdims=True)
    gidx = jnp.min(jnp.where(gl == gmax, row8, N_GROUPS), axis=0, keepdims=True)
    pg = 1.0 / jnp.sum(jnp.exp(gl - gmax), axis=0, keepdims=True)

    sel = jnp.zeros((EXPERTS_PER_GROUP, tb), F32)
    for g in range(N_GROUPS):
        lo = N_GROUPS + g * EXPERTS_PER_GROUP
        sel = jnp.where(gidx == g, lt_ref[lo:lo + EXPERTS_PER_GROUP, :], sel)
    m1 = jnp.max(sel, axis=0, keepdims=True)
    i1 = jnp.min(jnp.where(sel == m1, row8, EXPERTS_PER_GROUP), axis=0, keepdims=True)
    sel2 = jnp.where(row8 == i1, -jnp.inf, sel)
    m2 = jnp.max(sel2, axis=0, keepdims=True)
    i2 = jnp.min(jnp.where(sel2 == m2, row8, EXPERTS_PER_GROUP), axis=0, keepdims=True)
    e2 = jnp.exp(m2 - m1)
    den = 1.0 + e2
    gate0 = pg * (1.0 / den)
    gate1 = pg * (e2 / den)
    eid0 = gidx * EXPERTS_PER_GROUP + i1
    eid1 = gidx * EXPERTS_PER_GROUP + i2

    rowe = lax.broadcasted_iota(I32, (ne, tb), 0)
    hit0 = rowe == eid0
    hit1 = rowe == eid1
    onehot = jnp.where(hit0 | hit1, 1.0, 0.0)
    before = (lax.broadcasted_iota(I32, (tb, tb), 0) < lax.broadcasted_iota(I32, (tb, tb), 1))
    prefix = jnp.dot(onehot.astype(BF16), jnp.where(before, 1.0, 0.0).astype(BF16),
                     preferred_element_type=F32)
    base = prefix + carry_ref[:, 0:1]
    rank0 = jnp.sum(jnp.where(hit0, base, 0.0), axis=0, keepdims=True).astype(I32)
    rank1 = jnp.sum(jnp.where(hit1, base, 0.0), axis=0, keepdims=True).astype(I32)
    carry_ref[...] = carry_ref[...] + jnp.sum(onehot, axis=1, keepdims=True)
    cnt_ref[...] = carry_ref[...].astype(I32)

    info_ref[...] = jnp.where(row8 == 0, eid0, jnp.where(row8 == 1, eid1,
                              jnp.where(row8 == 2, rank0, jnp.where(row8 == 3, rank1, 0))))
    rowl = lax.broadcasted_iota(I32, (LANES, tb), 0)
    gates = jnp.where(rowl == 0, gate0, jnp.where(rowl == 1, gate1, 0.0))
    gt_ref[...] = gates.T


def _route(logits_t):
    t = logits_t.shape[1]
    tb = ROUTE_TOKENS
    ne = N_GROUPS * EXPERTS_PER_GROUP
    return pl.pallas_call(
        _route_kernel,
        grid=(t // tb,),
        in_specs=[pl.BlockSpec((LANES, tb), lambda i: (0, i))],
        out_specs=[
            pl.BlockSpec((SUBLANES, tb), lambda i: (0, i)),
            pl.BlockSpec((tb, LANES), lambda i: (i, 0)),
            pl.BlockSpec((ne, LANES), lambda i: (0, 0)),
        ],
        out_shape=[
            jax.ShapeDtypeStruct((SUBLANES, t), I32),
            jax.ShapeDtypeStruct((t, LANES), F32),
            jax.ShapeDtypeStruct((ne, LANES), I32),
        ],
        scratch_shapes=[pltpu.VMEM((ne, LANES), F32)],
        compiler_params=pltpu.CompilerParams(dimension_semantics=("arbitrary",)),
        name="route",
    )(logits_t)


def _plan_kernel(cnt_ref, exp_ref, blk_ref):
    ne = cnt_ref.shape[0]
    nbp = blk_ref.shape[1]
    shift = EXPERT_ROWS.bit_length() - 1
    pad_rows = lambda c: lax.shift_left(lax.shift_right_logical(c + (EXPERT_ROWS - 1), shift), shift)
    cnt = cnt_ref[...]
    padded = pad_rows(cnt)
    padded_lanes = pad_rows(cnt.astype(F32).T[:ne, :ne].astype(I32))
    e_sub = lax.broadcasted_iota(I32, (ne, ne), 0)
    e_lane = lax.broadcasted_iota(I32, (ne, ne), 1)
    pend = jnp.sum(jnp.where(e_lane <= e_sub, padded_lanes, 0), axis=1, keepdims=True)
    pstart = pend - padded[:, 0:1]

    lane = lax.broadcasted_iota(I32, (ne, LANES), 1)
    exp_ref[...] = jnp.where(lane == 0, pstart, jnp.where(lane == 1, cnt, jnp.where(lane == 2, padded, 0)))

    first_row = lax.broadcasted_iota(I32, (ne, nbp), 1) * EXPERT_ROWS
    e_col = lax.broadcasted_iota(I32, (ne, nbp), 0)
    blk_e = jnp.minimum(jnp.sum(jnp.where(pend <= first_row, 1, 0), axis=0, keepdims=True), ne - 1)
    later = jnp.where((e_col > blk_e) & (padded[:, 0:1] > 0), e_col, ne)
    nxt_e = jnp.min(later, axis=0, keepdims=True)
    nxt_e = jnp.where(nxt_e == ne, -1, nxt_e)
    n_used = lax.shift_right_logical(jnp.max(pend, axis=0, keepdims=True), shift)
    row8 = lax.broadcasted_iota(I32, (SUBLANES, nbp), 0)
    blk_ref[...] = jnp.where(row8 == 0, blk_e, jnp.where(row8 == 1, nxt_e, jnp.where(row8 == 2, n_used, 0)))


def _plan(counts, nb):
    ne = counts.shape[0]
    nbp = -(-nb // LANES) * LANES
    return pl.pallas_call(
        _plan_kernel,
        out_shape=[jax.ShapeDtypeStruct((ne, LANES), I32), jax.ShapeDtypeStruct((SUBLANES, nbp), I32)],
        name="plan",
    )(counts)


def _place_kernel(info_ref, ps_ref, dest_ref):
    tb = info_ref.shape[1]
    ne = ps_ref.shape[0]
    rowe = lax.broadcasted_iota(I32, (ne, tb), 0)
    row8 = lax.broadcasted_iota(I32, (SUBLANES, tb), 0)
    ps = ps_ref[:, 0:1]
    dest = jnp.zeros((SUBLANES, tb), I32)
    for k in range(TOP_K):
        start = jnp.sum(jnp.where(rowe == info_ref[k:k + 1, :], ps, 0), axis=0, keepdims=True)
        dest = jnp.where(row8 == k, start + info_ref[TOP_K + k:TOP_K + k + 1, :], dest)
    dest_ref[...] = dest


def _place(info, exp_tbl):
    t = info.shape[1]
    tb = PLACE_TOKENS
    ne = exp_tbl.shape[0]
    return pl.pallas_call(
        _place_kernel,
        grid=(t // tb,),
        in_specs=[pl.BlockSpec((SUBLANES, tb), lambda i: (0, i)),
                  pl.BlockSpec((ne, LANES), lambda i: (0, 0))],
        out_specs=pl.BlockSpec((SUBLANES, tb), lambda i: (0, i)),
        out_shape=jax.ShapeDtypeStruct((SUBLANES, t), I32),
        compiler_params=pltpu.CompilerParams(dimension_semantics=("arbitrary",)),
        name="place",
    )(info, exp_tbl)


def _row_copy(src_ref, src_row, dst_ref, dst_row, nt, sem):
    return pltpu.make_async_copy(
        src_ref.at[pl.ds(pl.multiple_of(src_row * nt, nt), nt), :],
        dst_ref.at[pl.ds(pl.multiple_of(dst_row * nt, nt), nt), :],
        sem)


def _rows_wait(ref, n_rows, nt, sem):
    pltpu.make_async_copy(ref.at[pl.ds(0, n_rows * nt), :], ref.at[pl.ds(0, n_rows * nt), :], sem).wait()


def _for_each_assignment(dest_ref, n_tok, start_copy):
    def group(g, c):
        t0 = g * MOVE_UNROLL
        rows = [[dest_ref[k * n_tok + t0 + u] for k in range(TOP_K)] for u in range(MOVE_UNROLL)]
        for u in range(MOVE_UNROLL):
            for k in range(TOP_K):
                start_copy(k, t0 + u, rows[u][k])
        return c

    lax.fori_loop(0, n_tok // MOVE_UNROLL, group, 0)


def _dispatch_kernel(nt, nb, plan_ref, dest_ref, h2_ref, xs_ref, zbuf, sem, zsem):
    td = h2_ref.shape[0] // nt
    ne = (plan_ref.shape[0] - 1) // 3
    n_used = plan_ref[3 * ne]
    blk_rows = zbuf.shape[0]

    def zero_rows(start_not_wait):
        def fire(c):
            c.start() if start_not_wait else c.wait()

        def expert(e, carry):
            cnt = plan_ref[3 * e + 1]
            row = plan_ref[3 * e] + cnt
            n = plan_ref[3 * e + 2] - cnt
            p = EXPERT_ROWS // 2
            while p >= 1:
                has = (n & p) != 0
                r, sz = row, p

                @pl.when(has)
                def _():
                    fire(pltpu.make_async_copy(
                        zbuf.at[pl.ds(0, sz * nt), :],
                        xs_ref.at[pl.ds(pl.multiple_of(r * nt, nt), sz * nt), :], zsem))

                row = row + jnp.where(has, p, 0)
                p //= 2
            return carry

        lax.fori_loop(0, ne, expert, 0)

        def tail(q, carry):
            fire(pltpu.make_async_copy(
                zbuf, xs_ref.at[pl.ds(pl.multiple_of(q * blk_rows, blk_rows), blk_rows), :], zsem))
            return carry

        lax.fori_loop(n_used, nb, tail, 0)

    @pl.when(pl.program_id(0) == 0)
    def _():
        zbuf[...] = jnp.zeros_like(zbuf)
        zero_rows(True)

    _for_each_assignment(
        dest_ref, td,
        lambda k, t, row: _row_copy(h2_ref, t, xs_ref, row, nt, sem).start(priority=k))
    _rows_wait(xs_ref, TOP_K * td, nt, sem)

    @pl.when(pl.program_id(0) == 0)
    def _():
        zero_rows(False)


def _dispatch(plan, dest, h2t, n_rows):
    td = DISPATCH_TOKENS
    t = dest.shape[0] // TOP_K
    nt = h2t.shape[0] // t
    nb = n_rows // EXPERT_ROWS
    grid_spec = pltpu.PrefetchScalarGridSpec(
        num_scalar_prefetch=1,
        grid=(t // td,),
        in_specs=[
            pl.BlockSpec((TOP_K * td,), lambda i, plan: (i,), memory_space=pltpu.SMEM),
            pl.BlockSpec((td * nt, LANES), lambda i, plan: (i, 0)),
        ],
        out_specs=pl.BlockSpec(memory_space=pl.ANY),
        scratch_shapes=[pltpu.VMEM((EXPERT_ROWS * nt, LANES), h2t.dtype),
                        pltpu.SemaphoreType.DMA, pltpu.SemaphoreType.DMA],
    )
    return pl.pallas_call(
        functools.partial(_dispatch_kernel, nt, nb),
        grid_spec=grid_spec,
        out_shape=jax.ShapeDtypeStruct((n_rows * nt, LANES), h2t.dtype),
        compiler_params=pltpu.CompilerParams(dimension_semantics=("arbitrary",)),
        name="dispatch",
    )(plan, dest, h2t)


def _experts_kernel(nb, be_ref, nx_ref, nu_ref, xs_hbm, wg_hbm, wu_hbm, wd_hbm, y_hbm,
                    xbuf, ybuf, wg_st, wu_st, wd_st, wgu_b, wd_b, hid_s, slot_ref, wsems, isems, osems):
    ring = xbuf.shape[0]
    rows = xbuf.shape[1]
    d, de = wg_st.shape[1], wg_st.shape[2]
    nt = d // LANES
    bs = rows // nt
    nu = nu_ref[0]

    def block(ref, q):
        return ref.at[pl.ds(pl.multiple_of(q * rows, rows), rows), :]

    def in_copy(q):
        return pltpu.make_async_copy(block(xs_hbm, q), xbuf.at[q % ring], isems.at[q % ring])

    def out_copy(q):
        return pltpu.make_async_copy(ybuf.at[q % ring], block(y_hbm, q), osems.at[q % ring])

    def weight_copies(ex, slot):
        return (pltpu.make_async_copy(wg_hbm.at[ex], wg_st.at[slot], wsems.at[slot]),
                pltpu.make_async_copy(wu_hbm.at[ex], wu_st.at[slot], wsems.at[slot]),
                pltpu.make_async_copy(wd_hbm.at[ex], wd_st.at[slot], wsems.at[slot]))

    slot_ref[0] = 0
    slot_ref[1] = 0
    for c in weight_copies(be_ref[0], 0):
        c.start(priority=WEIGHT_DMA_PRIORITY)
    for q in range(EXPERT_LOOKAHEAD):
        @pl.when(q < nu)
        def _():
            in_copy(q).start()
    xbuf[...] = jnp.zeros_like(xbuf)
    hid_s[...] = jnp.zeros_like(hid_s)
    wd_b[...] = jnp.zeros_like(wd_b)

    def step(j, carry):
        live = j < nu
        e = be_ref[jnp.minimum(j, nu - 1)]
        prev_slot = slot_ref[1]

        @pl.when(live & ((j == 0) | (e != be_ref[jnp.maximum(j - 1, 0)])))
        def _():
            slot = slot_ref[0]
            for c in weight_copies(e, slot):
                c.wait()
            nx = nx_ref[j]

            @pl.when(nx >= 0)
            def _():
                for c in weight_copies(nx, 1 - slot):
                    c.start(priority=WEIGHT_DMA_PRIORITY)

            wgu_b[:, :de] = wg_st[slot].astype(BF16)
            wgu_b[:, de:] = wu_st[slot].astype(BF16)
            wd_b[slot] = wd_st[slot].astype(BF16)
            slot_ref[1] = slot
            slot_ref[0] = 1 - slot

        @pl.when(j + EXPERT_LOOKAHEAD < nu)
        def _():
            in_copy(j + EXPERT_LOOKAHEAD).start()

        @pl.when(live)
        def _():
            in_copy(j).wait()

        @pl.when(j > ring)
        def _():
            out_copy(j - 1 - ring).wait()

        xb = _tiles_to_rows(xbuf.at[j % ring], bs, nt).astype(BF16)
        gu = jnp.dot(xb, wgu_b[...], preferred_element_type=F32)
        gate = gu[:, :de]
        hid = ((gate * _sigmoid(gate)) * gu[:, de:]).astype(BF16)
        y = jnp.dot(hid_s[(j + 1) % 2], wd_b[prev_slot], preferred_element_type=F32)
        hid_s[j % 2] = hid
        _rows_to_tiles(ybuf.at[(j + ring - 1) % ring], y, bs)

        @pl.when(j > 0)
        def _():
            out_copy(j - 1).start()

        return carry

    lax.fori_loop(0, nu + 1, step, 0)

    for q in range(1, ring + 1):
        @pl.when(nu - q >= 0)
        def _():
            out_copy(nu - q).wait()

    ybuf[0] = jnp.zeros(ybuf.shape[1:], ybuf.dtype)

    def zero_start(q, carry):
        pltpu.make_async_copy(ybuf.at[0], block(y_hbm, q), osems.at[0]).start()
        return carry

    def zero_wait(q, carry):
        pltpu.make_async_copy(ybuf.at[0], block(y_hbm, q), osems.at[0]).wait()
        return carry

    lax.fori_loop(nu, nb, zero_start, 0)
    lax.fori_loop(nu, nb, zero_wait, 0)


def _experts(blk_e, nxt_e, n_used, xs, w_gate, w_up, w_down):
    ne, d, de = w_gate.shape
    nt = d // LANES
    rows = EXPERT_ROWS * nt
    nb = xs.shape[0] // rows
    ring = EXPERT_LOOKAHEAD + 1
    any_spec = pl.BlockSpec(memory_space=pl.ANY)
    grid_spec = pltpu.PrefetchScalarGridSpec(
        num_scalar_prefetch=3,
        grid=(1,),
        in_specs=[any_spec, any_spec, any_spec, any_spec],
        out_specs=any_spec,
        scratch_shapes=[pltpu.VMEM((ring, rows, LANES), F32), pltpu.VMEM((ring, rows, LANES), F32),
                        pltpu.VMEM((2, d, de), F32), pltpu.VMEM((2, d, de), F32),
                        pltpu.VMEM((2, de, d), F32),
                        pltpu.VMEM((d, 2 * de), BF16), pltpu.VMEM((2, de, d), BF16),
                        pltpu.VMEM((2, EXPERT_ROWS, de), BF16),
                        pltpu.SMEM((2,), I32), pltpu.SemaphoreType.DMA((2,)),
                        pltpu.SemaphoreType.DMA((ring,)), pltpu.SemaphoreType.DMA((ring,))],
    )
    return pl.pallas_call(
        functools.partial(_experts_kernel, nb),
        grid_spec=grid_spec,
        out_shape=jax.ShapeDtypeStruct(xs.shape, F32),
        compiler_params=pltpu.CompilerParams(
            dimension_semantics=("arbitrary",), vmem_limit_bytes=VMEM_LIMIT),
        name="experts",
    )(blk_e, nxt_e, n_used, xs, w_gate, w_up, w_down)


def _combine_kernel(dcur_ref, dnxt_ref, x1_ref, gt_ref, y_ref, gfin_ref, o_ref, ybuf, sems):
    tc, d = x1_ref.shape
    nt = d // LANES
    i = pl.program_id(0)
    n = pl.num_programs(0)

    def issue_tile(dest_ref, slot):
        _for_each_assignment(
            dest_ref, tc,
            lambda k, t, row: _row_copy(y_ref, row, ybuf.at[slot, k], t, nt,
                                        sems.at[slot]).start(priority=k))

    @pl.when(i == 0)
    def _():
        issue_tile(dcur_ref, 0)

    @pl.when(i + 1 < n)
    def _():
        issue_tile(dnxt_ref, (i + 1) % 2)

    slot = i % 2
    _rows_wait(y_ref, TOP_K * tc, nt, sems.at[slot])
    y0 = _tiles_to_rows(ybuf.at[slot, 0], tc, nt)
    y1 = _tiles_to_rows(ybuf.at[slot, 1], tc, nt)
    gates = gt_ref[...]
    xo = x1_ref[...] + (gates[:, 0:1] * y0 + gates[:, 1:2] * y1)
    o_ref[...] = _rms(xo, gfin_ref[...])


def _combine(dest, x1, gates_t, y, g_final):
    t, d = x1.shape
    nt = d // LANES
    tc = COMBINE_TOKENS
    last = t // tc - 1
    return pl.pallas_call(
        _combine_kernel,
        grid=(t // tc,),
        in_specs=[
            pl.BlockSpec((TOP_K * tc,), lambda i: (i,), memory_space=pltpu.SMEM),
            pl.BlockSpec((TOP_K * tc,), lambda i: (jnp.minimum(i + 1, last),),
                         memory_space=pltpu.SMEM),
            pl.BlockSpec((tc, d), lambda i: (i, 0)),
            pl.BlockSpec((tc, LANES), lambda i: (i, 0)),
            pl.BlockSpec(memory_space=pl.ANY),
            pl.BlockSpec((1, d), lambda i: (0, 0)),
        ],
        out_specs=pl.BlockSpec((tc, d), lambda i: (i, 0)),
        out_shape=jax.ShapeDtypeStruct((t, d), F32),
        scratch_shapes=[pltpu.VMEM((2, TOP_K, tc * nt, LANES), F32), pltpu.SemaphoreType.DMA((2,))],
        compiler_params=pltpu.CompilerParams(dimension_semantics=("arbitrary",)),
        name="combine",
    )(dest, dest, x1, gates_t, y, g_final)


def _layer(x, g_mix, w_in, w_s, b_s, g_sgu, w_conv, w_out, g_ffn, w_rg, w_re, w_gate, w_up, w_down):
    b, s, d = x.shape
    t = b * s
    nt = d // LANES
    ne = w_gate.shape[0]
    bs = EXPERT_ROWS

    bias_full = jnp.repeat(b_s.T, d // N_GROUPS, axis=1)
    wr_t = jnp.concatenate([w_rg, w_re], axis=1).T
    wr_t = jnp.pad(wr_t, ((0, LANES - wr_t.shape[0]), (0, 0))).astype(BF16)
    x1, h2t, logits_t = _mixer(
        x, g_mix.reshape(1, d), w_in.astype(BF16), w_s.astype(BF16), bias_full, g_sgu.reshape(1, d),
        w_conv, w_out.astype(BF16), g_ffn.reshape(1, d), wr_t)

    info, gates_t, counts = _route(logits_t)

    n_rows = -(-(t * TOP_K + ne * (bs - 1)) // bs) * bs
    nb = n_rows // bs
    exp_tbl, blk_tbl = _plan(counts, nb)
    blk_e, nxt_e, n_used = blk_tbl[0, :nb], blk_tbl[1, :nb], blk_tbl[2, :1]

    dest = _place(info, exp_tbl)[:TOP_K]
    tiled = lambda n: dest.reshape(TOP_K, t // n, n).transpose(1, 0, 2).reshape(-1)
    plan = jnp.concatenate([exp_tbl[:, :3].reshape(-1), n_used])
    xs = _dispatch(plan, tiled(DISPATCH_TOKENS), h2t, n_rows)
    y = _experts(blk_e, nxt_e, n_used, xs, w_gate, w_up, w_down)
    return x1.reshape(t, d), tiled(COMBINE_TOKENS), gates_t, y


def kernel(x, g_mix, w_in, w_s, b_s, g_sgu, w_conv, w_out, g_ffn, w_router_group, w_router_expert,
           w_gate, w_up, w_down, g_final):
    b, s, d = x.shape
    depth = g_mix.shape[0]
    assert depth == 1, "the final RMSNorm is fused into the last layer's combine"
    assert s % MIX_ROWS == 0 and MIX_ROWS % CHUNK == 0 and d % LANES == 0
    assert EXPERT_ROWS & (EXPERT_ROWS - 1) == 0, "block bookkeeping uses shifts"
    assert all((b * s) % n == 0 for n in (ROUTE_TOKENS, PLACE_TOKENS, DISPATCH_TOKENS, COMBINE_TOKENS))
    l = 0
    x1, dest, gates_t, y = _layer(
        x, g_mix[l], w_in[l], w_s[l], b_s[l], g_sgu[l], w_conv[l], w_out[l], g_ffn[l],
        w_router_group[l], w_router_expert[l], w_gate[l], w_up[l], w_down[l])
    out = _combine(dest, x1, gates_t, y, g_final.reshape(1, d))
    return out.reshape(b, s, d)
```

```python
import functools

import jax
import jax.numpy as jnp
from jax import lax
from jax.experimental import pallas as pl
from jax.experimental.pallas import tpu as pltpu

F32 = jnp.float32
BF16 = jnp.bfloat16
I32 = jnp.int32
U32 = jnp.uint32

EPS = 1e-6
LANES = 128
SUBLANES = 8
CHUNK = 128
N_GROUPS = 8
EXPERTS_PER_GROUP = 8
TOP_K = 2
CONV_K = 3
N_BRANCH = 7

MIX_ROWS = 512
ROUTE_TOKENS = 512
PLACE_TOKENS = 2048
DISPATCH_TOKENS = 1024
COMBINE_TOKENS = 256
MOVE_UNROLL = 8
WEIGHT_DMA_PRIORITY = 1
EXPERT_ROWS = 128
EXPERT_LOOKAHEAD = 3
VMEM_LIMIT = 56 * 1024 * 1024


def _rms(x, g):
    return x * lax.rsqrt(jnp.mean(x * x, axis=-1, keepdims=True) + EPS) * g


def _sigmoid(x):
    return 0.5 * (1.0 + jnp.tanh(0.5 * x))


def _gelu_tanh(x):
    c = 0.7978845608028654
    return x * (0.5 * (1.0 + jnp.tanh(c * (x + 0.044715 * (x * x * x)))))


def _rows_to_tiles(dst_ref, val, rows):
    nt = val.shape[1] // LANES
    for c in range(nt):
        dst_ref[pl.ds(c, rows, stride=nt), :] = val[:, c * LANES:(c + 1) * LANES]


def _tiles_to_rows(src_ref, rows, nt):
    return jnp.concatenate([src_ref[pl.ds(c, rows, stride=nt), :] for c in range(nt)], axis=1)


def _pack_bf16_pairs(x):
    half = x.shape[1] // 2
    bits = pltpu.bitcast(x.astype(F32), U32)
    return lax.shift_right_logical(bits[:, :half], U32(16)) | bits[:, half:]


def _unpack_bf16_pairs(w):
    lo = pltpu.bitcast(lax.shift_left(w, U32(16)), F32)
    hi = pltpu.bitcast(w & U32(0xFFFF0000), F32)
    return jnp.concatenate([lo, hi], axis=1).astype(BF16)


def _mixer_kernel(x_ref, xp_ref, xn_ref, gmix_ref, win_ref, ws_ref, bias_ref, gsgu_ref, wconv_ref,
                  wout_ref, gffn_ref, wr_ref, x1_ref, h2t_ref, lt_ref, vg_ref, z_ref, acc_ref):
    ts, d = x_ref.shape[1], x_ref.shape[2]
    s = pl.program_id(1)
    ns = pl.num_programs(1)
    gw = d // N_GROUPS
    nc = ts // CHUNK

    x = x_ref[0]
    gmix = gmix_ref[...]
    h = _rms(x, gmix).astype(BF16)

    def proj(j, n=1):
        return jnp.dot(h, win_ref[:, j * d:(j + n) * d], preferred_element_type=F32)

    gv = _gelu_tanh(proj(1))
    for g in range(N_GROUPS):
        cs = slice(g * gw, (g + 1) * gw)
        blk = gv[:, cs]
        mu = jnp.mean(blk, axis=-1, keepdims=True)
        dv = blk - mu
        var = jnp.mean(dv * dv, axis=-1, keepdims=True)
        vg_ref[:, cs] = (dv * lax.rsqrt(var + EPS) * gsgu_ref[:, cs]).astype(BF16)
    for g in range(N_GROUPS):
        cs = slice(g * gw, (g + 1) * gw)
        vcat = jnp.concatenate([vg_ref[n * CHUNK:(n + 1) * CHUNK, cs] for n in range(nc)], axis=1)
        zg = jnp.dot(ws_ref[g], vcat, preferred_element_type=F32)
        for n in range(nc):
            z_ref[n * CHUNK:(n + 1) * CHUNK, cs] = zg[:, n * gw:(n + 1) * gw]
    u = _gelu_tanh(proj(0))
    ga = _sigmoid(proj(5))
    for n in range(nc):
        rs = slice(n * CHUNK, (n + 1) * CHUNK)
        acc_ref[rs, :] = ga[rs] * (u[rs] * (z_ref[rs, :] + bias_ref[...]))

    cx = proj(3, 2)
    z2 = cx[:, :d] * cx[:, d:]
    xh = jnp.concatenate([xp_ref[0], xn_ref[0]], axis=0)
    hh = _rms(xh, gmix).astype(BF16)
    cxh = jnp.dot(hh, win_ref[:, 3 * d:5 * d], preferred_element_type=F32)
    z2h = cxh[:, :d] * cxh[:, d:]
    prev = jnp.where(s > 0, z2h[SUBLANES - 1:SUBLANES, :], 0.0)
    nxt = jnp.where(s < ns - 1, z2h[SUBLANES:SUBLANES + 1, :], 0.0)
    row = lax.broadcasted_iota(I32, (ts, d), 0)
    zm1 = jnp.where(row == 0, prev, pltpu.roll(z2, 1, 0))
    zp1 = jnp.where(row == ts - 1, nxt, pltpu.roll(z2, ts - 1, 0))
    conv = wconv_ref[0:1, :] * zm1 + wconv_ref[1:2, :] * z2 + wconv_ref[2:3, :] * zp1
    cb = proj(2)
    gb = _sigmoid(proj(6))
    merged = acc_ref[...] + gb * (cb * conv)

    x1 = x + jnp.dot(merged.astype(BF16), wout_ref[...], preferred_element_type=F32)
    x1_ref[0] = x1

    h2 = _rms(x1, gffn_ref[...]).astype(BF16)
    lt_ref[...] = lax.dot_general(wr_ref[...], h2, (((1,), (1,)), ((), ())),
                                  preferred_element_type=F32)
    _rows_to_tiles(h2t_ref, _pack_bf16_pairs(h2), ts)


def _mixer(x, g_mix, w_in_b, w_s_b, bias_full, g_sgu, w_conv, w_out_b, g_ffn, wr_t):
    b, s, d = x.shape
    ts = MIX_ROWS
    ns = s // ts
    t = b * s
    ntp = d // (2 * LANES)
    hb = ts // SUBLANES
    last_hb = s // SUBLANES - 1

    const = lambda *shape: pl.BlockSpec(shape, lambda bi, si: (0,) * len(shape))
    in_specs = [
        pl.BlockSpec((1, ts, d), lambda bi, si: (bi, si, 0)),
        pl.BlockSpec((1, SUBLANES, d), lambda bi, si: (bi, jnp.maximum(si * hb - 1, 0), 0)),
        pl.BlockSpec((1, SUBLANES, d), lambda bi, si: (bi, jnp.minimum((si + 1) * hb, last_hb), 0)),
        const(1, d),
        pl.BlockSpec((d, N_BRANCH * d), lambda bi, si: (0, 0), pipeline_mode=pl.Buffered(1)),
        const(N_GROUPS, CHUNK, CHUNK),
        const(CHUNK, d),
        const(1, d),
        const(CONV_K, d),
        const(d, d),
        const(1, d),
        const(LANES, d),
    ]
    out_specs = [
        pl.BlockSpec((1, ts, d), lambda bi, si: (bi, si, 0)),
        pl.BlockSpec((ts * ntp, LANES), lambda bi, si: (bi * ns + si, 0)),
        pl.BlockSpec((LANES, ts), lambda bi, si: (0, bi * ns + si)),
    ]
    out_shape = [
        jax.ShapeDtypeStruct((b, s, d), F32),
        jax.ShapeDtypeStruct((t * ntp, LANES), U32),
        jax.ShapeDtypeStruct((LANES, t), F32),
    ]
    return pl.pallas_call(
        _mixer_kernel,
        grid=(b, ns),
        in_specs=in_specs,
        out_specs=out_specs,
        out_shape=out_shape,
        scratch_shapes=[pltpu.VMEM((ts, d), BF16), pltpu.VMEM((ts, d), F32), pltpu.VMEM((ts, d), F32)],
        compiler_params=pltpu.CompilerParams(
            dimension_semantics=("arbitrary", "arbitrary"), vmem_limit_bytes=VMEM_LIMIT),
        name="mixer",
    )(x, x, x, g_mix, w_in_b, w_s_b, bias_full, g_sgu, w_conv, w_out_b, g_ffn, wr_t)


def _route_kernel(lt_ref, info_ref, gt_ref, cnt_ref, carry_ref):
    tb = lt_ref.shape[1]
    ne = N_GROUPS * EXPERTS_PER_GROUP

    @pl.when(pl.program_id(0) == 0)
    def _():
        carry_ref[...] = jnp.zeros_like(carry_ref)

    row8 = lax.broadcasted_iota(I32, (SUBLANES, tb), 0)
    gl = lt_ref[0:N_GROUPS, :]
    gmax = jnp.max(gl, axis=0, keepdims=True)
    gidx = jnp.min(jnp.where(gl == gmax, row8, N_GROUPS), axis=0, keepdims=True)
    pg = 1.0 / jnp.sum(jnp.exp(gl - gmax), axis=0, keepdims=True)

    sel = jnp.zeros((EXPERTS_PER_GROUP, tb), F32)
    for g in range(N_GROUPS):
        lo = N_GROUPS + g * EXPERTS_PER_GROUP
        sel = jnp.where(gidx == g, lt_ref[lo:lo + EXPERTS_PER_GROUP, :], sel)
    m1 = jnp.max(sel, axis=0, keepdims=True)
    i1 = jnp.min(jnp.where(sel == m1, row8, EXPERTS_PER_GROUP), axis=0, keepdims=True)
    sel2 = jnp.where(row8 == i1, -jnp.inf, sel)
    m2 = jnp.max(sel2, axis=0, keepdims=True)
    i2 = jnp.min(jnp.where(sel2 == m2, row8, EXPERTS_PER_GROUP), axis=0, keepdims=True)
    e2 = jnp.exp(m2 - m1)
    den = 1.0 + e2
    gate0 = pg * (1.0 / den)
    gate1 = pg * (e2 / den)
    eid0 = gidx * EXPERTS_PER_GROUP + i1
    eid1 = gidx * EXPERTS_PER_GROUP + i2

    rowe = lax.broadcasted_iota(I32, (ne, tb), 0)
    hit0 = rowe == eid0
    hit1 = rowe == eid1
    onehot = jnp.where(hit0 | hit1, 1.0, 0.0)
    before = (lax.broadcasted_iota(I32, (tb, tb), 0) < lax.broadcasted_iota(I32, (tb, tb), 1))
    prefix = jnp.dot(onehot.astype(BF16), jnp.where(before, 1.0, 0.0).astype(BF16),
                     preferred_element_type=F32)
    base = prefix + carry_ref[:, 0:1]
    rank0 = jnp.sum(jnp.where(hit0, base, 0.0), axis=0, keepdims=True).astype(I32)
    rank1 = jnp.sum(jnp.where(hit1, base, 0.0), axis=0, keepdims=True).astype(I32)
    carry_ref[...] = carry_ref[...] + jnp.sum(onehot, axis=1, keepdims=True)
    cnt_ref[...] = carry_ref[...].astype(I32)

    info_ref[...] = jnp.where(row8 == 0, eid0, jnp.where(row8 == 1, eid1,
                              jnp.where(row8 == 2, rank0, jnp.where(row8 == 3, rank1, 0))))
    rowl = lax.broadcasted_iota(I32, (LANES, tb), 0)
    gates = jnp.where(rowl == 0, gate0, jnp.where(rowl == 1, gate1, 0.0))
    gt_ref[...] = gates.T


def _route(logits_t):
    t = logits_t.shape[1]
    tb = ROUTE_TOKENS
    ne = N_GROUPS * EXPERTS_PER_GROUP
    return pl.pallas_call(
        _route_kernel,
        grid=(t // tb,),
        in_specs=[pl.BlockSpec((LANES, tb), lambda i: (0, i))],
        out_specs=[
            pl.BlockSpec((SUBLANES, tb), lambda i: (0, i)),
            pl.BlockSpec((tb, LANES), lambda i: (i, 0)),
            pl.BlockSpec((ne, LANES), lambda i: (0, 0)),
        ],
        out_shape=[
            jax.ShapeDtypeStruct((SUBLANES, t), I32),
            jax.ShapeDtypeStruct((t, LANES), F32),
            jax.ShapeDtypeStruct((ne, LANES), I32),
        ],
        scratch_shapes=[pltpu.VMEM((ne, LANES), F32)],
        compiler_params=pltpu.CompilerParams(dimension_semantics=("arbitrary",)),
        name="route",
    )(logits_t)


def _plan_kernel(cnt_ref, exp_ref, blk_ref):
    ne = cnt_ref.shape[0]
    nbp = blk_ref.shape[1]
    shift = EXPERT_ROWS.bit_length() - 1
    pad_rows = lambda c: lax.shift_left(lax.shift_right_logical(c + (EXPERT_ROWS - 1), shift), shift)
    cnt = cnt_ref[...]
    padded = pad_rows(cnt)
    padded_lanes = pad_rows(cnt.astype(F32).T[:ne, :ne].astype(I32))
    e_sub = lax.broadcasted_iota(I32, (ne, ne), 0)
    e_lane = lax.broadcasted_iota(I32, (ne, ne), 1)
    pend = jnp.sum(jnp.where(e_lane <= e_sub, padded_lanes, 0), axis=1, keepdims=True)
    pstart = pend - padded[:, 0:1]

    lane = lax.broadcasted_iota(I32, (ne, LANES), 1)
    exp_ref[...] = jnp.where(lane == 0, pstart, jnp.where(lane == 1, cnt, jnp.where(lane == 2, padded, 0)))

    first_row = lax.broadcasted_iota(I32, (ne, nbp), 1) * EXPERT_ROWS
    e_col = lax.broadcasted_iota(I32, (ne, nbp), 0)
    blk_e = jnp.minimum(jnp.sum(jnp.where(pend <= first_row, 1, 0), axis=0, keepdims=True), ne - 1)
    later = jnp.where((e_col > blk_e) & (padded[:, 0:1] > 0), e_col, ne)
    nxt_e = jnp.min(later, axis=0, keepdims=True)
    nxt_e = jnp.where(nxt_e == ne, -1, nxt_e)
    n_used = lax.shift_right_logical(jnp.max(pend, axis=0, keepdims=True), shift)
    row8 = lax.broadcasted_iota(I32, (SUBLANES, nbp), 0)
    blk_ref[...] = jnp.where(row8 == 0, blk_e, jnp.where(row8 == 1, nxt_e, jnp.where(row8 == 2, n_used, 0)))


def _plan(counts, nb):
    ne = counts.shape[0]
    nbp = -(-nb // LANES) * LANES
    return pl.pallas_call(
        _plan_kernel,
        out_shape=[jax.ShapeDtypeStruct((ne, LANES), I32), jax.ShapeDtypeStruct((SUBLANES, nbp), I32)],
        name="plan",
    )(counts)


def _place_kernel(info_ref, ps_ref, dest_ref):
    tb = info_ref.shape[1]
    ne = ps_ref.shape[0]
    rowe = lax.broadcasted_iota(I32, (ne, tb), 0)
    row8 = lax.broadcasted_iota(I32, (SUBLANES, tb), 0)
    ps = ps_ref[:, 0:1]
    dest = jnp.zeros((SUBLANES, tb), I32)
    for k in range(TOP_K):
        start = jnp.sum(jnp.where(rowe == info_ref[k:k + 1, :], ps, 0), axis=0, keepdims=True)
        dest = jnp.where(row8 == k, start + info_ref[TOP_K + k:TOP_K + k + 1, :], dest)
    dest_ref[...] = dest


def _place(info, exp_tbl):
    t = info.shape[1]
    tb = PLACE_TOKENS
    ne = exp_tbl.shape[0]
    return pl.pallas_call(
        _place_kernel,
        grid=(t // tb,),
        in_specs=[pl.BlockSpec((SUBLANES, tb), lambda i: (0, i)),
                  pl.BlockSpec((ne, LANES), lambda i: (0, 0))],
        out_specs=pl.BlockSpec((SUBLANES, tb), lambda i: (0, i)),
        out_shape=jax.ShapeDtypeStruct((SUBLANES, t), I32),
        compiler_params=pltpu.CompilerParams(dimension_semantics=("arbitrary",)),
        name="place",
    )(info, exp_tbl)


def _row_copy(src_ref, src_row, dst_ref, dst_row, nt, sem):
    return pltpu.make_async_copy(
        src_ref.at[pl.ds(pl.multiple_of(src_row * nt, nt), nt), :],
        dst_ref.at[pl.ds(pl.multiple_of(dst_row * nt, nt), nt), :],
        sem)


def _rows_wait(ref, n_rows, nt, sem):
    pltpu.make_async_copy(ref.at[pl.ds(0, n_rows * nt), :], ref.at[pl.ds(0, n_rows * nt), :], sem).wait()


def _for_each_assignment(dest_ref, n_tok, start_copy):
    def group(g, c):
        t0 = g * MOVE_UNROLL
        rows = [[dest_ref[k * n_tok + t0 + u] for k in range(TOP_K)] for u in range(MOVE_UNROLL)]
        for u in range(MOVE_UNROLL):
            for k in range(TOP_K):
                start_copy(k, t0 + u, rows[u][k])
        return c

    lax.fori_loop(0, n_tok // MOVE_UNROLL, group, 0)


def _dispatch_kernel(nt, nb, plan_ref, dest_ref, h2_ref, xs_ref, zbuf, sem, zsem):
    td = h2_ref.shape[0] // nt
    ne = (plan_ref.shape[0] - 1) // 3
    n_used = plan_ref[3 * ne]
    blk_rows = zbuf.shape[0]

    def zero_rows(start_not_wait):
        def fire(c):
            c.start() if start_not_wait else c.wait()

        def expert(e, carry):
            cnt = plan_ref[3 * e + 1]
            row = plan_ref[3 * e] + cnt
            n = plan_ref[3 * e + 2] - cnt
            p = EXPERT_ROWS // 2
            while p >= 1:
                has = (n & p) != 0
                r, sz = row, p

                @pl.when(has)
                def _():
                    fire(pltpu.make_async_copy(
                        zbuf.at[pl.ds(0, sz * nt), :],
                        xs_ref.at[pl.ds(pl.multiple_of(r * nt, nt), sz * nt), :], zsem))

                row = row + jnp.where(has, p, 0)
                p //= 2
            return carry

        lax.fori_loop(0, ne, expert, 0)

        def tail(q, carry):
            fire(pltpu.make_async_copy(
                zbuf, xs_ref.at[pl.ds(pl.multiple_of(q * blk_rows, blk_rows), blk_rows), :], zsem))
            return carry

        lax.fori_loop(n_used, nb, tail, 0)

    @pl.when(pl.program_id(0) == 0)
    def _():
        zbuf[...] = jnp.zeros_like(zbuf)
        zero_rows(True)

    _for_each_assignment(
        dest_ref, td,
        lambda k, t, row: _row_copy(h2_ref, t, xs_ref, row, nt, sem).start(priority=k))
    _rows_wait(xs_ref, TOP_K * td, nt, sem)

    @pl.when(pl.program_id(0) == 0)
    def _():
        zero_rows(False)


def _dispatch(plan, dest, h2t, n_rows):
    td = DISPATCH_TOKENS
    t = dest.shape[0] // TOP_K
    nt = h2t.shape[0] // t
    nb = n_rows // EXPERT_ROWS
    grid_spec = pltpu.PrefetchScalarGridSpec(
        num_scalar_prefetch=1,
        grid=(t // td,),
        in_specs=[
            pl.BlockSpec((TOP_K * td,), lambda i, plan: (i,), memory_space=pltpu.SMEM),
            pl.BlockSpec((td * nt, LANES), lambda i, plan: (i, 0)),
        ],
        out_specs=pl.BlockSpec(memory_space=pl.ANY),
        scratch_shapes=[pltpu.VMEM((EXPERT_ROWS * nt, LANES), h2t.dtype),
                        pltpu.SemaphoreType.DMA, pltpu.SemaphoreType.DMA],
    )
    return pl.pallas_call(
        functools.partial(_dispatch_kernel, nt, nb),
        grid_spec=grid_spec,
        out_shape=jax.ShapeDtypeStruct((n_rows * nt, LANES), h2t.dtype),
        compiler_params=pltpu.CompilerParams(dimension_semantics=("arbitrary",)),
        name="dispatch",
    )(plan, dest, h2t)


def _experts_kernel(nb, be_ref, nx_ref, nu_ref, xs_hbm, wg_hbm, wu_hbm, wd_hbm, y_hbm,
                    xbuf, ybuf, wg_st, wu_st, wd_st, wgu_b, wd_b, hid_s, slot_ref, wsems, isems, osems):
    ring = xbuf.shape[0]
    d, de = wg_st.shape[1], wg_st.shape[2]
    nt = d // LANES
    bs = ybuf.shape[1] // nt
    ntp = xbuf.shape[1] // bs
    nu = nu_ref[0]

    def block(ref, q):
        rows = bs * (ntp if ref is xs_hbm else nt)
        return ref.at[pl.ds(pl.multiple_of(q * rows, rows), rows), :]

    def in_copy(q):
        return pltpu.make_async_copy(block(xs_hbm, q), xbuf.at[q % ring], isems.at[q % ring])

    def out_copy(q):
        return pltpu.make_async_copy(ybuf.at[q % ring], block(y_hbm, q), osems.at[q % ring])

    def weight_copies(ex, slot):
        return (pltpu.make_async_copy(wg_hbm.at[ex], wg_st.at[slot], wsems.at[slot]),
                pltpu.make_async_copy(wu_hbm.at[ex], wu_st.at[slot], wsems.at[slot]),
                pltpu.make_async_copy(wd_hbm.at[ex], wd_st.at[slot], wsems.at[slot]))

    slot_ref[0] = 0
    slot_ref[1] = 0
    xbuf[...] = jnp.zeros_like(xbuf)
    hid_s[...] = jnp.zeros_like(hid_s)
    wd_b[...] = jnp.zeros_like(wd_b)
    for c in weight_copies(be_ref[0], 0):
        c.start(priority=WEIGHT_DMA_PRIORITY)
    for q in range(EXPERT_LOOKAHEAD):
        @pl.when(q < nu)
        def _():
            in_copy(q).start()

    def step(j, carry):
        live = j < nu
        e = be_ref[jnp.minimum(j, nu - 1)]
        prev_slot = slot_ref[1]

        @pl.when(live & ((j == 0) | (e != be_ref[jnp.maximum(j - 1, 0)])))
        def _():
            slot = slot_ref[0]
            for c in weight_copies(e, slot):
                c.wait()
            nx = nx_ref[j]

            @pl.when(nx >= 0)
            def _():
                for c in weight_copies(nx, 1 - slot):
                    c.start(priority=WEIGHT_DMA_PRIORITY)

            wgu_b[:, :de] = wg_st[slot].astype(BF16)
            wgu_b[:, de:] = wu_st[slot].astype(BF16)
            wd_b[slot] = wd_st[slot].astype(BF16)
            slot_ref[1] = slot
            slot_ref[0] = 1 - slot

        @pl.when(j + EXPERT_LOOKAHEAD < nu)
        def _():
            in_copy(j + EXPERT_LOOKAHEAD).start()

        @pl.when(live)
        def _():
            in_copy(j).wait()

        @pl.when(j > ring)
        def _():
            out_copy(j - 1 - ring).wait()

        xb = _unpack_bf16_pairs(_tiles_to_rows(xbuf.at[j % ring], bs, ntp))
        gu = jnp.dot(xb, wgu_b[...], preferred_element_type=F32)
        gate = gu[:, :de]
        hid = ((gate * _sigmoid(gate)) * gu[:, de:]).astype(BF16)
        y = jnp.dot(hid_s[(j + 1) % 2], wd_b[prev_slot], preferred_element_type=F32)
        hid_s[j % 2] = hid
        _rows_to_tiles(ybuf.at[(j + ring - 1) % ring], y, bs)

        @pl.when(j > 0)
        def _():
            out_copy(j - 1).start()

        return carry

    lax.fori_loop(0, nu + 1, step, 0)

    for q in range(1, ring + 1):
        @pl.when(nu - q >= 0)
        def _():
            out_copy(nu - q).wait()

    ybuf[0] = jnp.zeros(ybuf.shape[1:], ybuf.dtype)

    def zero_start(q, carry):
        pltpu.make_async_copy(ybuf.at[0], block(y_hbm, q), osems.at[0]).start()
        return carry

    def zero_wait(q, carry):
        pltpu.make_async_copy(ybuf.at[0], block(y_hbm, q), osems.at[0]).wait()
        return carry

    lax.fori_loop(nu, nb, zero_start, 0)
    lax.fori_loop(nu, nb, zero_wait, 0)


def _experts(blk_e, nxt_e, n_used, xs, w_gate, w_up, w_down):
    ne, d, de = w_gate.shape
    nt = d // LANES
    rows = EXPERT_ROWS * nt
    in_rows = rows // 2
    nb = xs.shape[0] // in_rows
    ring = EXPERT_LOOKAHEAD + 1
    any_spec = pl.BlockSpec(memory_space=pl.ANY)
    grid_spec = pltpu.PrefetchScalarGridSpec(
        num_scalar_prefetch=3,
        grid=(1,),
        in_specs=[any_spec, any_spec, any_spec, any_spec],
        out_specs=any_spec,
        scratch_shapes=[pltpu.VMEM((ring, in_rows, LANES), U32), pltpu.VMEM((ring, rows, LANES), F32),
                        pltpu.VMEM((2, d, de), F32), pltpu.VMEM((2, d, de), F32),
                        pltpu.VMEM((2, de, d), F32),
                        pltpu.VMEM((d, 2 * de), BF16), pltpu.VMEM((2, de, d), BF16),
                        pltpu.VMEM((2, EXPERT_ROWS, de), BF16),
                        pltpu.SMEM((2,), I32), pltpu.SemaphoreType.DMA((2,)),
                        pltpu.SemaphoreType.DMA((ring,)), pltpu.SemaphoreType.DMA((ring,))],
    )
    return pl.pallas_call(
        functools.partial(_experts_kernel, nb),
        grid_spec=grid_spec,
        out_shape=jax.ShapeDtypeStruct((nb * rows, LANES), F32),
        compiler_params=pltpu.CompilerParams(
            dimension_semantics=("arbitrary",), vmem_limit_bytes=VMEM_LIMIT),
        name="experts",
    )(blk_e, nxt_e, n_used, xs, w_gate, w_up, w_down)


def _combine_kernel(dcur_ref, dnxt_ref, x1_ref, gt_ref, y_ref, gfin_ref, o_ref, ybuf, sems):
    tc, d = x1_ref.shape
    nt = d // LANES
    i = pl.program_id(0)
    n = pl.num_programs(0)

    def issue_tile(dest_ref, slot):
        _for_each_assignment(
            dest_ref, tc,
            lambda k, t, row: _row_copy(y_ref, row, ybuf.at[slot, k], t, nt,
                                        sems.at[slot]).start(priority=k))

    @pl.when(i == 0)
    def _():
        issue_tile(dcur_ref, 0)

    @pl.when(i + 1 < n)
    def _():
        issue_tile(dnxt_ref, (i + 1) % 2)

    slot = i % 2
    _rows_wait(y_ref, TOP_K * tc, nt, sems.at[slot])
    y0 = _tiles_to_rows(ybuf.at[slot, 0], tc, nt)
    y1 = _tiles_to_rows(ybuf.at[slot, 1], tc, nt)
    gates = gt_ref[...]
    xo = x1_ref[...] + (gates[:, 0:1] * y0 + gates[:, 1:2] * y1)
    o_ref[...] = _rms(xo, gfin_ref[...])


def _combine(dest, x1, gates_t, y, g_final):
    t, d = x1.shape
    nt = d // LANES
    tc = COMBINE_TOKENS
    last = t // tc - 1
    return pl.pallas_call(
        _combine_kernel,
        grid=(t // tc,),
        in_specs=[
            pl.BlockSpec((TOP_K * tc,), lambda i: (i,), memory_space=pltpu.SMEM),
            pl.BlockSpec((TOP_K * tc,), lambda i: (jnp.minimum(i + 1, last),),
                         memory_space=pltpu.SMEM),
            pl.BlockSpec((tc, d), lambda i: (i, 0)),
            pl.BlockSpec((tc, LANES), lambda i: (i, 0)),
            pl.BlockSpec(memory_space=pl.ANY),
            pl.BlockSpec((1, d), lambda i: (0, 0)),
        ],
        out_specs=pl.BlockSpec((tc, d), lambda i: (i, 0)),
        out_shape=jax.ShapeDtypeStruct((t, d), F32),
        scratch_shapes=[pltpu.VMEM((2, TOP_K, tc * nt, LANES), F32), pltpu.SemaphoreType.DMA((2,))],
        compiler_params=pltpu.CompilerParams(dimension_semantics=("arbitrary",)),
        name="combine",
    )(dest, dest, x1, gates_t, y, g_final)


def _layer(x, g_mix, w_in, w_s, b_s, g_sgu, w_conv, w_out, g_ffn, w_rg, w_re, w_gate, w_up, w_down):
    b, s, d = x.shape
    t = b * s
    nt = d // LANES
    ne = w_gate.shape[0]
    bs = EXPERT_ROWS

    bias_full = jnp.repeat(b_s.T, d // N_GROUPS, axis=1)
    wr_t = jnp.concatenate([w_rg, w_re], axis=1).T
    wr_t = jnp.pad(wr_t, ((0, LANES - wr_t.shape[0]), (0, 0))).astype(BF16)
    x1, h2t, logits_t = _mixer(
        x, g_mix.reshape(1, d), w_in.astype(BF16), w_s.astype(BF16), bias_full, g_sgu.reshape(1, d),
        w_conv, w_out.astype(BF16), g_ffn.reshape(1, d), wr_t)

    info, gates_t, counts = _route(logits_t)

    n_rows = -(-(t * TOP_K + ne * (bs - 1)) // bs) * bs
    nb = n_rows // bs
    exp_tbl, blk_tbl = _plan(counts, nb)
    blk_e, nxt_e, n_used = blk_tbl[0, :nb], blk_tbl[1, :nb], blk_tbl[2, :1]

    dest = _place(info, exp_tbl)[:TOP_K]
    tiled = lambda n: dest.reshape(TOP_K, t // n, n).transpose(1, 0, 2).reshape(-1)
    plan = jnp.concatenate([exp_tbl[:, :3].reshape(-1), n_used])
    xs = _dispatch(plan, tiled(DISPATCH_TOKENS), h2t, n_rows)
    y = _experts(blk_e, nxt_e, n_used, xs, w_gate, w_up, w_down)
    return x1.reshape(t, d), tiled(COMBINE_TOKENS), gates_t, y


def kernel(x, g_mix, w_in, w_s, b_s, g_sgu, w_conv, w_out, g_ffn, w_router_group, w_router_expert,
           w_gate, w_up, w_down, g_final):
    b, s, d = x.shape
    depth = g_mix.shape[0]
    assert depth == 1, "the final RMSNorm is fused into the last layer's combine"
    assert s % MIX_ROWS == 0 and MIX_ROWS % CHUNK == 0 and d % LANES == 0
    assert EXPERT_ROWS & (EXPERT_ROWS - 1) == 0, "block bookkeeping uses shifts"
    assert all((b * s) % n == 0 for n in (ROUTE_TOKENS, PLACE_TOKENS, DISPATCH_TOKENS, COMBINE_TOKENS))
    l = 0
    x1, dest, gates_t, y = _layer(
        x, g_mix[l], w_in[l], w_s[l], b_s[l], g_sgu[l], w_conv[l], w_out[l], g_ffn[l],
        w_router_group[l], w_router_expert[l], w_gate[l], w_up[l], w_down[l])
    out = _combine(dest, x1, gates_t, y, g_final.reshape(1, d))
    return out.reshape(b, s, d)
```

```python
import functools

import jax
import jax.numpy as jnp
from jax import lax
from jax.experimental import pallas as pl
from jax.experimental.pallas import tpu as pltpu

F32 = jnp.float32
BF16 = jnp.bfloat16
I32 = jnp.int32
U32 = jnp.uint32

EPS = 1e-6
LANES = 128
SUBLANES = 8
CHUNK = 128
N_GROUPS = 8
EXPERTS_PER_GROUP = 8
TOP_K = 2
CONV_K = 3
N_BRANCH = 7

MIX_ROWS = 512
ROUTE_TOKENS = 512
PLACE_TOKENS = 2048
DISPATCH_TOKENS = 1024
COMBINE_TOKENS = 256
MOVE_UNROLL = 8
WEIGHT_DMA_PRIORITY = 1
EXPERT_ROWS = 128
EXPERT_LOOKAHEAD = 3
VMEM_LIMIT = 56 * 1024 * 1024


def _rms(x, g):
    return x * lax.rsqrt(jnp.mean(x * x, axis=-1, keepdims=True) + EPS) * g


def _sigmoid(x):
    return 0.5 * (1.0 + jnp.tanh(0.5 * x))


def _gelu_tanh(x):
    c = 0.7978845608028654
    return x * (0.5 * (1.0 + jnp.tanh(c * (x + 0.044715 * (x * x * x)))))


def _rows_to_tiles(dst_ref, val, rows):
    nt = val.shape[1] // LANES
    for c in range(nt):
        dst_ref[pl.ds(c, rows, stride=nt), :] = val[:, c * LANES:(c + 1) * LANES]


def _tiles_to_rows(src_ref, rows, nt):
    return jnp.concatenate([src_ref[pl.ds(c, rows, stride=nt), :] for c in range(nt)], axis=1)


def _pack_bf16_pairs(x):
    half = x.shape[1] // 2
    bits = pltpu.bitcast(x.astype(F32), U32)
    return lax.shift_right_logical(bits[:, :half], U32(16)) | bits[:, half:]


def _unpack_bf16_pairs(w, dtype):
    lo = pltpu.bitcast(lax.shift_left(w, U32(16)), F32)
    hi = pltpu.bitcast(w & U32(0xFFFF0000), F32)
    return jnp.concatenate([lo, hi], axis=1).astype(dtype)


def _mixer_kernel(x_ref, xp_ref, xn_ref, gmix_ref, win_ref, ws_ref, bias_ref, gsgu_ref, wconv_ref,
                  wout_ref, gffn_ref, wr_ref, x1_ref, h2t_ref, lt_ref, vg_ref, z_ref, acc_ref):
    ts, d = x_ref.shape[1], x_ref.shape[2]
    s = pl.program_id(1)
    ns = pl.num_programs(1)
    gw = d // N_GROUPS
    nc = ts // CHUNK

    x = x_ref[0]
    gmix = gmix_ref[...]
    h = _rms(x, gmix).astype(BF16)

    def proj(j, n=1):
        return jnp.dot(h, win_ref[:, j * d:(j + n) * d], preferred_element_type=F32)

    gv = _gelu_tanh(proj(1))
    for g in range(N_GROUPS):
        cs = slice(g * gw, (g + 1) * gw)
        blk = gv[:, cs]
        mu = jnp.mean(blk, axis=-1, keepdims=True)
        dv = blk - mu
        var = jnp.mean(dv * dv, axis=-1, keepdims=True)
        vg_ref[:, cs] = (dv * lax.rsqrt(var + EPS) * gsgu_ref[:, cs]).astype(BF16)
    for g in range(N_GROUPS):
        cs = slice(g * gw, (g + 1) * gw)
        vcat = jnp.concatenate([vg_ref[n * CHUNK:(n + 1) * CHUNK, cs] for n in range(nc)], axis=1)
        zg = jnp.dot(ws_ref[g], vcat, preferred_element_type=F32)
        for n in range(nc):
            z_ref[n * CHUNK:(n + 1) * CHUNK, cs] = zg[:, n * gw:(n + 1) * gw]
    u = _gelu_tanh(proj(0))
    ga = _sigmoid(proj(5))
    for n in range(nc):
        rs = slice(n * CHUNK, (n + 1) * CHUNK)
        acc_ref[rs, :] = ga[rs] * (u[rs] * (z_ref[rs, :] + bias_ref[...]))

    cx = proj(3, 2)
    z2 = cx[:, :d] * cx[:, d:]
    xh = jnp.concatenate([xp_ref[0], xn_ref[0]], axis=0)
    hh = _rms(xh, gmix).astype(BF16)
    cxh = jnp.dot(hh, win_ref[:, 3 * d:5 * d], preferred_element_type=F32)
    z2h = cxh[:, :d] * cxh[:, d:]
    prev = jnp.where(s > 0, z2h[SUBLANES - 1:SUBLANES, :], 0.0)
    nxt = jnp.where(s < ns - 1, z2h[SUBLANES:SUBLANES + 1, :], 0.0)
    row = lax.broadcasted_iota(I32, (ts, d), 0)
    zm1 = jnp.where(row == 0, prev, pltpu.roll(z2, 1, 0))
    zp1 = jnp.where(row == ts - 1, nxt, pltpu.roll(z2, ts - 1, 0))
    conv = wconv_ref[0:1, :] * zm1 + wconv_ref[1:2, :] * z2 + wconv_ref[2:3, :] * zp1
    cb = proj(2)
    gb = _sigmoid(proj(6))
    merged = acc_ref[...] + gb * (cb * conv)

    x1 = x + jnp.dot(merged.astype(BF16), wout_ref[...], preferred_element_type=F32)
    x1_ref[0] = x1

    h2 = _rms(x1, gffn_ref[...]).astype(BF16)
    lt_ref[...] = lax.dot_general(wr_ref[...], h2, (((1,), (1,)), ((), ())),
                                  preferred_element_type=F32)
    _rows_to_tiles(h2t_ref, _pack_bf16_pairs(h2), ts)


def _mixer(x, g_mix, w_in_b, w_s_b, bias_full, g_sgu, w_conv, w_out_b, g_ffn, wr_t):
    b, s, d = x.shape
    ts = MIX_ROWS
    ns = s // ts
    t = b * s
    ntp = d // (2 * LANES)
    hb = ts // SUBLANES
    last_hb = s // SUBLANES - 1

    const = lambda *shape: pl.BlockSpec(shape, lambda bi, si: (0,) * len(shape))
    in_specs = [
        pl.BlockSpec((1, ts, d), lambda bi, si: (bi, si, 0)),
        pl.BlockSpec((1, SUBLANES, d), lambda bi, si: (bi, jnp.maximum(si * hb - 1, 0), 0)),
        pl.BlockSpec((1, SUBLANES, d), lambda bi, si: (bi, jnp.minimum((si + 1) * hb, last_hb), 0)),
        const(1, d),
        pl.BlockSpec((d, N_BRANCH * d), lambda bi, si: (0, 0), pipeline_mode=pl.Buffered(1)),
        const(N_GROUPS, CHUNK, CHUNK),
        const(CHUNK, d),
        const(1, d),
        const(CONV_K, d),
        const(d, d),
        const(1, d),
        const(LANES, d),
    ]
    out_specs = [
        pl.BlockSpec((1, ts, d), lambda bi, si: (bi, si, 0)),
        pl.BlockSpec((ts * ntp, LANES), lambda bi, si: (bi * ns + si, 0)),
        pl.BlockSpec((LANES, ts), lambda bi, si: (0, bi * ns + si)),
    ]
    out_shape = [
        jax.ShapeDtypeStruct((b, s, d), F32),
        jax.ShapeDtypeStruct((t * ntp, LANES), U32),
        jax.ShapeDtypeStruct((LANES, t), F32),
    ]
    return pl.pallas_call(
        _mixer_kernel,
        grid=(b, ns),
        in_specs=in_specs,
        out_specs=out_specs,
        out_shape=out_shape,
        scratch_shapes=[pltpu.VMEM((ts, d), BF16), pltpu.VMEM((ts, d), F32), pltpu.VMEM((ts, d), F32)],
        compiler_params=pltpu.CompilerParams(
            dimension_semantics=("arbitrary", "arbitrary"), vmem_limit_bytes=VMEM_LIMIT),
        name="mixer",
    )(x, x, x, g_mix, w_in_b, w_s_b, bias_full, g_sgu, w_conv, w_out_b, g_ffn, wr_t)


def _route_kernel(lt_ref, info_ref, gt_ref, cnt_ref, carry_ref):
    tb = lt_ref.shape[1]
    ne = N_GROUPS * EXPERTS_PER_GROUP

    @pl.when(pl.program_id(0) == 0)
    def _():
        carry_ref[...] = jnp.zeros_like(carry_ref)

    row8 = lax.broadcasted_iota(I32, (SUBLANES, tb), 0)
    gl = lt_ref[0:N_GROUPS, :]
    gmax = jnp.max(gl, axis=0, keepdims=True)
    gidx = jnp.min(jnp.where(gl == gmax, row8, N_GROUPS), axis=0, keepdims=True)
    pg = 1.0 / jnp.sum(jnp.exp(gl - gmax), axis=0, keepdims=True)

    sel = jnp.zeros((EXPERTS_PER_GROUP, tb), F32)
    for g in range(N_GROUPS):
        lo = N_GROUPS + g * EXPERTS_PER_GROUP
        sel = jnp.where(gidx == g, lt_ref[lo:lo + EXPERTS_PER_GROUP, :], sel)
    m1 = jnp.max(sel, axis=0, keepdims=True)
    i1 = jnp.min(jnp.where(sel == m1, row8, EXPERTS_PER_GROUP), axis=0, keepdims=True)
    sel2 = jnp.where(row8 == i1, -jnp.inf, sel)
    m2 = jnp.max(sel2, axis=0, keepdims=True)
    i2 = jnp.min(jnp.where(sel2 == m2, row8, EXPERTS_PER_GROUP), axis=0, keepdims=True)
    e2 = jnp.exp(m2 - m1)
    den = 1.0 + e2
    gate0 = pg * (1.0 / den)
    gate1 = pg * (e2 / den)
    eid0 = gidx * EXPERTS_PER_GROUP + i1
    eid1 = gidx * EXPERTS_PER_GROUP + i2

    rowe = lax.broadcasted_iota(I32, (ne, tb), 0)
    hit0 = rowe == eid0
    hit1 = rowe == eid1
    onehot = jnp.where(hit0 | hit1, 1.0, 0.0)
    before = (lax.broadcasted_iota(I32, (tb, tb), 0) < lax.broadcasted_iota(I32, (tb, tb), 1))
    prefix = jnp.dot(onehot.astype(BF16), jnp.where(before, 1.0, 0.0).astype(BF16),
                     preferred_element_type=F32)
    base = prefix + carry_ref[:, 0:1]
    rank0 = jnp.sum(jnp.where(hit0, base, 0.0), axis=0, keepdims=True).astype(I32)
    rank1 = jnp.sum(jnp.where(hit1, base, 0.0), axis=0, keepdims=True).astype(I32)
    carry_ref[...] = carry_ref[...] + jnp.sum(onehot, axis=1, keepdims=True)
    cnt_ref[...] = carry_ref[...].astype(I32)

    info_ref[...] = jnp.where(row8 == 0, eid0, jnp.where(row8 == 1, eid1,
                              jnp.where(row8 == 2, rank0, jnp.where(row8 == 3, rank1, 0))))
    rowl = lax.broadcasted_iota(I32, (LANES, tb), 0)
    gates = jnp.where(rowl == 0, gate0, jnp.where(rowl == 1, gate1, 0.0))
    gt_ref[...] = gates.T


def _route(logits_t):
    t = logits_t.shape[1]
    tb = ROUTE_TOKENS
    ne = N_GROUPS * EXPERTS_PER_GROUP
    return pl.pallas_call(
        _route_kernel,
        grid=(t // tb,),
        in_specs=[pl.BlockSpec((LANES, tb), lambda i: (0, i))],
        out_specs=[
            pl.BlockSpec((SUBLANES, tb), lambda i: (0, i)),
            pl.BlockSpec((tb, LANES), lambda i: (i, 0)),
            pl.BlockSpec((ne, LANES), lambda i: (0, 0)),
        ],
        out_shape=[
            jax.ShapeDtypeStruct((SUBLANES, t), I32),
            jax.ShapeDtypeStruct((t, LANES), F32),
            jax.ShapeDtypeStruct((ne, LANES), I32),
        ],
        scratch_shapes=[pltpu.VMEM((ne, LANES), F32)],
        compiler_params=pltpu.CompilerParams(dimension_semantics=("arbitrary",)),
        name="route",
    )(logits_t)


def _plan_kernel(cnt_ref, exp_ref, blk_ref):
    ne = cnt_ref.shape[0]
    nbp = blk_ref.shape[1]
    shift = EXPERT_ROWS.bit_length() - 1
    pad_rows = lambda c: lax.shift_left(lax.shift_right_logical(c + (EXPERT_ROWS - 1), shift), shift)
    cnt = cnt_ref[...]
    padded = pad_rows(cnt)
    padded_lanes = pad_rows(cnt.astype(F32).T[:ne, :ne].astype(I32))
    e_sub = lax.broadcasted_iota(I32, (ne, ne), 0)
    e_lane = lax.broadcasted_iota(I32, (ne, ne), 1)
    pend = jnp.sum(jnp.where(e_lane <= e_sub, padded_lanes, 0), axis=1, keepdims=True)
    pstart = pend - padded[:, 0:1]

    lane = lax.broadcasted_iota(I32, (ne, LANES), 1)
    exp_ref[...] = jnp.where(lane == 0, pstart, jnp.where(lane == 1, cnt, jnp.where(lane == 2, padded, 0)))

    first_row = lax.broadcasted_iota(I32, (ne, nbp), 1) * EXPERT_ROWS
    e_col = lax.broadcasted_iota(I32, (ne, nbp), 0)
    blk_e = jnp.minimum(jnp.sum(jnp.where(pend <= first_row, 1, 0), axis=0, keepdims=True), ne - 1)
    later = jnp.where((e_col > blk_e) & (padded[:, 0:1] > 0), e_col, ne)
    nxt_e = jnp.min(later, axis=0, keepdims=True)
    nxt_e = jnp.where(nxt_e == ne, -1, nxt_e)
    n_used = lax.shift_right_logical(jnp.max(pend, axis=0, keepdims=True), shift)
    row8 = lax.broadcasted_iota(I32, (SUBLANES, nbp), 0)
    blk_ref[...] = jnp.where(row8 == 0, blk_e, jnp.where(row8 == 1, nxt_e, jnp.where(row8 == 2, n_used, 0)))


def _plan(counts, nb):
    ne = counts.shape[0]
    nbp = -(-nb // LANES) * LANES
    return pl.pallas_call(
        _plan_kernel,
        out_shape=[jax.ShapeDtypeStruct((ne, LANES), I32), jax.ShapeDtypeStruct((SUBLANES, nbp), I32)],
        name="plan",
    )(counts)


def _place_kernel(info_ref, ps_ref, dest_ref):
    tb = info_ref.shape[1]
    ne = ps_ref.shape[0]
    rowe = lax.broadcasted_iota(I32, (ne, tb), 0)
    row8 = lax.broadcasted_iota(I32, (SUBLANES, tb), 0)
    ps = ps_ref[:, 0:1]
    dest = jnp.zeros((SUBLANES, tb), I32)
    for k in range(TOP_K):
        start = jnp.sum(jnp.where(rowe == info_ref[k:k + 1, :], ps, 0), axis=0, keepdims=True)
        dest = jnp.where(row8 == k, start + info_ref[TOP_K + k:TOP_K + k + 1, :], dest)
    dest_ref[...] = dest


def _place(info, exp_tbl):
    t = info.shape[1]
    tb = PLACE_TOKENS
    ne = exp_tbl.shape[0]
    return pl.pallas_call(
        _place_kernel,
        grid=(t // tb,),
        in_specs=[pl.BlockSpec((SUBLANES, tb), lambda i: (0, i)),
                  pl.BlockSpec((ne, LANES), lambda i: (0, 0))],
        out_specs=pl.BlockSpec((SUBLANES, tb), lambda i: (0, i)),
        out_shape=jax.ShapeDtypeStruct((SUBLANES, t), I32),
        compiler_params=pltpu.CompilerParams(dimension_semantics=("arbitrary",)),
        name="place",
    )(info, exp_tbl)


def _row_copy(src_ref, src_row, dst_ref, dst_row, nt, sem):
    first = lambda row: row * nt if isinstance(row, int) else pl.multiple_of(row * nt, nt)
    return pltpu.make_async_copy(
        src_ref.at[pl.ds(first(src_row), nt), :], dst_ref.at[pl.ds(first(dst_row), nt), :], sem)


def _rows_wait(ref, n_rows, nt, sem):
    pltpu.make_async_copy(ref.at[pl.ds(0, n_rows * nt), :], ref.at[pl.ds(0, n_rows * nt), :], sem).wait()


def _for_each_assignment(dest_ref, n_tok, start_copy, inline=False):
    def group(g, c):
        t0 = g * MOVE_UNROLL
        rows = [[dest_ref[k * n_tok + t0 + u] for k in range(TOP_K)] for u in range(MOVE_UNROLL)]
        for u in range(MOVE_UNROLL):
            for k in range(TOP_K):
                start_copy(k, t0 + u, rows[u][k])
        return c

    if inline:
        for g in range(n_tok // MOVE_UNROLL):
            group(g, 0)
    else:
        lax.fori_loop(0, n_tok // MOVE_UNROLL, group, 0)


def _dispatch_kernel(nt, nb, plan_ref, dest_ref, h2_ref, xs_ref, zbuf, sem, zsem):
    td = h2_ref.shape[0] // nt
    ne = (plan_ref.shape[0] - 1) // 3
    n_used = plan_ref[3 * ne]
    blk_rows = zbuf.shape[0]

    def zero_rows(start_not_wait):
        def fire(c):
            c.start() if start_not_wait else c.wait()

        def expert(e, carry):
            cnt = plan_ref[3 * e + 1]
            row = plan_ref[3 * e] + cnt
            n = plan_ref[3 * e + 2] - cnt
            p = EXPERT_ROWS // 2
            while p >= 1:
                has = (n & p) != 0
                r, sz = row, p

                @pl.when(has)
                def _():
                    fire(pltpu.make_async_copy(
                        zbuf.at[pl.ds(0, sz * nt), :],
                        xs_ref.at[pl.ds(pl.multiple_of(r * nt, nt), sz * nt), :], zsem))

                row = row + jnp.where(has, p, 0)
                p //= 2
            return carry

        lax.fori_loop(0, ne, expert, 0)

        def tail(q, carry):
            fire(pltpu.make_async_copy(
                zbuf, xs_ref.at[pl.ds(pl.multiple_of(q * blk_rows, blk_rows), blk_rows), :], zsem))
            return carry

        lax.fori_loop(n_used, nb, tail, 0)

    @pl.when(pl.program_id(0) == 0)
    def _():
        zbuf[...] = jnp.zeros_like(zbuf)
        zero_rows(True)

    _for_each_assignment(
        dest_ref, td,
        lambda k, t, row: _row_copy(h2_ref, t, xs_ref, row, nt, sem).start(priority=k))
    _rows_wait(xs_ref, TOP_K * td, nt, sem)

    @pl.when(pl.program_id(0) == 0)
    def _():
        zero_rows(False)


def _dispatch(plan, dest, h2t, n_rows):
    td = DISPATCH_TOKENS
    t = dest.shape[0] // TOP_K
    nt = h2t.shape[0] // t
    nb = n_rows // EXPERT_ROWS
    grid_spec = pltpu.PrefetchScalarGridSpec(
        num_scalar_prefetch=1,
        grid=(t // td,),
        in_specs=[
            pl.BlockSpec((TOP_K * td,), lambda i, plan: (i,), memory_space=pltpu.SMEM),
            pl.BlockSpec((td * nt, LANES), lambda i, plan: (i, 0)),
        ],
        out_specs=pl.BlockSpec(memory_space=pl.ANY),
        scratch_shapes=[pltpu.VMEM((EXPERT_ROWS * nt, LANES), h2t.dtype),
                        pltpu.SemaphoreType.DMA, pltpu.SemaphoreType.DMA],
    )
    return pl.pallas_call(
        functools.partial(_dispatch_kernel, nt, nb),
        grid_spec=grid_spec,
        out_shape=jax.ShapeDtypeStruct((n_rows * nt, LANES), h2t.dtype),
        compiler_params=pltpu.CompilerParams(dimension_semantics=("arbitrary",)),
        name="dispatch",
    )(plan, dest, h2t)


def _experts_kernel(nb, be_ref, nx_ref, nu_ref, xs_hbm, wg_hbm, wu_hbm, wd_hbm, y_hbm,
                    xbuf, ybuf, wg_st, wu_st, wd_st, wgu_b, wd_b, hid_s, slot_ref, wsems, isems, osems):
    ring = xbuf.shape[0]
    d, de = wg_st.shape[1], wg_st.shape[2]
    nt = d // LANES
    bs = EXPERT_ROWS
    ntp = xbuf.shape[1] // bs
    nu = nu_ref[0]

    def block(ref, q):
        rows = bs * ntp
        return ref.at[pl.ds(pl.multiple_of(q * rows, rows), rows), :]

    def in_copy(q):
        return pltpu.make_async_copy(block(xs_hbm, q), xbuf.at[q % ring], isems.at[q % ring])

    def out_copy(q):
        return pltpu.make_async_copy(ybuf.at[q % ring], block(y_hbm, q), osems.at[q % ring])

    def weight_copies(ex, slot):
        return (pltpu.make_async_copy(wg_hbm.at[ex], wg_st.at[slot], wsems.at[slot]),
                pltpu.make_async_copy(wu_hbm.at[ex], wu_st.at[slot], wsems.at[slot]),
                pltpu.make_async_copy(wd_hbm.at[ex], wd_st.at[slot], wsems.at[slot]))

    slot_ref[0] = 0
    slot_ref[1] = 0
    xbuf[...] = jnp.zeros_like(xbuf)
    hid_s[...] = jnp.zeros_like(hid_s)
    wd_b[...] = jnp.zeros_like(wd_b)
    for c in weight_copies(be_ref[0], 0):
        c.start(priority=WEIGHT_DMA_PRIORITY)
    for q in range(EXPERT_LOOKAHEAD):
        @pl.when(q < nu)
        def _():
            in_copy(q).start()

    def step(j, carry):
        live = j < nu
        e = be_ref[jnp.minimum(j, nu - 1)]
        prev_slot = slot_ref[1]

        @pl.when(live & ((j == 0) | (e != be_ref[jnp.maximum(j - 1, 0)])))
        def _():
            slot = slot_ref[0]
            for c in weight_copies(e, slot):
                c.wait()
            nx = nx_ref[j]

            @pl.when(nx >= 0)
            def _():
                for c in weight_copies(nx, 1 - slot):
                    c.start(priority=WEIGHT_DMA_PRIORITY)

            wgu_b[:, :de] = wg_st[slot].astype(BF16)
            wgu_b[:, de:] = wu_st[slot].astype(BF16)
            wd_b[slot] = wd_st[slot].astype(BF16)
            slot_ref[1] = slot
            slot_ref[0] = 1 - slot

        @pl.when(j + EXPERT_LOOKAHEAD < nu)
        def _():
            in_copy(j + EXPERT_LOOKAHEAD).start()

        @pl.when(live)
        def _():
            in_copy(j).wait()

        @pl.when(j > ring)
        def _():
            out_copy(j - 1 - ring).wait()

        xb = _unpack_bf16_pairs(_tiles_to_rows(xbuf.at[j % ring], bs, ntp), BF16)
        gu = jnp.dot(xb, wgu_b[...], preferred_element_type=F32)
        gate = gu[:, :de]
        hid = ((gate * _sigmoid(gate)) * gu[:, de:]).astype(BF16)
        y = jnp.dot(hid_s[(j + 1) % 2], wd_b[prev_slot], preferred_element_type=F32)
        hid_s[j % 2] = hid
        _rows_to_tiles(ybuf.at[(j + ring - 1) % ring], _pack_bf16_pairs(y.astype(BF16)), bs)

        @pl.when(j > 0)
        def _():
            out_copy(j - 1).start()

        return carry

    lax.fori_loop(0, nu + 1, step, 0)

    for q in range(1, ring + 1):
        @pl.when(nu - q >= 0)
        def _():
            out_copy(nu - q).wait()

    ybuf[0] = jnp.zeros(ybuf.shape[1:], ybuf.dtype)

    def zero_start(q, carry):
        pltpu.make_async_copy(ybuf.at[0], block(y_hbm, q), osems.at[0]).start()
        return carry

    def zero_wait(q, carry):
        pltpu.make_async_copy(ybuf.at[0], block(y_hbm, q), osems.at[0]).wait()
        return carry

    lax.fori_loop(nu, nb, zero_start, 0)
    lax.fori_loop(nu, nb, zero_wait, 0)


def _experts(blk_e, nxt_e, n_used, xs, w_gate, w_up, w_down):
    ne, d, de = w_gate.shape
    nt = d // LANES
    rows = EXPERT_ROWS * nt
    in_rows = rows // 2
    nb = xs.shape[0] // in_rows
    ring = EXPERT_LOOKAHEAD + 1
    any_spec = pl.BlockSpec(memory_space=pl.ANY)
    grid_spec = pltpu.PrefetchScalarGridSpec(
        num_scalar_prefetch=3,
        grid=(1,),
        in_specs=[any_spec, any_spec, any_spec, any_spec],
        out_specs=any_spec,
        scratch_shapes=[pltpu.VMEM((ring, in_rows, LANES), U32), pltpu.VMEM((ring, in_rows, LANES), U32),
                        pltpu.VMEM((2, d, de), F32), pltpu.VMEM((2, d, de), F32),
                        pltpu.VMEM((2, de, d), F32),
                        pltpu.VMEM((d, 2 * de), BF16), pltpu.VMEM((2, de, d), BF16),
                        pltpu.VMEM((2, EXPERT_ROWS, de), BF16),
                        pltpu.SMEM((2,), I32), pltpu.SemaphoreType.DMA((2,)),
                        pltpu.SemaphoreType.DMA((ring,)), pltpu.SemaphoreType.DMA((ring,))],
    )
    return pl.pallas_call(
        functools.partial(_experts_kernel, nb),
        grid_spec=grid_spec,
        out_shape=jax.ShapeDtypeStruct((nb * in_rows, LANES), U32),
        compiler_params=pltpu.CompilerParams(
            dimension_semantics=("arbitrary",), vmem_limit_bytes=VMEM_LIMIT),
        name="experts",
    )(blk_e, nxt_e, n_used, xs, w_gate, w_up, w_down)


def _combine_kernel(d0_ref, d1_ref, d2_ref, x1_ref, gt_ref, y_ref, gfin_ref, o_ref, ybuf, sems):
    tc, d = x1_ref.shape
    nt = ybuf.shape[2] // tc
    i = pl.program_id(0)
    n = pl.num_programs(0)
    slots = ybuf.shape[0]

    def issue_tile(dest_ref, slot, inline):
        _for_each_assignment(
            dest_ref, tc,
            lambda k, t, row: _row_copy(y_ref, row, ybuf.at[slot, k], t, nt,
                                        sems.at[slot]).start(priority=k),
            inline=inline)

    @pl.when(i == 0)
    def _():
        issue_tile(d0_ref, 0, False)
        issue_tile(d1_ref, 1, False)

    slot = i % slots
    _rows_wait(y_ref, TOP_K * tc, nt, sems.at[slot])
    issue_tile(d2_ref, (i + 2) % slots, True)
    y0 = _unpack_bf16_pairs(_tiles_to_rows(ybuf.at[slot, 0], tc, nt), F32)
    y1 = _unpack_bf16_pairs(_tiles_to_rows(ybuf.at[slot, 1], tc, nt), F32)
    gates = gt_ref[...]
    xo = x1_ref[...] + (gates[:, 0:1] * y0 + gates[:, 1:2] * y1)
    o_ref[...] = _rms(xo, gfin_ref[...])

    @pl.when(i == n - 1)
    def _():
        _rows_wait(y_ref, TOP_K * tc, nt, sems.at[(i + 1) % slots])
        _rows_wait(y_ref, TOP_K * tc, nt, sems.at[(i + 2) % slots])


def _combine(dest, x1, gates_t, y, g_final):
    t, d = x1.shape
    nt = d // LANES
    tc = COMBINE_TOKENS
    last = t // tc - 1
    assert last >= 1
    dest_spec = lambda ahead: pl.BlockSpec(
        (TOP_K * tc,), lambda i: (jnp.minimum(i + ahead, last),), memory_space=pltpu.SMEM)
    return pl.pallas_call(
        _combine_kernel,
        grid=(t // tc,),
        in_specs=[
            dest_spec(0), dest_spec(1), dest_spec(2),
            pl.BlockSpec((tc, d), lambda i: (i, 0)),
            pl.BlockSpec((tc, LANES), lambda i: (i, 0)),
            pl.BlockSpec(memory_space=pl.ANY),
            pl.BlockSpec((1, d), lambda i: (0, 0)),
        ],
        out_specs=pl.BlockSpec((tc, d), lambda i: (i, 0)),
        out_shape=jax.ShapeDtypeStruct((t, d), F32),
        scratch_shapes=[pltpu.VMEM((3, TOP_K, tc * nt // 2, LANES), U32),
                        pltpu.SemaphoreType.DMA((3,))],
        compiler_params=pltpu.CompilerParams(dimension_semantics=("arbitrary",)),
        name="combine",
    )(dest, dest, dest, x1, gates_t, y, g_final)


def _layer(x, g_mix, w_in, w_s, b_s, g_sgu, w_conv, w_out, g_ffn, w_rg, w_re, w_gate, w_up, w_down):
    b, s, d = x.shape
    t = b * s
    nt = d // LANES
    ne = w_gate.shape[0]
    bs = EXPERT_ROWS

    bias_full = jnp.repeat(b_s.T, d // N_GROUPS, axis=1)
    wr_t = jnp.concatenate([w_rg, w_re], axis=1).T
    wr_t = jnp.pad(wr_t, ((0, LANES - wr_t.shape[0]), (0, 0))).astype(BF16)
    x1, h2t, logits_t = _mixer(
        x, g_mix.reshape(1, d), w_in.astype(BF16), w_s.astype(BF16), bias_full, g_sgu.reshape(1, d),
        w_conv, w_out.astype(BF16), g_ffn.reshape(1, d), wr_t)

    info, gates_t, counts = _route(logits_t)

    n_rows = -(-(t * TOP_K + ne * (bs - 1)) // bs) * bs
    nb = n_rows // bs
    exp_tbl, blk_tbl = _plan(counts, nb)
    blk_e, nxt_e, n_used = blk_tbl[0, :nb], blk_tbl[1, :nb], blk_tbl[2, :1]

    dest = _place(info, exp_tbl)[:TOP_K]
    tiled = lambda n: dest.reshape(TOP_K, t // n, n).transpose(1, 0, 2).reshape(-1)
    plan = jnp.concatenate([exp_tbl[:, :3].reshape(-1), n_used])
    xs = _dispatch(plan, tiled(DISPATCH_TOKENS), h2t, n_rows)
    y = _experts(blk_e, nxt_e, n_used, xs, w_gate, w_up, w_down)
    return x1.reshape(t, d), tiled(COMBINE_TOKENS), gates_t, y


def kernel(x, g_mix, w_in, w_s, b_s, g_sgu, w_conv, w_out, g_ffn, w_router_group, w_router_expert,
           w_gate, w_up, w_down, g_final):
    b, s, d = x.shape
    depth = g_mix.shape[0]
    assert depth == 1, "the final RMSNorm is fused into the last layer's combine"
    assert s % MIX_ROWS == 0 and MIX_ROWS % CHUNK == 0 and d % LANES == 0
    assert EXPERT_ROWS & (EXPERT_ROWS - 1) == 0, "block bookkeeping uses shifts"
    assert all((b * s) % n == 0 for n in (ROUTE_TOKENS, PLACE_TOKENS, DISPATCH_TOKENS, COMBINE_TOKENS))
    l = 0
    x1, dest, gates_t, y = _layer(
        x, g_mix[l], w_in[l], w_s[l], b_s[l], g_sgu[l], w_conv[l], w_out[l], g_ffn[l],
        w_router_group[l], w_router_expert[l], w_gate[l], w_up[l], w_down[l])
    out = _combine(dest, x1, gates_t, y, g_final.reshape(1, d))
    return out.reshape(b, s, d)
```

```python
import functools

import jax
import jax.numpy as jnp
from jax import lax
from jax.experimental import pallas as pl
from jax.experimental.pallas import tpu as pltpu

F32 = jnp.float32
BF16 = jnp.bfloat16
I32 = jnp.int32
U32 = jnp.uint32

EPS = 1e-6
LANES = 128
SUBLANES = 8
CHUNK = 128
N_GROUPS = 8
EXPERTS_PER_GROUP = 8
TOP_K = 2
CONV_K = 3
N_BRANCH = 7

MIX_ROWS = 512
ROUTE_TOKENS = 512
PLACE_TOKENS = 2048
DISPATCH_TOKENS = 1024
COMBINE_TOKENS = 256
MOVE_UNROLL = 8
WEIGHT_STAGES = 3
WEIGHT_DMA_PRIORITIES = (1, 1, 0)
EXPERT_ROWS = 128
EXPERT_LOOKAHEAD = 3
VMEM_LIMIT = 56 * 1024 * 1024


def _rms(x, g):
    return x * lax.rsqrt(jnp.mean(x * x, axis=-1, keepdims=True) + EPS) * g


def _sigmoid(x):
    return 0.5 * (1.0 + jnp.tanh(0.5 * x))


def _gelu_tanh(x):
    c = 0.7978845608028654
    return x * (0.5 * (1.0 + jnp.tanh(c * (x + 0.044715 * (x * x * x)))))


def _rows_to_tiles(dst_ref, val, rows):
    nt = val.shape[1] // LANES
    for c in range(nt):
        dst_ref[pl.ds(c, rows, stride=nt), :] = val[:, c * LANES:(c + 1) * LANES]


def _tiles_to_rows(src_ref, rows, nt):
    return jnp.concatenate([src_ref[pl.ds(c, rows, stride=nt), :] for c in range(nt)], axis=1)


def _pack_bf16_pairs(x):
    half = x.shape[1] // 2
    bits = pltpu.bitcast(x.astype(F32), U32)
    return lax.shift_right_logical(bits[:, :half], U32(16)) | bits[:, half:]


def _unpack_bf16_pairs(w, dtype):
    lo = pltpu.bitcast(lax.shift_left(w, U32(16)), F32)
    hi = pltpu.bitcast(w & U32(0xFFFF0000), F32)
    return jnp.concatenate([lo, hi], axis=1).astype(dtype)


def _mixer_kernel(x_ref, xp_ref, xn_ref, gmix_ref, win_ref, ws_ref, bias_ref, gsgu_ref, wconv_ref,
                  wout_ref, gffn_ref, wr_ref, x1_ref, h2t_ref, lt_ref, vg_ref, z_ref, acc_ref):
    ts, d = x_ref.shape[1], x_ref.shape[2]
    s = pl.program_id(1)
    ns = pl.num_programs(1)
    gw = d // N_GROUPS
    nc = ts // CHUNK

    x = x_ref[0]
    gmix = gmix_ref[...]
    h = _rms(x, gmix).astype(BF16)

    def proj(j, n=1):
        return jnp.dot(h, win_ref[:, j * d:(j + n) * d], preferred_element_type=F32)

    gv = _gelu_tanh(proj(1))
    for g in range(N_GROUPS):
        cs = slice(g * gw, (g + 1) * gw)
        blk = gv[:, cs]
        mu = jnp.mean(blk, axis=-1, keepdims=True)
        dv = blk - mu
        var = jnp.mean(dv * dv, axis=-1, keepdims=True)
        vg_ref[:, cs] = (dv * lax.rsqrt(var + EPS) * gsgu_ref[:, cs]).astype(BF16)
    for g in range(N_GROUPS):
        cs = slice(g * gw, (g + 1) * gw)
        vcat = jnp.concatenate([vg_ref[n * CHUNK:(n + 1) * CHUNK, cs] for n in range(nc)], axis=1)
        zg = jnp.dot(ws_ref[g], vcat, preferred_element_type=F32)
        for n in range(nc):
            z_ref[n * CHUNK:(n + 1) * CHUNK, cs] = zg[:, n * gw:(n + 1) * gw]
    u = _gelu_tanh(proj(0))
    ga = _sigmoid(proj(5))
    for n in range(nc):
        rs = slice(n * CHUNK, (n + 1) * CHUNK)
        acc_ref[rs, :] = ga[rs] * (u[rs] * (z_ref[rs, :] + bias_ref[...]))

    cx = proj(3, 2)
    z2 = cx[:, :d] * cx[:, d:]
    xh = jnp.concatenate([xp_ref[0], xn_ref[0]], axis=0)
    hh = _rms(xh, gmix).astype(BF16)
    cxh = jnp.dot(hh, win_ref[:, 3 * d:5 * d], preferred_element_type=F32)
    z2h = cxh[:, :d] * cxh[:, d:]
    prev = jnp.where(s > 0, z2h[SUBLANES - 1:SUBLANES, :], 0.0)
    nxt = jnp.where(s < ns - 1, z2h[SUBLANES:SUBLANES + 1, :], 0.0)
    row = lax.broadcasted_iota(I32, (ts, d), 0)
    zm1 = jnp.where(row == 0, prev, pltpu.roll(z2, 1, 0))
    zp1 = jnp.where(row == ts - 1, nxt, pltpu.roll(z2, ts - 1, 0))
    conv = wconv_ref[0:1, :] * zm1 + wconv_ref[1:2, :] * z2 + wconv_ref[2:3, :] * zp1
    cb = proj(2)
    gb = _sigmoid(proj(6))
    merged = acc_ref[...] + gb * (cb * conv)

    x1 = x + jnp.dot(merged.astype(BF16), wout_ref[...], preferred_element_type=F32)
    x1_ref[0] = x1

    h2 = _rms(x1, gffn_ref[...]).astype(BF16)
    lt_ref[...] = lax.dot_general(wr_ref[...], h2, (((1,), (1,)), ((), ())),
                                  preferred_element_type=F32)
    _rows_to_tiles(h2t_ref, _pack_bf16_pairs(h2), ts)


def _mixer(x, g_mix, w_in_b, w_s_b, bias_full, g_sgu, w_conv, w_out_b, g_ffn, wr_t):
    b, s, d = x.shape
    ts = MIX_ROWS
    ns = s // ts
    t = b * s
    ntp = d // (2 * LANES)
    hb = ts // SUBLANES
    last_hb = s // SUBLANES - 1

    const = lambda *shape: pl.BlockSpec(shape, lambda bi, si: (0,) * len(shape))
    in_specs = [
        pl.BlockSpec((1, ts, d), lambda bi, si: (bi, si, 0)),
        pl.BlockSpec((1, SUBLANES, d), lambda bi, si: (bi, jnp.maximum(si * hb - 1, 0), 0)),
        pl.BlockSpec((1, SUBLANES, d), lambda bi, si: (bi, jnp.minimum((si + 1) * hb, last_hb), 0)),
        const(1, d),
        pl.BlockSpec((d, N_BRANCH * d), lambda bi, si: (0, 0), pipeline_mode=pl.Buffered(1)),
        const(N_GROUPS, CHUNK, CHUNK),
        const(CHUNK, d),
        const(1, d),
        const(CONV_K, d),
        const(d, d),
        const(1, d),
        const(LANES, d),
    ]
    out_specs = [
        pl.BlockSpec((1, ts, d), lambda bi, si: (bi, si, 0)),
        pl.BlockSpec((ts * ntp, LANES), lambda bi, si: (bi * ns + si, 0)),
        pl.BlockSpec((LANES, ts), lambda bi, si: (0, bi * ns + si)),
    ]
    out_shape = [
        jax.ShapeDtypeStruct((b, s, d), F32),
        jax.ShapeDtypeStruct((t * ntp, LANES), U32),
        jax.ShapeDtypeStruct((LANES, t), F32),
    ]
    return pl.pallas_call(
        _mixer_kernel,
        grid=(b, ns),
        in_specs=in_specs,
        out_specs=out_specs,
        out_shape=out_shape,
        scratch_shapes=[pltpu.VMEM((ts, d), BF16), pltpu.VMEM((ts, d), F32), pltpu.VMEM((ts, d), F32)],
        compiler_params=pltpu.CompilerParams(
            dimension_semantics=("arbitrary", "arbitrary"), vmem_limit_bytes=VMEM_LIMIT),
        name="mixer",
    )(x, x, x, g_mix, w_in_b, w_s_b, bias_full, g_sgu, w_conv, w_out_b, g_ffn, wr_t)


def _route_kernel(lt_ref, info_ref, gt_ref, cnt_ref, carry_ref):
    tb = lt_ref.shape[1]
    ne = N_GROUPS * EXPERTS_PER_GROUP

    @pl.when(pl.program_id(0) == 0)
    def _():
        carry_ref[...] = jnp.zeros_like(carry_ref)

    row8 = lax.broadcasted_iota(I32, (SUBLANES, tb), 0)
    gl = lt_ref[0:N_GROUPS, :]
    gmax = jnp.max(gl, axis=0, keepdims=True)
    gidx = jnp.min(jnp.where(gl == gmax, row8, N_GROUPS), axis=0, keepdims=True)
    pg = 1.0 / jnp.sum(jnp.exp(gl - gmax), axis=0, keepdims=True)

    sel = jnp.zeros((EXPERTS_PER_GROUP, tb), F32)
    for g in range(N_GROUPS):
        lo = N_GROUPS + g * EXPERTS_PER_GROUP
        sel = jnp.where(gidx == g, lt_ref[lo:lo + EXPERTS_PER_GROUP, :], sel)
    m1 = jnp.max(sel, axis=0, keepdims=True)
    i1 = jnp.min(jnp.where(sel == m1, row8, EXPERTS_PER_GROUP), axis=0, keepdims=True)
    sel2 = jnp.where(row8 == i1, -jnp.inf, sel)
    m2 = jnp.max(sel2, axis=0, keepdims=True)
    i2 = jnp.min(jnp.where(sel2 == m2, row8, EXPERTS_PER_GROUP), axis=0, keepdims=True)
    e2 = jnp.exp(m2 - m1)
    den = 1.0 + e2
    gate0 = pg * (1.0 / den)
    gate1 = pg * (e2 / den)
    eid0 = gidx * EXPERTS_PER_GROUP + i1
    eid1 = gidx * EXPERTS_PER_GROUP + i2

    rowe = lax.broadcasted_iota(I32, (ne, tb), 0)
    hit0 = rowe == eid0
    hit1 = rowe == eid1
    onehot = jnp.where(hit0 | hit1, 1.0, 0.0)
    before = (lax.broadcasted_iota(I32, (tb, tb), 0) < lax.broadcasted_iota(I32, (tb, tb), 1))
    prefix = jnp.dot(onehot.astype(BF16), jnp.where(before, 1.0, 0.0).astype(BF16),
                     preferred_element_type=F32)
    base = prefix + carry_ref[:, 0:1]
    rank0 = jnp.sum(jnp.where(hit0, base, 0.0), axis=0, keepdims=True).astype(I32)
    rank1 = jnp.sum(jnp.where(hit1, base, 0.0), axis=0, keepdims=True).astype(I32)
    carry_ref[...] = carry_ref[...] + jnp.sum(onehot, axis=1, keepdims=True)
    cnt_ref[...] = carry_ref[...].astype(I32)

    info_ref[...] = jnp.where(row8 == 0, eid0, jnp.where(row8 == 1, eid1,
                              jnp.where(row8 == 2, rank0, jnp.where(row8 == 3, rank1, 0))))
    rowl = lax.broadcasted_iota(I32, (LANES, tb), 0)
    gates = jnp.where(rowl == 0, gate0, jnp.where(rowl == 1, gate1, 0.0))
    gt_ref[...] = gates.T


def _route(logits_t):
    t = logits_t.shape[1]
    tb = ROUTE_TOKENS
    ne = N_GROUPS * EXPERTS_PER_GROUP
    return pl.pallas_call(
        _route_kernel,
        grid=(t // tb,),
        in_specs=[pl.BlockSpec((LANES, tb), lambda i: (0, i))],
        out_specs=[
            pl.BlockSpec((SUBLANES, tb), lambda i: (0, i)),
            pl.BlockSpec((tb, LANES), lambda i: (i, 0)),
            pl.BlockSpec((ne, LANES), lambda i: (0, 0)),
        ],
        out_shape=[
            jax.ShapeDtypeStruct((SUBLANES, t), I32),
            jax.ShapeDtypeStruct((t, LANES), F32),
            jax.ShapeDtypeStruct((ne, LANES), I32),
        ],
        scratch_shapes=[pltpu.VMEM((ne, LANES), F32)],
        compiler_params=pltpu.CompilerParams(dimension_semantics=("arbitrary",)),
        name="route",
    )(logits_t)


def _plan_kernel(cnt_ref, exp_ref, blk_ref):
    ne = cnt_ref.shape[0]
    nbp = blk_ref.shape[1]
    shift = EXPERT_ROWS.bit_length() - 1
    pad_rows = lambda c: lax.shift_left(lax.shift_right_logical(c + (EXPERT_ROWS - 1), shift), shift)
    cnt = cnt_ref[...]
    padded = pad_rows(cnt)
    padded_lanes = pad_rows(cnt.astype(F32).T[:ne, :ne].astype(I32))
    e_sub = lax.broadcasted_iota(I32, (ne, ne), 0)
    e_lane = lax.broadcasted_iota(I32, (ne, ne), 1)
    pend = jnp.sum(jnp.where(e_lane <= e_sub, padded_lanes, 0), axis=1, keepdims=True)
    pstart = pend - padded[:, 0:1]

    lane = lax.broadcasted_iota(I32, (ne, LANES), 1)
    exp_ref[...] = jnp.where(lane == 0, pstart, jnp.where(lane == 1, cnt, jnp.where(lane == 2, padded, 0)))

    first_row = lax.broadcasted_iota(I32, (ne, nbp), 1) * EXPERT_ROWS
    e_col = lax.broadcasted_iota(I32, (ne, nbp), 0)
    blk_e = jnp.minimum(jnp.sum(jnp.where(pend <= first_row, 1, 0), axis=0, keepdims=True), ne - 1)
    has_rows = padded[:, 0:1] > 0
    nxt_e = jnp.min(jnp.where((e_col > blk_e) & has_rows, e_col, ne), axis=0, keepdims=True)
    nxt2_e = jnp.min(jnp.where((e_col > nxt_e) & has_rows, e_col, ne), axis=0, keepdims=True)
    nxt_e = jnp.where(nxt_e == ne, -1, nxt_e)
    nxt2_e = jnp.where(nxt2_e == ne, -1, nxt2_e)
    n_used = lax.shift_right_logical(jnp.max(pend, axis=0, keepdims=True), shift)
    row8 = lax.broadcasted_iota(I32, (SUBLANES, nbp), 0)
    blk_ref[...] = jnp.where(row8 == 0, blk_e, jnp.where(row8 == 1, nxt_e, jnp.where(
        row8 == 2, n_used, jnp.where(row8 == 3, nxt2_e, 0))))


def _plan(counts, nb):
    ne = counts.shape[0]
    nbp = -(-nb // LANES) * LANES
    return pl.pallas_call(
        _plan_kernel,
        out_shape=[jax.ShapeDtypeStruct((ne, LANES), I32), jax.ShapeDtypeStruct((SUBLANES, nbp), I32)],
        name="plan",
    )(counts)


def _place_kernel(info_ref, ps_ref, dest_ref):
    tb = info_ref.shape[1]
    ne = ps_ref.shape[0]
    rowe = lax.broadcasted_iota(I32, (ne, tb), 0)
    row8 = lax.broadcasted_iota(I32, (SUBLANES, tb), 0)
    ps = ps_ref[:, 0:1]
    dest = jnp.zeros((SUBLANES, tb), I32)
    for k in range(TOP_K):
        start = jnp.sum(jnp.where(rowe == info_ref[k:k + 1, :], ps, 0), axis=0, keepdims=True)
        dest = jnp.where(row8 == k, start + info_ref[TOP_K + k:TOP_K + k + 1, :], dest)
    dest_ref[...] = dest


def _place(info, exp_tbl):
    t = info.shape[1]
    tb = PLACE_TOKENS
    ne = exp_tbl.shape[0]
    return pl.pallas_call(
        _place_kernel,
        grid=(t // tb,),
        in_specs=[pl.BlockSpec((SUBLANES, tb), lambda i: (0, i)),
                  pl.BlockSpec((ne, LANES), lambda i: (0, 0))],
        out_specs=pl.BlockSpec((SUBLANES, tb), lambda i: (0, i)),
        out_shape=jax.ShapeDtypeStruct((SUBLANES, t), I32),
        compiler_params=pltpu.CompilerParams(dimension_semantics=("arbitrary",)),
        name="place",
    )(info, exp_tbl)


def _row_copy(src_ref, src_row, dst_ref, dst_row, nt, sem):
    first = lambda row: row * nt if isinstance(row, int) else pl.multiple_of(row * nt, nt)
    return pltpu.make_async_copy(
        src_ref.at[pl.ds(first(src_row), nt), :], dst_ref.at[pl.ds(first(dst_row), nt), :], sem)


def _rows_wait(ref, n_rows, nt, sem):
    pltpu.make_async_copy(ref.at[pl.ds(0, n_rows * nt), :], ref.at[pl.ds(0, n_rows * nt), :], sem).wait()


def _for_each_assignment(dest_ref, n_tok, start_copy, inline=False):
    def group(g, c):
        t0 = g * MOVE_UNROLL
        rows = [[dest_ref[k * n_tok + t0 + u] for k in range(TOP_K)] for u in range(MOVE_UNROLL)]
        for u in range(MOVE_UNROLL):
            for k in range(TOP_K):
                start_copy(k, t0 + u, rows[u][k])
        return c

    if inline:
        for g in range(n_tok // MOVE_UNROLL):
            group(g, 0)
    else:
        lax.fori_loop(0, n_tok // MOVE_UNROLL, group, 0)


def _dispatch_kernel(nt, nb, plan_ref, dest_ref, h2_ref, xs_ref, zbuf, sem, zsem):
    td = h2_ref.shape[0] // nt
    ne = (plan_ref.shape[0] - 1) // 3
    n_used = plan_ref[3 * ne]
    blk_rows = zbuf.shape[0]

    def zero_rows(start_not_wait):
        def fire(c):
            c.start() if start_not_wait else c.wait()

        def expert(e, carry):
            cnt = plan_ref[3 * e + 1]
            row = plan_ref[3 * e] + cnt
            n = plan_ref[3 * e + 2] - cnt
            p = EXPERT_ROWS // 2
            while p >= 1:
                has = (n & p) != 0
                r, sz = row, p

                @pl.when(has)
                def _():
                    fire(pltpu.make_async_copy(
                        zbuf.at[pl.ds(0, sz * nt), :],
                        xs_ref.at[pl.ds(pl.multiple_of(r * nt, nt), sz * nt), :], zsem))

                row = row + jnp.where(has, p, 0)
                p //= 2
            return carry

        lax.fori_loop(0, ne, expert, 0)

        def tail(q, carry):
            fire(pltpu.make_async_copy(
                zbuf, xs_ref.at[pl.ds(pl.multiple_of(q * blk_rows, blk_rows), blk_rows), :], zsem))
            return carry

        lax.fori_loop(n_used, nb, tail, 0)

    @pl.when(pl.program_id(0) == 0)
    def _():
        zbuf[...] = jnp.zeros_like(zbuf)
        zero_rows(True)

    _for_each_assignment(
        dest_ref, td,
        lambda k, t, row: _row_copy(h2_ref, t, xs_ref, row, nt, sem).start(priority=k))
    _rows_wait(xs_ref, TOP_K * td, nt, sem)

    @pl.when(pl.program_id(0) == 0)
    def _():
        zero_rows(False)


def _dispatch(plan, dest, h2t, n_rows):
    td = DISPATCH_TOKENS
    t = dest.shape[0] // TOP_K
    nt = h2t.shape[0] // t
    nb = n_rows // EXPERT_ROWS
    grid_spec = pltpu.PrefetchScalarGridSpec(
        num_scalar_prefetch=1,
        grid=(t // td,),
        in_specs=[
            pl.BlockSpec((TOP_K * td,), lambda i, plan: (i,), memory_space=pltpu.SMEM),
            pl.BlockSpec((td * nt, LANES), lambda i, plan: (i, 0)),
        ],
        out_specs=pl.BlockSpec(memory_space=pl.ANY),
        scratch_shapes=[pltpu.VMEM((EXPERT_ROWS * nt, LANES), h2t.dtype),
                        pltpu.SemaphoreType.DMA, pltpu.SemaphoreType.DMA],
    )
    return pl.pallas_call(
        functools.partial(_dispatch_kernel, nt, nb),
        grid_spec=grid_spec,
        out_shape=jax.ShapeDtypeStruct((n_rows * nt, LANES), h2t.dtype),
        compiler_params=pltpu.CompilerParams(dimension_semantics=("arbitrary",)),
        name="dispatch",
    )(plan, dest, h2t)


def _experts_kernel(nb, be_ref, nx_ref, nx2_ref, nu_ref, xs_hbm, wg_hbm, wu_hbm, wd_hbm, y_hbm,
                    xbuf, ybuf, wg_st, wu_st, wd_st, wgu_b, wd_b, hid_s, slot_ref, wsems, isems, osems):
    ring = xbuf.shape[0]
    stages, d, de = wg_st.shape
    nt = d // LANES
    bs = EXPERT_ROWS
    ntp = xbuf.shape[1] // bs
    nu = nu_ref[0]

    def block(ref, q):
        rows = bs * ntp
        return ref.at[pl.ds(pl.multiple_of(q * rows, rows), rows), :]

    def in_copy(q):
        return pltpu.make_async_copy(block(xs_hbm, q), xbuf.at[q % ring], isems.at[q % ring])

    def out_copy(q):
        return pltpu.make_async_copy(ybuf.at[q % ring], block(y_hbm, q), osems.at[q % ring])

    def weight_copies(ex, slot):
        return (pltpu.make_async_copy(wg_hbm.at[ex], wg_st.at[slot], wsems.at[slot]),
                pltpu.make_async_copy(wu_hbm.at[ex], wu_st.at[slot], wsems.at[slot]),
                pltpu.make_async_copy(wd_hbm.at[ex], wd_st.at[slot], wsems.at[slot]))

    def start_weights(ex, slot):
        for c, prio in zip(weight_copies(ex, slot), WEIGHT_DMA_PRIORITIES):
            c.start(priority=prio)

    slot_ref[0] = 0
    slot_ref[1] = 0
    xbuf[...] = jnp.zeros_like(xbuf)
    hid_s[...] = jnp.zeros_like(hid_s)
    wd_b[...] = jnp.zeros_like(wd_b)
    start_weights(be_ref[0], 0)

    @pl.when(nx_ref[0] >= 0)
    def _():
        start_weights(nx_ref[0], 1)

    for q in range(EXPERT_LOOKAHEAD):
        @pl.when(q < nu)
        def _():
            in_copy(q).start()

    def step(j, carry):
        live = j < nu
        e = be_ref[jnp.minimum(j, nu - 1)]
        prev_slot = slot_ref[1]

        @pl.when(live & ((j == 0) | (e != be_ref[jnp.maximum(j - 1, 0)])))
        def _():
            seq = slot_ref[0]
            slot = lax.rem(seq, stages)
            for c in weight_copies(e, slot):
                c.wait()
            nx2 = nx2_ref[j]

            @pl.when(nx2 >= 0)
            def _():
                start_weights(nx2, lax.rem(seq + 2, stages))

            half = lax.rem(seq, 2)
            wgu_b[:, :de] = wg_st[slot].astype(BF16)
            wgu_b[:, de:] = wu_st[slot].astype(BF16)
            wd_b[half] = wd_st[slot].astype(BF16)
            slot_ref[1] = half
            slot_ref[0] = seq + 1

        @pl.when(j + EXPERT_LOOKAHEAD < nu)
        def _():
            in_copy(j + EXPERT_LOOKAHEAD).start()

        @pl.when(live)
        def _():
            in_copy(j).wait()

        @pl.when(j > ring)
        def _():
            out_copy(j - 1 - ring).wait()

        xb = _unpack_bf16_pairs(_tiles_to_rows(xbuf.at[j % ring], bs, ntp), BF16)
        gu = jnp.dot(xb, wgu_b[...], preferred_element_type=F32)
        gate = gu[:, :de]
        hid = ((gate * _sigmoid(gate)) * gu[:, de:]).astype(BF16)
        y = jnp.dot(hid_s[(j + 1) % 2], wd_b[prev_slot], preferred_element_type=F32)
        hid_s[j % 2] = hid
        _rows_to_tiles(ybuf.at[(j + ring - 1) % ring], _pack_bf16_pairs(y.astype(BF16)), bs)

        @pl.when(j > 0)
        def _():
            out_copy(j - 1).start()

        return carry

    lax.fori_loop(0, nu + 1, step, 0)

    for q in range(1, ring + 1):
        @pl.when(nu - q >= 0)
        def _():
            out_copy(nu - q).wait()

    ybuf[0] = jnp.zeros(ybuf.shape[1:], ybuf.dtype)

    def zero_start(q, carry):
        pltpu.make_async_copy(ybuf.at[0], block(y_hbm, q), osems.at[0]).start()
        return carry

    def zero_wait(q, carry):
        pltpu.make_async_copy(ybuf.at[0], block(y_hbm, q), osems.at[0]).wait()
        return carry

    lax.fori_loop(nu, nb, zero_start, 0)
    lax.fori_loop(nu, nb, zero_wait, 0)


def _experts(blk_e, nxt_e, nxt2_e, n_used, xs, w_gate, w_up, w_down):
    ne, d, de = w_gate.shape
    nt = d // LANES
    rows = EXPERT_ROWS * nt
    in_rows = rows // 2
    nb = xs.shape[0] // in_rows
    ring = EXPERT_LOOKAHEAD + 1
    any_spec = pl.BlockSpec(memory_space=pl.ANY)
    grid_spec = pltpu.PrefetchScalarGridSpec(
        num_scalar_prefetch=4,
        grid=(1,),
        in_specs=[any_spec, any_spec, any_spec, any_spec],
        out_specs=any_spec,
        scratch_shapes=[pltpu.VMEM((ring, in_rows, LANES), U32), pltpu.VMEM((ring, in_rows, LANES), U32),
                        pltpu.VMEM((WEIGHT_STAGES, d, de), F32), pltpu.VMEM((WEIGHT_STAGES, d, de), F32),
                        pltpu.VMEM((WEIGHT_STAGES, de, d), F32),
                        pltpu.VMEM((d, 2 * de), BF16), pltpu.VMEM((2, de, d), BF16),
                        pltpu.VMEM((2, EXPERT_ROWS, de), BF16),
                        pltpu.SMEM((2,), I32), pltpu.SemaphoreType.DMA((WEIGHT_STAGES,)),
                        pltpu.SemaphoreType.DMA((ring,)), pltpu.SemaphoreType.DMA((ring,))],
    )
    return pl.pallas_call(
        functools.partial(_experts_kernel, nb),
        grid_spec=grid_spec,
        out_shape=jax.ShapeDtypeStruct((nb * in_rows, LANES), U32),
        compiler_params=pltpu.CompilerParams(
            dimension_semantics=("arbitrary",), vmem_limit_bytes=VMEM_LIMIT),
        name="experts",
    )(blk_e, nxt_e, nxt2_e, n_used, xs, w_gate, w_up, w_down)


def _combine_kernel(d0_ref, d1_ref, d2_ref, x1_ref, gt_ref, y_ref, gfin_ref, o_ref, ybuf, sems):
    tc, d = x1_ref.shape
    nt = ybuf.shape[2] // tc
    i = pl.program_id(0)
    n = pl.num_programs(0)
    slots = ybuf.shape[0]

    def issue_tile(dest_ref, slot, inline):
        _for_each_assignment(
            dest_ref, tc,
            lambda k, t, row: _row_copy(y_ref, row, ybuf.at[slot, k], t, nt,
                                        sems.at[slot]).start(priority=k),
            inline=inline)

    @pl.when(i == 0)
    def _():
        issue_tile(d0_ref, 0, False)
        issue_tile(d1_ref, 1, False)

    slot = i % slots
    _rows_wait(y_ref, TOP_K * tc, nt, sems.at[slot])
    issue_tile(d2_ref, (i + 2) % slots, True)
    y0 = _unpack_bf16_pairs(_tiles_to_rows(ybuf.at[slot, 0], tc, nt), F32)
    y1 = _unpack_bf16_pairs(_tiles_to_rows(ybuf.at[slot, 1], tc, nt), F32)
    gates = gt_ref[...]
    xo = x1_ref[...] + (gates[:, 0:1] * y0 + gates[:, 1:2] * y1)
    o_ref[...] = _rms(xo, gfin_ref[...])

    @pl.when(i == n - 1)
    def _():
        _rows_wait(y_ref, TOP_K * tc, nt, sems.at[(i + 1) % slots])
        _rows_wait(y_ref, TOP_K * tc, nt, sems.at[(i + 2) % slots])


def _combine(dest, x1, gates_t, y, g_final):
    t, d = x1.shape
    nt = d // LANES
    tc = COMBINE_TOKENS
    last = t // tc - 1
    assert last >= 1
    dest_spec = lambda ahead: pl.BlockSpec(
        (TOP_K * tc,), lambda i: (jnp.minimum(i + ahead, last),), memory_space=pltpu.SMEM)
    return pl.pallas_call(
        _combine_kernel,
        grid=(t // tc,),
        in_specs=[
            dest_spec(0), dest_spec(1), dest_spec(2),
            pl.BlockSpec((tc, d), lambda i: (i, 0)),
            pl.BlockSpec((tc, LANES), lambda i: (i, 0)),
            pl.BlockSpec(memory_space=pl.ANY),
            pl.BlockSpec((1, d), lambda i: (0, 0)),
        ],
        out_specs=pl.BlockSpec((tc, d), lambda i: (i, 0)),
        out_shape=jax.ShapeDtypeStruct((t, d), F32),
        scratch_shapes=[pltpu.VMEM((3, TOP_K, tc * nt // 2, LANES), U32),
                        pltpu.SemaphoreType.DMA((3,))],
        compiler_params=pltpu.CompilerParams(dimension_semantics=("arbitrary",)),
        name="combine",
    )(dest, dest, dest, x1, gates_t, y, g_final)


def _layer(x, g_mix, w_in, w_s, b_s, g_sgu, w_conv, w_out, g_ffn, w_rg, w_re, w_gate, w_up, w_down):
    b, s, d = x.shape
    t = b * s
    nt = d // LANES
    ne = w_gate.shape[0]
    bs = EXPERT_ROWS

    bias_full = jnp.repeat(b_s.T, d // N_GROUPS, axis=1)
    wr_t = jnp.concatenate([w_rg, w_re], axis=1).T
    wr_t = jnp.pad(wr_t, ((0, LANES - wr_t.shape[0]), (0, 0))).astype(BF16)
    x1, h2t, logits_t = _mixer(
        x, g_mix.reshape(1, d), w_in.astype(BF16), w_s.astype(BF16), bias_full, g_sgu.reshape(1, d),
        w_conv, w_out.astype(BF16), g_ffn.reshape(1, d), wr_t)

    info, gates_t, counts = _route(logits_t)

    n_rows = -(-(t * TOP_K + ne * (bs - 1)) // bs) * bs
    nb = n_rows // bs
    exp_tbl, blk_tbl = _plan(counts, nb)
    blk_e, nxt_e, n_used, nxt2_e = blk_tbl[0, :nb], blk_tbl[1, :nb], blk_tbl[2, :1], blk_tbl[3, :nb]

    dest = _place(info, exp_tbl)[:TOP_K]
    tiled = lambda n: dest.reshape(TOP_K, t // n, n).transpose(1, 0, 2).reshape(-1)
    plan = jnp.concatenate([exp_tbl[:, :3].reshape(-1), n_used])
    xs = _dispatch(plan, tiled(DISPATCH_TOKENS), h2t, n_rows)
    y = _experts(blk_e, nxt_e, nxt2_e, n_used, xs, w_gate, w_up, w_down)
    return x1.reshape(t, d), tiled(COMBINE_TOKENS), gates_t, y


def kernel(x, g_mix, w_in, w_s, b_s, g_sgu, w_conv, w_out, g_ffn, w_router_group, w_router_expert,
           w_gate, w_up, w_down, g_final):
    b, s, d = x.shape
    depth = g_mix.shape[0]
    assert depth == 1, "the final RMSNorm is fused into the last layer's combine"
    assert s % MIX_ROWS == 0 and MIX_ROWS % CHUNK == 0 and d % LANES == 0
    assert EXPERT_ROWS & (EXPERT_ROWS - 1) == 0, "block bookkeeping uses shifts"
    assert all((b * s) % n == 0 for n in (ROUTE_TOKENS, PLACE_TOKENS, DISPATCH_TOKENS, COMBINE_TOKENS))
    l = 0
    x1, dest, gates_t, y = _layer(
        x, g_mix[l], w_in[l], w_s[l], b_s[l], g_sgu[l], w_conv[l], w_out[l], g_ffn[l],
        w_router_group[l], w_router_expert[l], w_gate[l], w_up[l], w_down[l])
    out = _combine(dest, x1, gates_t, y, g_final.reshape(1, d))
    return out.reshape(b, s, d)
```

```python
import functools

import jax
import jax.numpy as jnp
from jax import lax
from jax.experimental import pallas as pl
from jax.experimental.pallas import tpu as pltpu

F32 = jnp.float32
BF16 = jnp.bfloat16
I32 = jnp.int32
U32 = jnp.uint32

EPS = 1e-6
LANES = 128
SUBLANES = 8
CHUNK = 128
N_GROUPS = 8
EXPERTS_PER_GROUP = 8
TOP_K = 2
CONV_K = 3
N_BRANCH = 7

MIX_ROWS = 512
ROUTE_TOKENS = 512
PLACE_TOKENS = 2048
DISPATCH_TOKENS = 1024
COMBINE_TOKENS = 256
MOVE_UNROLL = 8
WEIGHT_STAGES = 3
WEIGHT_DMA_PRIORITIES = (1, 1, 0)
EXPERT_ROWS = 128
EXPERT_LOOKAHEAD = 3
VMEM_LIMIT = 56 * 1024 * 1024


def _rms(x, g):
    return x * lax.rsqrt(jnp.mean(x * x, axis=-1, keepdims=True) + EPS) * g


def _sigmoid(x):
    return 0.5 * (1.0 + jnp.tanh(0.5 * x))


def _gelu_tanh(x):
    c = 0.7978845608028654
    return x * (0.5 * (1.0 + jnp.tanh(c * (x + 0.044715 * (x * x * x)))))


def _rows_to_tiles(dst_ref, val, rows):
    nt = val.shape[1] // LANES
    for c in range(nt):
        dst_ref[pl.ds(c, rows, stride=nt), :] = val[:, c * LANES:(c + 1) * LANES]


def _tiles_to_rows(src_ref, rows, nt):
    return jnp.concatenate([src_ref[pl.ds(c, rows, stride=nt), :] for c in range(nt)], axis=1)


def _pack_bf16_pairs(x):
    half = x.shape[1] // 2
    bits = pltpu.bitcast(x.astype(F32), U32)
    return lax.shift_right_logical(bits[:, :half], U32(16)) | bits[:, half:]


def _unpack_bf16_pairs(w, dtype):
    lo = pltpu.bitcast(lax.shift_left(w, U32(16)), F32)
    hi = pltpu.bitcast(w & U32(0xFFFF0000), F32)
    return jnp.concatenate([lo, hi], axis=1).astype(dtype)


def _mixer_kernel(x_ref, xp_ref, xn_ref, gmix_ref, win_ref, ws_ref, bias_ref, gsgu_ref, wconv_ref,
                  wout_ref, gffn_ref, wr_ref, x1_ref, h2t_ref, lt_ref, vg_ref, z_ref, acc_ref):
    ts, d = x_ref.shape[1], x_ref.shape[2]
    s = pl.program_id(1)
    ns = pl.num_programs(1)
    gw = d // N_GROUPS
    nc = ts // CHUNK

    x = x_ref[0]
    gmix = gmix_ref[...]
    h = _rms(x, gmix).astype(BF16)

    def proj(j, n=1):
        return jnp.dot(h, win_ref[:, j * d:(j + n) * d], preferred_element_type=F32)

    v_raw = proj(1)
    xh = jnp.concatenate([xp_ref[0], xn_ref[0]], axis=0)
    h_ext = jnp.concatenate([h, _rms(xh, gmix).astype(BF16)], axis=0)
    cx = jnp.dot(h_ext, win_ref[:, 3 * d:5 * d], preferred_element_type=F32)

    gv = _gelu_tanh(v_raw)
    for g in range(N_GROUPS):
        cs = slice(g * gw, (g + 1) * gw)
        blk = gv[:, cs]
        mu = jnp.mean(blk, axis=-1, keepdims=True)
        dv = blk - mu
        var = jnp.mean(dv * dv, axis=-1, keepdims=True)
        vg_ref[:, cs] = (dv * lax.rsqrt(var + EPS) * gsgu_ref[:, cs]).astype(BF16)
    for g in range(N_GROUPS):
        cs = slice(g * gw, (g + 1) * gw)
        vcat = jnp.concatenate([vg_ref[n * CHUNK:(n + 1) * CHUNK, cs] for n in range(nc)], axis=1)
        zg = jnp.dot(ws_ref[g], vcat, preferred_element_type=F32)
        for n in range(nc):
            z_ref[n * CHUNK:(n + 1) * CHUNK, cs] = zg[:, n * gw:(n + 1) * gw]
    u = _gelu_tanh(proj(0))
    ga = _sigmoid(proj(5))
    for n in range(nc):
        rs = slice(n * CHUNK, (n + 1) * CHUNK)
        acc_ref[rs, :] = ga[rs] * (u[rs] * (z_ref[rs, :] + bias_ref[...]))

    z2 = cx[:ts, :d] * cx[:ts, d:]
    z2h = cx[ts:, :d] * cx[ts:, d:]
    prev = jnp.where(s > 0, z2h[SUBLANES - 1:SUBLANES, :], 0.0)
    nxt = jnp.where(s < ns - 1, z2h[SUBLANES:SUBLANES + 1, :], 0.0)
    row = lax.broadcasted_iota(I32, (ts, d), 0)
    zm1 = jnp.where(row == 0, prev, pltpu.roll(z2, 1, 0))
    zp1 = jnp.where(row == ts - 1, nxt, pltpu.roll(z2, ts - 1, 0))
    conv = wconv_ref[0:1, :] * zm1 + wconv_ref[1:2, :] * z2 + wconv_ref[2:3, :] * zp1
    cb = proj(2)
    gb = _sigmoid(proj(6))
    merged = acc_ref[...] + gb * (cb * conv)

    x1 = x + jnp.dot(merged.astype(BF16), wout_ref[...], preferred_element_type=F32)
    x1_ref[0] = x1

    h2 = _rms(x1, gffn_ref[...]).astype(BF16)
    lt_ref[...] = lax.dot_general(wr_ref[...], h2, (((1,), (1,)), ((), ())),
                                  preferred_element_type=F32)
    _rows_to_tiles(h2t_ref, _pack_bf16_pairs(h2), ts)


def _mixer(x, g_mix, w_in_b, w_s_b, bias_full, g_sgu, w_conv, w_out_b, g_ffn, wr_t):
    b, s, d = x.shape
    ts = MIX_ROWS
    ns = s // ts
    t = b * s
    ntp = d // (2 * LANES)
    hb = ts // SUBLANES
    last_hb = s // SUBLANES - 1

    const = lambda *shape: pl.BlockSpec(shape, lambda bi, si: (0,) * len(shape))
    in_specs = [
        pl.BlockSpec((1, ts, d), lambda bi, si: (bi, si, 0)),
        pl.BlockSpec((1, SUBLANES, d), lambda bi, si: (bi, jnp.maximum(si * hb - 1, 0), 0)),
        pl.BlockSpec((1, SUBLANES, d), lambda bi, si: (bi, jnp.minimum((si + 1) * hb, last_hb), 0)),
        const(1, d),
        pl.BlockSpec((d, N_BRANCH * d), lambda bi, si: (0, 0), pipeline_mode=pl.Buffered(1)),
        const(N_GROUPS, CHUNK, CHUNK),
        const(CHUNK, d),
        const(1, d),
        const(CONV_K, d),
        const(d, d),
        const(1, d),
        const(LANES, d),
    ]
    out_specs = [
        pl.BlockSpec((1, ts, d), lambda bi, si: (bi, si, 0)),
        pl.BlockSpec((ts * ntp, LANES), lambda bi, si: (bi * ns + si, 0)),
        pl.BlockSpec((LANES, ts), lambda bi, si: (0, bi * ns + si)),
    ]
    out_shape = [
        jax.ShapeDtypeStruct((b, s, d), F32),
        jax.ShapeDtypeStruct((t * ntp, LANES), U32),
        jax.ShapeDtypeStruct((LANES, t), F32),
    ]
    return pl.pallas_call(
        _mixer_kernel,
        grid=(b, ns),
        in_specs=in_specs,
        out_specs=out_specs,
        out_shape=out_shape,
        scratch_shapes=[pltpu.VMEM((ts, d), BF16), pltpu.VMEM((ts, d), F32), pltpu.VMEM((ts, d), F32)],
        compiler_params=pltpu.CompilerParams(
            dimension_semantics=("arbitrary", "arbitrary"), vmem_limit_bytes=VMEM_LIMIT),
        name="mixer",
    )(x, x, x, g_mix, w_in_b, w_s_b, bias_full, g_sgu, w_conv, w_out_b, g_ffn, wr_t)


def _route_kernel(lt_ref, info_ref, gt_ref, cnt_ref, carry_ref):
    tb = lt_ref.shape[1]
    ne = N_GROUPS * EXPERTS_PER_GROUP

    @pl.when(pl.program_id(0) == 0)
    def _():
        carry_ref[...] = jnp.zeros_like(carry_ref)

    row8 = lax.broadcasted_iota(I32, (SUBLANES, tb), 0)
    gl = lt_ref[0:N_GROUPS, :]
    gmax = jnp.max(gl, axis=0, keepdims=True)
    gidx = jnp.min(jnp.where(gl == gmax, row8, N_GROUPS), axis=0, keepdims=True)
    pg = 1.0 / jnp.sum(jnp.exp(gl - gmax), axis=0, keepdims=True)

    sel = jnp.zeros((EXPERTS_PER_GROUP, tb), F32)
    for g in range(N_GROUPS):
        lo = N_GROUPS + g * EXPERTS_PER_GROUP
        sel = jnp.where(gidx == g, lt_ref[lo:lo + EXPERTS_PER_GROUP, :], sel)
    m1 = jnp.max(sel, axis=0, keepdims=True)
    i1 = jnp.min(jnp.where(sel == m1, row8, EXPERTS_PER_GROUP), axis=0, keepdims=True)
    sel2 = jnp.where(row8 == i1, -jnp.inf, sel)
    m2 = jnp.max(sel2, axis=0, keepdims=True)
    i2 = jnp.min(jnp.where(sel2 == m2, row8, EXPERTS_PER_GROUP), axis=0, keepdims=True)
    e2 = jnp.exp(m2 - m1)
    den = 1.0 + e2
    gate0 = pg * (1.0 / den)
    gate1 = pg * (e2 / den)
    eid0 = gidx * EXPERTS_PER_GROUP + i1
    eid1 = gidx * EXPERTS_PER_GROUP + i2

    rowe = lax.broadcasted_iota(I32, (ne, tb), 0)
    hit0 = rowe == eid0
    hit1 = rowe == eid1
    onehot = jnp.where(hit0 | hit1, 1.0, 0.0)
    before = (lax.broadcasted_iota(I32, (tb, tb), 0) < lax.broadcasted_iota(I32, (tb, tb), 1))
    prefix = jnp.dot(onehot.astype(BF16), jnp.where(before, 1.0, 0.0).astype(BF16),
                     preferred_element_type=F32)
    base = prefix + carry_ref[:, 0:1]
    rank0 = jnp.sum(jnp.where(hit0, base, 0.0), axis=0, keepdims=True).astype(I32)
    rank1 = jnp.sum(jnp.where(hit1, base, 0.0), axis=0, keepdims=True).astype(I32)
    carry_ref[...] = carry_ref[...] + jnp.sum(onehot, axis=1, keepdims=True)
    cnt_ref[...] = carry_ref[...].astype(I32)

    info_ref[...] = jnp.where(row8 == 0, eid0, jnp.where(row8 == 1, eid1,
                              jnp.where(row8 == 2, rank0, jnp.where(row8 == 3, rank1, 0))))
    rowl = lax.broadcasted_iota(I32, (LANES, tb), 0)
    gates = jnp.where(rowl == 0, gate0, jnp.where(rowl == 1, gate1, 0.0))
    gt_ref[...] = gates.T


def _route(logits_t):
    t = logits_t.shape[1]
    tb = ROUTE_TOKENS
    ne = N_GROUPS * EXPERTS_PER_GROUP
    return pl.pallas_call(
        _route_kernel,
        grid=(t // tb,),
        in_specs=[pl.BlockSpec((LANES, tb), lambda i: (0, i))],
        out_specs=[
            pl.BlockSpec((SUBLANES, tb), lambda i: (0, i)),
            pl.BlockSpec((tb, LANES), lambda i: (i, 0)),
            pl.BlockSpec((ne, LANES), lambda i: (0, 0)),
        ],
        out_shape=[
            jax.ShapeDtypeStruct((SUBLANES, t), I32),
            jax.ShapeDtypeStruct((t, LANES), F32),
            jax.ShapeDtypeStruct((ne, LANES), I32),
        ],
        scratch_shapes=[pltpu.VMEM((ne, LANES), F32)],
        compiler_params=pltpu.CompilerParams(dimension_semantics=("arbitrary",)),
        name="route",
    )(logits_t)


def _plan_kernel(cnt_ref, exp_ref, blk_ref):
    ne = cnt_ref.shape[0]
    nbp = blk_ref.shape[1]
    shift = EXPERT_ROWS.bit_length() - 1
    pad_rows = lambda c: lax.shift_left(lax.shift_right_logical(c + (EXPERT_ROWS - 1), shift), shift)
    cnt = cnt_ref[...]
    padded = pad_rows(cnt)
    padded_lanes = pad_rows(cnt.astype(F32).T[:ne, :ne].astype(I32))
    e_sub = lax.broadcasted_iota(I32, (ne, ne), 0)
    e_lane = lax.broadcasted_iota(I32, (ne, ne), 1)
    pend = jnp.sum(jnp.where(e_lane <= e_sub, padded_lanes, 0), axis=1, keepdims=True)
    pstart = pend - padded[:, 0:1]

    lane = lax.broadcasted_iota(I32, (ne, LANES), 1)
    exp_ref[...] = jnp.where(lane == 0, pstart, jnp.where(lane == 1, cnt, jnp.where(lane == 2, padded, 0)))

    first_row = lax.broadcasted_iota(I32, (ne, nbp), 1) * EXPERT_ROWS
    e_col = lax.broadcasted_iota(I32, (ne, nbp), 0)
    blk_e = jnp.minimum(jnp.sum(jnp.where(pend <= first_row, 1, 0), axis=0, keepdims=True), ne - 1)
    has_rows = padded[:, 0:1] > 0
    nxt_e = jnp.min(jnp.where((e_col > blk_e) & has_rows, e_col, ne), axis=0, keepdims=True)
    nxt2_e = jnp.min(jnp.where((e_col > nxt_e) & has_rows, e_col, ne), axis=0, keepdims=True)
    nxt_e = jnp.where(nxt_e == ne, -1, nxt_e)
    nxt2_e = jnp.where(nxt2_e == ne, -1, nxt2_e)
    n_used = lax.shift_right_logical(jnp.max(pend, axis=0, keepdims=True), shift)
    row8 = lax.broadcasted_iota(I32, (SUBLANES, nbp), 0)
    blk_ref[...] = jnp.where(row8 == 0, blk_e, jnp.where(row8 == 1, nxt_e, jnp.where(
        row8 == 2, n_used, jnp.where(row8 == 3, nxt2_e, 0))))


def _plan(counts, nb):
    ne = counts.shape[0]
    nbp = -(-nb // LANES) * LANES
    return pl.pallas_call(
        _plan_kernel,
        out_shape=[jax.ShapeDtypeStruct((ne, LANES), I32), jax.ShapeDtypeStruct((SUBLANES, nbp), I32)],
        name="plan",
    )(counts)


def _place_kernel(info_ref, ps_ref, dest_ref):
    tb = info_ref.shape[1]
    ne = ps_ref.shape[0]
    rowe = lax.broadcasted_iota(I32, (ne, tb), 0)
    row8 = lax.broadcasted_iota(I32, (SUBLANES, tb), 0)
    ps = ps_ref[:, 0:1]
    dest = jnp.zeros((SUBLANES, tb), I32)
    for k in range(TOP_K):
        start = jnp.sum(jnp.where(rowe == info_ref[k:k + 1, :], ps, 0), axis=0, keepdims=True)
        dest = jnp.where(row8 == k, start + info_ref[TOP_K + k:TOP_K + k + 1, :], dest)
    dest_ref[...] = dest


def _place(info, exp_tbl):
    t = info.shape[1]
    tb = PLACE_TOKENS
    ne = exp_tbl.shape[0]
    return pl.pallas_call(
        _place_kernel,
        grid=(t // tb,),
        in_specs=[pl.BlockSpec((SUBLANES, tb), lambda i: (0, i)),
                  pl.BlockSpec((ne, LANES), lambda i: (0, 0))],
        out_specs=pl.BlockSpec((SUBLANES, tb), lambda i: (0, i)),
        out_shape=jax.ShapeDtypeStruct((SUBLANES, t), I32),
        compiler_params=pltpu.CompilerParams(dimension_semantics=("arbitrary",)),
        name="place",
    )(info, exp_tbl)


def _row_copy(src_ref, src_row, dst_ref, dst_row, nt, sem):
    first = lambda row: row * nt if isinstance(row, int) else pl.multiple_of(row * nt, nt)
    return pltpu.make_async_copy(
        src_ref.at[pl.ds(first(src_row), nt), :], dst_ref.at[pl.ds(first(dst_row), nt), :], sem)


def _rows_wait(ref, n_rows, nt, sem):
    pltpu.make_async_copy(ref.at[pl.ds(0, n_rows * nt), :], ref.at[pl.ds(0, n_rows * nt), :], sem).wait()


def _for_each_assignment(dest_ref, n_tok, start_copy, inline=False):
    def group(g, c):
        t0 = g * MOVE_UNROLL
        rows = [[dest_ref[k * n_tok + t0 + u] for k in range(TOP_K)] for u in range(MOVE_UNROLL)]
        for u in range(MOVE_UNROLL):
            for k in range(TOP_K):
                start_copy(k, t0 + u, rows[u][k])
        return c

    if inline:
        for g in range(n_tok // MOVE_UNROLL):
            group(g, 0)
    else:
        lax.fori_loop(0, n_tok // MOVE_UNROLL, group, 0)


def _dispatch_kernel(nt, nb, plan_ref, dest_ref, h2_ref, xs_ref, zbuf, sem, zsem):
    td = h2_ref.shape[0] // nt
    ne = (plan_ref.shape[0] - 1) // 3
    n_used = plan_ref[3 * ne]
    blk_rows = zbuf.shape[0]

    def zero_rows(start_not_wait):
        def fire(c):
            c.start() if start_not_wait else c.wait()

        def expert(e, carry):
            cnt = plan_ref[3 * e + 1]
            row = plan_ref[3 * e] + cnt
            n = plan_ref[3 * e + 2] - cnt
            p = EXPERT_ROWS // 2
            while p >= 1:
                has = (n & p) != 0
                r, sz = row, p

                @pl.when(has)
                def _():
                    fire(pltpu.make_async_copy(
                        zbuf.at[pl.ds(0, sz * nt), :],
                        xs_ref.at[pl.ds(pl.multiple_of(r * nt, nt), sz * nt), :], zsem))

                row = row + jnp.where(has, p, 0)
                p //= 2
            return carry

        lax.fori_loop(0, ne, expert, 0)

        def tail(q, carry):
            fire(pltpu.make_async_copy(
                zbuf, xs_ref.at[pl.ds(pl.multiple_of(q * blk_rows, blk_rows), blk_rows), :], zsem))
            return carry

        lax.fori_loop(n_used, nb, tail, 0)

    @pl.when(pl.program_id(0) == 0)
    def _():
        zbuf[...] = jnp.zeros_like(zbuf)
        zero_rows(True)

    _for_each_assignment(
        dest_ref, td,
        lambda k, t, row: _row_copy(h2_ref, t, xs_ref, row, nt, sem).start(priority=k))
    _rows_wait(xs_ref, TOP_K * td, nt, sem)

    @pl.when(pl.program_id(0) == 0)
    def _():
        zero_rows(False)


def _dispatch(plan, dest, h2t, n_rows):
    td = DISPATCH_TOKENS
    t = dest.shape[0] // TOP_K
    nt = h2t.shape[0] // t
    nb = n_rows // EXPERT_ROWS
    grid_spec = pltpu.PrefetchScalarGridSpec(
        num_scalar_prefetch=1,
        grid=(t // td,),
        in_specs=[
            pl.BlockSpec((TOP_K * td,), lambda i, plan: (i,), memory_space=pltpu.SMEM),
            pl.BlockSpec((td * nt, LANES), lambda i, plan: (i, 0)),
        ],
        out_specs=pl.BlockSpec(memory_space=pl.ANY),
        scratch_shapes=[pltpu.VMEM((EXPERT_ROWS * nt, LANES), h2t.dtype),
                        pltpu.SemaphoreType.DMA, pltpu.SemaphoreType.DMA],
    )
    return pl.pallas_call(
        functools.partial(_dispatch_kernel, nt, nb),
        grid_spec=grid_spec,
        out_shape=jax.ShapeDtypeStruct((n_rows * nt, LANES), h2t.dtype),
        compiler_params=pltpu.CompilerParams(dimension_semantics=("arbitrary",)),
        name="dispatch",
    )(plan, dest, h2t)


def _experts_kernel(nb, be_ref, nx_ref, nx2_ref, nu_ref, xs_hbm, wg_hbm, wu_hbm, wd_hbm, y_hbm,
                    xbuf, ybuf, wg_st, wu_st, wd_st, wgu_b, wd_b, hid_s, slot_ref, wsems, isems, osems):
    ring = xbuf.shape[0]
    stages, d, de = wg_st.shape
    nt = d // LANES
    bs = EXPERT_ROWS
    ntp = xbuf.shape[1] // bs
    nu = nu_ref[0]

    def block(ref, q):
        rows = bs * ntp
        return ref.at[pl.ds(pl.multiple_of(q * rows, rows), rows), :]

    def in_copy(q):
        return pltpu.make_async_copy(block(xs_hbm, q), xbuf.at[q % ring], isems.at[q % ring])

    def out_copy(q):
        return pltpu.make_async_copy(ybuf.at[q % ring], block(y_hbm, q), osems.at[q % ring])

    def weight_copies(ex, slot):
        return (pltpu.make_async_copy(wg_hbm.at[ex], wg_st.at[slot], wsems.at[slot]),
                pltpu.make_async_copy(wu_hbm.at[ex], wu_st.at[slot], wsems.at[slot]),
                pltpu.make_async_copy(wd_hbm.at[ex], wd_st.at[slot], wsems.at[slot]))

    def start_weights(ex, slot):
        for c, prio in zip(weight_copies(ex, slot), WEIGHT_DMA_PRIORITIES):
            c.start(priority=prio)

    slot_ref[0] = 0
    slot_ref[1] = 0
    xbuf[...] = jnp.zeros_like(xbuf)
    hid_s[...] = jnp.zeros_like(hid_s)
    wd_b[...] = jnp.zeros_like(wd_b)
    start_weights(be_ref[0], 0)

    @pl.when(nx_ref[0] >= 0)
    def _():
        start_weights(nx_ref[0], 1)

    for q in range(EXPERT_LOOKAHEAD):
        @pl.when(q < nu)
        def _():
            in_copy(q).start()

    def step(j, carry):
        live = j < nu
        e = be_ref[jnp.minimum(j, nu - 1)]
        prev_slot = slot_ref[1]

        @pl.when(live & ((j == 0) | (e != be_ref[jnp.maximum(j - 1, 0)])))
        def _():
            seq = slot_ref[0]
            slot = lax.rem(seq, stages)
            for c in weight_copies(e, slot):
                c.wait()
            nx2 = nx2_ref[j]

            @pl.when(nx2 >= 0)
            def _():
                start_weights(nx2, lax.rem(seq + 2, stages))

            half = lax.rem(seq, 2)
            wgu_b[:, :de] = wg_st[slot].astype(BF16)
            wgu_b[:, de:] = wu_st[slot].astype(BF16)
            wd_b[half] = wd_st[slot].astype(BF16)
            slot_ref[1] = half
            slot_ref[0] = seq + 1

        @pl.when(j + EXPERT_LOOKAHEAD < nu)
        def _():
            in_copy(j + EXPERT_LOOKAHEAD).start()

        @pl.when(live)
        def _():
            in_copy(j).wait()

        @pl.when(j > ring)
        def _():
            out_copy(j - 1 - ring).wait()

        xb = _unpack_bf16_pairs(_tiles_to_rows(xbuf.at[j % ring], bs, ntp), BF16)
        gu = jnp.dot(xb, wgu_b[...], preferred_element_type=F32)
        gate = gu[:, :de]
        hid = ((gate * _sigmoid(gate)) * gu[:, de:]).astype(BF16)
        y = jnp.dot(hid_s[(j + 1) % 2], wd_b[prev_slot], preferred_element_type=F32)
        hid_s[j % 2] = hid
        _rows_to_tiles(ybuf.at[(j + ring - 1) % ring], _pack_bf16_pairs(y.astype(BF16)), bs)

        @pl.when(j > 0)
        def _():
            out_copy(j - 1).start()

        return carry

    lax.fori_loop(0, nu + 1, step, 0)

    for q in range(1, ring + 1):
        @pl.when(nu - q >= 0)
        def _():
            out_copy(nu - q).wait()

    ybuf[0] = jnp.zeros(ybuf.shape[1:], ybuf.dtype)

    def zero_start(q, carry):
        pltpu.make_async_copy(ybuf.at[0], block(y_hbm, q), osems.at[0]).start()
        return carry

    def zero_wait(q, carry):
        pltpu.make_async_copy(ybuf.at[0], block(y_hbm, q), osems.at[0]).wait()
        return carry

    lax.fori_loop(nu, nb, zero_start, 0)
    lax.fori_loop(nu, nb, zero_wait, 0)


def _experts(blk_e, nxt_e, nxt2_e, n_used, xs, w_gate, w_up, w_down):
    ne, d, de = w_gate.shape
    nt = d // LANES
    rows = EXPERT_ROWS * nt
    in_rows = rows // 2
    nb = xs.shape[0] // in_rows
    ring = EXPERT_LOOKAHEAD + 1
    any_spec = pl.BlockSpec(memory_space=pl.ANY)
    grid_spec = pltpu.PrefetchScalarGridSpec(
        num_scalar_prefetch=4,
        grid=(1,),
        in_specs=[any_spec, any_spec, any_spec, any_spec],
        out_specs=any_spec,
        scratch_shapes=[pltpu.VMEM((ring, in_rows, LANES), U32), pltpu.VMEM((ring, in_rows, LANES), U32),
                        pltpu.VMEM((WEIGHT_STAGES, d, de), F32), pltpu.VMEM((WEIGHT_STAGES, d, de), F32),
                        pltpu.VMEM((WEIGHT_STAGES, de, d), F32),
                        pltpu.VMEM((d, 2 * de), BF16), pltpu.VMEM((2, de, d), BF16),
                        pltpu.VMEM((2, EXPERT_ROWS, de), BF16),
                        pltpu.SMEM((2,), I32), pltpu.SemaphoreType.DMA((WEIGHT_STAGES,)),
                        pltpu.SemaphoreType.DMA((ring,)), pltpu.SemaphoreType.DMA((ring,))],
    )
    return pl.pallas_call(
        functools.partial(_experts_kernel, nb),
        grid_spec=grid_spec,
        out_shape=jax.ShapeDtypeStruct((nb * in_rows, LANES), U32),
        compiler_params=pltpu.CompilerParams(
            dimension_semantics=("arbitrary",), vmem_limit_bytes=VMEM_LIMIT),
        name="experts",
    )(blk_e, nxt_e, nxt2_e, n_used, xs, w_gate, w_up, w_down)


def _combine_kernel(d0_ref, d1_ref, d2_ref, x1_ref, gt_ref, y_ref, gfin_ref, o_ref, ybuf, sems):
    tc, d = x1_ref.shape
    nt = ybuf.shape[2] // tc
    i = pl.program_id(0)
    n = pl.num_programs(0)
    slots = ybuf.shape[0]

    def issue_tile(dest_ref, slot, inline):
        _for_each_assignment(
            dest_ref, tc,
            lambda k, t, row: _row_copy(y_ref, row, ybuf.at[slot, k], t, nt,
                                        sems.at[slot]).start(priority=k),
            inline=inline)

    @pl.when(i == 0)
    def _():
        issue_tile(d0_ref, 0, False)
        issue_tile(d1_ref, 1, False)

    slot = i % slots
    _rows_wait(y_ref, TOP_K * tc, nt, sems.at[slot])
    issue_tile(d2_ref, (i + 2) % slots, True)
    y0 = _unpack_bf16_pairs(_tiles_to_rows(ybuf.at[slot, 0], tc, nt), F32)
    y1 = _unpack_bf16_pairs(_tiles_to_rows(ybuf.at[slot, 1], tc, nt), F32)
    gates = gt_ref[...]
    xo = x1_ref[...] + (gates[:, 0:1] * y0 + gates[:, 1:2] * y1)
    o_ref[...] = _rms(xo, gfin_ref[...])

    @pl.when(i == n - 1)
    def _():
        _rows_wait(y_ref, TOP_K * tc, nt, sems.at[(i + 1) % slots])
        _rows_wait(y_ref, TOP_K * tc, nt, sems.at[(i + 2) % slots])


def _combine(dest, x1, gates_t, y, g_final):
    t, d = x1.shape
    nt = d // LANES
    tc = COMBINE_TOKENS
    last = t // tc - 1
    assert last >= 1
    dest_spec = lambda ahead: pl.BlockSpec(
        (TOP_K * tc,), lambda i: (jnp.minimum(i + ahead, last),), memory_space=pltpu.SMEM)
    return pl.pallas_call(
        _combine_kernel,
        grid=(t // tc,),
        in_specs=[
            dest_spec(0), dest_spec(1), dest_spec(2),
            pl.BlockSpec((tc, d), lambda i: (i, 0)),
            pl.BlockSpec((tc, LANES), lambda i: (i, 0)),
            pl.BlockSpec(memory_space=pl.ANY),
            pl.BlockSpec((1, d), lambda i: (0, 0)),
        ],
        out_specs=pl.BlockSpec((tc, d), lambda i: (i, 0)),
        out_shape=jax.ShapeDtypeStruct((t, d), F32),
        scratch_shapes=[pltpu.VMEM((3, TOP_K, tc * nt // 2, LANES), U32),
                        pltpu.SemaphoreType.DMA((3,))],
        compiler_params=pltpu.CompilerParams(dimension_semantics=("arbitrary",)),
        name="combine",
    )(dest, dest, dest, x1, gates_t, y, g_final)


def _layer(x, g_mix, w_in, w_s, b_s, g_sgu, w_conv, w_out, g_ffn, w_rg, w_re, w_gate, w_up, w_down):
    b, s, d = x.shape
    t = b * s
    nt = d // LANES
    ne = w_gate.shape[0]
    bs = EXPERT_ROWS

    bias_full = jnp.repeat(b_s.T, d // N_GROUPS, axis=1)
    wr_t = jnp.concatenate([w_rg, w_re], axis=1).T
    wr_t = jnp.pad(wr_t, ((0, LANES - wr_t.shape[0]), (0, 0))).astype(BF16)
    x1, h2t, logits_t = _mixer(
        x, g_mix.reshape(1, d), w_in.astype(BF16), w_s.astype(BF16), bias_full, g_sgu.reshape(1, d),
        w_conv, w_out.astype(BF16), g_ffn.reshape(1, d), wr_t)

    info, gates_t, counts = _route(logits_t)

    n_rows = -(-(t * TOP_K + ne * (bs - 1)) // bs) * bs
    nb = n_rows // bs
    exp_tbl, blk_tbl = _plan(counts, nb)
    blk_e, nxt_e, n_used, nxt2_e = blk_tbl[0, :nb], blk_tbl[1, :nb], blk_tbl[2, :1], blk_tbl[3, :nb]

    dest = _place(info, exp_tbl)[:TOP_K]
    tiled = lambda n: dest.reshape(TOP_K, t // n, n).transpose(1, 0, 2).reshape(-1)
    plan = jnp.concatenate([exp_tbl[:, :3].reshape(-1), n_used])
    xs = _dispatch(plan, tiled(DISPATCH_TOKENS), h2t, n_rows)
    y = _experts(blk_e, nxt_e, nxt2_e, n_used, xs, w_gate, w_up, w_down)
    return x1.reshape(t, d), tiled(COMBINE_TOKENS), gates_t, y


def kernel(x, g_mix, w_in, w_s, b_s, g_sgu, w_conv, w_out, g_ffn, w_router_group, w_router_expert,
           w_gate, w_up, w_down, g_final):
    b, s, d = x.shape
    depth = g_mix.shape[0]
    assert depth == 1, "the final RMSNorm is fused into the last layer's combine"
    assert s % MIX_ROWS == 0 and MIX_ROWS % CHUNK == 0 and d % LANES == 0
    assert EXPERT_ROWS & (EXPERT_ROWS - 1) == 0, "block bookkeeping uses shifts"
    assert all((b * s) % n == 0 for n in (ROUTE_TOKENS, PLACE_TOKENS, DISPATCH_TOKENS, COMBINE_TOKENS))
    l = 0
    x1, dest, gates_t, y = _layer(
        x, g_mix[l], w_in[l], w_s[l], b_s[l], g_sgu[l], w_conv[l], w_out[l], g_ffn[l],
        w_router_group[l], w_router_expert[l], w_gate[l], w_up[l], w_down[l])
    out = _combine(dest, x1, gates_t, y, g_final.reshape(1, d))
    return out.reshape(b, s, d)
```

```python
import functools

import jax
import jax.numpy as jnp
from jax import lax
from jax.experimental import pallas as pl
from jax.experimental.pallas import tpu as pltpu

F32 = jnp.float32
BF16 = jnp.bfloat16
I32 = jnp.int32
U32 = jnp.uint32

EPS = 1e-6
LANES = 128
SUBLANES = 8
CHUNK = 128
N_GROUPS = 8
EXPERTS_PER_GROUP = 8
TOP_K = 2
CONV_K = 3
N_BRANCH = 7

MIX_ROWS = 512
ROUTE_TOKENS = 2048
ROUTE_CHUNK = 512
PLACE_TOKENS = 2048
DISPATCH_TOKENS = 2048
COMBINE_TOKENS = 512
MOVE_UNROLL = 8
WEIGHT_STAGES = 3
WEIGHT_DMA_PRIORITIES = (1, 1, 0)
EXPERT_ROWS = 128
EXPERT_LOOKAHEAD = 3
VMEM_LIMIT = 56 * 1024 * 1024


def _rms(x, g):
    return x * lax.rsqrt(jnp.mean(x * x, axis=-1, keepdims=True) + EPS) * g


def _sigmoid(x):
    return 0.5 * (1.0 + jnp.tanh(0.5 * x))


def _gelu_tanh(x):
    c = 0.7978845608028654
    return x * (0.5 * (1.0 + jnp.tanh(c * (x + 0.044715 * (x * x * x)))))


def _rows_to_tiles(dst_ref, val, rows):
    nt = val.shape[1] // LANES
    for c in range(nt):
        dst_ref[pl.ds(c, rows, stride=nt), :] = val[:, c * LANES:(c + 1) * LANES]


def _tiles_to_rows(src_ref, rows, nt):
    return jnp.concatenate([src_ref[pl.ds(c, rows, stride=nt), :] for c in range(nt)], axis=1)


def _pack_bf16_pairs(x):
    half = x.shape[1] // 2
    bits = pltpu.bitcast(x.astype(F32), U32)
    return lax.shift_right_logical(bits[:, :half], U32(16)) | bits[:, half:]


def _unpack_bf16_pairs(w, dtype):
    lo = pltpu.bitcast(lax.shift_left(w, U32(16)), F32)
    hi = pltpu.bitcast(w & U32(0xFFFF0000), F32)
    return jnp.concatenate([lo, hi], axis=1).astype(dtype)


def _mixer_kernel(x_ref, xp_ref, xn_ref, gmix_ref, win_ref, ws_ref, bias_ref, gsgu_ref, wconv_ref,
                  wout_ref, gffn_ref, wr_ref, x1_ref, h2t_ref, lt_ref, vg_ref, z_ref, acc_ref):
    ts, d = x_ref.shape[1], x_ref.shape[2]
    s = pl.program_id(1)
    ns = pl.num_programs(1)
    gw = d // N_GROUPS
    nc = ts // CHUNK

    x = x_ref[0]
    gmix = gmix_ref[...]
    h = _rms(x, gmix).astype(BF16)

    def proj(j, n=1):
        return jnp.dot(h, win_ref[:, j * d:(j + n) * d], preferred_element_type=F32)

    v_raw = proj(1)
    xh = jnp.concatenate([xp_ref[0], xn_ref[0]], axis=0)
    h_ext = jnp.concatenate([h, _rms(xh, gmix).astype(BF16)], axis=0)
    cx = jnp.dot(h_ext, win_ref[:, 3 * d:5 * d], preferred_element_type=F32)

    gv = _gelu_tanh(v_raw)
    for g in range(N_GROUPS):
        cs = slice(g * gw, (g + 1) * gw)
        blk = gv[:, cs]
        mu = jnp.mean(blk, axis=-1, keepdims=True)
        dv = blk - mu
        var = jnp.mean(dv * dv, axis=-1, keepdims=True)
        vg_ref[:, cs] = (dv * lax.rsqrt(var + EPS) * gsgu_ref[:, cs]).astype(BF16)
    for g in range(N_GROUPS):
        cs = slice(g * gw, (g + 1) * gw)
        vcat = jnp.concatenate([vg_ref[n * CHUNK:(n + 1) * CHUNK, cs] for n in range(nc)], axis=1)
        zg = jnp.dot(ws_ref[g], vcat, preferred_element_type=F32)
        for n in range(nc):
            z_ref[n * CHUNK:(n + 1) * CHUNK, cs] = zg[:, n * gw:(n + 1) * gw]
    u = _gelu_tanh(proj(0))
    ga = _sigmoid(proj(5))
    for n in range(nc):
        rs = slice(n * CHUNK, (n + 1) * CHUNK)
        acc_ref[rs, :] = ga[rs] * (u[rs] * (z_ref[rs, :] + bias_ref[...]))

    z2 = cx[:ts, :d] * cx[:ts, d:]
    z2h = cx[ts:, :d] * cx[ts:, d:]
    prev = jnp.where(s > 0, z2h[SUBLANES - 1:SUBLANES, :], 0.0)
    nxt = jnp.where(s < ns - 1, z2h[SUBLANES:SUBLANES + 1, :], 0.0)
    row = lax.broadcasted_iota(I32, (ts, d), 0)
    zm1 = jnp.where(row == 0, prev, pltpu.roll(z2, 1, 0))
    zp1 = jnp.where(row == ts - 1, nxt, pltpu.roll(z2, ts - 1, 0))
    conv = wconv_ref[0:1, :] * zm1 + wconv_ref[1:2, :] * z2 + wconv_ref[2:3, :] * zp1
    cb = proj(2)
    gb = _sigmoid(proj(6))
    merged = acc_ref[...] + gb * (cb * conv)

    x1 = x + jnp.dot(merged.astype(BF16), wout_ref[...], preferred_element_type=F32)
    x1_ref[0] = x1

    h2 = _rms(x1, gffn_ref[...]).astype(BF16)
    lt_ref[...] = lax.dot_general(wr_ref[...], h2, (((1,), (1,)), ((), ())),
                                  preferred_element_type=F32)
    _rows_to_tiles(h2t_ref, _pack_bf16_pairs(h2), ts)


def _mixer(x, g_mix, w_in_b, w_s_b, bias_full, g_sgu, w_conv, w_out_b, g_ffn, wr_t):
    b, s, d = x.shape
    ts = MIX_ROWS
    ns = s // ts
    t = b * s
    ntp = d // (2 * LANES)
    hb = ts // SUBLANES
    last_hb = s // SUBLANES - 1

    const = lambda *shape: pl.BlockSpec(shape, lambda bi, si: (0,) * len(shape))
    in_specs = [
        pl.BlockSpec((1, ts, d), lambda bi, si: (bi, si, 0)),
        pl.BlockSpec((1, SUBLANES, d), lambda bi, si: (bi, jnp.maximum(si * hb - 1, 0), 0)),
        pl.BlockSpec((1, SUBLANES, d), lambda bi, si: (bi, jnp.minimum((si + 1) * hb, last_hb), 0)),
        const(1, d),
        pl.BlockSpec((d, N_BRANCH * d), lambda bi, si: (0, 0), pipeline_mode=pl.Buffered(1)),
        const(N_GROUPS, CHUNK, CHUNK),
        const(CHUNK, d),
        const(1, d),
        const(CONV_K, d),
        const(d, d),
        const(1, d),
        const(LANES, d),
    ]
    out_specs = [
        pl.BlockSpec((1, ts, d), lambda bi, si: (bi, si, 0)),
        pl.BlockSpec((ts * ntp, LANES), lambda bi, si: (bi * ns + si, 0)),
        pl.BlockSpec((LANES, ts), lambda bi, si: (0, bi * ns + si)),
    ]
    out_shape = [
        jax.ShapeDtypeStruct((b, s, d), F32),
        jax.ShapeDtypeStruct((t * ntp, LANES), U32),
        jax.ShapeDtypeStruct((LANES, t), F32),
    ]
    return pl.pallas_call(
        _mixer_kernel,
        grid=(b, ns),
        in_specs=in_specs,
        out_specs=out_specs,
        out_shape=out_shape,
        scratch_shapes=[pltpu.VMEM((ts, d), BF16), pltpu.VMEM((ts, d), F32), pltpu.VMEM((ts, d), F32)],
        compiler_params=pltpu.CompilerParams(
            dimension_semantics=("arbitrary", "arbitrary"), vmem_limit_bytes=VMEM_LIMIT),
        name="mixer",
    )(x, x, x, g_mix, w_in_b, w_s_b, bias_full, g_sgu, w_conv, w_out_b, g_ffn, wr_t)


def _route_kernel(lt_ref, info_ref, gt_ref, cnt_ref, carry_ref):
    tb = lt_ref.shape[1]
    ne = N_GROUPS * EXPERTS_PER_GROUP

    @pl.when(pl.program_id(0) == 0)
    def _():
        carry_ref[...] = jnp.zeros_like(carry_ref)

    row8 = lax.broadcasted_iota(I32, (SUBLANES, tb), 0)
    gl = lt_ref[0:N_GROUPS, :]
    gmax = jnp.max(gl, axis=0, keepdims=True)
    gidx = jnp.min(jnp.where(gl == gmax, row8, N_GROUPS), axis=0, keepdims=True)
    pg = 1.0 / jnp.sum(jnp.exp(gl - gmax), axis=0, keepdims=True)

    sel = jnp.zeros((EXPERTS_PER_GROUP, tb), F32)
    for g in range(N_GROUPS):
        lo = N_GROUPS + g * EXPERTS_PER_GROUP
        sel = jnp.where(gidx == g, lt_ref[lo:lo + EXPERTS_PER_GROUP, :], sel)
    m1 = jnp.max(sel, axis=0, keepdims=True)
    i1 = jnp.min(jnp.where(sel == m1, row8, EXPERTS_PER_GROUP), axis=0, keepdims=True)
    sel2 = jnp.where(row8 == i1, -jnp.inf, sel)
    m2 = jnp.max(sel2, axis=0, keepdims=True)
    i2 = jnp.min(jnp.where(sel2 == m2, row8, EXPERTS_PER_GROUP), axis=0, keepdims=True)
    e2 = jnp.exp(m2 - m1)
    den = 1.0 + e2
    gate0 = pg * (1.0 / den)
    gate1 = pg * (e2 / den)
    eid0 = gidx * EXPERTS_PER_GROUP + i1
    eid1 = gidx * EXPERTS_PER_GROUP + i2

    rowe = lax.broadcasted_iota(I32, (ne, tb), 0)
    hit0 = rowe == eid0
    hit1 = rowe == eid1
    onehot = jnp.where(hit0 | hit1, 1.0, 0.0)
    sub = ROUTE_CHUNK
    before = (lax.broadcasted_iota(I32, (sub, sub), 0) < lax.broadcasted_iota(I32, (sub, sub), 1))
    before = jnp.where(before, 1.0, 0.0).astype(BF16)
    carry = carry_ref[:, 0:1]
    parts = []
    for c in range(tb // sub):
        part = onehot[:, c * sub:(c + 1) * sub]
        parts.append(jnp.dot(part.astype(BF16), before, preferred_element_type=F32) + carry)
        carry = carry + jnp.sum(part, axis=1, keepdims=True)
    base = jnp.concatenate(parts, axis=1)
    rank0 = jnp.sum(jnp.where(hit0, base, 0.0), axis=0, keepdims=True).astype(I32)
    rank1 = jnp.sum(jnp.where(hit1, base, 0.0), axis=0, keepdims=True).astype(I32)
    carry_ref[...] = jnp.broadcast_to(carry, carry_ref.shape)
    cnt_ref[...] = carry_ref[...].astype(I32)

    info_ref[...] = jnp.where(row8 == 0, eid0, jnp.where(row8 == 1, eid1,
                              jnp.where(row8 == 2, rank0, jnp.where(row8 == 3, rank1, 0))))
    rowl = lax.broadcasted_iota(I32, (LANES, tb), 0)
    gates = jnp.where(rowl == 0, gate0, jnp.where(rowl == 1, gate1, 0.0))
    gt_ref[...] = gates.T


def _route(logits_t):
    t = logits_t.shape[1]
    tb = ROUTE_TOKENS
    ne = N_GROUPS * EXPERTS_PER_GROUP
    return pl.pallas_call(
        _route_kernel,
        grid=(t // tb,),
        in_specs=[pl.BlockSpec((LANES, tb), lambda i: (0, i))],
        out_specs=[
            pl.BlockSpec((SUBLANES, tb), lambda i: (0, i)),
            pl.BlockSpec((tb, LANES), lambda i: (i, 0)),
            pl.BlockSpec((ne, LANES), lambda i: (0, 0)),
        ],
        out_shape=[
            jax.ShapeDtypeStruct((SUBLANES, t), I32),
            jax.ShapeDtypeStruct((t, LANES), F32),
            jax.ShapeDtypeStruct((ne, LANES), I32),
        ],
        scratch_shapes=[pltpu.VMEM((ne, LANES), F32)],
        compiler_params=pltpu.CompilerParams(dimension_semantics=("arbitrary",)),
        name="route",
    )(logits_t)


def _plan_kernel(cnt_ref, exp_ref, blk_ref):
    ne = cnt_ref.shape[0]
    nbp = blk_ref.shape[1]
    shift = EXPERT_ROWS.bit_length() - 1
    pad_rows = lambda c: lax.shift_left(lax.shift_right_logical(c + (EXPERT_ROWS - 1), shift), shift)
    cnt = cnt_ref[...]
    padded = pad_rows(cnt)
    padded_lanes = pad_rows(cnt.astype(F32).T[:ne, :ne].astype(I32))
    e_sub = lax.broadcasted_iota(I32, (ne, ne), 0)
    e_lane = lax.broadcasted_iota(I32, (ne, ne), 1)
    pend = jnp.sum(jnp.where(e_lane <= e_sub, padded_lanes, 0), axis=1, keepdims=True)
    pstart = pend - padded[:, 0:1]

    lane = lax.broadcasted_iota(I32, (ne, LANES), 1)
    exp_ref[...] = jnp.where(lane == 0, pstart, jnp.where(lane == 1, cnt, jnp.where(lane == 2, padded, 0)))

    first_row = lax.broadcasted_iota(I32, (ne, nbp), 1) * EXPERT_ROWS
    e_col = lax.broadcasted_iota(I32, (ne, nbp), 0)
    blk_e = jnp.minimum(jnp.sum(jnp.where(pend <= first_row, 1, 0), axis=0, keepdims=True), ne - 1)
    has_rows = padded[:, 0:1] > 0
    nxt_e = jnp.min(jnp.where((e_col > blk_e) & has_rows, e_col, ne), axis=0, keepdims=True)
    nxt2_e = jnp.min(jnp.where((e_col > nxt_e) & has_rows, e_col, ne), axis=0, keepdims=True)
    nxt_e = jnp.where(nxt_e == ne, -1, nxt_e)
    nxt2_e = jnp.where(nxt2_e == ne, -1, nxt2_e)
    n_used = lax.shift_right_logical(jnp.max(pend, axis=0, keepdims=True), shift)
    row8 = lax.broadcasted_iota(I32, (SUBLANES, nbp), 0)
    blk_ref[...] = jnp.where(row8 == 0, blk_e, jnp.where(row8 == 1, nxt_e, jnp.where(
        row8 == 2, n_used, jnp.where(row8 == 3, nxt2_e, 0))))


def _plan(counts, nb):
    ne = counts.shape[0]
    nbp = -(-nb // LANES) * LANES
    return pl.pallas_call(
        _plan_kernel,
        out_shape=[jax.ShapeDtypeStruct((ne, LANES), I32), jax.ShapeDtypeStruct((SUBLANES, nbp), I32)],
        name="plan",
    )(counts)


def _place_kernel(info_ref, ps_ref, dest_ref):
    tb = info_ref.shape[1]
    ne = ps_ref.shape[0]
    rowe = lax.broadcasted_iota(I32, (ne, tb), 0)
    row8 = lax.broadcasted_iota(I32, (SUBLANES, tb), 0)
    ps = ps_ref[:, 0:1]
    dest = jnp.zeros((SUBLANES, tb), I32)
    for k in range(TOP_K):
        start = jnp.sum(jnp.where(rowe == info_ref[k:k + 1, :], ps, 0), axis=0, keepdims=True)
        dest = jnp.where(row8 == k, start + info_ref[TOP_K + k:TOP_K + k + 1, :], dest)
    dest_ref[...] = dest


def _place(info, exp_tbl):
    t = info.shape[1]
    tb = PLACE_TOKENS
    ne = exp_tbl.shape[0]
    return pl.pallas_call(
        _place_kernel,
        grid=(t // tb,),
        in_specs=[pl.BlockSpec((SUBLANES, tb), lambda i: (0, i)),
                  pl.BlockSpec((ne, LANES), lambda i: (0, 0))],
        out_specs=pl.BlockSpec((SUBLANES, tb), lambda i: (0, i)),
        out_shape=jax.ShapeDtypeStruct((SUBLANES, t), I32),
        compiler_params=pltpu.CompilerParams(dimension_semantics=("arbitrary",)),
        name="place",
    )(info, exp_tbl)


def _row_copy(src_ref, src_row, dst_ref, dst_row, nt, sem):
    first = lambda row: row * nt if isinstance(row, int) else pl.multiple_of(row * nt, nt)
    return pltpu.make_async_copy(
        src_ref.at[pl.ds(first(src_row), nt), :], dst_ref.at[pl.ds(first(dst_row), nt), :], sem)


def _rows_wait(ref, n_rows, nt, sem):
    pltpu.make_async_copy(ref.at[pl.ds(0, n_rows * nt), :], ref.at[pl.ds(0, n_rows * nt), :], sem).wait()


def _for_each_assignment(dest_ref, n_tok, start_copy, inline=False):
    def group(g, c):
        t0 = g * MOVE_UNROLL
        rows = [[dest_ref[k * n_tok + t0 + u] for k in range(TOP_K)] for u in range(MOVE_UNROLL)]
        for u in range(MOVE_UNROLL):
            for k in range(TOP_K):
                start_copy(k, t0 + u, rows[u][k])
        return c

    if inline:
        for g in range(n_tok // MOVE_UNROLL):
            group(g, 0)
    else:
        lax.fori_loop(0, n_tok // MOVE_UNROLL, group, 0)


def _dispatch_kernel(nt, nb, plan_ref, dest_ref, h2_ref, xs_ref, zbuf, sem, zsem):
    td = h2_ref.shape[0] // nt
    ne = (plan_ref.shape[0] - 1) // 3
    n_used = plan_ref[3 * ne]
    blk_rows = zbuf.shape[0]

    def zero_rows(start_not_wait):
        def fire(c):
            c.start() if start_not_wait else c.wait()

        def expert(e, carry):
            cnt = plan_ref[3 * e + 1]
            row = plan_ref[3 * e] + cnt
            n = plan_ref[3 * e + 2] - cnt
            p = EXPERT_ROWS // 2
            while p >= 1:
                has = (n & p) != 0
                r, sz = row, p

                @pl.when(has)
                def _():
                    fire(pltpu.make_async_copy(
                        zbuf.at[pl.ds(0, sz * nt), :],
                        xs_ref.at[pl.ds(pl.multiple_of(r * nt, nt), sz * nt), :], zsem))

                row = row + jnp.where(has, p, 0)
                p //= 2
            return carry

        lax.fori_loop(0, ne, expert, 0)

        def tail(q, carry):
            fire(pltpu.make_async_copy(
                zbuf, xs_ref.at[pl.ds(pl.multiple_of(q * blk_rows, blk_rows), blk_rows), :], zsem))
            return carry

        lax.fori_loop(n_used, nb, tail, 0)

    @pl.when(pl.program_id(0) == 0)
    def _():
        zbuf[...] = jnp.zeros_like(zbuf)
        zero_rows(True)

    _for_each_assignment(
        dest_ref, td,
        lambda k, t, row: _row_copy(h2_ref, t, xs_ref, row, nt, sem).start(priority=k))
    _rows_wait(xs_ref, TOP_K * td, nt, sem)

    @pl.when(pl.program_id(0) == 0)
    def _():
        zero_rows(False)


def _dispatch(plan, dest, h2t, n_rows):
    td = DISPATCH_TOKENS
    t = dest.shape[0] // TOP_K
    nt = h2t.shape[0] // t
    nb = n_rows // EXPERT_ROWS
    grid_spec = pltpu.PrefetchScalarGridSpec(
        num_scalar_prefetch=1,
        grid=(t // td,),
        in_specs=[
            pl.BlockSpec((TOP_K * td,), lambda i, plan: (i,), memory_space=pltpu.SMEM),
            pl.BlockSpec((td * nt, LANES), lambda i, plan: (i, 0)),
        ],
        out_specs=pl.BlockSpec(memory_space=pl.ANY),
        scratch_shapes=[pltpu.VMEM((EXPERT_ROWS * nt, LANES), h2t.dtype),
                        pltpu.SemaphoreType.DMA, pltpu.SemaphoreType.DMA],
    )
    return pl.pallas_call(
        functools.partial(_dispatch_kernel, nt, nb),
        grid_spec=grid_spec,
        out_shape=jax.ShapeDtypeStruct((n_rows * nt, LANES), h2t.dtype),
        compiler_params=pltpu.CompilerParams(dimension_semantics=("arbitrary",)),
        name="dispatch",
    )(plan, dest, h2t)


def _experts_kernel(nb, be_ref, nx_ref, nx2_ref, nu_ref, xs_hbm, wg_hbm, wu_hbm, wd_hbm, y_hbm,
                    xbuf, ybuf, wg_st, wu_st, wd_st, wgu_b, wd_b, hid_s, slot_ref, wsems, isems, osems):
    ring = xbuf.shape[0]
    stages, d, de = wg_st.shape
    bs = EXPERT_ROWS
    ntp = xbuf.shape[1] // bs
    nu = nu_ref[0]

    def block(ref, q):
        rows = bs * ntp
        return ref.at[pl.ds(pl.multiple_of(q * rows, rows), rows), :]

    def in_copy(q):
        return pltpu.make_async_copy(block(xs_hbm, q), xbuf.at[q % ring], isems.at[q % ring])

    def out_copy(q):
        return pltpu.make_async_copy(ybuf.at[q % ring], block(y_hbm, q), osems.at[q % ring])

    def weight_copies(ex, slot):
        return (pltpu.make_async_copy(wg_hbm.at[ex], wg_st.at[slot], wsems.at[slot]),
                pltpu.make_async_copy(wu_hbm.at[ex], wu_st.at[slot], wsems.at[slot]),
                pltpu.make_async_copy(wd_hbm.at[ex], wd_st.at[slot], wsems.at[slot]))

    def start_weights(ex, slot):
        for c, prio in zip(weight_copies(ex, slot), WEIGHT_DMA_PRIORITIES):
            c.start(priority=prio)

    slot_ref[0] = 0
    slot_ref[1] = 0
    xbuf[...] = jnp.zeros_like(xbuf)
    hid_s[...] = jnp.zeros_like(hid_s)
    wd_b[...] = jnp.zeros_like(wd_b)
    start_weights(be_ref[0], 0)

    @pl.when(nx_ref[0] >= 0)
    def _():
        start_weights(nx_ref[0], 1)

    for q in range(EXPERT_LOOKAHEAD):
        @pl.when(q < nu)
        def _():
            in_copy(q).start()

    def step(j, carry):
        live = j < nu
        e = be_ref[jnp.minimum(j, nu - 1)]
        prev_slot = slot_ref[1]

        @pl.when(live & ((j == 0) | (e != be_ref[jnp.maximum(j - 1, 0)])))
        def _():
            seq = slot_ref[0]
            slot = lax.rem(seq, stages)
            for c in weight_copies(e, slot):
                c.wait()
            nx2 = nx2_ref[j]

            @pl.when(nx2 >= 0)
            def _():
                start_weights(nx2, lax.rem(seq + 2, stages))

            half = lax.rem(seq, 2)
            wgu_b[:, :de] = wg_st[slot].astype(BF16)
            wgu_b[:, de:] = wu_st[slot].astype(BF16)
            wd_b[half] = wd_st[slot].astype(BF16)
            slot_ref[1] = half
            slot_ref[0] = seq + 1

        @pl.when(j + EXPERT_LOOKAHEAD < nu)
        def _():
            in_copy(j + EXPERT_LOOKAHEAD).start()

        @pl.when(live)
        def _():
            in_copy(j).wait()

        @pl.when(j > ring)
        def _():
            out_copy(j - 1 - ring).wait()

        xb = _unpack_bf16_pairs(_tiles_to_rows(xbuf.at[j % ring], bs, ntp), BF16)
        gu = jnp.dot(xb, wgu_b[...], preferred_element_type=F32)
        gate = gu[:, :de]
        hid = ((gate * _sigmoid(gate)) * gu[:, de:]).astype(BF16)
        y = jnp.dot(hid_s[(j + 1) % 2], wd_b[prev_slot], preferred_element_type=F32)
        hid_s[j % 2] = hid
        _rows_to_tiles(ybuf.at[(j + ring - 1) % ring], _pack_bf16_pairs(y.astype(BF16)), bs)

        @pl.when(j > 0)
        def _():
            out_copy(j - 1).start()

        return carry

    lax.fori_loop(0, nu + 1, step, 0)

    for q in range(1, ring + 1):
        @pl.when(nu - q >= 0)
        def _():
            out_copy(nu - q).wait()

    ybuf[0] = jnp.zeros(ybuf.shape[1:], ybuf.dtype)

    def zero_start(q, carry):
        pltpu.make_async_copy(ybuf.at[0], block(y_hbm, q), osems.at[0]).start()
        return carry

    def zero_wait(q, carry):
        pltpu.make_async_copy(ybuf.at[0], block(y_hbm, q), osems.at[0]).wait()
        return carry

    lax.fori_loop(nu, nb, zero_start, 0)
    lax.fori_loop(nu, nb, zero_wait, 0)


def _experts(blk_e, nxt_e, nxt2_e, n_used, xs, w_gate, w_up, w_down):
    ne, d, de = w_gate.shape
    in_rows = EXPERT_ROWS * (d // LANES) // 2
    nb = xs.shape[0] // in_rows
    ring = EXPERT_LOOKAHEAD + 1
    any_spec = pl.BlockSpec(memory_space=pl.ANY)
    grid_spec = pltpu.PrefetchScalarGridSpec(
        num_scalar_prefetch=4,
        grid=(1,),
        in_specs=[any_spec, any_spec, any_spec, any_spec],
        out_specs=any_spec,
        scratch_shapes=[pltpu.VMEM((ring, in_rows, LANES), U32), pltpu.VMEM((ring, in_rows, LANES), U32),
                        pltpu.VMEM((WEIGHT_STAGES, d, de), F32), pltpu.VMEM((WEIGHT_STAGES, d, de), F32),
                        pltpu.VMEM((WEIGHT_STAGES, de, d), F32),
                        pltpu.VMEM((d, 2 * de), BF16), pltpu.VMEM((2, de, d), BF16),
                        pltpu.VMEM((2, EXPERT_ROWS, de), BF16),
                        pltpu.SMEM((2,), I32), pltpu.SemaphoreType.DMA((WEIGHT_STAGES,)),
                        pltpu.SemaphoreType.DMA((ring,)), pltpu.SemaphoreType.DMA((ring,))],
    )
    return pl.pallas_call(
        functools.partial(_experts_kernel, nb),
        grid_spec=grid_spec,
        out_shape=jax.ShapeDtypeStruct((nb * in_rows, LANES), U32),
        compiler_params=pltpu.CompilerParams(
            dimension_semantics=("arbitrary",), vmem_limit_bytes=VMEM_LIMIT),
        name="experts",
    )(blk_e, nxt_e, nxt2_e, n_used, xs, w_gate, w_up, w_down)


def _combine_kernel(d0_ref, d1_ref, d2_ref, x1_ref, gt_ref, y_ref, gfin_ref, o_ref, ybuf, sems):
    tc, d = x1_ref.shape
    nt = ybuf.shape[2] // tc
    i = pl.program_id(0)
    n = pl.num_programs(0)
    slots = ybuf.shape[0]

    def issue_tile(dest_ref, slot, inline):
        _for_each_assignment(
            dest_ref, tc,
            lambda k, t, row: _row_copy(y_ref, row, ybuf.at[slot, k], t, nt,
                                        sems.at[slot]).start(priority=k),
            inline=inline)

    @pl.when(i == 0)
    def _():
        issue_tile(d0_ref, 0, False)
        issue_tile(d1_ref, 1, False)

    slot = i % slots
    _rows_wait(y_ref, TOP_K * tc, nt, sems.at[slot])
    issue_tile(d2_ref, (i + 2) % slots, True)
    y0 = _unpack_bf16_pairs(_tiles_to_rows(ybuf.at[slot, 0], tc, nt), F32)
    y1 = _unpack_bf16_pairs(_tiles_to_rows(ybuf.at[slot, 1], tc, nt), F32)
    gates = gt_ref[...]
    xo = x1_ref[...] + (gates[:, 0:1] * y0 + gates[:, 1:2] * y1)
    o_ref[...] = _rms(xo, gfin_ref[...])

    @pl.when(i == n - 1)
    def _():
        _rows_wait(y_ref, TOP_K * tc, nt, sems.at[(i + 1) % slots])
        _rows_wait(y_ref, TOP_K * tc, nt, sems.at[(i + 2) % slots])


def _combine(dest, x1, gates_t, y, g_final):
    t, d = x1.shape
    nt = d // LANES
    tc = COMBINE_TOKENS
    last = t // tc - 1
    assert last >= 1
    dest_spec = lambda ahead: pl.BlockSpec(
        (TOP_K * tc,), lambda i: (jnp.minimum(i + ahead, last),), memory_space=pltpu.SMEM)
    return pl.pallas_call(
        _combine_kernel,
        grid=(t // tc,),
        in_specs=[
            dest_spec(0), dest_spec(1), dest_spec(2),
            pl.BlockSpec((tc, d), lambda i: (i, 0)),
            pl.BlockSpec((tc, LANES), lambda i: (i, 0)),
            pl.BlockSpec(memory_space=pl.ANY),
            pl.BlockSpec((1, d), lambda i: (0, 0)),
        ],
        out_specs=pl.BlockSpec((tc, d), lambda i: (i, 0)),
        out_shape=jax.ShapeDtypeStruct((t, d), F32),
        scratch_shapes=[pltpu.VMEM((3, TOP_K, tc * nt // 2, LANES), U32),
                        pltpu.SemaphoreType.DMA((3,))],
        compiler_params=pltpu.CompilerParams(dimension_semantics=("arbitrary",)),
        name="combine",
    )(dest, dest, dest, x1, gates_t, y, g_final)


def _layer(x, g_mix, w_in, w_s, b_s, g_sgu, w_conv, w_out, g_ffn, w_rg, w_re, w_gate, w_up, w_down):
    b, s, d = x.shape
    t = b * s
    ne = w_gate.shape[0]
    bs = EXPERT_ROWS

    bias_full = jnp.repeat(b_s.T, d // N_GROUPS, axis=1)
    wr_t = jnp.concatenate([w_rg, w_re], axis=1).T
    wr_t = jnp.pad(wr_t, ((0, LANES - wr_t.shape[0]), (0, 0))).astype(BF16)
    x1, h2t, logits_t = _mixer(
        x, g_mix.reshape(1, d), w_in.astype(BF16), w_s.astype(BF16), bias_full, g_sgu.reshape(1, d),
        w_conv, w_out.astype(BF16), g_ffn.reshape(1, d), wr_t)

    info, gates_t, counts = _route(logits_t)

    n_rows = -(-(t * TOP_K + ne * (bs - 1)) // bs) * bs
    nb = n_rows // bs
    exp_tbl, blk_tbl = _plan(counts, nb)
    blk_e, nxt_e, n_used, nxt2_e = blk_tbl[0, :nb], blk_tbl[1, :nb], blk_tbl[2, :1], blk_tbl[3, :nb]

    dest = _place(info, exp_tbl)[:TOP_K]
    tiled = lambda n: dest.reshape(TOP_K, t // n, n).transpose(1, 0, 2).reshape(-1)
    plan = jnp.concatenate([exp_tbl[:, :3].reshape(-1), n_used])
    xs = _dispatch(plan, tiled(DISPATCH_TOKENS), h2t, n_rows)
    y = _experts(blk_e, nxt_e, nxt2_e, n_used, xs, w_gate, w_up, w_down)
    return x1.reshape(t, d), tiled(COMBINE_TOKENS), gates_t, y


def kernel(x, g_mix, w_in, w_s, b_s, g_sgu, w_conv, w_out, g_ffn, w_router_group, w_router_expert,
           w_gate, w_up, w_down, g_final):
    b, s, d = x.shape
    depth = g_mix.shape[0]
    assert depth == 1, "the final RMSNorm is fused into the last layer's combine"
    assert s % MIX_ROWS == 0 and MIX_ROWS % CHUNK == 0 and d % LANES == 0
    assert EXPERT_ROWS & (EXPERT_ROWS - 1) == 0, "block bookkeeping uses shifts"
    assert all((b * s) % n == 0 for n in (ROUTE_TOKENS, PLACE_TOKENS, DISPATCH_TOKENS, COMBINE_TOKENS))
    l = 0
    x1, dest, gates_t, y = _layer(
        x, g_mix[l], w_in[l], w_s[l], b_s[l], g_sgu[l], w_conv[l], w_out[l], g_ffn[l],
        w_router_group[l], w_router_expert[l], w_gate[l], w_up[l], w_down[l])
    out = _combine(dest, x1, gates_t, y, g_final.reshape(1, d))
    return out.reshape(b, s, d)
```

```python
import functools

import jax
import jax.numpy as jnp
from jax import lax
from jax.experimental import pallas as pl
from jax.experimental.pallas import tpu as pltpu

F32 = jnp.float32
BF16 = jnp.bfloat16
I32 = jnp.int32
U32 = jnp.uint32

EPS = 1e-6
LANES = 128
SUBLANES = 8
CHUNK = 128
N_GROUPS = 8
EXPERTS_PER_GROUP = 8
TOP_K = 2
CONV_K = 3
N_BRANCH = 7

MIX_ROWS = 512
ROUTE_TOKENS = 2048
ROUTE_CHUNK = 512
PLACE_TOKENS = 2048
DISPATCH_TOKENS = 2048
COMBINE_TOKENS = 512
MOVE_UNROLL = 8
WEIGHT_STAGES = 3
WEIGHT_DMA_PRIORITIES = (1, 1, 0)
EXPERT_ROWS = 128
EXPERT_LOOKAHEAD = 6
VMEM_LIMIT = 56 * 1024 * 1024


def _rms(x, g):
    return x * lax.rsqrt(jnp.mean(x * x, axis=-1, keepdims=True) + EPS) * g


def _sigmoid(x):
    return 0.5 * (1.0 + jnp.tanh(0.5 * x))


def _gelu_tanh(x):
    c = 0.7978845608028654
    return x * (0.5 * (1.0 + jnp.tanh(c * (x + 0.044715 * (x * x * x)))))


def _rows_to_tiles(dst_ref, val, rows):
    nt = val.shape[1] // LANES
    for c in range(nt):
        dst_ref[pl.ds(c, rows, stride=nt), :] = val[:, c * LANES:(c + 1) * LANES]


def _tiles_to_rows(src_ref, rows, nt):
    return jnp.concatenate([src_ref[pl.ds(c, rows, stride=nt), :] for c in range(nt)], axis=1)


def _pack_bf16_pairs(x):
    half = x.shape[1] // 2
    bits = pltpu.bitcast(x.astype(F32), U32)
    return lax.shift_right_logical(bits[:, :half], U32(16)) | bits[:, half:]


def _unpack_bf16_pairs(w, dtype):
    lo = pltpu.bitcast(lax.shift_left(w, U32(16)), F32)
    hi = pltpu.bitcast(w & U32(0xFFFF0000), F32)
    return jnp.concatenate([lo, hi], axis=1).astype(dtype)


def _mixer_kernel(x_ref, xp_ref, xn_ref, gmix_ref, win_ref, ws_ref, bias_ref, gsgu_ref, wconv_ref,
                  wout_ref, gffn_ref, wr_ref, x1_ref, h2t_ref, lt_ref, vg_ref, z_ref, acc_ref):
    ts, d = x_ref.shape[1], x_ref.shape[2]
    s = pl.program_id(1)
    ns = pl.num_programs(1)
    gw = d // N_GROUPS
    nc = ts // CHUNK

    x = x_ref[0]
    gmix = gmix_ref[...]
    h = _rms(x, gmix).astype(BF16)

    def proj(j, n=1):
        return jnp.dot(h, win_ref[:, j * d:(j + n) * d], preferred_element_type=F32)

    v_raw = proj(1)
    xh = jnp.concatenate([xp_ref[0], xn_ref[0]], axis=0)
    h_ext = jnp.concatenate([h, _rms(xh, gmix).astype(BF16)], axis=0)
    cx = jnp.dot(h_ext, win_ref[:, 3 * d:5 * d], preferred_element_type=F32)

    gv = _gelu_tanh(v_raw)
    for g in range(N_GROUPS):
        cs = slice(g * gw, (g + 1) * gw)
        blk = gv[:, cs]
        mu = jnp.mean(blk, axis=-1, keepdims=True)
        dv = blk - mu
        var = jnp.mean(dv * dv, axis=-1, keepdims=True)
        vg_ref[:, cs] = (dv * lax.rsqrt(var + EPS) * gsgu_ref[:, cs]).astype(BF16)
    for g in range(N_GROUPS):
        cs = slice(g * gw, (g + 1) * gw)
        vcat = jnp.concatenate([vg_ref[n * CHUNK:(n + 1) * CHUNK, cs] for n in range(nc)], axis=1)
        zg = jnp.dot(ws_ref[g], vcat, preferred_element_type=F32)
        for n in range(nc):
            z_ref[n * CHUNK:(n + 1) * CHUNK, cs] = zg[:, n * gw:(n + 1) * gw]
    u = _gelu_tanh(proj(0))
    ga = _sigmoid(proj(5))
    for n in range(nc):
        rs = slice(n * CHUNK, (n + 1) * CHUNK)
        acc_ref[rs, :] = ga[rs] * (u[rs] * (z_ref[rs, :] + bias_ref[...]))

    z2 = cx[:ts, :d] * cx[:ts, d:]
    z2h = cx[ts:, :d] * cx[ts:, d:]
    prev = jnp.where(s > 0, z2h[SUBLANES - 1:SUBLANES, :], 0.0)
    nxt = jnp.where(s < ns - 1, z2h[SUBLANES:SUBLANES + 1, :], 0.0)
    row = lax.broadcasted_iota(I32, (ts, d), 0)
    zm1 = jnp.where(row == 0, prev, pltpu.roll(z2, 1, 0))
    zp1 = jnp.where(row == ts - 1, nxt, pltpu.roll(z2, ts - 1, 0))
    conv = wconv_ref[0:1, :] * zm1 + wconv_ref[1:2, :] * z2 + wconv_ref[2:3, :] * zp1
    cb = proj(2)
    gb = _sigmoid(proj(6))
    merged = acc_ref[...] + gb * (cb * conv)

    x1 = x + jnp.dot(merged.astype(BF16), wout_ref[...], preferred_element_type=F32)
    x1_ref[0] = x1

    h2 = _rms(x1, gffn_ref[...]).astype(BF16)
    lt_ref[...] = lax.dot_general(wr_ref[...], h2, (((1,), (1,)), ((), ())),
                                  preferred_element_type=F32)
    _rows_to_tiles(h2t_ref, _pack_bf16_pairs(h2), ts)


def _mixer(x, g_mix, w_in_b, w_s_b, bias_full, g_sgu, w_conv, w_out_b, g_ffn, wr_t):
    b, s, d = x.shape
    ts = MIX_ROWS
    ns = s // ts
    t = b * s
    ntp = d // (2 * LANES)
    hb = ts // SUBLANES
    last_hb = s // SUBLANES - 1

    const = lambda *shape: pl.BlockSpec(shape, lambda bi, si: (0,) * len(shape))
    in_specs = [
        pl.BlockSpec((1, ts, d), lambda bi, si: (bi, si, 0)),
        pl.BlockSpec((1, SUBLANES, d), lambda bi, si: (bi, jnp.maximum(si * hb - 1, 0), 0)),
        pl.BlockSpec((1, SUBLANES, d), lambda bi, si: (bi, jnp.minimum((si + 1) * hb, last_hb), 0)),
        const(1, d),
        pl.BlockSpec((d, N_BRANCH * d), lambda bi, si: (0, 0), pipeline_mode=pl.Buffered(1)),
        const(N_GROUPS, CHUNK, CHUNK),
        const(CHUNK, d),
        const(1, d),
        const(CONV_K, d),
        const(d, d),
        const(1, d),
        const(LANES, d),
    ]
    out_specs = [
        pl.BlockSpec((1, ts, d), lambda bi, si: (bi, si, 0)),
        pl.BlockSpec((ts * ntp, LANES), lambda bi, si: (bi * ns + si, 0)),
        pl.BlockSpec((LANES, ts), lambda bi, si: (0, bi * ns + si)),
    ]
    out_shape = [
        jax.ShapeDtypeStruct((b, s, d), F32),
        jax.ShapeDtypeStruct((t * ntp, LANES), U32),
        jax.ShapeDtypeStruct((LANES, t), F32),
    ]
    return pl.pallas_call(
        _mixer_kernel,
        grid=(b, ns),
        in_specs=in_specs,
        out_specs=out_specs,
        out_shape=out_shape,
        scratch_shapes=[pltpu.VMEM((ts, d), BF16), pltpu.VMEM((ts, d), F32), pltpu.VMEM((ts, d), F32)],
        compiler_params=pltpu.CompilerParams(
            dimension_semantics=("arbitrary", "arbitrary"), vmem_limit_bytes=VMEM_LIMIT),
        name="mixer",
    )(x, x, x, g_mix, w_in_b, w_s_b, bias_full, g_sgu, w_conv, w_out_b, g_ffn, wr_t)


def _route_kernel(lt_ref, info_ref, gt_ref, cnt_ref, carry_ref):
    tb = lt_ref.shape[1]
    ne = N_GROUPS * EXPERTS_PER_GROUP

    @pl.when(pl.program_id(0) == 0)
    def _():
        carry_ref[...] = jnp.zeros_like(carry_ref)

    row8 = lax.broadcasted_iota(I32, (SUBLANES, tb), 0)
    gl = lt_ref[0:N_GROUPS, :]
    gmax = jnp.max(gl, axis=0, keepdims=True)
    gidx = jnp.min(jnp.where(gl == gmax, row8, N_GROUPS), axis=0, keepdims=True)
    pg = 1.0 / jnp.sum(jnp.exp(gl - gmax), axis=0, keepdims=True)

    sel = jnp.zeros((EXPERTS_PER_GROUP, tb), F32)
    for g in range(N_GROUPS):
        lo = N_GROUPS + g * EXPERTS_PER_GROUP
        sel = jnp.where(gidx == g, lt_ref[lo:lo + EXPERTS_PER_GROUP, :], sel)
    m1 = jnp.max(sel, axis=0, keepdims=True)
    i1 = jnp.min(jnp.where(sel == m1, row8, EXPERTS_PER_GROUP), axis=0, keepdims=True)
    sel2 = jnp.where(row8 == i1, -jnp.inf, sel)
    m2 = jnp.max(sel2, axis=0, keepdims=True)
    i2 = jnp.min(jnp.where(sel2 == m2, row8, EXPERTS_PER_GROUP), axis=0, keepdims=True)
    e2 = jnp.exp(m2 - m1)
    den = 1.0 + e2
    gate0 = pg * (1.0 / den)
    gate1 = pg * (e2 / den)
    eid0 = gidx * EXPERTS_PER_GROUP + i1
    eid1 = gidx * EXPERTS_PER_GROUP + i2

    rowe = lax.broadcasted_iota(I32, (ne, tb), 0)
    hit0 = rowe == eid0
    hit1 = rowe == eid1
    onehot = jnp.where(hit0 | hit1, 1.0, 0.0)
    sub = ROUTE_CHUNK
    before = (lax.broadcasted_iota(I32, (sub, sub), 0) < lax.broadcasted_iota(I32, (sub, sub), 1))
    before = jnp.where(before, 1.0, 0.0).astype(BF16)
    carry = carry_ref[:, 0:1]
    parts = []
    for c in range(tb // sub):
        part = onehot[:, c * sub:(c + 1) * sub]
        parts.append(jnp.dot(part.astype(BF16), before, preferred_element_type=F32) + carry)
        carry = carry + jnp.sum(part, axis=1, keepdims=True)
    base = jnp.concatenate(parts, axis=1)
    rank0 = jnp.sum(jnp.where(hit0, base, 0.0), axis=0, keepdims=True).astype(I32)
    rank1 = jnp.sum(jnp.where(hit1, base, 0.0), axis=0, keepdims=True).astype(I32)
    carry_ref[...] = jnp.broadcast_to(carry, carry_ref.shape)
    cnt_ref[...] = carry_ref[...].astype(I32)

    info_ref[...] = jnp.where(row8 == 0, eid0, jnp.where(row8 == 1, eid1,
                              jnp.where(row8 == 2, rank0, jnp.where(row8 == 3, rank1, 0))))
    rowl = lax.broadcasted_iota(I32, (LANES, tb), 0)
    gates = jnp.where(rowl == 0, gate0, jnp.where(rowl == 1, gate1, 0.0))
    gt_ref[...] = gates.T


def _route(logits_t):
    t = logits_t.shape[1]
    tb = ROUTE_TOKENS
    ne = N_GROUPS * EXPERTS_PER_GROUP
    return pl.pallas_call(
        _route_kernel,
        grid=(t // tb,),
        in_specs=[pl.BlockSpec((LANES, tb), lambda i: (0, i))],
        out_specs=[
            pl.BlockSpec((SUBLANES, tb), lambda i: (0, i)),
            pl.BlockSpec((tb, LANES), lambda i: (i, 0)),
            pl.BlockSpec((ne, LANES), lambda i: (0, 0)),
        ],
        out_shape=[
            jax.ShapeDtypeStruct((SUBLANES, t), I32),
            jax.ShapeDtypeStruct((t, LANES), F32),
            jax.ShapeDtypeStruct((ne, LANES), I32),
        ],
        scratch_shapes=[pltpu.VMEM((ne, LANES), F32)],
        compiler_params=pltpu.CompilerParams(dimension_semantics=("arbitrary",)),
        name="route",
    )(logits_t)


def _plan_kernel(cnt_ref, exp_ref, blk_ref):
    ne = cnt_ref.shape[0]
    nbp = blk_ref.shape[1]
    shift = EXPERT_ROWS.bit_length() - 1
    pad_rows = lambda c: lax.shift_left(lax.shift_right_logical(c + (EXPERT_ROWS - 1), shift), shift)
    cnt = cnt_ref[...]
    padded = pad_rows(cnt)
    padded_lanes = pad_rows(cnt.astype(F32).T[:ne, :ne].astype(I32))
    e_sub = lax.broadcasted_iota(I32, (ne, ne), 0)
    e_lane = lax.broadcasted_iota(I32, (ne, ne), 1)
    pend = jnp.sum(jnp.where(e_lane <= e_sub, padded_lanes, 0), axis=1, keepdims=True)
    pstart = pend - padded[:, 0:1]

    lane = lax.broadcasted_iota(I32, (ne, LANES), 1)
    exp_ref[...] = jnp.where(lane == 0, pstart, jnp.where(lane == 1, cnt, jnp.where(lane == 2, padded, 0)))

    first_row = lax.broadcasted_iota(I32, (ne, nbp), 1) * EXPERT_ROWS
    e_col = lax.broadcasted_iota(I32, (ne, nbp), 0)
    blk_e = jnp.minimum(jnp.sum(jnp.where(pend <= first_row, 1, 0), axis=0, keepdims=True), ne - 1)
    has_rows = padded[:, 0:1] > 0
    nxt_e = jnp.min(jnp.where((e_col > blk_e) & has_rows, e_col, ne), axis=0, keepdims=True)
    nxt2_e = jnp.min(jnp.where((e_col > nxt_e) & has_rows, e_col, ne), axis=0, keepdims=True)
    nxt_e = jnp.where(nxt_e == ne, -1, nxt_e)
    nxt2_e = jnp.where(nxt2_e == ne, -1, nxt2_e)
    n_used = lax.shift_right_logical(jnp.max(pend, axis=0, keepdims=True), shift)
    row8 = lax.broadcasted_iota(I32, (SUBLANES, nbp), 0)
    blk_ref[...] = jnp.where(row8 == 0, blk_e, jnp.where(row8 == 1, nxt_e, jnp.where(
        row8 == 2, n_used, jnp.where(row8 == 3, nxt2_e, 0))))


def _plan(counts, nb):
    ne = counts.shape[0]
    nbp = -(-nb // LANES) * LANES
    return pl.pallas_call(
        _plan_kernel,
        out_shape=[jax.ShapeDtypeStruct((ne, LANES), I32), jax.ShapeDtypeStruct((SUBLANES, nbp), I32)],
        name="plan",
    )(counts)


def _place_kernel(info_ref, ps_ref, dest_ref):
    tb = info_ref.shape[1]
    ne = ps_ref.shape[0]
    rowe = lax.broadcasted_iota(I32, (ne, tb), 0)
    row8 = lax.broadcasted_iota(I32, (SUBLANES, tb), 0)
    ps = ps_ref[:, 0:1]
    dest = jnp.zeros((SUBLANES, tb), I32)
    for k in range(TOP_K):
        start = jnp.sum(jnp.where(rowe == info_ref[k:k + 1, :], ps, 0), axis=0, keepdims=True)
        dest = jnp.where(row8 == k, start + info_ref[TOP_K + k:TOP_K + k + 1, :], dest)
    dest_ref[...] = dest


def _place(info, exp_tbl):
    t = info.shape[1]
    tb = PLACE_TOKENS
    ne = exp_tbl.shape[0]
    return pl.pallas_call(
        _place_kernel,
        grid=(t // tb,),
        in_specs=[pl.BlockSpec((SUBLANES, tb), lambda i: (0, i)),
                  pl.BlockSpec((ne, LANES), lambda i: (0, 0))],
        out_specs=pl.BlockSpec((SUBLANES, tb), lambda i: (0, i)),
        out_shape=jax.ShapeDtypeStruct((SUBLANES, t), I32),
        compiler_params=pltpu.CompilerParams(dimension_semantics=("arbitrary",)),
        name="place",
    )(info, exp_tbl)


def _row_copy(src_ref, src_row, dst_ref, dst_row, nt, sem):
    first = lambda row: row * nt if isinstance(row, int) else pl.multiple_of(row * nt, nt)
    return pltpu.make_async_copy(
        src_ref.at[pl.ds(first(src_row), nt), :], dst_ref.at[pl.ds(first(dst_row), nt), :], sem)


def _rows_wait(ref, n_rows, nt, sem):
    pltpu.make_async_copy(ref.at[pl.ds(0, n_rows * nt), :], ref.at[pl.ds(0, n_rows * nt), :], sem).wait()


def _for_each_assignment(dest_ref, n_tok, start_copy, inline=False):
    def group(g, c):
        t0 = g * MOVE_UNROLL
        rows = [[dest_ref[k * n_tok + t0 + u] for k in range(TOP_K)] for u in range(MOVE_UNROLL)]
        for u in range(MOVE_UNROLL):
            for k in range(TOP_K):
                start_copy(k, t0 + u, rows[u][k])
        return c

    if inline:
        for g in range(n_tok // MOVE_UNROLL):
            group(g, 0)
    else:
        lax.fori_loop(0, n_tok // MOVE_UNROLL, group, 0)


def _dispatch_kernel(nt, nb, plan_ref, dest_ref, h2_ref, xs_ref, zbuf, sem, zsem):
    td = h2_ref.shape[0] // nt
    ne = (plan_ref.shape[0] - 1) // 3
    n_used = plan_ref[3 * ne]
    blk_rows = zbuf.shape[0]

    def zero_rows(start_not_wait):
        def fire(c):
            c.start() if start_not_wait else c.wait()

        def expert(e, carry):
            cnt = plan_ref[3 * e + 1]
            row = plan_ref[3 * e] + cnt
            n = plan_ref[3 * e + 2] - cnt
            p = EXPERT_ROWS // 2
            while p >= 1:
                has = (n & p) != 0
                r, sz = row, p

                @pl.when(has)
                def _():
                    fire(pltpu.make_async_copy(
                        zbuf.at[pl.ds(0, sz * nt), :],
                        xs_ref.at[pl.ds(pl.multiple_of(r * nt, nt), sz * nt), :], zsem))

                row = row + jnp.where(has, p, 0)
                p //= 2
            return carry

        lax.fori_loop(0, ne, expert, 0)

        def tail(q, carry):
            fire(pltpu.make_async_copy(
                zbuf, xs_ref.at[pl.ds(pl.multiple_of(q * blk_rows, blk_rows), blk_rows), :], zsem))
            return carry

        lax.fori_loop(n_used, nb, tail, 0)

    @pl.when(pl.program_id(0) == 0)
    def _():
        zbuf[...] = jnp.zeros_like(zbuf)
        zero_rows(True)

    _for_each_assignment(
        dest_ref, td,
        lambda k, t, row: _row_copy(h2_ref, t, xs_ref, row, nt, sem).start(priority=k))
    _rows_wait(xs_ref, TOP_K * td, nt, sem)

    @pl.when(pl.program_id(0) == 0)
    def _():
        zero_rows(False)


def _dispatch(plan, dest, h2t, n_rows):
    td = DISPATCH_TOKENS
    t = dest.shape[0] // TOP_K
    nt = h2t.shape[0] // t
    nb = n_rows // EXPERT_ROWS
    grid_spec = pltpu.PrefetchScalarGridSpec(
        num_scalar_prefetch=1,
        grid=(t // td,),
        in_specs=[
            pl.BlockSpec((TOP_K * td,), lambda i, plan: (i,), memory_space=pltpu.SMEM),
            pl.BlockSpec((td * nt, LANES), lambda i, plan: (i, 0)),
        ],
        out_specs=pl.BlockSpec(memory_space=pl.ANY),
        scratch_shapes=[pltpu.VMEM((EXPERT_ROWS * nt, LANES), h2t.dtype),
                        pltpu.SemaphoreType.DMA, pltpu.SemaphoreType.DMA],
    )
    return pl.pallas_call(
        functools.partial(_dispatch_kernel, nt, nb),
        grid_spec=grid_spec,
        out_shape=jax.ShapeDtypeStruct((n_rows * nt, LANES), h2t.dtype),
        compiler_params=pltpu.CompilerParams(dimension_semantics=("arbitrary",)),
        name="dispatch",
    )(plan, dest, h2t)


def _experts_kernel(nb, be_ref, nx_ref, nx2_ref, nu_ref, xs_hbm, wg_hbm, wu_hbm, wd_hbm, y_hbm,
                    xbuf, ybuf, wg_st, wu_st, wd_st, wgu_b, wd_b, hid_s, slot_ref, wsems, isems, osems):
    ring = xbuf.shape[0]
    stages, d, de = wg_st.shape
    bs = EXPERT_ROWS
    ntp = xbuf.shape[1] // bs
    nu = nu_ref[0]

    def block(ref, q):
        rows = bs * ntp
        return ref.at[pl.ds(pl.multiple_of(q * rows, rows), rows), :]

    def in_copy(q):
        return pltpu.make_async_copy(block(xs_hbm, q), xbuf.at[q % ring], isems.at[q % ring])

    def out_copy(q):
        return pltpu.make_async_copy(ybuf.at[q % ring], block(y_hbm, q), osems.at[q % ring])

    def weight_copies(ex, slot):
        return (pltpu.make_async_copy(wg_hbm.at[ex], wg_st.at[slot], wsems.at[slot]),
                pltpu.make_async_copy(wu_hbm.at[ex], wu_st.at[slot], wsems.at[slot]),
                pltpu.make_async_copy(wd_hbm.at[ex], wd_st.at[slot], wsems.at[slot]))

    def start_weights(ex, slot):
        for c, prio in zip(weight_copies(ex, slot), WEIGHT_DMA_PRIORITIES):
            c.start(priority=prio)

    slot_ref[0] = 0
    slot_ref[1] = 0
    xbuf[...] = jnp.zeros_like(xbuf)
    hid_s[...] = jnp.zeros_like(hid_s)
    wd_b[...] = jnp.zeros_like(wd_b)
    start_weights(be_ref[0], 0)

    @pl.when(nx_ref[0] >= 0)
    def _():
        start_weights(nx_ref[0], 1)

    for q in range(EXPERT_LOOKAHEAD):
        @pl.when(q < nu)
        def _():
            in_copy(q).start()

    def step(j, carry):
        live = j < nu
        e = be_ref[jnp.minimum(j, nu - 1)]
        prev_slot = slot_ref[1]

        @pl.when(live & ((j == 0) | (e != be_ref[jnp.maximum(j - 1, 0)])))
        def _():
            seq = slot_ref[0]
            slot = lax.rem(seq, stages)
            for c in weight_copies(e, slot):
                c.wait()
            nx2 = nx2_ref[j]

            @pl.when(nx2 >= 0)
            def _():
                start_weights(nx2, lax.rem(seq + 2, stages))

            half = lax.rem(seq, 2)
            wgu_b[:, :de] = wg_st[slot].astype(BF16)
            wgu_b[:, de:] = wu_st[slot].astype(BF16)
            wd_b[half] = wd_st[slot].astype(BF16)
            slot_ref[1] = half
            slot_ref[0] = seq + 1

        @pl.when(j + EXPERT_LOOKAHEAD < nu)
        def _():
            in_copy(j + EXPERT_LOOKAHEAD).start()

        @pl.when(live)
        def _():
            in_copy(j).wait()

        @pl.when(j > ring)
        def _():
            out_copy(j - 1 - ring).wait()

        xb = _unpack_bf16_pairs(_tiles_to_rows(xbuf.at[j % ring], bs, ntp), BF16)
        gu = jnp.dot(xb, wgu_b[...], preferred_element_type=F32)
        gate = gu[:, :de]
        hid = ((gate * _sigmoid(gate)) * gu[:, de:]).astype(BF16)
        y = jnp.dot(hid_s[(j + 1) % 2], wd_b[prev_slot], preferred_element_type=F32)
        hid_s[j % 2] = hid
        _rows_to_tiles(ybuf.at[(j + ring - 1) % ring], _pack_bf16_pairs(y.astype(BF16)), bs)

        @pl.when(j > 0)
        def _():
            out_copy(j - 1).start()

        return carry

    lax.fori_loop(0, nu + 1, step, 0)

    for q in range(1, ring + 1):
        @pl.when(nu - q >= 0)
        def _():
            out_copy(nu - q).wait()

    ybuf[0] = jnp.zeros(ybuf.shape[1:], ybuf.dtype)

    def zero_start(q, carry):
        pltpu.make_async_copy(ybuf.at[0], block(y_hbm, q), osems.at[0]).start()
        return carry

    def zero_wait(q, carry):
        pltpu.make_async_copy(ybuf.at[0], block(y_hbm, q), osems.at[0]).wait()
        return carry

    lax.fori_loop(nu, nb, zero_start, 0)
    lax.fori_loop(nu, nb, zero_wait, 0)


def _experts(blk_e, nxt_e, nxt2_e, n_used, xs, w_gate, w_up, w_down):
    ne, d, de = w_gate.shape
    in_rows = EXPERT_ROWS * (d // LANES) // 2
    nb = xs.shape[0] // in_rows
    ring = EXPERT_LOOKAHEAD + 1
    any_spec = pl.BlockSpec(memory_space=pl.ANY)
    grid_spec = pltpu.PrefetchScalarGridSpec(
        num_scalar_prefetch=4,
        grid=(1,),
        in_specs=[any_spec, any_spec, any_spec, any_spec],
        out_specs=any_spec,
        scratch_shapes=[pltpu.VMEM((ring, in_rows, LANES), U32), pltpu.VMEM((ring, in_rows, LANES), U32),
                        pltpu.VMEM((WEIGHT_STAGES, d, de), F32), pltpu.VMEM((WEIGHT_STAGES, d, de), F32),
                        pltpu.VMEM((WEIGHT_STAGES, de, d), F32),
                        pltpu.VMEM((d, 2 * de), BF16), pltpu.VMEM((2, de, d), BF16),
                        pltpu.VMEM((2, EXPERT_ROWS, de), BF16),
                        pltpu.SMEM((2,), I32), pltpu.SemaphoreType.DMA((WEIGHT_STAGES,)),
                        pltpu.SemaphoreType.DMA((ring,)), pltpu.SemaphoreType.DMA((ring,))],
    )
    return pl.pallas_call(
        functools.partial(_experts_kernel, nb),
        grid_spec=grid_spec,
        out_shape=jax.ShapeDtypeStruct((nb * in_rows, LANES), U32),
        compiler_params=pltpu.CompilerParams(
            dimension_semantics=("arbitrary",), vmem_limit_bytes=VMEM_LIMIT),
        name="experts",
    )(blk_e, nxt_e, nxt2_e, n_used, xs, w_gate, w_up, w_down)


def _combine_kernel(d0_ref, d1_ref, d2_ref, x1_ref, gt_ref, y_ref, gfin_ref, o_ref, ybuf, sems):
    tc, d = x1_ref.shape
    nt = ybuf.shape[2] // tc
    i = pl.program_id(0)
    n = pl.num_programs(0)
    slots = ybuf.shape[0]

    def issue_tile(dest_ref, slot, inline):
        _for_each_assignment(
            dest_ref, tc,
            lambda k, t, row: _row_copy(y_ref, row, ybuf.at[slot, k], t, nt,
                                        sems.at[slot]).start(priority=k),
            inline=inline)

    @pl.when(i == 0)
    def _():
        issue_tile(d0_ref, 0, False)
        issue_tile(d1_ref, 1, False)

    slot = i % slots
    _rows_wait(y_ref, TOP_K * tc, nt, sems.at[slot])
    issue_tile(d2_ref, (i + 2) % slots, True)
    y0 = _unpack_bf16_pairs(_tiles_to_rows(ybuf.at[slot, 0], tc, nt), F32)
    y1 = _unpack_bf16_pairs(_tiles_to_rows(ybuf.at[slot, 1], tc, nt), F32)
    gates = gt_ref[...]
    xo = x1_ref[...] + (gates[:, 0:1] * y0 + gates[:, 1:2] * y1)
    o_ref[...] = _rms(xo, gfin_ref[...])

    @pl.when(i == n - 1)
    def _():
        _rows_wait(y_ref, TOP_K * tc, nt, sems.at[(i + 1) % slots])
        _rows_wait(y_ref, TOP_K * tc, nt, sems.at[(i + 2) % slots])


def _combine(dest, x1, gates_t, y, g_final):
    t, d = x1.shape
    nt = d // LANES
    tc = COMBINE_TOKENS
    last = t // tc - 1
    assert last >= 1
    dest_spec = lambda ahead: pl.BlockSpec(
        (TOP_K * tc,), lambda i: (jnp.minimum(i + ahead, last),), memory_space=pltpu.SMEM)
    return pl.pallas_call(
        _combine_kernel,
        grid=(t // tc,),
        in_specs=[
            dest_spec(0), dest_spec(1), dest_spec(2),
            pl.BlockSpec((tc, d), lambda i: (i, 0)),
            pl.BlockSpec((tc, LANES), lambda i: (i, 0)),
            pl.BlockSpec(memory_space=pl.ANY),
            pl.BlockSpec((1, d), lambda i: (0, 0)),
        ],
        out_specs=pl.BlockSpec((tc, d), lambda i: (i, 0)),
        out_shape=jax.ShapeDtypeStruct((t, d), F32),
        scratch_shapes=[pltpu.VMEM((3, TOP_K, tc * nt // 2, LANES), U32),
                        pltpu.SemaphoreType.DMA((3,))],
        compiler_params=pltpu.CompilerParams(dimension_semantics=("arbitrary",)),
        name="combine",
    )(dest, dest, dest, x1, gates_t, y, g_final)


def _layer(x, g_mix, w_in, w_s, b_s, g_sgu, w_conv, w_out, g_ffn, w_rg, w_re, w_gate, w_up, w_down):
    b, s, d = x.shape
    t = b * s
    ne = w_gate.shape[0]
    bs = EXPERT_ROWS

    bias_full = jnp.repeat(b_s.T, d // N_GROUPS, axis=1)
    wr_t = jnp.concatenate([w_rg, w_re], axis=1).T
    wr_t = jnp.pad(wr_t, ((0, LANES - wr_t.shape[0]), (0, 0))).astype(BF16)
    x1, h2t, logits_t = _mixer(
        x, g_mix.reshape(1, d), w_in.astype(BF16), w_s.astype(BF16), bias_full, g_sgu.reshape(1, d),
        w_conv, w_out.astype(BF16), g_ffn.reshape(1, d), wr_t)

    info, gates_t, counts = _route(logits_t)

    n_rows = -(-(t * TOP_K + ne * (bs - 1)) // bs) * bs
    nb = n_rows // bs
    exp_tbl, blk_tbl = _plan(counts, nb)
    blk_e, nxt_e, n_used, nxt2_e = blk_tbl[0, :nb], blk_tbl[1, :nb], blk_tbl[2, :1], blk_tbl[3, :nb]

    dest = _place(info, exp_tbl)[:TOP_K]
    tiled = lambda n: dest.reshape(TOP_K, t // n, n).transpose(1, 0, 2).reshape(-1)
    plan = jnp.concatenate([exp_tbl[:, :3].reshape(-1), n_used])
    xs = _dispatch(plan, tiled(DISPATCH_TOKENS), h2t, n_rows)
    y = _experts(blk_e, nxt_e, nxt2_e, n_used, xs, w_gate, w_up, w_down)
    return x1.reshape(t, d), tiled(COMBINE_TOKENS), gates_t, y


def kernel(x, g_mix, w_in, w_s, b_s, g_sgu, w_conv, w_out, g_ffn, w_router_group, w_router_expert,
           w_gate, w_up, w_down, g_final):
    b, s, d = x.shape
    depth = g_mix.shape[0]
    assert depth == 1, "the final RMSNorm is fused into the last layer's combine"
    assert s % MIX_ROWS == 0 and MIX_ROWS % CHUNK == 0 and d % LANES == 0
    assert EXPERT_ROWS & (EXPERT_ROWS - 1) == 0, "block bookkeeping uses shifts"
    assert all((b * s) % n == 0 for n in (ROUTE_TOKENS, PLACE_TOKENS, DISPATCH_TOKENS, COMBINE_TOKENS))
    l = 0
    x1, dest, gates_t, y = _layer(
        x, g_mix[l], w_in[l], w_s[l], b_s[l], g_sgu[l], w_conv[l], w_out[l], g_ffn[l],
        w_router_group[l], w_router_expert[l], w_gate[l], w_up[l], w_down[l])
    out = _combine(dest, x1, gates_t, y, g_final.reshape(1, d))
    return out.reshape(b, s, d)
```

```python
import functools

import jax
import jax.numpy as jnp
from jax import lax
from jax.experimental import pallas as pl
from jax.experimental.pallas import tpu as pltpu

F32 = jnp.float32
BF16 = jnp.bfloat16
I32 = jnp.int32
U32 = jnp.uint32

EPS = 1e-6
LANES = 128
SUBLANES = 8
CHUNK = 128
N_GROUPS = 8
EXPERTS_PER_GROUP = 8
TOP_K = 2
CONV_K = 3
N_BRANCH = 7

MIX_ROWS = 512
ROUTE_TOKENS = 2048
ROUTE_CHUNK = 512
PLACE_TOKENS = 2048
DISPATCH_TOKENS = 2048
COMBINE_TOKENS = 512
MOVE_UNROLL = 8
WEIGHT_STAGES = 3
WEIGHT_DMA_PRIORITIES = (1, 1, 0)
EXPERT_ROWS = 128
EXPERT_LOOKAHEAD = 3
VMEM_LIMIT = 56 * 1024 * 1024


def _rms(x, g):
    return x * lax.rsqrt(jnp.mean(x * x, axis=-1, keepdims=True) + EPS) * g


def _sigmoid(x):
    return 0.5 * (1.0 + jnp.tanh(0.5 * x))


def _gelu_tanh(x):
    c = 0.7978845608028654
    return x * (0.5 * (1.0 + jnp.tanh(c * (x + 0.044715 * (x * x * x)))))


def _rows_to_tiles(dst_ref, val, rows):
    nt = val.shape[1] // LANES
    for c in range(nt):
        dst_ref[pl.ds(c, rows, stride=nt), :] = val[:, c * LANES:(c + 1) * LANES]


def _tiles_to_rows(src_ref, rows, nt):
    return jnp.concatenate([src_ref[pl.ds(c, rows, stride=nt), :] for c in range(nt)], axis=1)


def _pack_bf16_pairs(x):
    half = x.shape[1] // 2
    bits = pltpu.bitcast(x.astype(F32), U32)
    return lax.shift_right_logical(bits[:, :half], U32(16)) | bits[:, half:]


def _unpack_bf16_pairs(w, dtype):
    lo = pltpu.bitcast(lax.shift_left(w, U32(16)), F32)
    hi = pltpu.bitcast(w & U32(0xFFFF0000), F32)
    return jnp.concatenate([lo, hi], axis=1).astype(dtype)


def _mixer_kernel(x_ref, xp_ref, xn_ref, gmix_ref, win_ref, ws_ref, bias_ref, gsgu_ref, wconv_ref,
                  wout_ref, gffn_ref, wr_ref, x1_ref, h2t_ref, lt_ref, vg_ref, z_ref, acc_ref):
    ts, d = x_ref.shape[1], x_ref.shape[2]
    s = pl.program_id(1)
    ns = pl.num_programs(1)
    gw = d // N_GROUPS
    nc = ts // CHUNK

    x = x_ref[0]
    gmix = gmix_ref[...]
    h = _rms(x, gmix).astype(BF16)

    def proj(j, n=1):
        return jnp.dot(h, win_ref[:, j * d:(j + n) * d], preferred_element_type=F32)

    v_raw = proj(1)
    xh = jnp.concatenate([xp_ref[0], xn_ref[0]], axis=0)
    h_ext = jnp.concatenate([h, _rms(xh, gmix).astype(BF16)], axis=0)
    cx = jnp.dot(h_ext, win_ref[:, 3 * d:5 * d], preferred_element_type=F32)

    gv = _gelu_tanh(v_raw)
    for g in range(N_GROUPS):
        cs = slice(g * gw, (g + 1) * gw)
        blk = gv[:, cs]
        mu = jnp.mean(blk, axis=-1, keepdims=True)
        dv = blk - mu
        var = jnp.mean(dv * dv, axis=-1, keepdims=True)
        vg_ref[:, cs] = (dv * lax.rsqrt(var + EPS) * gsgu_ref[:, cs]).astype(BF16)
    for g in range(N_GROUPS):
        cs = slice(g * gw, (g + 1) * gw)
        vcat = jnp.concatenate([vg_ref[n * CHUNK:(n + 1) * CHUNK, cs] for n in range(nc)], axis=1)
        zg = jnp.dot(ws_ref[g], vcat, preferred_element_type=F32)
        for n in range(nc):
            z_ref[n * CHUNK:(n + 1) * CHUNK, cs] = zg[:, n * gw:(n + 1) * gw]
    u = _gelu_tanh(proj(0))
    ga = _sigmoid(proj(5))
    for n in range(nc):
        rs = slice(n * CHUNK, (n + 1) * CHUNK)
        acc_ref[rs, :] = ga[rs] * (u[rs] * (z_ref[rs, :] + bias_ref[...]))

    z2 = cx[:ts, :d] * cx[:ts, d:]
    z2h = cx[ts:, :d] * cx[ts:, d:]
    prev = jnp.where(s > 0, z2h[SUBLANES - 1:SUBLANES, :], 0.0)
    nxt = jnp.where(s < ns - 1, z2h[SUBLANES:SUBLANES + 1, :], 0.0)
    row = lax.broadcasted_iota(I32, (ts, d), 0)
    zm1 = jnp.where(row == 0, prev, pltpu.roll(z2, 1, 0))
    zp1 = jnp.where(row == ts - 1, nxt, pltpu.roll(z2, ts - 1, 0))
    conv = wconv_ref[0:1, :] * zm1 + wconv_ref[1:2, :] * z2 + wconv_ref[2:3, :] * zp1
    cb = proj(2)
    gb = _sigmoid(proj(6))
    merged = acc_ref[...] + gb * (cb * conv)

    x1 = x + jnp.dot(merged.astype(BF16), wout_ref[...], preferred_element_type=F32)
    x1_ref[0] = x1

    h2 = _rms(x1, gffn_ref[...]).astype(BF16)
    lt_ref[...] = lax.dot_general(wr_ref[...], h2, (((1,), (1,)), ((), ())),
                                  preferred_element_type=F32)
    _rows_to_tiles(h2t_ref, _pack_bf16_pairs(h2), ts)


def _mixer(x, g_mix, w_in_b, w_s_b, bias_full, g_sgu, w_conv, w_out_b, g_ffn, wr_t):
    b, s, d = x.shape
    ts = MIX_ROWS
    ns = s // ts
    t = b * s
    ntp = d // (2 * LANES)
    hb = ts // SUBLANES
    last_hb = s // SUBLANES - 1

    const = lambda *shape: pl.BlockSpec(shape, lambda bi, si: (0,) * len(shape))
    in_specs = [
        pl.BlockSpec((1, ts, d), lambda bi, si: (bi, si, 0)),
        pl.BlockSpec((1, SUBLANES, d), lambda bi, si: (bi, jnp.maximum(si * hb - 1, 0), 0)),
        pl.BlockSpec((1, SUBLANES, d), lambda bi, si: (bi, jnp.minimum((si + 1) * hb, last_hb), 0)),
        const(1, d),
        pl.BlockSpec((d, N_BRANCH * d), lambda bi, si: (0, 0), pipeline_mode=pl.Buffered(1)),
        const(N_GROUPS, CHUNK, CHUNK),
        const(CHUNK, d),
        const(1, d),
        const(CONV_K, d),
        const(d, d),
        const(1, d),
        const(LANES, d),
    ]
    out_specs = [
        pl.BlockSpec((1, ts, d), lambda bi, si: (bi, si, 0)),
        pl.BlockSpec((ts * ntp, LANES), lambda bi, si: (bi * ns + si, 0)),
        pl.BlockSpec((LANES, ts), lambda bi, si: (0, bi * ns + si)),
    ]
    out_shape = [
        jax.ShapeDtypeStruct((b, s, d), F32),
        jax.ShapeDtypeStruct((t * ntp, LANES), U32),
        jax.ShapeDtypeStruct((LANES, t), F32),
    ]
    return pl.pallas_call(
        _mixer_kernel,
        grid=(b, ns),
        in_specs=in_specs,
        out_specs=out_specs,
        out_shape=out_shape,
        scratch_shapes=[pltpu.VMEM((ts, d), BF16), pltpu.VMEM((ts, d), F32), pltpu.VMEM((ts, d), F32)],
        compiler_params=pltpu.CompilerParams(
            dimension_semantics=("arbitrary", "arbitrary"), vmem_limit_bytes=VMEM_LIMIT),
        name="mixer",
    )(x, x, x, g_mix, w_in_b, w_s_b, bias_full, g_sgu, w_conv, w_out_b, g_ffn, wr_t)


def _route_kernel(lt_ref, info_ref, gt_ref, cnt_ref, carry_ref):
    tb = lt_ref.shape[1]
    ne = N_GROUPS * EXPERTS_PER_GROUP

    @pl.when(pl.program_id(0) == 0)
    def _():
        carry_ref[...] = jnp.zeros_like(carry_ref)

    row8 = lax.broadcasted_iota(I32, (SUBLANES, tb), 0)
    gl = lt_ref[0:N_GROUPS, :]
    gmax = jnp.max(gl, axis=0, keepdims=True)
    gidx = jnp.min(jnp.where(gl == gmax, row8, N_GROUPS), axis=0, keepdims=True)
    pg = 1.0 / jnp.sum(jnp.exp(gl - gmax), axis=0, keepdims=True)

    sel = jnp.zeros((EXPERTS_PER_GROUP, tb), F32)
    for g in range(N_GROUPS):
        lo = N_GROUPS + g * EXPERTS_PER_GROUP
        sel = jnp.where(gidx == g, lt_ref[lo:lo + EXPERTS_PER_GROUP, :], sel)
    m1 = jnp.max(sel, axis=0, keepdims=True)
    i1 = jnp.min(jnp.where(sel == m1, row8, EXPERTS_PER_GROUP), axis=0, keepdims=True)
    sel2 = jnp.where(row8 == i1, -jnp.inf, sel)
    m2 = jnp.max(sel2, axis=0, keepdims=True)
    i2 = jnp.min(jnp.where(sel2 == m2, row8, EXPERTS_PER_GROUP), axis=0, keepdims=True)
    e2 = jnp.exp(m2 - m1)
    den = 1.0 + e2
    gate0 = pg * (1.0 / den)
    gate1 = pg * (e2 / den)
    eid0 = gidx * EXPERTS_PER_GROUP + i1
    eid1 = gidx * EXPERTS_PER_GROUP + i2

    rowe = lax.broadcasted_iota(I32, (ne, tb), 0)
    hit0 = rowe == eid0
    hit1 = rowe == eid1
    onehot = jnp.where(hit0 | hit1, 1.0, 0.0)
    sub = ROUTE_CHUNK
    before = (lax.broadcasted_iota(I32, (sub, sub), 0) < lax.broadcasted_iota(I32, (sub, sub), 1))
    before = jnp.where(before, 1.0, 0.0).astype(BF16)
    carry = carry_ref[:, 0:1]
    parts = []
    for c in range(tb // sub):
        part = onehot[:, c * sub:(c + 1) * sub]
        parts.append(jnp.dot(part.astype(BF16), before, preferred_element_type=F32) + carry)
        carry = carry + jnp.sum(part, axis=1, keepdims=True)
    base = jnp.concatenate(parts, axis=1)
    rank0 = jnp.sum(jnp.where(hit0, base, 0.0), axis=0, keepdims=True).astype(I32)
    rank1 = jnp.sum(jnp.where(hit1, base, 0.0), axis=0, keepdims=True).astype(I32)
    carry_ref[...] = jnp.broadcast_to(carry, carry_ref.shape)
    cnt_ref[...] = carry_ref[...].astype(I32)

    info_ref[...] = jnp.where(row8 == 0, eid0, jnp.where(row8 == 1, eid1,
                              jnp.where(row8 == 2, rank0, jnp.where(row8 == 3, rank1, 0))))
    rowl = lax.broadcasted_iota(I32, (LANES, tb), 0)
    gates = jnp.where(rowl == 0, gate0, jnp.where(rowl == 1, gate1, 0.0))
    gt_ref[...] = gates.T


def _route(logits_t):
    t = logits_t.shape[1]
    tb = ROUTE_TOKENS
    ne = N_GROUPS * EXPERTS_PER_GROUP
    return pl.pallas_call(
        _route_kernel,
        grid=(t // tb,),
        in_specs=[pl.BlockSpec((LANES, tb), lambda i: (0, i))],
        out_specs=[
            pl.BlockSpec((SUBLANES, tb), lambda i: (0, i)),
            pl.BlockSpec((tb, LANES), lambda i: (i, 0)),
            pl.BlockSpec((ne, LANES), lambda i: (0, 0)),
        ],
        out_shape=[
            jax.ShapeDtypeStruct((SUBLANES, t), I32),
            jax.ShapeDtypeStruct((t, LANES), F32),
            jax.ShapeDtypeStruct((ne, LANES), I32),
        ],
        scratch_shapes=[pltpu.VMEM((ne, LANES), F32)],
        compiler_params=pltpu.CompilerParams(dimension_semantics=("arbitrary",)),
        name="route",
    )(logits_t)


def _plan_kernel(cnt_ref, exp_ref, blk_ref):
    ne = cnt_ref.shape[0]
    nbp = blk_ref.shape[1]
    shift = EXPERT_ROWS.bit_length() - 1
    pad_rows = lambda c: lax.shift_left(lax.shift_right_logical(c + (EXPERT_ROWS - 1), shift), shift)
    cnt = cnt_ref[...]
    padded = pad_rows(cnt)
    padded_lanes = pad_rows(cnt.astype(F32).T[:ne, :ne].astype(I32))
    e_sub = lax.broadcasted_iota(I32, (ne, ne), 0)
    e_lane = lax.broadcasted_iota(I32, (ne, ne), 1)
    pend = jnp.sum(jnp.where(e_lane <= e_sub, padded_lanes, 0), axis=1, keepdims=True)
    pstart = pend - padded[:, 0:1]

    lane = lax.broadcasted_iota(I32, (ne, LANES), 1)
    exp_ref[...] = jnp.where(lane == 0, pstart, jnp.where(lane == 1, cnt, jnp.where(lane == 2, padded, 0)))

    first_row = lax.broadcasted_iota(I32, (ne, nbp), 1) * EXPERT_ROWS
    e_col = lax.broadcasted_iota(I32, (ne, nbp), 0)
    blk_e = jnp.minimum(jnp.sum(jnp.where(pend <= first_row, 1, 0), axis=0, keepdims=True), ne - 1)
    has_rows = padded[:, 0:1] > 0
    nxt_e = jnp.min(jnp.where((e_col > blk_e) & has_rows, e_col, ne), axis=0, keepdims=True)
    nxt2_e = jnp.min(jnp.where((e_col > nxt_e) & has_rows, e_col, ne), axis=0, keepdims=True)
    nxt_e = jnp.where(nxt_e == ne, -1, nxt_e)
    nxt2_e = jnp.where(nxt2_e == ne, -1, nxt2_e)
    n_used = lax.shift_right_logical(jnp.max(pend, axis=0, keepdims=True), shift)
    blk_seq = jnp.sum(jnp.where((e_col < blk_e) & has_rows, 1, 0), axis=0, keepdims=True)
    row8 = lax.broadcasted_iota(I32, (SUBLANES, nbp), 0)
    blk_ref[...] = jnp.where(row8 == 0, blk_e, jnp.where(row8 == 1, nxt_e, jnp.where(
        row8 == 2, n_used, jnp.where(row8 == 3, nxt2_e, jnp.where(row8 == 4, blk_seq, 0)))))


def _plan(counts, nb):
    ne = counts.shape[0]
    nbp = -(-nb // LANES) * LANES
    return pl.pallas_call(
        _plan_kernel,
        out_shape=[jax.ShapeDtypeStruct((ne, LANES), I32), jax.ShapeDtypeStruct((SUBLANES, nbp), I32)],
        name="plan",
    )(counts)


def _place_kernel(info_ref, ps_ref, dest_ref):
    tb = info_ref.shape[1]
    ne = ps_ref.shape[0]
    rowe = lax.broadcasted_iota(I32, (ne, tb), 0)
    row8 = lax.broadcasted_iota(I32, (SUBLANES, tb), 0)
    ps = ps_ref[:, 0:1]
    dest = jnp.zeros((SUBLANES, tb), I32)
    for k in range(TOP_K):
        start = jnp.sum(jnp.where(rowe == info_ref[k:k + 1, :], ps, 0), axis=0, keepdims=True)
        dest = jnp.where(row8 == k, start + info_ref[TOP_K + k:TOP_K + k + 1, :], dest)
    dest_ref[...] = dest


def _place(info, exp_tbl):
    t = info.shape[1]
    tb = PLACE_TOKENS
    ne = exp_tbl.shape[0]
    return pl.pallas_call(
        _place_kernel,
        grid=(t // tb,),
        in_specs=[pl.BlockSpec((SUBLANES, tb), lambda i: (0, i)),
                  pl.BlockSpec((ne, LANES), lambda i: (0, 0))],
        out_specs=pl.BlockSpec((SUBLANES, tb), lambda i: (0, i)),
        out_shape=jax.ShapeDtypeStruct((SUBLANES, t), I32),
        compiler_params=pltpu.CompilerParams(dimension_semantics=("arbitrary",)),
        name="place",
    )(info, exp_tbl)


def _row_copy(src_ref, src_row, dst_ref, dst_row, nt, sem):
    first = lambda row: row * nt if isinstance(row, int) else pl.multiple_of(row * nt, nt)
    return pltpu.make_async_copy(
        src_ref.at[pl.ds(first(src_row), nt), :], dst_ref.at[pl.ds(first(dst_row), nt), :], sem)


def _rows_wait(ref, n_rows, nt, sem):
    pltpu.make_async_copy(ref.at[pl.ds(0, n_rows * nt), :], ref.at[pl.ds(0, n_rows * nt), :], sem).wait()


def _for_each_assignment(dest_ref, n_tok, start_copy, inline=False):
    def group(g, c):
        t0 = g * MOVE_UNROLL
        rows = [[dest_ref[k * n_tok + t0 + u] for k in range(TOP_K)] for u in range(MOVE_UNROLL)]
        for u in range(MOVE_UNROLL):
            for k in range(TOP_K):
                start_copy(k, t0 + u, rows[u][k])
        return c

    if inline:
        for g in range(n_tok // MOVE_UNROLL):
            group(g, 0)
    else:
        lax.fori_loop(0, n_tok // MOVE_UNROLL, group, 0)


def _dispatch_kernel(nt, nb, plan_ref, dest_ref, h2_ref, xs_ref, zbuf, sem, zsem):
    td = h2_ref.shape[0] // nt
    ne = (plan_ref.shape[0] - 1) // 3
    n_used = plan_ref[3 * ne]
    blk_rows = zbuf.shape[0]

    def zero_rows(start_not_wait):
        def fire(c):
            c.start() if start_not_wait else c.wait()

        def expert(e, carry):
            cnt = plan_ref[3 * e + 1]
            row = plan_ref[3 * e] + cnt
            n = plan_ref[3 * e + 2] - cnt
            p = EXPERT_ROWS // 2
            while p >= 1:
                has = (n & p) != 0
                r, sz = row, p

                @pl.when(has)
                def _():
                    fire(pltpu.make_async_copy(
                        zbuf.at[pl.ds(0, sz * nt), :],
                        xs_ref.at[pl.ds(pl.multiple_of(r * nt, nt), sz * nt), :], zsem))

                row = row + jnp.where(has, p, 0)
                p //= 2
            return carry

        lax.fori_loop(0, ne, expert, 0)

        def tail(q, carry):
            fire(pltpu.make_async_copy(
                zbuf, xs_ref.at[pl.ds(pl.multiple_of(q * blk_rows, blk_rows), blk_rows), :], zsem))
            return carry

        lax.fori_loop(n_used, nb, tail, 0)

    @pl.when(pl.program_id(0) == 0)
    def _():
        zbuf[...] = jnp.zeros_like(zbuf)
        zero_rows(True)

    _for_each_assignment(
        dest_ref, td,
        lambda k, t, row: _row_copy(h2_ref, t, xs_ref, row, nt, sem).start(priority=k))
    _rows_wait(xs_ref, TOP_K * td, nt, sem)

    @pl.when(pl.program_id(0) == 0)
    def _():
        zero_rows(False)


def _dispatch(plan, dest, h2t, n_rows):
    td = DISPATCH_TOKENS
    t = dest.shape[0] // TOP_K
    nt = h2t.shape[0] // t
    nb = n_rows // EXPERT_ROWS
    grid_spec = pltpu.PrefetchScalarGridSpec(
        num_scalar_prefetch=1,
        grid=(t // td,),
        in_specs=[
            pl.BlockSpec((TOP_K * td,), lambda i, plan: (i,), memory_space=pltpu.SMEM),
            pl.BlockSpec((td * nt, LANES), lambda i, plan: (i, 0)),
        ],
        out_specs=pl.BlockSpec(memory_space=pl.ANY),
        scratch_shapes=[pltpu.VMEM((EXPERT_ROWS * nt, LANES), h2t.dtype),
                        pltpu.SemaphoreType.DMA, pltpu.SemaphoreType.DMA],
    )
    return pl.pallas_call(
        functools.partial(_dispatch_kernel, nt, nb),
        grid_spec=grid_spec,
        out_shape=jax.ShapeDtypeStruct((n_rows * nt, LANES), h2t.dtype),
        compiler_params=pltpu.CompilerParams(dimension_semantics=("arbitrary",)),
        name="dispatch",
    )(plan, dest, h2t)


def _experts_kernel(nb, be_ref, seq_ref, nx_ref, nx2_ref, nu_ref, xs_hbm, wg_hbm, wu_hbm, wd_hbm,
                    y_hbm, xbuf, ybuf, wg_st, wu_st, wd_st, wgu_b, wd_b, hid_s, wsems, isems, osems):
    ring = xbuf.shape[0]
    stages, d, de = wg_st.shape
    bs = EXPERT_ROWS
    ntp = xbuf.shape[1] // bs
    nu = nu_ref[0]

    def block(ref, q):
        rows = bs * ntp
        return ref.at[pl.ds(pl.multiple_of(q * rows, rows), rows), :]

    def in_copy(q):
        return pltpu.make_async_copy(block(xs_hbm, q), xbuf.at[q % ring], isems.at[q % ring])

    def out_copy(q):
        return pltpu.make_async_copy(ybuf.at[q % ring], block(y_hbm, q), osems.at[q % ring])

    def weight_copies(ex, slot):
        return (pltpu.make_async_copy(wg_hbm.at[ex], wg_st.at[slot], wsems.at[slot]),
                pltpu.make_async_copy(wu_hbm.at[ex], wu_st.at[slot], wsems.at[slot]),
                pltpu.make_async_copy(wd_hbm.at[ex], wd_st.at[slot], wsems.at[slot]))

    def start_weights(ex, slot):
        for c, prio in zip(weight_copies(ex, slot), WEIGHT_DMA_PRIORITIES):
            c.start(priority=prio)

    xbuf[...] = jnp.zeros_like(xbuf)
    hid_s[...] = jnp.zeros_like(hid_s)
    wgu_b[...] = jnp.zeros_like(wgu_b)
    wd_b[...] = jnp.zeros_like(wd_b)
    start_weights(be_ref[0], 0)

    @pl.when(nx_ref[0] >= 0)
    def _():
        start_weights(nx_ref[0], 1)

    for q in range(EXPERT_LOOKAHEAD):
        @pl.when(q < nu)
        def _():
            in_copy(q).start()

    seq_of = lambda j: seq_ref[jnp.clip(j, 0, nb - 1)]

    def load_expert_if_first(j):
        jc = jnp.clip(j, 0, nb - 1)
        e = be_ref[jc]

        @pl.when((j < nu) & ((j == 0) | (e != be_ref[jnp.maximum(jc - 1, 0)])))
        def _():
            seq = seq_ref[jc]
            slot = lax.rem(seq, stages)
            for c in weight_copies(e, slot):
                c.wait()
            nx2 = nx2_ref[jc]

            @pl.when(nx2 >= 0)
            def _():
                start_weights(nx2, lax.rem(seq + 2, stages))

            half = lax.rem(seq, 2)
            wgu_b[half, :, :de] = wg_st[slot].astype(BF16)
            wgu_b[half, :, de:] = wu_st[slot].astype(BF16)
            wd_b[lax.rem(seq, 3)] = wd_st[slot].astype(BF16)

    def up_proj(j):
        xb = _unpack_bf16_pairs(_tiles_to_rows(xbuf.at[j % ring], bs, ntp), BF16)
        gu = jnp.dot(xb, wgu_b[lax.rem(seq_of(j), 2)], preferred_element_type=F32)
        gate = gu[:, :de]
        return ((gate * _sigmoid(gate)) * gu[:, de:]).astype(BF16)

    def down_proj(hid, j):
        y = jnp.dot(hid, wd_b[lax.rem(seq_of(j), 3)], preferred_element_type=F32)
        _rows_to_tiles(ybuf.at[(j + ring) % ring], _pack_bf16_pairs(y.astype(BF16)), bs)

    def step(m, carry):
        j0 = 2 * m
        for j in (j0, j0 + 1):
            load_expert_if_first(j)
        for j in (j0, j0 + 1):
            @pl.when(j + EXPERT_LOOKAHEAD < nu)
            def _():
                in_copy(j + EXPERT_LOOKAHEAD).start()

            @pl.when(j < nu)
            def _():
                in_copy(j).wait()
        for j in (j0 - 1, j0):
            @pl.when(j >= ring)
            def _():
                out_copy(j - ring).wait()

        hid0 = up_proj(j0)
        down_proj(hid_s[...], j0 - 1)
        hid1 = up_proj(j0 + 1)
        down_proj(hid0, j0)
        hid_s[...] = hid1

        for j in (j0 - 1, j0):
            @pl.when((j >= 0) & (j < nu))
            def _():
                out_copy(j).start()

        return carry

    lax.fori_loop(0, nu // 2 + 1, step, 0)

    last_written = 2 * (nu // 2)
    for r in range(1, ring + 1):
        q = last_written - ring + r

        @pl.when((q >= 0) & (q < nu))
        def _():
            out_copy(q).wait()

    ybuf[0] = jnp.zeros(ybuf.shape[1:], ybuf.dtype)

    def zero_start(q, carry):
        pltpu.make_async_copy(ybuf.at[0], block(y_hbm, q), osems.at[0]).start()
        return carry

    def zero_wait(q, carry):
        pltpu.make_async_copy(ybuf.at[0], block(y_hbm, q), osems.at[0]).wait()
        return carry

    lax.fori_loop(nu, nb, zero_start, 0)
    lax.fori_loop(nu, nb, zero_wait, 0)


def _experts(blk_e, blk_seq, nxt_e, nxt2_e, n_used, xs, w_gate, w_up, w_down):
    ne, d, de = w_gate.shape
    in_rows = EXPERT_ROWS * (d // LANES) // 2
    nb = xs.shape[0] // in_rows
    ring = EXPERT_LOOKAHEAD + 2
    any_spec = pl.BlockSpec(memory_space=pl.ANY)
    grid_spec = pltpu.PrefetchScalarGridSpec(
        num_scalar_prefetch=5,
        grid=(1,),
        in_specs=[any_spec, any_spec, any_spec, any_spec],
        out_specs=any_spec,
        scratch_shapes=[pltpu.VMEM((ring, in_rows, LANES), U32), pltpu.VMEM((ring, in_rows, LANES), U32),
                        pltpu.VMEM((WEIGHT_STAGES, d, de), F32), pltpu.VMEM((WEIGHT_STAGES, d, de), F32),
                        pltpu.VMEM((WEIGHT_STAGES, de, d), F32),
                        pltpu.VMEM((2, d, 2 * de), BF16), pltpu.VMEM((3, de, d), BF16),
                        pltpu.VMEM((EXPERT_ROWS, de), BF16),
                        pltpu.SemaphoreType.DMA((WEIGHT_STAGES,)),
                        pltpu.SemaphoreType.DMA((ring,)), pltpu.SemaphoreType.DMA((ring,))],
    )
    return pl.pallas_call(
        functools.partial(_experts_kernel, nb),
        grid_spec=grid_spec,
        out_shape=jax.ShapeDtypeStruct((nb * in_rows, LANES), U32),
        compiler_params=pltpu.CompilerParams(
            dimension_semantics=("arbitrary",), vmem_limit_bytes=VMEM_LIMIT),
        name="experts",
    )(blk_e, blk_seq, nxt_e, nxt2_e, n_used, xs, w_gate, w_up, w_down)


def _combine_kernel(d0_ref, d1_ref, d2_ref, x1_ref, gt_ref, y_ref, gfin_ref, o_ref, ybuf, sems):
    tc, d = x1_ref.shape
    nt = ybuf.shape[2] // tc
    i = pl.program_id(0)
    n = pl.num_programs(0)
    slots = ybuf.shape[0]

    def issue_tile(dest_ref, slot, inline):
        _for_each_assignment(
            dest_ref, tc,
            lambda k, t, row: _row_copy(y_ref, row, ybuf.at[slot, k], t, nt,
                                        sems.at[slot]).start(priority=k),
            inline=inline)

    @pl.when(i == 0)
    def _():
        issue_tile(d0_ref, 0, False)
        issue_tile(d1_ref, 1, False)

    slot = i % slots
    _rows_wait(y_ref, TOP_K * tc, nt, sems.at[slot])
    issue_tile(d2_ref, (i + 2) % slots, True)
    y0 = _unpack_bf16_pairs(_tiles_to_rows(ybuf.at[slot, 0], tc, nt), F32)
    y1 = _unpack_bf16_pairs(_tiles_to_rows(ybuf.at[slot, 1], tc, nt), F32)
    gates = gt_ref[...]
    xo = x1_ref[...] + (gates[:, 0:1] * y0 + gates[:, 1:2] * y1)
    o_ref[...] = _rms(xo, gfin_ref[...])

    @pl.when(i == n - 1)
    def _():
        _rows_wait(y_ref, TOP_K * tc, nt, sems.at[(i + 1) % slots])
        _rows_wait(y_ref, TOP_K * tc, nt, sems.at[(i + 2) % slots])


def _combine(dest, x1, gates_t, y, g_final):
    t, d = x1.shape
    nt = d // LANES
    tc = COMBINE_TOKENS
    last = t // tc - 1
    assert last >= 1
    dest_spec = lambda ahead: pl.BlockSpec(
        (TOP_K * tc,), lambda i: (jnp.minimum(i + ahead, last),), memory_space=pltpu.SMEM)
    return pl.pallas_call(
        _combine_kernel,
        grid=(t // tc,),
        in_specs=[
            dest_spec(0), dest_spec(1), dest_spec(2),
            pl.BlockSpec((tc, d), lambda i: (i, 0)),
            pl.BlockSpec((tc, LANES), lambda i: (i, 0)),
            pl.BlockSpec(memory_space=pl.ANY),
            pl.BlockSpec((1, d), lambda i: (0, 0)),
        ],
        out_specs=pl.BlockSpec((tc, d), lambda i: (i, 0)),
        out_shape=jax.ShapeDtypeStruct((t, d), F32),
        scratch_shapes=[pltpu.VMEM((3, TOP_K, tc * nt // 2, LANES), U32),
                        pltpu.SemaphoreType.DMA((3,))],
        compiler_params=pltpu.CompilerParams(dimension_semantics=("arbitrary",)),
        name="combine",
    )(dest, dest, dest, x1, gates_t, y, g_final)


def _layer(x, g_mix, w_in, w_s, b_s, g_sgu, w_conv, w_out, g_ffn, w_rg, w_re, w_gate, w_up, w_down):
    b, s, d = x.shape
    t = b * s
    ne = w_gate.shape[0]
    bs = EXPERT_ROWS

    bias_full = jnp.repeat(b_s.T, d // N_GROUPS, axis=1)
    wr_t = jnp.concatenate([w_rg, w_re], axis=1).T
    wr_t = jnp.pad(wr_t, ((0, LANES - wr_t.shape[0]), (0, 0))).astype(BF16)
    x1, h2t, logits_t = _mixer(
        x, g_mix.reshape(1, d), w_in.astype(BF16), w_s.astype(BF16), bias_full, g_sgu.reshape(1, d),
        w_conv, w_out.astype(BF16), g_ffn.reshape(1, d), wr_t)

    info, gates_t, counts = _route(logits_t)

    n_rows = -(-(t * TOP_K + ne * (bs - 1)) // bs) * bs
    nb = n_rows // bs
    exp_tbl, blk_tbl = _plan(counts, nb)
    blk_e, nxt_e, n_used, nxt2_e, blk_seq = (
        blk_tbl[0, :nb], blk_tbl[1, :nb], blk_tbl[2, :1], blk_tbl[3, :nb], blk_tbl[4, :nb])

    dest = _place(info, exp_tbl)[:TOP_K]
    tiled = lambda n: dest.reshape(TOP_K, t // n, n).transpose(1, 0, 2).reshape(-1)
    plan = jnp.concatenate([exp_tbl[:, :3].reshape(-1), n_used])
    xs = _dispatch(plan, tiled(DISPATCH_TOKENS), h2t, n_rows)
    y = _experts(blk_e, blk_seq, nxt_e, nxt2_e, n_used, xs, w_gate, w_up, w_down)
    return x1.reshape(t, d), tiled(COMBINE_TOKENS), gates_t, y


def kernel(x, g_mix, w_in, w_s, b_s, g_sgu, w_conv, w_out, g_ffn, w_router_group, w_router_expert,
           w_gate, w_up, w_down, g_final):
    b, s, d = x.shape
    depth = g_mix.shape[0]
    assert depth == 1, "the final RMSNorm is fused into the last layer's combine"
    assert s % MIX_ROWS == 0 and MIX_ROWS % CHUNK == 0 and d % LANES == 0
    assert EXPERT_ROWS & (EXPERT_ROWS - 1) == 0, "block bookkeeping uses shifts"
    assert all((b * s) % n == 0 for n in (ROUTE_TOKENS, PLACE_TOKENS, DISPATCH_TOKENS, COMBINE_TOKENS))
    l = 0
    x1, dest, gates_t, y = _layer(
        x, g_mix[l], w_in[l], w_s[l], b_s[l], g_sgu[l], w_conv[l], w_out[l], g_ffn[l],
        w_router_group[l], w_router_expert[l], w_gate[l], w_up[l], w_down[l])
    out = _combine(dest, x1, gates_t, y, g_final.reshape(1, d))
    return out.reshape(b, s, d)
```

```python
import functools

import jax
import jax.numpy as jnp
from jax import lax
from jax.experimental import pallas as pl
from jax.experimental.pallas import tpu as pltpu

F32 = jnp.float32
BF16 = jnp.bfloat16
I32 = jnp.int32
U32 = jnp.uint32

EPS = 1e-6
LANES = 128
SUBLANES = 8
CHUNK = 128
N_GROUPS = 8
EXPERTS_PER_GROUP = 8
TOP_K = 2
CONV_K = 3
N_BRANCH = 7

MIX_ROWS = 512
ROUTE_TOKENS = 2048
ROUTE_CHUNK = 512
PLACE_TOKENS = 2048
DISPATCH_TOKENS = 2048
COMBINE_TOKENS = 512
MOVE_UNROLL = 8
WEIGHT_STAGES = 3
WEIGHT_DMA_PRIORITIES = (1, 1, 0)
EXPERT_ROWS = 256
EXPERT_LOOKAHEAD = 3
VMEM_LIMIT = 56 * 1024 * 1024


def _rms(x, g):
    return x * lax.rsqrt(jnp.mean(x * x, axis=-1, keepdims=True) + EPS) * g


def _sigmoid(x):
    return 0.5 * (1.0 + jnp.tanh(0.5 * x))


def _gelu_tanh(x):
    c = 0.7978845608028654
    return x * (0.5 * (1.0 + jnp.tanh(c * (x + 0.044715 * (x * x * x)))))


def _rows_to_tiles(dst_ref, val, rows):
    nt = val.shape[1] // LANES
    for c in range(nt):
        dst_ref[pl.ds(c, rows, stride=nt), :] = val[:, c * LANES:(c + 1) * LANES]


def _tiles_to_rows(src_ref, rows, nt):
    return jnp.concatenate([src_ref[pl.ds(c, rows, stride=nt), :] for c in range(nt)], axis=1)


def _pack_bf16_pairs(x):
    half = x.shape[1] // 2
    bits = pltpu.bitcast(x.astype(F32), U32)
    return lax.shift_right_logical(bits[:, :half], U32(16)) | bits[:, half:]


def _unpack_bf16_pairs(w, dtype):
    lo = pltpu.bitcast(lax.shift_left(w, U32(16)), F32)
    hi = pltpu.bitcast(w & U32(0xFFFF0000), F32)
    return jnp.concatenate([lo, hi], axis=1).astype(dtype)


def _mixer_kernel(x_ref, xp_ref, xn_ref, gmix_ref, win_ref, ws_ref, bias_ref, gsgu_ref, wconv_ref,
                  wout_ref, gffn_ref, wr_ref, x1_ref, h2t_ref, lt_ref, vg_ref, z_ref, acc_ref):
    ts, d = x_ref.shape[1], x_ref.shape[2]
    s = pl.program_id(1)
    ns = pl.num_programs(1)
    gw = d // N_GROUPS
    nc = ts // CHUNK

    x = x_ref[0]
    gmix = gmix_ref[...]
    h = _rms(x, gmix).astype(BF16)

    def proj(j, n=1):
        return jnp.dot(h, win_ref[:, j * d:(j + n) * d], preferred_element_type=F32)

    v_raw = proj(1)
    xh = jnp.concatenate([xp_ref[0], xn_ref[0]], axis=0)
    h_ext = jnp.concatenate([h, _rms(xh, gmix).astype(BF16)], axis=0)
    cx = jnp.dot(h_ext, win_ref[:, 3 * d:5 * d], preferred_element_type=F32)

    gv = _gelu_tanh(v_raw)
    for g in range(N_GROUPS):
        cs = slice(g * gw, (g + 1) * gw)
        blk = gv[:, cs]
        mu = jnp.mean(blk, axis=-1, keepdims=True)
        dv = blk - mu
        var = jnp.mean(dv * dv, axis=-1, keepdims=True)
        vg_ref[:, cs] = (dv * lax.rsqrt(var + EPS) * gsgu_ref[:, cs]).astype(BF16)
    for g in range(N_GROUPS):
        cs = slice(g * gw, (g + 1) * gw)
        vcat = jnp.concatenate([vg_ref[n * CHUNK:(n + 1) * CHUNK, cs] for n in range(nc)], axis=1)
        zg = jnp.dot(ws_ref[g], vcat, preferred_element_type=F32)
        for n in range(nc):
            z_ref[n * CHUNK:(n + 1) * CHUNK, cs] = zg[:, n * gw:(n + 1) * gw]
    u = _gelu_tanh(proj(0))
    ga = _sigmoid(proj(5))
    for n in range(nc):
        rs = slice(n * CHUNK, (n + 1) * CHUNK)
        acc_ref[rs, :] = ga[rs] * (u[rs] * (z_ref[rs, :] + bias_ref[...]))

    z2 = cx[:ts, :d] * cx[:ts, d:]
    z2h = cx[ts:, :d] * cx[ts:, d:]
    prev = jnp.where(s > 0, z2h[SUBLANES - 1:SUBLANES, :], 0.0)
    nxt = jnp.where(s < ns - 1, z2h[SUBLANES:SUBLANES + 1, :], 0.0)
    row = lax.broadcasted_iota(I32, (ts, d), 0)
    zm1 = jnp.where(row == 0, prev, pltpu.roll(z2, 1, 0))
    zp1 = jnp.where(row == ts - 1, nxt, pltpu.roll(z2, ts - 1, 0))
    conv = wconv_ref[0:1, :] * zm1 + wconv_ref[1:2, :] * z2 + wconv_ref[2:3, :] * zp1
    cb = proj(2)
    gb = _sigmoid(proj(6))
    merged = acc_ref[...] + gb * (cb * conv)

    x1 = x + jnp.dot(merged.astype(BF16), wout_ref[...], preferred_element_type=F32)
    x1_ref[0] = x1

    h2 = _rms(x1, gffn_ref[...]).astype(BF16)
    lt_ref[...] = lax.dot_general(wr_ref[...], h2, (((1,), (1,)), ((), ())),
                                  preferred_element_type=F32)
    _rows_to_tiles(h2t_ref, _pack_bf16_pairs(h2), ts)


def _mixer(x, g_mix, w_in_b, w_s_b, bias_full, g_sgu, w_conv, w_out_b, g_ffn, wr_t):
    b, s, d = x.shape
    ts = MIX_ROWS
    ns = s // ts
    t = b * s
    ntp = d // (2 * LANES)
    hb = ts // SUBLANES
    last_hb = s // SUBLANES - 1

    const = lambda *shape: pl.BlockSpec(shape, lambda bi, si: (0,) * len(shape))
    in_specs = [
        pl.BlockSpec((1, ts, d), lambda bi, si: (bi, si, 0)),
        pl.BlockSpec((1, SUBLANES, d), lambda bi, si: (bi, jnp.maximum(si * hb - 1, 0), 0)),
        pl.BlockSpec((1, SUBLANES, d), lambda bi, si: (bi, jnp.minimum((si + 1) * hb, last_hb), 0)),
        const(1, d),
        pl.BlockSpec((d, N_BRANCH * d), lambda bi, si: (0, 0), pipeline_mode=pl.Buffered(1)),
        const(N_GROUPS, CHUNK, CHUNK),
        const(CHUNK, d),
        const(1, d),
        const(CONV_K, d),
        const(d, d),
        const(1, d),
        const(LANES, d),
    ]
    out_specs = [
        pl.BlockSpec((1, ts, d), lambda bi, si: (bi, si, 0)),
        pl.BlockSpec((ts * ntp, LANES), lambda bi, si: (bi * ns + si, 0)),
        pl.BlockSpec((LANES, ts), lambda bi, si: (0, bi * ns + si)),
    ]
    out_shape = [
        jax.ShapeDtypeStruct((b, s, d), F32),
        jax.ShapeDtypeStruct((t * ntp, LANES), U32),
        jax.ShapeDtypeStruct((LANES, t), F32),
    ]
    return pl.pallas_call(
        _mixer_kernel,
        grid=(b, ns),
        in_specs=in_specs,
        out_specs=out_specs,
        out_shape=out_shape,
        scratch_shapes=[pltpu.VMEM((ts, d), BF16), pltpu.VMEM((ts, d), F32), pltpu.VMEM((ts, d), F32)],
        compiler_params=pltpu.CompilerParams(
            dimension_semantics=("arbitrary", "arbitrary"), vmem_limit_bytes=VMEM_LIMIT),
        name="mixer",
    )(x, x, x, g_mix, w_in_b, w_s_b, bias_full, g_sgu, w_conv, w_out_b, g_ffn, wr_t)


def _route_kernel(lt_ref, info_ref, gt_ref, cnt_ref, carry_ref):
    tb = lt_ref.shape[1]
    ne = N_GROUPS * EXPERTS_PER_GROUP

    @pl.when(pl.program_id(0) == 0)
    def _():
        carry_ref[...] = jnp.zeros_like(carry_ref)

    row8 = lax.broadcasted_iota(I32, (SUBLANES, tb), 0)
    gl = lt_ref[0:N_GROUPS, :]
    gmax = jnp.max(gl, axis=0, keepdims=True)
    gidx = jnp.min(jnp.where(gl == gmax, row8, N_GROUPS), axis=0, keepdims=True)
    pg = 1.0 / jnp.sum(jnp.exp(gl - gmax), axis=0, keepdims=True)

    sel = jnp.zeros((EXPERTS_PER_GROUP, tb), F32)
    for g in range(N_GROUPS):
        lo = N_GROUPS + g * EXPERTS_PER_GROUP
        sel = jnp.where(gidx == g, lt_ref[lo:lo + EXPERTS_PER_GROUP, :], sel)
    m1 = jnp.max(sel, axis=0, keepdims=True)
    i1 = jnp.min(jnp.where(sel == m1, row8, EXPERTS_PER_GROUP), axis=0, keepdims=True)
    sel2 = jnp.where(row8 == i1, -jnp.inf, sel)
    m2 = jnp.max(sel2, axis=0, keepdims=True)
    i2 = jnp.min(jnp.where(sel2 == m2, row8, EXPERTS_PER_GROUP), axis=0, keepdims=True)
    e2 = jnp.exp(m2 - m1)
    den = 1.0 + e2
    gate0 = pg * (1.0 / den)
    gate1 = pg * (e2 / den)
    eid0 = gidx * EXPERTS_PER_GROUP + i1
    eid1 = gidx * EXPERTS_PER_GROUP + i2

    rowe = lax.broadcasted_iota(I32, (ne, tb), 0)
    hit0 = rowe == eid0
    hit1 = rowe == eid1
    onehot = jnp.where(hit0 | hit1, 1.0, 0.0)
    sub = ROUTE_CHUNK
    before = (lax.broadcasted_iota(I32, (sub, sub), 0) < lax.broadcasted_iota(I32, (sub, sub), 1))
    before = jnp.where(before, 1.0, 0.0).astype(BF16)
    carry = carry_ref[:, 0:1]
    parts = []
    for c in range(tb // sub):
        part = onehot[:, c * sub:(c + 1) * sub]
        parts.append(jnp.dot(part.astype(BF16), before, preferred_element_type=F32) + carry)
        carry = carry + jnp.sum(part, axis=1, keepdims=True)
    base = jnp.concatenate(parts, axis=1)
    rank0 = jnp.sum(jnp.where(hit0, base, 0.0), axis=0, keepdims=True).astype(I32)
    rank1 = jnp.sum(jnp.where(hit1, base, 0.0), axis=0, keepdims=True).astype(I32)
    carry_ref[...] = jnp.broadcast_to(carry, carry_ref.shape)
    cnt_ref[...] = carry_ref[...].astype(I32)

    info_ref[...] = jnp.where(row8 == 0, eid0, jnp.where(row8 == 1, eid1,
                              jnp.where(row8 == 2, rank0, jnp.where(row8 == 3, rank1, 0))))
    rowl = lax.broadcasted_iota(I32, (LANES, tb), 0)
    gates = jnp.where(rowl == 0, gate0, jnp.where(rowl == 1, gate1, 0.0))
    gt_ref[...] = gates.T


def _route(logits_t):
    t = logits_t.shape[1]
    tb = ROUTE_TOKENS
    ne = N_GROUPS * EXPERTS_PER_GROUP
    return pl.pallas_call(
        _route_kernel,
        grid=(t // tb,),
        in_specs=[pl.BlockSpec((LANES, tb), lambda i: (0, i))],
        out_specs=[
            pl.BlockSpec((SUBLANES, tb), lambda i: (0, i)),
            pl.BlockSpec((tb, LANES), lambda i: (i, 0)),
            pl.BlockSpec((ne, LANES), lambda i: (0, 0)),
        ],
        out_shape=[
            jax.ShapeDtypeStruct((SUBLANES, t), I32),
            jax.ShapeDtypeStruct((t, LANES), F32),
            jax.ShapeDtypeStruct((ne, LANES), I32),
        ],
        scratch_shapes=[pltpu.VMEM((ne, LANES), F32)],
        compiler_params=pltpu.CompilerParams(dimension_semantics=("arbitrary",)),
        name="route",
    )(logits_t)


def _plan_kernel(cnt_ref, exp_ref, blk_ref):
    ne = cnt_ref.shape[0]
    nbp = blk_ref.shape[1]
    shift = EXPERT_ROWS.bit_length() - 1
    pad_rows = lambda c: lax.shift_left(lax.shift_right_logical(c + (EXPERT_ROWS - 1), shift), shift)
    cnt = cnt_ref[...]
    padded = pad_rows(cnt)
    padded_lanes = pad_rows(cnt.astype(F32).T[:ne, :ne].astype(I32))
    e_sub = lax.broadcasted_iota(I32, (ne, ne), 0)
    e_lane = lax.broadcasted_iota(I32, (ne, ne), 1)
    pend = jnp.sum(jnp.where(e_lane <= e_sub, padded_lanes, 0), axis=1, keepdims=True)
    pstart = pend - padded[:, 0:1]

    lane = lax.broadcasted_iota(I32, (ne, LANES), 1)
    exp_ref[...] = jnp.where(lane == 0, pstart, jnp.where(lane == 1, cnt, jnp.where(lane == 2, padded, 0)))

    first_row = lax.broadcasted_iota(I32, (ne, nbp), 1) * EXPERT_ROWS
    e_col = lax.broadcasted_iota(I32, (ne, nbp), 0)
    blk_e = jnp.minimum(jnp.sum(jnp.where(pend <= first_row, 1, 0), axis=0, keepdims=True), ne - 1)
    has_rows = padded[:, 0:1] > 0
    nxt_e = jnp.min(jnp.where((e_col > blk_e) & has_rows, e_col, ne), axis=0, keepdims=True)
    nxt2_e = jnp.min(jnp.where((e_col > nxt_e) & has_rows, e_col, ne), axis=0, keepdims=True)
    nxt_e = jnp.where(nxt_e == ne, -1, nxt_e)
    nxt2_e = jnp.where(nxt2_e == ne, -1, nxt2_e)
    n_used = lax.shift_right_logical(jnp.max(pend, axis=0, keepdims=True), shift)
    blk_seq = jnp.sum(jnp.where((e_col < blk_e) & has_rows, 1, 0), axis=0, keepdims=True)
    row8 = lax.broadcasted_iota(I32, (SUBLANES, nbp), 0)
    blk_ref[...] = jnp.where(row8 == 0, blk_e, jnp.where(row8 == 1, nxt_e, jnp.where(
        row8 == 2, n_used, jnp.where(row8 == 3, nxt2_e, jnp.where(row8 == 4, blk_seq, 0)))))


def _plan(counts, nb):
    ne = counts.shape[0]
    nbp = -(-nb // LANES) * LANES
    return pl.pallas_call(
        _plan_kernel,
        out_shape=[jax.ShapeDtypeStruct((ne, LANES), I32), jax.ShapeDtypeStruct((SUBLANES, nbp), I32)],
        name="plan",
    )(counts)


def _place_kernel(info_ref, ps_ref, dest_ref):
    tb = info_ref.shape[1]
    ne = ps_ref.shape[0]
    rowe = lax.broadcasted_iota(I32, (ne, tb), 0)
    row8 = lax.broadcasted_iota(I32, (SUBLANES, tb), 0)
    ps = ps_ref[:, 0:1]
    dest = jnp.zeros((SUBLANES, tb), I32)
    for k in range(TOP_K):
        start = jnp.sum(jnp.where(rowe == info_ref[k:k + 1, :], ps, 0), axis=0, keepdims=True)
        dest = jnp.where(row8 == k, start + info_ref[TOP_K + k:TOP_K + k + 1, :], dest)
    dest_ref[...] = dest


def _place(info, exp_tbl):
    t = info.shape[1]
    tb = PLACE_TOKENS
    ne = exp_tbl.shape[0]
    return pl.pallas_call(
        _place_kernel,
        grid=(t // tb,),
        in_specs=[pl.BlockSpec((SUBLANES, tb), lambda i: (0, i)),
                  pl.BlockSpec((ne, LANES), lambda i: (0, 0))],
        out_specs=pl.BlockSpec((SUBLANES, tb), lambda i: (0, i)),
        out_shape=jax.ShapeDtypeStruct((SUBLANES, t), I32),
        compiler_params=pltpu.CompilerParams(dimension_semantics=("arbitrary",)),
        name="place",
    )(info, exp_tbl)


def _row_copy(src_ref, src_row, dst_ref, dst_row, nt, sem):
    first = lambda row: row * nt if isinstance(row, int) else pl.multiple_of(row * nt, nt)
    return pltpu.make_async_copy(
        src_ref.at[pl.ds(first(src_row), nt), :], dst_ref.at[pl.ds(first(dst_row), nt), :], sem)


def _rows_wait(ref, n_rows, nt, sem):
    pltpu.make_async_copy(ref.at[pl.ds(0, n_rows * nt), :], ref.at[pl.ds(0, n_rows * nt), :], sem).wait()


def _for_each_assignment(dest_ref, n_tok, start_copy, inline=False):
    def group(g, c):
        t0 = g * MOVE_UNROLL
        rows = [[dest_ref[k * n_tok + t0 + u] for k in range(TOP_K)] for u in range(MOVE_UNROLL)]
        for u in range(MOVE_UNROLL):
            for k in range(TOP_K):
                start_copy(k, t0 + u, rows[u][k])
        return c

    if inline:
        for g in range(n_tok // MOVE_UNROLL):
            group(g, 0)
    else:
        lax.fori_loop(0, n_tok // MOVE_UNROLL, group, 0)


def _dispatch_kernel(nt, nb, plan_ref, dest_ref, h2_ref, xs_ref, zbuf, sem, zsem):
    td = h2_ref.shape[0] // nt
    ne = (plan_ref.shape[0] - 1) // 3
    n_used = plan_ref[3 * ne]
    blk_rows = zbuf.shape[0]

    def zero_rows(start_not_wait):
        def fire(c):
            c.start() if start_not_wait else c.wait()

        def expert(e, carry):
            cnt = plan_ref[3 * e + 1]
            row = plan_ref[3 * e] + cnt
            n = plan_ref[3 * e + 2] - cnt
            p = EXPERT_ROWS // 2
            while p >= 1:
                has = (n & p) != 0
                r, sz = row, p

                @pl.when(has)
                def _():
                    fire(pltpu.make_async_copy(
                        zbuf.at[pl.ds(0, sz * nt), :],
                        xs_ref.at[pl.ds(pl.multiple_of(r * nt, nt), sz * nt), :], zsem))

                row = row + jnp.where(has, p, 0)
                p //= 2
            return carry

        lax.fori_loop(0, ne, expert, 0)

        def tail(q, carry):
            fire(pltpu.make_async_copy(
                zbuf, xs_ref.at[pl.ds(pl.multiple_of(q * blk_rows, blk_rows), blk_rows), :], zsem))
            return carry

        lax.fori_loop(n_used, nb, tail, 0)

    @pl.when(pl.program_id(0) == 0)
    def _():
        zbuf[...] = jnp.zeros_like(zbuf)
        zero_rows(True)

    _for_each_assignment(
        dest_ref, td,
        lambda k, t, row: _row_copy(h2_ref, t, xs_ref, row, nt, sem).start(priority=k))
    _rows_wait(xs_ref, TOP_K * td, nt, sem)

    @pl.when(pl.program_id(0) == 0)
    def _():
        zero_rows(False)


def _dispatch(plan, dest, h2t, n_rows):
    td = DISPATCH_TOKENS
    t = dest.shape[0] // TOP_K
    nt = h2t.shape[0] // t
    nb = n_rows // EXPERT_ROWS
    grid_spec = pltpu.PrefetchScalarGridSpec(
        num_scalar_prefetch=1,
        grid=(t // td,),
        in_specs=[
            pl.BlockSpec((TOP_K * td,), lambda i, plan: (i,), memory_space=pltpu.SMEM),
            pl.BlockSpec((td * nt, LANES), lambda i, plan: (i, 0)),
        ],
        out_specs=pl.BlockSpec(memory_space=pl.ANY),
        scratch_shapes=[pltpu.VMEM((EXPERT_ROWS * nt, LANES), h2t.dtype),
                        pltpu.SemaphoreType.DMA, pltpu.SemaphoreType.DMA],
    )
    return pl.pallas_call(
        functools.partial(_dispatch_kernel, nt, nb),
        grid_spec=grid_spec,
        out_shape=jax.ShapeDtypeStruct((n_rows * nt, LANES), h2t.dtype),
        compiler_params=pltpu.CompilerParams(dimension_semantics=("arbitrary",)),
        name="dispatch",
    )(plan, dest, h2t)


def _experts_kernel(nb, be_ref, seq_ref, nx_ref, nx2_ref, nu_ref, xs_hbm, wg_hbm, wu_hbm, wd_hbm,
                    y_hbm, xbuf, ybuf, wg_st, wu_st, wd_st, wgu_b, wd_b, hid_s, wsems, isems, osems):
    ring = xbuf.shape[0]
    stages, d, de = wg_st.shape
    bs = EXPERT_ROWS
    ntp = xbuf.shape[1] // bs
    nu = nu_ref[0]

    def block(ref, q):
        rows = bs * ntp
        return ref.at[pl.ds(pl.multiple_of(q * rows, rows), rows), :]

    def in_copy(q):
        return pltpu.make_async_copy(block(xs_hbm, q), xbuf.at[q % ring], isems.at[q % ring])

    def out_copy(q):
        return pltpu.make_async_copy(ybuf.at[q % ring], block(y_hbm, q), osems.at[q % ring])

    def weight_copies(ex, slot):
        return (pltpu.make_async_copy(wg_hbm.at[ex], wg_st.at[slot], wsems.at[slot]),
                pltpu.make_async_copy(wu_hbm.at[ex], wu_st.at[slot], wsems.at[slot]),
                pltpu.make_async_copy(wd_hbm.at[ex], wd_st.at[slot], wsems.at[slot]))

    def start_weights(ex, slot):
        for c, prio in zip(weight_copies(ex, slot), WEIGHT_DMA_PRIORITIES):
            c.start(priority=prio)

    xbuf[...] = jnp.zeros_like(xbuf)
    hid_s[...] = jnp.zeros_like(hid_s)
    wgu_b[...] = jnp.zeros_like(wgu_b)
    wd_b[...] = jnp.zeros_like(wd_b)
    start_weights(be_ref[0], 0)

    @pl.when(nx_ref[0] >= 0)
    def _():
        start_weights(nx_ref[0], 1)

    for q in range(EXPERT_LOOKAHEAD):
        @pl.when(q < nu)
        def _():
            in_copy(q).start()

    seq_of = lambda j: seq_ref[jnp.clip(j, 0, nb - 1)]

    def load_expert_if_first(j):
        jc = jnp.clip(j, 0, nb - 1)
        e = be_ref[jc]

        @pl.when((j < nu) & ((j == 0) | (e != be_ref[jnp.maximum(jc - 1, 0)])))
        def _():
            seq = seq_ref[jc]
            slot = lax.rem(seq, stages)
            for c in weight_copies(e, slot):
                c.wait()
            nx2 = nx2_ref[jc]

            @pl.when(nx2 >= 0)
            def _():
                start_weights(nx2, lax.rem(seq + 2, stages))

            half = lax.rem(seq, 2)
            wgu_b[half, :, :de] = wg_st[slot].astype(BF16)
            wgu_b[half, :, de:] = wu_st[slot].astype(BF16)
            wd_b[lax.rem(seq, 3)] = wd_st[slot].astype(BF16)

    def up_proj(j):
        xb = _unpack_bf16_pairs(_tiles_to_rows(xbuf.at[j % ring], bs, ntp), BF16)
        gu = jnp.dot(xb, wgu_b[lax.rem(seq_of(j), 2)], preferred_element_type=F32)
        gate = gu[:, :de]
        return ((gate * _sigmoid(gate)) * gu[:, de:]).astype(BF16)

    def down_proj(hid, j):
        y = jnp.dot(hid, wd_b[lax.rem(seq_of(j), 3)], preferred_element_type=F32)
        _rows_to_tiles(ybuf.at[(j + ring) % ring], _pack_bf16_pairs(y.astype(BF16)), bs)

    def step(m, carry):
        j0 = 2 * m
        for j in (j0, j0 + 1):
            load_expert_if_first(j)
        for j in (j0, j0 + 1):
            @pl.when(j + EXPERT_LOOKAHEAD < nu)
            def _():
                in_copy(j + EXPERT_LOOKAHEAD).start()

            @pl.when(j < nu)
            def _():
                in_copy(j).wait()
        for j in (j0 - 1, j0):
            @pl.when(j >= ring)
            def _():
                out_copy(j - ring).wait()

        hid0 = up_proj(j0)
        down_proj(hid_s[...], j0 - 1)
        hid1 = up_proj(j0 + 1)
        down_proj(hid0, j0)
        hid_s[...] = hid1

        for j in (j0 - 1, j0):
            @pl.when((j >= 0) & (j < nu))
            def _():
                out_copy(j).start()

        return carry

    lax.fori_loop(0, nu // 2 + 1, step, 0)

    last_written = 2 * (nu // 2)
    for r in range(1, ring + 1):
        q = last_written - ring + r

        @pl.when((q >= 0) & (q < nu))
        def _():
            out_copy(q).wait()

    ybuf[0] = jnp.zeros(ybuf.shape[1:], ybuf.dtype)

    def zero_start(q, carry):
        pltpu.make_async_copy(ybuf.at[0], block(y_hbm, q), osems.at[0]).start()
        return carry

    def zero_wait(q, carry):
        pltpu.make_async_copy(ybuf.at[0], block(y_hbm, q), osems.at[0]).wait()
        return carry

    lax.fori_loop(nu, nb, zero_start, 0)
    lax.fori_loop(nu, nb, zero_wait, 0)


def _experts(blk_e, blk_seq, nxt_e, nxt2_e, n_used, xs, w_gate, w_up, w_down):
    ne, d, de = w_gate.shape
    in_rows = EXPERT_ROWS * (d // LANES) // 2
    nb = xs.shape[0] // in_rows
    ring = EXPERT_LOOKAHEAD + 2
    any_spec = pl.BlockSpec(memory_space=pl.ANY)
    grid_spec = pltpu.PrefetchScalarGridSpec(
        num_scalar_prefetch=5,
        grid=(1,),
        in_specs=[any_spec, any_spec, any_spec, any_spec],
        out_specs=any_spec,
        scratch_shapes=[pltpu.VMEM((ring, in_rows, LANES), U32), pltpu.VMEM((ring, in_rows, LANES), U32),
                        pltpu.VMEM((WEIGHT_STAGES, d, de), F32), pltpu.VMEM((WEIGHT_STAGES, d, de), F32),
                        pltpu.VMEM((WEIGHT_STAGES, de, d), F32),
                        pltpu.VMEM((2, d, 2 * de), BF16), pltpu.VMEM((3, de, d), BF16),
                        pltpu.VMEM((EXPERT_ROWS, de), BF16),
                        pltpu.SemaphoreType.DMA((WEIGHT_STAGES,)),
                        pltpu.SemaphoreType.DMA((ring,)), pltpu.SemaphoreType.DMA((ring,))],
    )
    return pl.pallas_call(
        functools.partial(_experts_kernel, nb),
        grid_spec=grid_spec,
        out_shape=jax.ShapeDtypeStruct((nb * in_rows, LANES), U32),
        compiler_params=pltpu.CompilerParams(
            dimension_semantics=("arbitrary",), vmem_limit_bytes=VMEM_LIMIT),
        name="experts",
    )(blk_e, blk_seq, nxt_e, nxt2_e, n_used, xs, w_gate, w_up, w_down)


def _combine_kernel(d0_ref, d1_ref, d2_ref, x1_ref, gt_ref, y_ref, gfin_ref, o_ref, ybuf, sems):
    tc, d = x1_ref.shape
    nt = ybuf.shape[2] // tc
    i = pl.program_id(0)
    n = pl.num_programs(0)
    slots = ybuf.shape[0]

    def issue_tile(dest_ref, slot, inline):
        _for_each_assignment(
            dest_ref, tc,
            lambda k, t, row: _row_copy(y_ref, row, ybuf.at[slot, k], t, nt,
                                        sems.at[slot]).start(priority=k),
            inline=inline)

    @pl.when(i == 0)
    def _():
        issue_tile(d0_ref, 0, False)
        issue_tile(d1_ref, 1, False)

    slot = i % slots
    _rows_wait(y_ref, TOP_K * tc, nt, sems.at[slot])
    issue_tile(d2_ref, (i + 2) % slots, True)
    y0 = _unpack_bf16_pairs(_tiles_to_rows(ybuf.at[slot, 0], tc, nt), F32)
    y1 = _unpack_bf16_pairs(_tiles_to_rows(ybuf.at[slot, 1], tc, nt), F32)
    gates = gt_ref[...]
    xo = x1_ref[...] + (gates[:, 0:1] * y0 + gates[:, 1:2] * y1)
    o_ref[...] = _rms(xo, gfin_ref[...])

    @pl.when(i == n - 1)
    def _():
        _rows_wait(y_ref, TOP_K * tc, nt, sems.at[(i + 1) % slots])
        _rows_wait(y_ref, TOP_K * tc, nt, sems.at[(i + 2) % slots])


def _combine(dest, x1, gates_t, y, g_final):
    t, d = x1.shape
    nt = d // LANES
    tc = COMBINE_TOKENS
    last = t // tc - 1
    assert last >= 1
    dest_spec = lambda ahead: pl.BlockSpec(
        (TOP_K * tc,), lambda i: (jnp.minimum(i + ahead, last),), memory_space=pltpu.SMEM)
    return pl.pallas_call(
        _combine_kernel,
        grid=(t // tc,),
        in_specs=[
            dest_spec(0), dest_spec(1), dest_spec(2),
            pl.BlockSpec((tc, d), lambda i: (i, 0)),
            pl.BlockSpec((tc, LANES), lambda i: (i, 0)),
            pl.BlockSpec(memory_space=pl.ANY),
            pl.BlockSpec((1, d), lambda i: (0, 0)),
        ],
        out_specs=pl.BlockSpec((tc, d), lambda i: (i, 0)),
        out_shape=jax.ShapeDtypeStruct((t, d), F32),
        scratch_shapes=[pltpu.VMEM((3, TOP_K, tc * nt // 2, LANES), U32),
                        pltpu.SemaphoreType.DMA((3,))],
        compiler_params=pltpu.CompilerParams(dimension_semantics=("arbitrary",)),
        name="combine",
    )(dest, dest, dest, x1, gates_t, y, g_final)


def _layer(x, g_mix, w_in, w_s, b_s, g_sgu, w_conv, w_out, g_ffn, w_rg, w_re, w_gate, w_up, w_down):
    b, s, d = x.shape
    t = b * s
    ne = w_gate.shape[0]
    bs = EXPERT_ROWS

    bias_full = jnp.repeat(b_s.T, d // N_GROUPS, axis=1)
    wr_t = jnp.concatenate([w_rg, w_re], axis=1).T
    wr_t = jnp.pad(wr_t, ((0, LANES - wr_t.shape[0]), (0, 0))).astype(BF16)
    x1, h2t, logits_t = _mixer(
        x, g_mix.reshape(1, d), w_in.astype(BF16), w_s.astype(BF16), bias_full, g_sgu.reshape(1, d),
        w_conv, w_out.astype(BF16), g_ffn.reshape(1, d), wr_t)

    info, gates_t, counts = _route(logits_t)

    n_rows = -(-(t * TOP_K + ne * (bs - 1)) // bs) * bs
    nb = n_rows // bs
    exp_tbl, blk_tbl = _plan(counts, nb)
    blk_e, nxt_e, n_used, nxt2_e, blk_seq = (
        blk_tbl[0, :nb], blk_tbl[1, :nb], blk_tbl[2, :1], blk_tbl[3, :nb], blk_tbl[4, :nb])

    dest = _place(info, exp_tbl)[:TOP_K]
    tiled = lambda n: dest.reshape(TOP_K, t // n, n).transpose(1, 0, 2).reshape(-1)
    plan = jnp.concatenate([exp_tbl[:, :3].reshape(-1), n_used])
    xs = _dispatch(plan, tiled(DISPATCH_TOKENS), h2t, n_rows)
    y = _experts(blk_e, blk_seq, nxt_e, nxt2_e, n_used, xs, w_gate, w_up, w_down)
    return x1.reshape(t, d), tiled(COMBINE_TOKENS), gates_t, y


def kernel(x, g_mix, w_in, w_s, b_s, g_sgu, w_conv, w_out, g_ffn, w_router_group, w_router_expert,
           w_gate, w_up, w_down, g_final):
    b, s, d = x.shape
    depth = g_mix.shape[0]
    assert depth == 1, "the final RMSNorm is fused into the last layer's combine"
    assert s % MIX_ROWS == 0 and MIX_ROWS % CHUNK == 0 and d % LANES == 0
    assert EXPERT_ROWS & (EXPERT_ROWS - 1) == 0, "block bookkeeping uses shifts"
    assert all((b * s) % n == 0 for n in (ROUTE_TOKENS, PLACE_TOKENS, DISPATCH_TOKENS, COMBINE_TOKENS))
    l = 0
    x1, dest, gates_t, y = _layer(
        x, g_mix[l], w_in[l], w_s[l], b_s[l], g_sgu[l], w_conv[l], w_out[l], g_ffn[l],
        w_router_group[l], w_router_expert[l], w_gate[l], w_up[l], w_down[l])
    out = _combine(dest, x1, gates_t, y, g_final.reshape(1, d))
    return out.reshape(b, s, d)
```

```python
import functools

import jax
import jax.numpy as jnp
from jax import lax
from jax.experimental import pallas as pl
from jax.experimental.pallas import tpu as pltpu

F32 = jnp.float32
BF16 = jnp.bfloat16
I32 = jnp.int32
U32 = jnp.uint32

EPS = 1e-6
LANES = 128
SUBLANES = 8
CHUNK = 128
N_GROUPS = 8
EXPERTS_PER_GROUP = 8
TOP_K = 2
CONV_K = 3
N_BRANCH = 7

MIX_ROWS = 512
ROUTE_TOKENS = 2048
ROUTE_CHUNK = 512
PLACE_TOKENS = 2048
DISPATCH_TOKENS = 2048
COMBINE_TOKENS = 512
MOVE_UNROLL = 8
WEIGHT_STAGES = 3
WEIGHT_DMA_PRIORITIES = (1, 1, 0)
EXPERT_ROWS = 256
EXPERT_LOOKAHEAD = 3
VMEM_LIMIT = 56 * 1024 * 1024


def _rms(x, g):
    return x * lax.rsqrt(jnp.mean(x * x, axis=-1, keepdims=True) + EPS) * g


def _sigmoid(x):
    return 0.5 * (1.0 + jnp.tanh(0.5 * x))


def _gelu_tanh(x):
    c = 0.7978845608028654
    return x * (0.5 * (1.0 + jnp.tanh(c * (x + 0.044715 * (x * x * x)))))


def _rows_to_tiles(dst_ref, val, rows):
    nt = val.shape[1] // LANES
    for c in range(nt):
        dst_ref[pl.ds(c, rows, stride=nt), :] = val[:, c * LANES:(c + 1) * LANES]


def _tiles_to_rows(src_ref, rows, nt):
    return jnp.concatenate([src_ref[pl.ds(c, rows, stride=nt), :] for c in range(nt)], axis=1)


def _pack_bf16_pairs(x):
    half = x.shape[1] // 2
    bits = pltpu.bitcast(x.astype(F32), U32)
    return lax.shift_right_logical(bits[:, :half], U32(16)) | bits[:, half:]


def _unpack_bf16_pairs(w, dtype):
    lo = pltpu.bitcast(lax.shift_left(w, U32(16)), F32)
    hi = pltpu.bitcast(w & U32(0xFFFF0000), F32)
    return jnp.concatenate([lo, hi], axis=1).astype(dtype)


def _mixer_kernel(x_ref, xp_ref, xn_ref, gmix_ref, win_hbm, ws_ref, bias_ref, gsgu_ref, wconv_ref,
                  wout_hbm, gffn_ref, wr_ref, x1_ref, h2t_ref, lt_ref,
                  vg_ref, z_ref, acc_ref, win_ref, wout_ref, stage, wsems):
    ts, d = x_ref.shape[1], x_ref.shape[2]
    s = pl.program_id(1)
    ns = pl.num_programs(1)
    gw = d // N_GROUPS
    nc = ts // CHUNK

    @pl.when((pl.program_id(0) == 0) & (s == 0))
    def _():
        def slab_copy(j):
            src = win_hbm.at[:, pl.ds(j * d, d)] if j < N_BRANCH else wout_hbm
            return pltpu.make_async_copy(src, stage.at[j % 2], wsems.at[j % 2])

        slab_copy(0).start()
        for j in range(N_BRANCH + 1):
            if j < N_BRANCH:
                slab_copy(j + 1).start()
            slab_copy(j).wait()
            if j < N_BRANCH:
                win_ref[:, j * d:(j + 1) * d] = stage[j % 2].astype(BF16)
            else:
                wout_ref[...] = stage[j % 2].astype(BF16)

    x = x_ref[0]
    gmix = gmix_ref[...]
    h = _rms(x, gmix).astype(BF16)

    def proj(j, n=1):
        return jnp.dot(h, win_ref[:, j * d:(j + n) * d], preferred_element_type=F32)

    v_raw = proj(1)
    xh = jnp.concatenate([xp_ref[0], xn_ref[0]], axis=0)
    h_ext = jnp.concatenate([h, _rms(xh, gmix).astype(BF16)], axis=0)
    cx = jnp.dot(h_ext, win_ref[:, 3 * d:5 * d], preferred_element_type=F32)

    gv = _gelu_tanh(v_raw)
    for g in range(N_GROUPS):
        cs = slice(g * gw, (g + 1) * gw)
        blk = gv[:, cs]
        mu = jnp.mean(blk, axis=-1, keepdims=True)
        dv = blk - mu
        var = jnp.mean(dv * dv, axis=-1, keepdims=True)
        vg_ref[:, cs] = (dv * lax.rsqrt(var + EPS) * gsgu_ref[:, cs]).astype(BF16)
    for g in range(N_GROUPS):
        cs = slice(g * gw, (g + 1) * gw)
        vcat = jnp.concatenate([vg_ref[n * CHUNK:(n + 1) * CHUNK, cs] for n in range(nc)], axis=1)
        zg = jnp.dot(ws_ref[g].astype(BF16), vcat, preferred_element_type=F32)
        for n in range(nc):
            z_ref[n * CHUNK:(n + 1) * CHUNK, cs] = zg[:, n * gw:(n + 1) * gw]
    u = _gelu_tanh(proj(0))
    ga = _sigmoid(proj(5))
    for n in range(nc):
        rs = slice(n * CHUNK, (n + 1) * CHUNK)
        acc_ref[rs, :] = ga[rs] * (u[rs] * (z_ref[rs, :] + bias_ref[...]))

    z2 = cx[:ts, :d] * cx[:ts, d:]
    z2h = cx[ts:, :d] * cx[ts:, d:]
    prev = jnp.where(s > 0, z2h[SUBLANES - 1:SUBLANES, :], 0.0)
    nxt = jnp.where(s < ns - 1, z2h[SUBLANES:SUBLANES + 1, :], 0.0)
    row = lax.broadcasted_iota(I32, (ts, d), 0)
    zm1 = jnp.where(row == 0, prev, pltpu.roll(z2, 1, 0))
    zp1 = jnp.where(row == ts - 1, nxt, pltpu.roll(z2, ts - 1, 0))
    conv = wconv_ref[0:1, :] * zm1 + wconv_ref[1:2, :] * z2 + wconv_ref[2:3, :] * zp1
    cb = proj(2)
    gb = _sigmoid(proj(6))
    merged = acc_ref[...] + gb * (cb * conv)

    x1 = x + jnp.dot(merged.astype(BF16), wout_ref[...], preferred_element_type=F32)
    x1_ref[0] = x1

    h2 = _rms(x1, gffn_ref[...]).astype(BF16)
    lt_ref[...] = lax.dot_general(wr_ref[...].astype(BF16), h2, (((1,), (1,)), ((), ())),
                                  preferred_element_type=F32)
    _rows_to_tiles(h2t_ref, _pack_bf16_pairs(h2), ts)


def _mixer(x, g_mix, w_in, w_s, bias_full, g_sgu, w_conv, w_out, g_ffn, wr_t):
    b, s, d = x.shape
    ts = MIX_ROWS
    ns = s // ts
    t = b * s
    ntp = d // (2 * LANES)
    hb = ts // SUBLANES
    last_hb = s // SUBLANES - 1

    const = lambda *shape: pl.BlockSpec(shape, lambda bi, si: (0,) * len(shape))
    in_specs = [
        pl.BlockSpec((1, ts, d), lambda bi, si: (bi, si, 0)),
        pl.BlockSpec((1, SUBLANES, d), lambda bi, si: (bi, jnp.maximum(si * hb - 1, 0), 0)),
        pl.BlockSpec((1, SUBLANES, d), lambda bi, si: (bi, jnp.minimum((si + 1) * hb, last_hb), 0)),
        const(1, d),
        pl.BlockSpec(memory_space=pl.ANY),
        const(N_GROUPS, CHUNK, CHUNK),
        const(CHUNK, d),
        const(1, d),
        const(CONV_K, d),
        pl.BlockSpec(memory_space=pl.ANY),
        const(1, d),
        const(LANES, d),
    ]
    out_specs = [
        pl.BlockSpec((1, ts, d), lambda bi, si: (bi, si, 0)),
        pl.BlockSpec((ts * ntp, LANES), lambda bi, si: (bi * ns + si, 0)),
        pl.BlockSpec((LANES, ts), lambda bi, si: (0, bi * ns + si)),
    ]
    out_shape = [
        jax.ShapeDtypeStruct((b, s, d), F32),
        jax.ShapeDtypeStruct((t * ntp, LANES), U32),
        jax.ShapeDtypeStruct((LANES, t), F32),
    ]
    return pl.pallas_call(
        _mixer_kernel,
        grid=(b, ns),
        in_specs=in_specs,
        out_specs=out_specs,
        out_shape=out_shape,
        scratch_shapes=[pltpu.VMEM((ts, d), BF16), pltpu.VMEM((ts, d), F32), pltpu.VMEM((ts, d), F32),
                        pltpu.VMEM((d, N_BRANCH * d), BF16), pltpu.VMEM((d, d), BF16),
                        pltpu.VMEM((2, d, d), F32), pltpu.SemaphoreType.DMA((2,))],
        compiler_params=pltpu.CompilerParams(
            dimension_semantics=("arbitrary", "arbitrary"), vmem_limit_bytes=VMEM_LIMIT),
        name="mixer",
    )(x, x, x, g_mix, w_in, w_s, bias_full, g_sgu, w_conv, w_out, g_ffn, wr_t)


def _route_kernel(lt_ref, info_ref, gt_ref, cnt_ref, carry_ref):
    tb = lt_ref.shape[1]
    ne = N_GROUPS * EXPERTS_PER_GROUP

    @pl.when(pl.program_id(0) == 0)
    def _():
        carry_ref[...] = jnp.zeros_like(carry_ref)

    row8 = lax.broadcasted_iota(I32, (SUBLANES, tb), 0)
    gl = lt_ref[0:N_GROUPS, :]
    gmax = jnp.max(gl, axis=0, keepdims=True)
    gidx = jnp.min(jnp.where(gl == gmax, row8, N_GROUPS), axis=0, keepdims=True)
    pg = 1.0 / jnp.sum(jnp.exp(gl - gmax), axis=0, keepdims=True)

    sel = jnp.zeros((EXPERTS_PER_GROUP, tb), F32)
    for g in range(N_GROUPS):
        lo = N_GROUPS + g * EXPERTS_PER_GROUP
        sel = jnp.where(gidx == g, lt_ref[lo:lo + EXPERTS_PER_GROUP, :], sel)
    m1 = jnp.max(sel, axis=0, keepdims=True)
    i1 = jnp.min(jnp.where(sel == m1, row8, EXPERTS_PER_GROUP), axis=0, keepdims=True)
    sel2 = jnp.where(row8 == i1, -jnp.inf, sel)
    m2 = jnp.max(sel2, axis=0, keepdims=True)
    i2 = jnp.min(jnp.where(sel2 == m2, row8, EXPERTS_PER_GROUP), axis=0, keepdims=True)
    e2 = jnp.exp(m2 - m1)
    den = 1.0 + e2
    gate0 = pg * (1.0 / den)
    gate1 = pg * (e2 / den)
    eid0 = gidx * EXPERTS_PER_GROUP + i1
    eid1 = gidx * EXPERTS_PER_GROUP + i2

    rowe = lax.broadcasted_iota(I32, (ne, tb), 0)
    hit0 = rowe == eid0
    hit1 = rowe == eid1
    onehot = jnp.where(hit0 | hit1, 1.0, 0.0)
    sub = ROUTE_CHUNK
    before = (lax.broadcasted_iota(I32, (sub, sub), 0) < lax.broadcasted_iota(I32, (sub, sub), 1))
    before = jnp.where(before, 1.0, 0.0).astype(BF16)
    carry = carry_ref[:, 0:1]
    parts = []
    for c in range(tb // sub):
        part = onehot[:, c * sub:(c + 1) * sub]
        parts.append(jnp.dot(part.astype(BF16), before, preferred_element_type=F32) + carry)
        carry = carry + jnp.sum(part, axis=1, keepdims=True)
    base = jnp.concatenate(parts, axis=1)
    rank0 = jnp.sum(jnp.where(hit0, base, 0.0), axis=0, keepdims=True).astype(I32)
    rank1 = jnp.sum(jnp.where(hit1, base, 0.0), axis=0, keepdims=True).astype(I32)
    carry_ref[...] = jnp.broadcast_to(carry, carry_ref.shape)
    cnt_ref[...] = carry_ref[...].astype(I32)

    info_ref[...] = jnp.where(row8 == 0, eid0, jnp.where(row8 == 1, eid1,
                              jnp.where(row8 == 2, rank0, jnp.where(row8 == 3, rank1, 0))))
    rowl = lax.broadcasted_iota(I32, (LANES, tb), 0)
    gates = jnp.where(rowl == 0, gate0, jnp.where(rowl == 1, gate1, 0.0))
    gt_ref[...] = gates.T


def _route(logits_t):
    t = logits_t.shape[1]
    tb = ROUTE_TOKENS
    ne = N_GROUPS * EXPERTS_PER_GROUP
    return pl.pallas_call(
        _route_kernel,
        grid=(t // tb,),
        in_specs=[pl.BlockSpec((LANES, tb), lambda i: (0, i))],
        out_specs=[
            pl.BlockSpec((SUBLANES, tb), lambda i: (0, i)),
            pl.BlockSpec((tb, LANES), lambda i: (i, 0)),
            pl.BlockSpec((ne, LANES), lambda i: (0, 0)),
        ],
        out_shape=[
            jax.ShapeDtypeStruct((SUBLANES, t), I32),
            jax.ShapeDtypeStruct((t, LANES), F32),
            jax.ShapeDtypeStruct((ne, LANES), I32),
        ],
        scratch_shapes=[pltpu.VMEM((ne, LANES), F32)],
        compiler_params=pltpu.CompilerParams(dimension_semantics=("arbitrary",)),
        name="route",
    )(logits_t)


def _plan_kernel(cnt_ref, exp_ref, blk_ref):
    ne = cnt_ref.shape[0]
    nbp = blk_ref.shape[1]
    shift = EXPERT_ROWS.bit_length() - 1
    pad_rows = lambda c: lax.shift_left(lax.shift_right_logical(c + (EXPERT_ROWS - 1), shift), shift)
    cnt = cnt_ref[...]
    padded = pad_rows(cnt)
    padded_lanes = pad_rows(cnt.astype(F32).T[:ne, :ne].astype(I32))
    e_sub = lax.broadcasted_iota(I32, (ne, ne), 0)
    e_lane = lax.broadcasted_iota(I32, (ne, ne), 1)
    pend = jnp.sum(jnp.where(e_lane <= e_sub, padded_lanes, 0), axis=1, keepdims=True)
    pstart = pend - padded[:, 0:1]

    lane = lax.broadcasted_iota(I32, (ne, LANES), 1)
    exp_ref[...] = jnp.where(lane == 0, pstart, jnp.where(lane == 1, cnt, jnp.where(lane == 2, padded, 0)))

    first_row = lax.broadcasted_iota(I32, (ne, nbp), 1) * EXPERT_ROWS
    e_col = lax.broadcasted_iota(I32, (ne, nbp), 0)
    blk_e = jnp.minimum(jnp.sum(jnp.where(pend <= first_row, 1, 0), axis=0, keepdims=True), ne - 1)
    has_rows = padded[:, 0:1] > 0
    nxt_e = jnp.min(jnp.where((e_col > blk_e) & has_rows, e_col, ne), axis=0, keepdims=True)
    nxt2_e = jnp.min(jnp.where((e_col > nxt_e) & has_rows, e_col, ne), axis=0, keepdims=True)
    nxt_e = jnp.where(nxt_e == ne, -1, nxt_e)
    nxt2_e = jnp.where(nxt2_e == ne, -1, nxt2_e)
    n_used = lax.shift_right_logical(jnp.max(pend, axis=0, keepdims=True), shift)
    blk_seq = jnp.sum(jnp.where((e_col < blk_e) & has_rows, 1, 0), axis=0, keepdims=True)
    row8 = lax.broadcasted_iota(I32, (SUBLANES, nbp), 0)
    blk_ref[...] = jnp.where(row8 == 0, blk_e, jnp.where(row8 == 1, nxt_e, jnp.where(
        row8 == 2, n_used, jnp.where(row8 == 3, nxt2_e, jnp.where(row8 == 4, blk_seq, 0)))))


def _plan(counts, nb):
    ne = counts.shape[0]
    nbp = -(-nb // LANES) * LANES
    return pl.pallas_call(
        _plan_kernel,
        out_shape=[jax.ShapeDtypeStruct((ne, LANES), I32), jax.ShapeDtypeStruct((SUBLANES, nbp), I32)],
        name="plan",
    )(counts)


def _place_kernel(info_ref, ps_ref, dest_ref):
    tb = info_ref.shape[1]
    ne = ps_ref.shape[0]
    rowe = lax.broadcasted_iota(I32, (ne, tb), 0)
    row8 = lax.broadcasted_iota(I32, (SUBLANES, tb), 0)
    ps = ps_ref[:, 0:1]
    dest = jnp.zeros((SUBLANES, tb), I32)
    for k in range(TOP_K):
        start = jnp.sum(jnp.where(rowe == info_ref[k:k + 1, :], ps, 0), axis=0, keepdims=True)
        dest = jnp.where(row8 == k, start + info_ref[TOP_K + k:TOP_K + k + 1, :], dest)
    dest_ref[...] = dest


def _place(info, exp_tbl):
    t = info.shape[1]
    tb = PLACE_TOKENS
    ne = exp_tbl.shape[0]
    return pl.pallas_call(
        _place_kernel,
        grid=(t // tb,),
        in_specs=[pl.BlockSpec((SUBLANES, tb), lambda i: (0, i)),
                  pl.BlockSpec((ne, LANES), lambda i: (0, 0))],
        out_specs=pl.BlockSpec((SUBLANES, tb), lambda i: (0, i)),
        out_shape=jax.ShapeDtypeStruct((SUBLANES, t), I32),
        compiler_params=pltpu.CompilerParams(dimension_semantics=("arbitrary",)),
        name="place",
    )(info, exp_tbl)


def _row_copy(src_ref, src_row, dst_ref, dst_row, nt, sem):
    first = lambda row: row * nt if isinstance(row, int) else pl.multiple_of(row * nt, nt)
    return pltpu.make_async_copy(
        src_ref.at[pl.ds(first(src_row), nt), :], dst_ref.at[pl.ds(first(dst_row), nt), :], sem)


def _rows_wait(ref, n_rows, nt, sem):
    pltpu.make_async_copy(ref.at[pl.ds(0, n_rows * nt), :], ref.at[pl.ds(0, n_rows * nt), :], sem).wait()


def _for_each_assignment(dest_ref, n_tok, start_copy, inline=False):
    def group(g, c):
        t0 = g * MOVE_UNROLL
        rows = [[dest_ref[k * n_tok + t0 + u] for k in range(TOP_K)] for u in range(MOVE_UNROLL)]
        for u in range(MOVE_UNROLL):
            for k in range(TOP_K):
                start_copy(k, t0 + u, rows[u][k])
        return c

    if inline:
        for g in range(n_tok // MOVE_UNROLL):
            group(g, 0)
    else:
        lax.fori_loop(0, n_tok // MOVE_UNROLL, group, 0)


def _dispatch_kernel(nt, nb, plan_ref, dest_ref, h2_ref, xs_ref, zbuf, sem, zsem):
    td = h2_ref.shape[0] // nt
    ne = (plan_ref.shape[0] - 1) // 3
    n_used = plan_ref[3 * ne]
    blk_rows = zbuf.shape[0]

    def zero_rows(start_not_wait):
        def fire(c):
            c.start() if start_not_wait else c.wait()

        def expert(e, carry):
            cnt = plan_ref[3 * e + 1]
            row = plan_ref[3 * e] + cnt
            n = plan_ref[3 * e + 2] - cnt
            p = EXPERT_ROWS // 2
            while p >= 1:
                has = (n & p) != 0
                r, sz = row, p

                @pl.when(has)
                def _():
                    fire(pltpu.make_async_copy(
                        zbuf.at[pl.ds(0, sz * nt), :],
                        xs_ref.at[pl.ds(pl.multiple_of(r * nt, nt), sz * nt), :], zsem))

                row = row + jnp.where(has, p, 0)
                p //= 2
            return carry

        lax.fori_loop(0, ne, expert, 0)

        def tail(q, carry):
            fire(pltpu.make_async_copy(
                zbuf, xs_ref.at[pl.ds(pl.multiple_of(q * blk_rows, blk_rows), blk_rows), :], zsem))
            return carry

        lax.fori_loop(n_used, nb, tail, 0)

    @pl.when(pl.program_id(0) == 0)
    def _():
        zbuf[...] = jnp.zeros_like(zbuf)
        zero_rows(True)

    _for_each_assignment(
        dest_ref, td,
        lambda k, t, row: _row_copy(h2_ref, t, xs_ref, row, nt, sem).start(priority=k))
    _rows_wait(xs_ref, TOP_K * td, nt, sem)

    @pl.when(pl.program_id(0) == 0)
    def _():
        zero_rows(False)


def _dispatch(plan, dest, h2t, n_rows):
    td = DISPATCH_TOKENS
    t = dest.shape[0] // TOP_K
    nt = h2t.shape[0] // t
    nb = n_rows // EXPERT_ROWS
    grid_spec = pltpu.PrefetchScalarGridSpec(
        num_scalar_prefetch=1,
        grid=(t // td,),
        in_specs=[
            pl.BlockSpec((TOP_K * td,), lambda i, plan: (i,), memory_space=pltpu.SMEM),
            pl.BlockSpec((td * nt, LANES), lambda i, plan: (i, 0)),
        ],
        out_specs=pl.BlockSpec(memory_space=pl.ANY),
        scratch_shapes=[pltpu.VMEM((EXPERT_ROWS * nt, LANES), h2t.dtype),
                        pltpu.SemaphoreType.DMA, pltpu.SemaphoreType.DMA],
    )
    return pl.pallas_call(
        functools.partial(_dispatch_kernel, nt, nb),
        grid_spec=grid_spec,
        out_shape=jax.ShapeDtypeStruct((n_rows * nt, LANES), h2t.dtype),
        compiler_params=pltpu.CompilerParams(dimension_semantics=("arbitrary",)),
        name="dispatch",
    )(plan, dest, h2t)


def _experts_kernel(nb, be_ref, seq_ref, nx_ref, nx2_ref, nu_ref, xs_hbm, wg_hbm, wu_hbm, wd_hbm,
                    y_hbm, xbuf, ybuf, wg_st, wu_st, wd_st, wgu_b, wd_b, hid_s, wsems, isems, osems):
    ring = xbuf.shape[0]
    stages, d, de = wg_st.shape
    bs = EXPERT_ROWS
    ntp = xbuf.shape[1] // bs
    nu = nu_ref[0]

    def block(ref, q):
        rows = bs * ntp
        return ref.at[pl.ds(pl.multiple_of(q * rows, rows), rows), :]

    def in_copy(q):
        return pltpu.make_async_copy(block(xs_hbm, q), xbuf.at[q % ring], isems.at[q % ring])

    def out_copy(q):
        return pltpu.make_async_copy(ybuf.at[q % ring], block(y_hbm, q), osems.at[q % ring])

    def weight_copies(ex, slot):
        return (pltpu.make_async_copy(wg_hbm.at[ex], wg_st.at[slot], wsems.at[slot]),
                pltpu.make_async_copy(wu_hbm.at[ex], wu_st.at[slot], wsems.at[slot]),
                pltpu.make_async_copy(wd_hbm.at[ex], wd_st.at[slot], wsems.at[slot]))

    def start_weights(ex, slot):
        for c, prio in zip(weight_copies(ex, slot), WEIGHT_DMA_PRIORITIES):
            c.start(priority=prio)

    xbuf[...] = jnp.zeros_like(xbuf)
    hid_s[...] = jnp.zeros_like(hid_s)
    wgu_b[...] = jnp.zeros_like(wgu_b)
    wd_b[...] = jnp.zeros_like(wd_b)
    start_weights(be_ref[0], 0)

    @pl.when(nx_ref[0] >= 0)
    def _():
        start_weights(nx_ref[0], 1)

    for q in range(EXPERT_LOOKAHEAD):
        @pl.when(q < nu)
        def _():
            in_copy(q).start()

    seq_of = lambda j: seq_ref[jnp.clip(j, 0, nb - 1)]

    def load_expert_if_first(j):
        jc = jnp.clip(j, 0, nb - 1)
        e = be_ref[jc]

        @pl.when((j < nu) & ((j == 0) | (e != be_ref[jnp.maximum(jc - 1, 0)])))
        def _():
            seq = seq_ref[jc]
            slot = lax.rem(seq, stages)
            for c in weight_copies(e, slot):
                c.wait()
            nx2 = nx2_ref[jc]

            @pl.when(nx2 >= 0)
            def _():
                start_weights(nx2, lax.rem(seq + 2, stages))

            half = lax.rem(seq, 2)
            wgu_b[half, :, :de] = wg_st[slot].astype(BF16)
            wgu_b[half, :, de:] = wu_st[slot].astype(BF16)
            wd_b[lax.rem(seq, 3)] = wd_st[slot].astype(BF16)

    def up_proj(j):
        xb = _unpack_bf16_pairs(_tiles_to_rows(xbuf.at[j % ring], bs, ntp), BF16)
        gu = jnp.dot(xb, wgu_b[lax.rem(seq_of(j), 2)], preferred_element_type=F32)
        gate = gu[:, :de]
        return ((gate * _sigmoid(gate)) * gu[:, de:]).astype(BF16)

    def down_proj(hid, j):
        y = jnp.dot(hid, wd_b[lax.rem(seq_of(j), 3)], preferred_element_type=F32)
        _rows_to_tiles(ybuf.at[(j + ring) % ring], _pack_bf16_pairs(y.astype(BF16)), bs)

    def step(m, carry):
        j0 = 2 * m
        for j in (j0, j0 + 1):
            load_expert_if_first(j)
        for j in (j0, j0 + 1):
            @pl.when(j + EXPERT_LOOKAHEAD < nu)
            def _():
                in_copy(j + EXPERT_LOOKAHEAD).start()

            @pl.when(j < nu)
            def _():
                in_copy(j).wait()
        for j in (j0 - 1, j0):
            @pl.when(j >= ring)
            def _():
                out_copy(j - ring).wait()

        hid0 = up_proj(j0)
        down_proj(hid_s[...], j0 - 1)
        hid1 = up_proj(j0 + 1)
        down_proj(hid0, j0)
        hid_s[...] = hid1

        for j in (j0 - 1, j0):
            @pl.when((j >= 0) & (j < nu))
            def _():
                out_copy(j).start()

        return carry

    lax.fori_loop(0, nu // 2 + 1, step, 0)

    last_written = 2 * (nu // 2)
    for r in range(1, ring + 1):
        q = last_written - ring + r

        @pl.when((q >= 0) & (q < nu))
        def _():
            out_copy(q).wait()

    ybuf[0] = jnp.zeros(ybuf.shape[1:], ybuf.dtype)

    def zero_start(q, carry):
        pltpu.make_async_copy(ybuf.at[0], block(y_hbm, q), osems.at[0]).start()
        return carry

    def zero_wait(q, carry):
        pltpu.make_async_copy(ybuf.at[0], block(y_hbm, q), osems.at[0]).wait()
        return carry

    lax.fori_loop(nu, nb, zero_start, 0)
    lax.fori_loop(nu, nb, zero_wait, 0)


def _experts(blk_e, blk_seq, nxt_e, nxt2_e, n_used, xs, w_gate, w_up, w_down):
    ne, d, de = w_gate.shape
    in_rows = EXPERT_ROWS * (d // LANES) // 2
    nb = xs.shape[0] // in_rows
    ring = EXPERT_LOOKAHEAD + 2
    any_spec = pl.BlockSpec(memory_space=pl.ANY)
    grid_spec = pltpu.PrefetchScalarGridSpec(
        num_scalar_prefetch=5,
        grid=(1,),
        in_specs=[any_spec, any_spec, any_spec, any_spec],
        out_specs=any_spec,
        scratch_shapes=[pltpu.VMEM((ring, in_rows, LANES), U32), pltpu.VMEM((ring, in_rows, LANES), U32),
                        pltpu.VMEM((WEIGHT_STAGES, d, de), F32), pltpu.VMEM((WEIGHT_STAGES, d, de), F32),
                        pltpu.VMEM((WEIGHT_STAGES, de, d), F32),
                        pltpu.VMEM((2, d, 2 * de), BF16), pltpu.VMEM((3, de, d), BF16),
                        pltpu.VMEM((EXPERT_ROWS, de), BF16),
                        pltpu.SemaphoreType.DMA((WEIGHT_STAGES,)),
                        pltpu.SemaphoreType.DMA((ring,)), pltpu.SemaphoreType.DMA((ring,))],
    )
    return pl.pallas_call(
        functools.partial(_experts_kernel, nb),
        grid_spec=grid_spec,
        out_shape=jax.ShapeDtypeStruct((nb * in_rows, LANES), U32),
        compiler_params=pltpu.CompilerParams(
            dimension_semantics=("arbitrary",), vmem_limit_bytes=VMEM_LIMIT),
        name="experts",
    )(blk_e, blk_seq, nxt_e, nxt2_e, n_used, xs, w_gate, w_up, w_down)


def _combine_kernel(d0_ref, d1_ref, d2_ref, x1_ref, gt_ref, y_ref, gfin_ref, o_ref, ybuf, sems):
    tc, d = x1_ref.shape
    nt = ybuf.shape[2] // tc
    i = pl.program_id(0)
    n = pl.num_programs(0)
    slots = ybuf.shape[0]

    def issue_tile(dest_ref, slot, inline):
        _for_each_assignment(
            dest_ref, tc,
            lambda k, t, row: _row_copy(y_ref, row, ybuf.at[slot, k], t, nt,
                                        sems.at[slot]).start(priority=k),
            inline=inline)

    @pl.when(i == 0)
    def _():
        issue_tile(d0_ref, 0, False)
        issue_tile(d1_ref, 1, False)

    slot = i % slots
    _rows_wait(y_ref, TOP_K * tc, nt, sems.at[slot])
    issue_tile(d2_ref, (i + 2) % slots, True)
    y0 = _unpack_bf16_pairs(_tiles_to_rows(ybuf.at[slot, 0], tc, nt), F32)
    y1 = _unpack_bf16_pairs(_tiles_to_rows(ybuf.at[slot, 1], tc, nt), F32)
    gates = gt_ref[...]
    xo = x1_ref[...] + (gates[:, 0:1] * y0 + gates[:, 1:2] * y1)
    o_ref[...] = _rms(xo, gfin_ref[...])

    @pl.when(i == n - 1)
    def _():
        _rows_wait(y_ref, TOP_K * tc, nt, sems.at[(i + 1) % slots])
        _rows_wait(y_ref, TOP_K * tc, nt, sems.at[(i + 2) % slots])


def _combine(dest, x1, gates_t, y, g_final):
    t, d = x1.shape
    nt = d // LANES
    tc = COMBINE_TOKENS
    last = t // tc - 1
    assert last >= 1
    dest_spec = lambda ahead: pl.BlockSpec(
        (TOP_K * tc,), lambda i: (jnp.minimum(i + ahead, last),), memory_space=pltpu.SMEM)
    return pl.pallas_call(
        _combine_kernel,
        grid=(t // tc,),
        in_specs=[
            dest_spec(0), dest_spec(1), dest_spec(2),
            pl.BlockSpec((tc, d), lambda i: (i, 0)),
            pl.BlockSpec((tc, LANES), lambda i: (i, 0)),
            pl.BlockSpec(memory_space=pl.ANY),
            pl.BlockSpec((1, d), lambda i: (0, 0)),
        ],
        out_specs=pl.BlockSpec((tc, d), lambda i: (i, 0)),
        out_shape=jax.ShapeDtypeStruct((t, d), F32),
        scratch_shapes=[pltpu.VMEM((3, TOP_K, tc * nt // 2, LANES), U32),
                        pltpu.SemaphoreType.DMA((3,))],
        compiler_params=pltpu.CompilerParams(dimension_semantics=("arbitrary",)),
        name="combine",
    )(dest, dest, dest, x1, gates_t, y, g_final)


def _layer(x, g_mix, w_in, w_s, b_s, g_sgu, w_conv, w_out, g_ffn, w_rg, w_re, w_gate, w_up, w_down):
    b, s, d = x.shape
    t = b * s
    ne = w_gate.shape[0]
    bs = EXPERT_ROWS

    bias_full = jnp.repeat(b_s.T, d // N_GROUPS, axis=1)
    wr_t = jnp.concatenate([w_rg, w_re], axis=1).T
    wr_t = jnp.pad(wr_t, ((0, LANES - wr_t.shape[0]), (0, 0)))
    x1, h2t, logits_t = _mixer(
        x, g_mix.reshape(1, d), w_in, w_s, bias_full, g_sgu.reshape(1, d),
        w_conv, w_out, g_ffn.reshape(1, d), wr_t)

    info, gates_t, counts = _route(logits_t)

    n_rows = -(-(t * TOP_K + ne * (bs - 1)) // bs) * bs
    nb = n_rows // bs
    exp_tbl, blk_tbl = _plan(counts, nb)
    blk_e, nxt_e, n_used, nxt2_e, blk_seq = (
        blk_tbl[0, :nb], blk_tbl[1, :nb], blk_tbl[2, :1], blk_tbl[3, :nb], blk_tbl[4, :nb])

    dest = _place(info, exp_tbl)[:TOP_K]
    tiled = lambda n: dest.reshape(TOP_K, t // n, n).transpose(1, 0, 2).reshape(-1)
    plan = jnp.concatenate([exp_tbl[:, :3].reshape(-1), n_used])
    xs = _dispatch(plan, tiled(DISPATCH_TOKENS), h2t, n_rows)
    y = _experts(blk_e, blk_seq, nxt_e, nxt2_e, n_used, xs, w_gate, w_up, w_down)
    return x1.reshape(t, d), tiled(COMBINE_TOKENS), gates_t, y


def kernel(x, g_mix, w_in, w_s, b_s, g_sgu, w_conv, w_out, g_ffn, w_router_group, w_router_expert,
           w_gate, w_up, w_down, g_final):
    b, s, d = x.shape
    depth = g_mix.shape[0]
    assert depth == 1, "the final RMSNorm is fused into the last layer's combine"
    assert s % MIX_ROWS == 0 and MIX_ROWS % CHUNK == 0 and d % LANES == 0
    assert EXPERT_ROWS & (EXPERT_ROWS - 1) == 0, "block bookkeeping uses shifts"
    assert all((b * s) % n == 0 for n in (ROUTE_TOKENS, PLACE_TOKENS, DISPATCH_TOKENS, COMBINE_TOKENS))
    l = 0
    x1, dest, gates_t, y = _layer(
        x, g_mix[l], w_in[l], w_s[l], b_s[l], g_sgu[l], w_conv[l], w_out[l], g_ffn[l],
        w_router_group[l], w_router_expert[l], w_gate[l], w_up[l], w_down[l])
    out = _combine(dest, x1, gates_t, y, g_final.reshape(1, d))
    return out.reshape(b, s, d)
```

```python
import functools

import jax
import jax.numpy as jnp
from jax import lax
from jax.experimental import pallas as pl
from jax.experimental.pallas import tpu as pltpu

F32 = jnp.float32
BF16 = jnp.bfloat16
I32 = jnp.int32
U32 = jnp.uint32

EPS = 1e-6
LANES = 128
SUBLANES = 8
CHUNK = 128
N_GROUPS = 8
EXPERTS_PER_GROUP = 8
TOP_K = 2
CONV_K = 3
N_BRANCH = 7

MIX_ROWS = 512
ROUTE_TOKENS = 2048
ROUTE_CHUNK = 512
PLACE_TOKENS = 2048
DISPATCH_TOKENS = 2048
COMBINE_TOKENS = 512
MOVE_UNROLL = 8
WEIGHT_STAGES = 3
WEIGHT_DMA_PRIORITIES = (1, 1, 0)
EXPERT_ROWS = 256
EXPERT_LOOKAHEAD = 3
VMEM_LIMIT = 56 * 1024 * 1024


def _rms(x, g):
    return x * lax.rsqrt(jnp.mean(x * x, axis=-1, keepdims=True) + EPS) * g


def _sigmoid(x):
    return 0.5 * (1.0 + jnp.tanh(0.5 * x))


def _gelu_tanh(x):
    c = 0.7978845608028654
    return x * (0.5 * (1.0 + jnp.tanh(c * (x + 0.044715 * (x * x * x)))))


def _rows_to_tiles(dst_ref, val, rows):
    nt = val.shape[1] // LANES
    for c in range(nt):
        dst_ref[pl.ds(c, rows, stride=nt), :] = val[:, c * LANES:(c + 1) * LANES]


def _tiles_to_rows(src_ref, rows, nt):
    return jnp.concatenate([src_ref[pl.ds(c, rows, stride=nt), :] for c in range(nt)], axis=1)


def _pack_bf16_pairs(x):
    half = x.shape[1] // 2
    bits = pltpu.bitcast(x.astype(F32), U32)
    return lax.shift_right_logical(bits[:, :half], U32(16)) | bits[:, half:]


def _unpack_bf16_pairs(w, dtype):
    lo = pltpu.bitcast(lax.shift_left(w, U32(16)), F32)
    hi = pltpu.bitcast(w & U32(0xFFFF0000), F32)
    return jnp.concatenate([lo, hi], axis=1).astype(dtype)


def _mixer_kernel(x_ref, xp_ref, xn_ref, gmix_ref, win_hbm, ws_ref, bias_ref, gsgu_ref, wconv_ref,
                  wout_hbm, gffn_ref, wr_ref, x1_ref, h2t_ref, lt_ref,
                  vg_ref, z_ref, acc_ref, win_ref, wout_ref, stage, wsems):
    ts, d = x_ref.shape[1], x_ref.shape[2]
    s = pl.program_id(1)
    ns = pl.num_programs(1)
    gw = d // N_GROUPS
    nc = ts // CHUNK

    @pl.when((pl.program_id(0) == 0) & (s == 0))
    def _():
        def slab_copy(j):
            src = win_hbm.at[:, pl.ds(j * d, d)] if j < N_BRANCH else wout_hbm
            return pltpu.make_async_copy(src, stage.at[j % 2], wsems.at[j % 2])

        slab_copy(0).start()
        for j in range(N_BRANCH + 1):
            if j < N_BRANCH:
                slab_copy(j + 1).start()
            slab_copy(j).wait()
            if j < N_BRANCH:
                win_ref[:, j * d:(j + 1) * d] = stage[j % 2].astype(BF16)
            else:
                wout_ref[...] = stage[j % 2].astype(BF16)

    x = x_ref[0]
    gmix = gmix_ref[...]
    h = _rms(x, gmix).astype(BF16)

    def proj(j, n=1):
        return jnp.dot(h, win_ref[:, j * d:(j + n) * d], preferred_element_type=F32)

    v_raw = proj(1)
    xh = jnp.concatenate([xp_ref[0], xn_ref[0]], axis=0)
    h_ext = jnp.concatenate([h, _rms(xh, gmix).astype(BF16)], axis=0)
    cx = jnp.dot(h_ext, win_ref[:, 3 * d:5 * d], preferred_element_type=F32)

    gv = _gelu_tanh(v_raw)
    for g in range(N_GROUPS):
        cs = slice(g * gw, (g + 1) * gw)
        blk = gv[:, cs]
        mu = jnp.mean(blk, axis=-1, keepdims=True)
        dv = blk - mu
        var = jnp.mean(dv * dv, axis=-1, keepdims=True)
        vg_ref[:, cs] = (dv * lax.rsqrt(var + EPS) * gsgu_ref[:, cs]).astype(BF16)
    for g in range(N_GROUPS):
        cs = slice(g * gw, (g + 1) * gw)
        vcat = jnp.concatenate([vg_ref[n * CHUNK:(n + 1) * CHUNK, cs] for n in range(nc)], axis=1)
        zg = jnp.dot(ws_ref[g].astype(BF16), vcat, preferred_element_type=F32)
        for n in range(nc):
            z_ref[n * CHUNK:(n + 1) * CHUNK, cs] = zg[:, n * gw:(n + 1) * gw]
    u = _gelu_tanh(proj(0))
    ga = _sigmoid(proj(5))
    for n in range(nc):
        rs = slice(n * CHUNK, (n + 1) * CHUNK)
        acc_ref[rs, :] = ga[rs] * (u[rs] * (z_ref[rs, :] + bias_ref[...]))

    z2 = cx[:ts, :d] * cx[:ts, d:]
    z2h = cx[ts:, :d] * cx[ts:, d:]
    prev = jnp.where(s > 0, z2h[SUBLANES - 1:SUBLANES, :], 0.0)
    nxt = jnp.where(s < ns - 1, z2h[SUBLANES:SUBLANES + 1, :], 0.0)
    row = lax.broadcasted_iota(I32, (ts, d), 0)
    zm1 = jnp.where(row == 0, prev, pltpu.roll(z2, 1, 0))
    zp1 = jnp.where(row == ts - 1, nxt, pltpu.roll(z2, ts - 1, 0))
    conv = wconv_ref[0:1, :] * zm1 + wconv_ref[1:2, :] * z2 + wconv_ref[2:3, :] * zp1
    cb = proj(2)
    gb = _sigmoid(proj(6))
    merged = acc_ref[...] + gb * (cb * conv)

    x1 = x + jnp.dot(merged.astype(BF16), wout_ref[...], preferred_element_type=F32)
    x1_ref[0] = x1

    h2 = _rms(x1, gffn_ref[...]).astype(BF16)
    lt_ref[...] = lax.dot_general(wr_ref[...].astype(BF16), h2, (((1,), (1,)), ((), ())),
                                  preferred_element_type=F32)
    _rows_to_tiles(h2t_ref, _pack_bf16_pairs(h2), ts)


def _mixer(x, g_mix, w_in, w_s, bias_full, g_sgu, w_conv, w_out, g_ffn, wr_t):
    b, s, d = x.shape
    ts = MIX_ROWS
    ns = s // ts
    t = b * s
    ntp = d // (2 * LANES)
    hb = ts // SUBLANES
    last_hb = s // SUBLANES - 1

    const = lambda *shape: pl.BlockSpec(shape, lambda bi, si: (0,) * len(shape))
    in_specs = [
        pl.BlockSpec((1, ts, d), lambda bi, si: (bi, si, 0)),
        pl.BlockSpec((1, SUBLANES, d), lambda bi, si: (bi, jnp.maximum(si * hb - 1, 0), 0)),
        pl.BlockSpec((1, SUBLANES, d), lambda bi, si: (bi, jnp.minimum((si + 1) * hb, last_hb), 0)),
        const(1, d),
        pl.BlockSpec(memory_space=pl.ANY),
        const(N_GROUPS, CHUNK, CHUNK),
        const(CHUNK, d),
        const(1, d),
        const(CONV_K, d),
        pl.BlockSpec(memory_space=pl.ANY),
        const(1, d),
        const(LANES, d),
    ]
    out_specs = [
        pl.BlockSpec((1, ts, d), lambda bi, si: (bi, si, 0)),
        pl.BlockSpec((ts * ntp, LANES), lambda bi, si: (bi * ns + si, 0)),
        pl.BlockSpec((LANES, ts), lambda bi, si: (0, bi * ns + si)),
    ]
    out_shape = [
        jax.ShapeDtypeStruct((b, s, d), F32),
        jax.ShapeDtypeStruct((t * ntp, LANES), U32),
        jax.ShapeDtypeStruct((LANES, t), F32),
    ]
    return pl.pallas_call(
        _mixer_kernel,
        grid=(b, ns),
        in_specs=in_specs,
        out_specs=out_specs,
        out_shape=out_shape,
        scratch_shapes=[pltpu.VMEM((ts, d), BF16), pltpu.VMEM((ts, d), F32), pltpu.VMEM((ts, d), F32),
                        pltpu.VMEM((d, N_BRANCH * d), BF16), pltpu.VMEM((d, d), BF16),
                        pltpu.VMEM((2, d, d), F32), pltpu.SemaphoreType.DMA((2,))],
        compiler_params=pltpu.CompilerParams(
            dimension_semantics=("arbitrary", "arbitrary"), vmem_limit_bytes=VMEM_LIMIT),
        name="mixer",
    )(x, x, x, g_mix, w_in, w_s, bias_full, g_sgu, w_conv, w_out, g_ffn, wr_t)


def _route_kernel(lt_ref, info_ref, gt_ref, cnt_ref, carry_ref):
    tb = lt_ref.shape[1]
    ne = N_GROUPS * EXPERTS_PER_GROUP

    @pl.when(pl.program_id(0) == 0)
    def _():
        carry_ref[...] = jnp.zeros_like(carry_ref)

    row8 = lax.broadcasted_iota(I32, (SUBLANES, tb), 0)
    gl = lt_ref[0:N_GROUPS, :]
    gmax = jnp.max(gl, axis=0, keepdims=True)
    gidx = jnp.min(jnp.where(gl == gmax, row8, N_GROUPS), axis=0, keepdims=True)
    pg = 1.0 / jnp.sum(jnp.exp(gl - gmax), axis=0, keepdims=True)

    sel = jnp.zeros((EXPERTS_PER_GROUP, tb), F32)
    for g in range(N_GROUPS):
        lo = N_GROUPS + g * EXPERTS_PER_GROUP
        sel = jnp.where(gidx == g, lt_ref[lo:lo + EXPERTS_PER_GROUP, :], sel)
    m1 = jnp.max(sel, axis=0, keepdims=True)
    i1 = jnp.min(jnp.where(sel == m1, row8, EXPERTS_PER_GROUP), axis=0, keepdims=True)
    sel2 = jnp.where(row8 == i1, -jnp.inf, sel)
    m2 = jnp.max(sel2, axis=0, keepdims=True)
    i2 = jnp.min(jnp.where(sel2 == m2, row8, EXPERTS_PER_GROUP), axis=0, keepdims=True)
    e2 = jnp.exp(m2 - m1)
    den = 1.0 + e2
    gate0 = pg * (1.0 / den)
    gate1 = pg * (e2 / den)
    eid0 = gidx * EXPERTS_PER_GROUP + i1
    eid1 = gidx * EXPERTS_PER_GROUP + i2

    rowe = lax.broadcasted_iota(I32, (ne, tb), 0)
    hit0 = rowe == eid0
    hit1 = rowe == eid1
    onehot = jnp.where(hit0 | hit1, 1.0, 0.0)
    sub = ROUTE_CHUNK
    before = (lax.broadcasted_iota(I32, (sub, sub), 0) < lax.broadcasted_iota(I32, (sub, sub), 1))
    before = jnp.where(before, 1.0, 0.0).astype(BF16)
    carry = carry_ref[:, 0:1]
    parts = []
    for c in range(tb // sub):
        part = onehot[:, c * sub:(c + 1) * sub]
        parts.append(jnp.dot(part.astype(BF16), before, preferred_element_type=F32) + carry)
        carry = carry + jnp.sum(part, axis=1, keepdims=True)
    base = jnp.concatenate(parts, axis=1)
    rank0 = jnp.sum(jnp.where(hit0, base, 0.0), axis=0, keepdims=True).astype(I32)
    rank1 = jnp.sum(jnp.where(hit1, base, 0.0), axis=0, keepdims=True).astype(I32)
    carry_ref[...] = jnp.broadcast_to(carry, carry_ref.shape)
    cnt_ref[...] = carry_ref[...].astype(I32)

    info_ref[...] = jnp.where(row8 == 0, eid0, jnp.where(row8 == 1, eid1,
                              jnp.where(row8 == 2, rank0, jnp.where(row8 == 3, rank1, 0))))
    rowl = lax.broadcasted_iota(I32, (LANES, tb), 0)
    gates = jnp.where(rowl == 0, gate0, jnp.where(rowl == 1, gate1, 0.0))
    gt_ref[...] = gates.T


def _route(logits_t):
    t = logits_t.shape[1]
    tb = ROUTE_TOKENS
    ne = N_GROUPS * EXPERTS_PER_GROUP
    return pl.pallas_call(
        _route_kernel,
        grid=(t // tb,),
        in_specs=[pl.BlockSpec((LANES, tb), lambda i: (0, i))],
        out_specs=[
            pl.BlockSpec((SUBLANES, tb), lambda i: (0, i)),
            pl.BlockSpec((tb, LANES), lambda i: (i, 0)),
            pl.BlockSpec((ne, LANES), lambda i: (0, 0)),
        ],
        out_shape=[
            jax.ShapeDtypeStruct((SUBLANES, t), I32),
            jax.ShapeDtypeStruct((t, LANES), F32),
            jax.ShapeDtypeStruct((ne, LANES), I32),
        ],
        scratch_shapes=[pltpu.VMEM((ne, LANES), F32)],
        compiler_params=pltpu.CompilerParams(dimension_semantics=("arbitrary",)),
        name="route",
    )(logits_t)


BLK_EXPERT, BLK_NEXT, BLK_USED, BLK_NEXT2, BLK_SEQ = range(5)
EXP_START, EXP_COUNT, EXP_PADDED = range(3)


class _TableRow:
    def __init__(self, ref, offset):
        self.ref, self.offset = ref, offset

    def __getitem__(self, j):
        return self.ref[self.offset + j]


def _plan_kernel(cnt_ref, exp_ref, blk_ref):
    ne = cnt_ref.shape[0]
    nbp = blk_ref.shape[1]
    shift = EXPERT_ROWS.bit_length() - 1
    pad_rows = lambda c: lax.shift_left(lax.shift_right_logical(c + (EXPERT_ROWS - 1), shift), shift)
    cnt = cnt_ref[...]
    padded = pad_rows(cnt)
    padded_lanes = pad_rows(cnt.astype(F32).T[:ne, :ne].astype(I32))
    e_sub = lax.broadcasted_iota(I32, (ne, ne), 0)
    e_lane = lax.broadcasted_iota(I32, (ne, ne), 1)
    pend = jnp.sum(jnp.where(e_lane <= e_sub, padded_lanes, 0), axis=1, keepdims=True)
    pstart = pend - padded[:, 0:1]

    lane = lax.broadcasted_iota(I32, (ne, LANES), 1)
    exp_ref[...] = jnp.where(lane == EXP_START, pstart, jnp.where(
        lane == EXP_COUNT, cnt, jnp.where(lane == EXP_PADDED, padded, 0)))

    first_row = lax.broadcasted_iota(I32, (ne, nbp), 1) * EXPERT_ROWS
    e_col = lax.broadcasted_iota(I32, (ne, nbp), 0)
    blk_e = jnp.minimum(jnp.sum(jnp.where(pend <= first_row, 1, 0), axis=0, keepdims=True), ne - 1)
    has_rows = padded[:, 0:1] > 0
    nxt_e = jnp.min(jnp.where((e_col > blk_e) & has_rows, e_col, ne), axis=0, keepdims=True)
    nxt2_e = jnp.min(jnp.where((e_col > nxt_e) & has_rows, e_col, ne), axis=0, keepdims=True)
    nxt_e = jnp.where(nxt_e == ne, -1, nxt_e)
    nxt2_e = jnp.where(nxt2_e == ne, -1, nxt2_e)
    n_used = lax.shift_right_logical(jnp.max(pend, axis=0, keepdims=True), shift)
    blk_seq = jnp.sum(jnp.where((e_col < blk_e) & has_rows, 1, 0), axis=0, keepdims=True)
    row8 = lax.broadcasted_iota(I32, (SUBLANES, nbp), 0)
    blk_ref[...] = jnp.where(row8 == BLK_EXPERT, blk_e, jnp.where(row8 == BLK_NEXT, nxt_e, jnp.where(
        row8 == BLK_USED, n_used, jnp.where(row8 == BLK_NEXT2, nxt2_e, jnp.where(
            row8 == BLK_SEQ, blk_seq, 0)))))


def _plan(counts, nb):
    ne = counts.shape[0]
    nbp = -(-nb // LANES) * LANES
    return pl.pallas_call(
        _plan_kernel,
        out_shape=[jax.ShapeDtypeStruct((ne, LANES), I32), jax.ShapeDtypeStruct((SUBLANES, nbp), I32)],
        name="plan",
    )(counts)


def _place_kernel(info_ref, exp_ref, *dest_refs):
    tb = info_ref.shape[1]
    ne = exp_ref.shape[0]
    rowe = lax.broadcasted_iota(I32, (ne, tb), 0)
    ps = exp_ref[:, EXP_START:EXP_START + 1]
    dest = []
    for k in range(TOP_K):
        start = jnp.sum(jnp.where(rowe == info_ref[k:k + 1, :], ps, 0), axis=0, keepdims=True)
        dest.append(start + info_ref[TOP_K + k:TOP_K + k + 1, :])
    for ref in dest_refs:
        n = ref.shape[2] // TOP_K
        for q in range(tb // n):
            ref[q] = jnp.concatenate([dk[:, q * n:(q + 1) * n] for dk in dest], axis=1)


def _place(info, exp_tbl, tile_tokens):
    t = info.shape[1]
    tb = PLACE_TOKENS
    ne = exp_tbl.shape[0]
    assert all(tb % n == 0 for n in tile_tokens)
    outs = pl.pallas_call(
        _place_kernel,
        grid=(t // tb,),
        in_specs=[pl.BlockSpec((SUBLANES, tb), lambda i: (0, i)),
                  pl.BlockSpec((ne, LANES), lambda i: (0, 0))],
        out_specs=[pl.BlockSpec((tb // n, 1, TOP_K * n), lambda i: (i, 0, 0)) for n in tile_tokens],
        out_shape=[jax.ShapeDtypeStruct((t // n, 1, TOP_K * n), I32) for n in tile_tokens],
        compiler_params=pltpu.CompilerParams(dimension_semantics=("arbitrary",)),
        name="place",
    )(info, exp_tbl)
    return [o.reshape(-1) for o in outs]


def _row_copy(src_ref, src_row, dst_ref, dst_row, nt, sem):
    first = lambda row: row * nt if isinstance(row, int) else pl.multiple_of(row * nt, nt)
    return pltpu.make_async_copy(
        src_ref.at[pl.ds(first(src_row), nt), :], dst_ref.at[pl.ds(first(dst_row), nt), :], sem)


def _rows_wait(ref, n_rows, nt, sem):
    pltpu.make_async_copy(ref.at[pl.ds(0, n_rows * nt), :], ref.at[pl.ds(0, n_rows * nt), :], sem).wait()


def _for_each_assignment(dest_ref, n_tok, start_copy, inline=False):
    def group(g, c):
        t0 = g * MOVE_UNROLL
        rows = [[dest_ref[k * n_tok + t0 + u] for k in range(TOP_K)] for u in range(MOVE_UNROLL)]
        for u in range(MOVE_UNROLL):
            for k in range(TOP_K):
                start_copy(k, t0 + u, rows[u][k])
        return c

    if inline:
        for g in range(n_tok // MOVE_UNROLL):
            group(g, 0)
    else:
        lax.fori_loop(0, n_tok // MOVE_UNROLL, group, 0)


def _dispatch_kernel(nt, nb, exp_ref, blk_ref, dest_ref, h2_ref, xs_ref, zbuf, sem, zsem):
    td = h2_ref.shape[0] // nt
    ne = exp_ref.shape[0] // LANES
    n_used = blk_ref[BLK_USED * (blk_ref.shape[0] // SUBLANES)]
    blk_rows = zbuf.shape[0]

    def zero_rows(start_not_wait):
        def fire(c):
            c.start() if start_not_wait else c.wait()

        def expert(e, carry):
            cnt = exp_ref[e * LANES + EXP_COUNT]
            row = exp_ref[e * LANES + EXP_START] + cnt
            n = exp_ref[e * LANES + EXP_PADDED] - cnt
            p = EXPERT_ROWS // 2
            while p >= 1:
                has = (n & p) != 0
                r, sz = row, p

                @pl.when(has)
                def _():
                    fire(pltpu.make_async_copy(
                        zbuf.at[pl.ds(0, sz * nt), :],
                        xs_ref.at[pl.ds(pl.multiple_of(r * nt, nt), sz * nt), :], zsem))

                row = row + jnp.where(has, p, 0)
                p //= 2
            return carry

        lax.fori_loop(0, ne, expert, 0)

        def tail(q, carry):
            fire(pltpu.make_async_copy(
                zbuf, xs_ref.at[pl.ds(pl.multiple_of(q * blk_rows, blk_rows), blk_rows), :], zsem))
            return carry

        lax.fori_loop(n_used, nb, tail, 0)

    @pl.when(pl.program_id(0) == 0)
    def _():
        zbuf[...] = jnp.zeros_like(zbuf)
        zero_rows(True)

    _for_each_assignment(
        dest_ref, td,
        lambda k, t, row: _row_copy(h2_ref, t, xs_ref, row, nt, sem).start(priority=k))
    _rows_wait(xs_ref, TOP_K * td, nt, sem)

    @pl.when(pl.program_id(0) == 0)
    def _():
        zero_rows(False)


def _dispatch(exp_flat, blk_flat, dest, h2t, n_rows):
    td = DISPATCH_TOKENS
    t = dest.shape[0] // TOP_K
    nt = h2t.shape[0] // t
    nb = n_rows // EXPERT_ROWS
    grid_spec = pltpu.PrefetchScalarGridSpec(
        num_scalar_prefetch=2,
        grid=(t // td,),
        in_specs=[
            pl.BlockSpec((TOP_K * td,), lambda i, *_: (i,), memory_space=pltpu.SMEM),
            pl.BlockSpec((td * nt, LANES), lambda i, *_: (i, 0)),
        ],
        out_specs=pl.BlockSpec(memory_space=pl.ANY),
        scratch_shapes=[pltpu.VMEM((EXPERT_ROWS * nt, LANES), h2t.dtype),
                        pltpu.SemaphoreType.DMA, pltpu.SemaphoreType.DMA],
    )
    return pl.pallas_call(
        functools.partial(_dispatch_kernel, nt, nb),
        grid_spec=grid_spec,
        out_shape=jax.ShapeDtypeStruct((n_rows * nt, LANES), h2t.dtype),
        compiler_params=pltpu.CompilerParams(dimension_semantics=("arbitrary",)),
        name="dispatch",
    )(exp_flat, blk_flat, dest, h2t)


def _experts_kernel(nb, blk_ref, xs_hbm, wg_hbm, wu_hbm, wd_hbm,
                    y_hbm, xbuf, ybuf, wg_st, wu_st, wd_st, wgu_b, wd_b, hid_s, wsems, isems, osems):
    ring = xbuf.shape[0]
    stages, d, de = wg_st.shape
    bs = EXPERT_ROWS
    ntp = xbuf.shape[1] // bs
    nbp = blk_ref.shape[0] // SUBLANES
    be_ref, nx_ref, nx2_ref, seq_ref = (
        _TableRow(blk_ref, r * nbp) for r in (BLK_EXPERT, BLK_NEXT, BLK_NEXT2, BLK_SEQ))
    nu = blk_ref[BLK_USED * nbp]

    def block(ref, q):
        rows = bs * ntp
        return ref.at[pl.ds(pl.multiple_of(q * rows, rows), rows), :]

    def in_copy(q):
        return pltpu.make_async_copy(block(xs_hbm, q), xbuf.at[q % ring], isems.at[q % ring])

    def out_copy(q):
        return pltpu.make_async_copy(ybuf.at[q % ring], block(y_hbm, q), osems.at[q % ring])

    def weight_copies(ex, slot):
        return (pltpu.make_async_copy(wg_hbm.at[ex], wg_st.at[slot], wsems.at[slot]),
                pltpu.make_async_copy(wu_hbm.at[ex], wu_st.at[slot], wsems.at[slot]),
                pltpu.make_async_copy(wd_hbm.at[ex], wd_st.at[slot], wsems.at[slot]))

    def start_weights(ex, slot):
        for c, prio in zip(weight_copies(ex, slot), WEIGHT_DMA_PRIORITIES):
            c.start(priority=prio)

    xbuf[...] = jnp.zeros_like(xbuf)
    hid_s[...] = jnp.zeros_like(hid_s)
    wgu_b[...] = jnp.zeros_like(wgu_b)
    wd_b[...] = jnp.zeros_like(wd_b)
    start_weights(be_ref[0], 0)

    @pl.when(nx_ref[0] >= 0)
    def _():
        start_weights(nx_ref[0], 1)

    for q in range(EXPERT_LOOKAHEAD):
        @pl.when(q < nu)
        def _():
            in_copy(q).start()

    seq_of = lambda j: seq_ref[jnp.clip(j, 0, nb - 1)]

    def load_expert_if_first(j):
        jc = jnp.clip(j, 0, nb - 1)
        e = be_ref[jc]

        @pl.when((j < nu) & ((j == 0) | (e != be_ref[jnp.maximum(jc - 1, 0)])))
        def _():
            seq = seq_ref[jc]
            slot = lax.rem(seq, stages)
            for c in weight_copies(e, slot):
                c.wait()
            nx2 = nx2_ref[jc]

            @pl.when(nx2 >= 0)
            def _():
                start_weights(nx2, lax.rem(seq + 2, stages))

            half = lax.rem(seq, 2)
            wgu_b[half, :, :de] = wg_st[slot].astype(BF16)
            wgu_b[half, :, de:] = wu_st[slot].astype(BF16)
            wd_b[lax.rem(seq, 3)] = wd_st[slot].astype(BF16)

    def up_proj(j):
        xb = _unpack_bf16_pairs(_tiles_to_rows(xbuf.at[j % ring], bs, ntp), BF16)
        gu = jnp.dot(xb, wgu_b[lax.rem(seq_of(j), 2)], preferred_element_type=F32)
        gate = gu[:, :de]
        return ((gate * _sigmoid(gate)) * gu[:, de:]).astype(BF16)

    def down_proj(hid, j):
        y = jnp.dot(hid, wd_b[lax.rem(seq_of(j), 3)], preferred_element_type=F32)
        _rows_to_tiles(ybuf.at[(j + ring) % ring], _pack_bf16_pairs(y.astype(BF16)), bs)

    def step(m, carry):
        j0 = 2 * m
        for j in (j0, j0 + 1):
            load_expert_if_first(j)
        for j in (j0, j0 + 1):
            @pl.when(j + EXPERT_LOOKAHEAD < nu)
            def _():
                in_copy(j + EXPERT_LOOKAHEAD).start()

            @pl.when(j < nu)
            def _():
                in_copy(j).wait()
        for j in (j0 - 1, j0):
            @pl.when(j >= ring)
            def _():
                out_copy(j - ring).wait()

        hid0 = up_proj(j0)
        down_proj(hid_s[...], j0 - 1)
        hid1 = up_proj(j0 + 1)
        down_proj(hid0, j0)
        hid_s[...] = hid1

        for j in (j0 - 1, j0):
            @pl.when((j >= 0) & (j < nu))
            def _():
                out_copy(j).start()

        return carry

    lax.fori_loop(0, nu // 2 + 1, step, 0)

    last_written = 2 * (nu // 2)
    for r in range(1, ring + 1):
        q = last_written - ring + r

        @pl.when((q >= 0) & (q < nu))
        def _():
            out_copy(q).wait()

    ybuf[0] = jnp.zeros(ybuf.shape[1:], ybuf.dtype)

    def zero_start(q, carry):
        pltpu.make_async_copy(ybuf.at[0], block(y_hbm, q), osems.at[0]).start()
        return carry

    def zero_wait(q, carry):
        pltpu.make_async_copy(ybuf.at[0], block(y_hbm, q), osems.at[0]).wait()
        return carry

    lax.fori_loop(nu, nb, zero_start, 0)
    lax.fori_loop(nu, nb, zero_wait, 0)


def _experts(blk_flat, xs, w_gate, w_up, w_down):
    ne, d, de = w_gate.shape
    in_rows = EXPERT_ROWS * (d // LANES) // 2
    nb = xs.shape[0] // in_rows
    ring = EXPERT_LOOKAHEAD + 2
    any_spec = pl.BlockSpec(memory_space=pl.ANY)
    grid_spec = pltpu.PrefetchScalarGridSpec(
        num_scalar_prefetch=1,
        grid=(1,),
        in_specs=[any_spec, any_spec, any_spec, any_spec],
        out_specs=any_spec,
        scratch_shapes=[pltpu.VMEM((ring, in_rows, LANES), U32), pltpu.VMEM((ring, in_rows, LANES), U32),
                        pltpu.VMEM((WEIGHT_STAGES, d, de), F32), pltpu.VMEM((WEIGHT_STAGES, d, de), F32),
                        pltpu.VMEM((WEIGHT_STAGES, de, d), F32),
                        pltpu.VMEM((2, d, 2 * de), BF16), pltpu.VMEM((3, de, d), BF16),
                        pltpu.VMEM((EXPERT_ROWS, de), BF16),
                        pltpu.SemaphoreType.DMA((WEIGHT_STAGES,)),
                        pltpu.SemaphoreType.DMA((ring,)), pltpu.SemaphoreType.DMA((ring,))],
    )
    return pl.pallas_call(
        functools.partial(_experts_kernel, nb),
        grid_spec=grid_spec,
        out_shape=jax.ShapeDtypeStruct((nb * in_rows, LANES), U32),
        compiler_params=pltpu.CompilerParams(
            dimension_semantics=("arbitrary",), vmem_limit_bytes=VMEM_LIMIT),
        name="experts",
    )(blk_flat, xs, w_gate, w_up, w_down)


def _combine_kernel(d0_ref, d1_ref, d2_ref, x1_ref, gt_ref, y_ref, gfin_ref, o_ref, ybuf, sems):
    tc, d = x1_ref.shape
    nt = ybuf.shape[2] // tc
    i = pl.program_id(0)
    n = pl.num_programs(0)
    slots = ybuf.shape[0]

    def issue_tile(dest_ref, slot, inline):
        _for_each_assignment(
            dest_ref, tc,
            lambda k, t, row: _row_copy(y_ref, row, ybuf.at[slot, k], t, nt,
                                        sems.at[slot]).start(priority=k),
            inline=inline)

    @pl.when(i == 0)
    def _():
        issue_tile(d0_ref, 0, False)
        issue_tile(d1_ref, 1, False)

    slot = i % slots
    _rows_wait(y_ref, TOP_K * tc, nt, sems.at[slot])
    issue_tile(d2_ref, (i + 2) % slots, True)
    y0 = _unpack_bf16_pairs(_tiles_to_rows(ybuf.at[slot, 0], tc, nt), F32)
    y1 = _unpack_bf16_pairs(_tiles_to_rows(ybuf.at[slot, 1], tc, nt), F32)
    gates = gt_ref[...]
    xo = x1_ref[...] + (gates[:, 0:1] * y0 + gates[:, 1:2] * y1)
    o_ref[...] = _rms(xo, gfin_ref[...])

    @pl.when(i == n - 1)
    def _():
        _rows_wait(y_ref, TOP_K * tc, nt, sems.at[(i + 1) % slots])
        _rows_wait(y_ref, TOP_K * tc, nt, sems.at[(i + 2) % slots])


def _combine(dest, x1, gates_t, y, g_final):
    t, d = x1.shape
    nt = d // LANES
    tc = COMBINE_TOKENS
    last = t // tc - 1
    assert last >= 1
    dest_spec = lambda ahead: pl.BlockSpec(
        (TOP_K * tc,), lambda i: (jnp.minimum(i + ahead, last),), memory_space=pltpu.SMEM)
    return pl.pallas_call(
        _combine_kernel,
        grid=(t // tc,),
        in_specs=[
            dest_spec(0), dest_spec(1), dest_spec(2),
            pl.BlockSpec((tc, d), lambda i: (i, 0)),
            pl.BlockSpec((tc, LANES), lambda i: (i, 0)),
            pl.BlockSpec(memory_space=pl.ANY),
            pl.BlockSpec((1, d), lambda i: (0, 0)),
        ],
        out_specs=pl.BlockSpec((tc, d), lambda i: (i, 0)),
        out_shape=jax.ShapeDtypeStruct((t, d), F32),
        scratch_shapes=[pltpu.VMEM((3, TOP_K, tc * nt // 2, LANES), U32),
                        pltpu.SemaphoreType.DMA((3,))],
        compiler_params=pltpu.CompilerParams(dimension_semantics=("arbitrary",)),
        name="combine",
    )(dest, dest, dest, x1, gates_t, y, g_final)


def _layer(x, g_mix, w_in, w_s, b_s, g_sgu, w_conv, w_out, g_ffn, w_rg, w_re, w_gate, w_up, w_down):
    b, s, d = x.shape
    t = b * s
    ne = w_gate.shape[0]
    bs = EXPERT_ROWS

    bias_full = jnp.repeat(b_s.T, d // N_GROUPS, axis=1)
    wr_t = jnp.concatenate([w_rg, w_re], axis=1).T
    wr_t = jnp.pad(wr_t, ((0, LANES - wr_t.shape[0]), (0, 0)))
    x1, h2t, logits_t = _mixer(
        x, g_mix.reshape(1, d), w_in, w_s, bias_full, g_sgu.reshape(1, d),
        w_conv, w_out, g_ffn.reshape(1, d), wr_t)

    info, gates_t, counts = _route(logits_t)

    n_rows = -(-(t * TOP_K + ne * (bs - 1)) // bs) * bs
    nb = n_rows // bs
    exp_tbl, blk_tbl = _plan(counts, nb)
    blk_flat = blk_tbl.reshape(-1)

    dest_dispatch, dest_combine = _place(info, exp_tbl, (DISPATCH_TOKENS, COMBINE_TOKENS))
    xs = _dispatch(exp_tbl.reshape(-1), blk_flat, dest_dispatch, h2t, n_rows)
    y = _experts(blk_flat, xs, w_gate, w_up, w_down)
    return x1.reshape(t, d), dest_combine, gates_t, y


def kernel(x, g_mix, w_in, w_s, b_s, g_sgu, w_conv, w_out, g_ffn, w_router_group, w_router_expert,
           w_gate, w_up, w_down, g_final):
    b, s, d = x.shape
    depth = g_mix.shape[0]
    assert depth == 1, "the final RMSNorm is fused into the last layer's combine"
    assert s % MIX_ROWS == 0 and MIX_ROWS % CHUNK == 0 and d % LANES == 0
    assert EXPERT_ROWS & (EXPERT_ROWS - 1) == 0, "block bookkeeping uses shifts"
    assert all((b * s) % n == 0 for n in (ROUTE_TOKENS, PLACE_TOKENS, DISPATCH_TOKENS, COMBINE_TOKENS))
    l = 0
    x1, dest, gates_t, y = _layer(
        x, g_mix[l], w_in[l], w_s[l], b_s[l], g_sgu[l], w_conv[l], w_out[l], g_ffn[l],
        w_router_group[l], w_router_expert[l], w_gate[l], w_up[l], w_down[l])
    out = _combine(dest, x1, gates_t, y, g_final.reshape(1, d))
    return out.reshape(b, s, d)
```

```python
import functools

import jax
import jax.numpy as jnp
from jax import lax
from jax.experimental import pallas as pl
from jax.experimental.pallas import tpu as pltpu

F32 = jnp.float32
BF16 = jnp.bfloat16
I32 = jnp.int32
U32 = jnp.uint32

EPS = 1e-6
LANES = 128
SUBLANES = 8
CHUNK = 128
N_GROUPS = 8
EXPERTS_PER_GROUP = 8
TOP_K = 2
CONV_K = 3
N_BRANCH = 7

MIX_ROWS = 512
ROUTE_TOKENS = 2048
ROUTE_CHUNK = 512
PLACE_TOKENS = 2048
DISPATCH_TOKENS = 2048
COMBINE_TOKENS = 1024
MOVE_UNROLL = 8
WEIGHT_STAGES = 3
WEIGHT_DMA_PRIORITIES = (1, 1, 0)
EXPERT_ROWS = 256
EXPERT_LOOKAHEAD = 3
VMEM_LIMIT = 56 * 1024 * 1024


def _rms(x, g):
    return x * lax.rsqrt(jnp.mean(x * x, axis=-1, keepdims=True) + EPS) * g


def _sigmoid(x):
    return 0.5 * (1.0 + jnp.tanh(0.5 * x))


def _gelu_tanh(x):
    c = 0.7978845608028654
    return x * (0.5 * (1.0 + jnp.tanh(c * (x + 0.044715 * (x * x * x)))))


def _rows_to_tiles(dst_ref, val, rows):
    nt = val.shape[1] // LANES
    for c in range(nt):
        dst_ref[pl.ds(c, rows, stride=nt), :] = val[:, c * LANES:(c + 1) * LANES]


def _tiles_to_rows(src_ref, rows, nt):
    return jnp.concatenate([src_ref[pl.ds(c, rows, stride=nt), :] for c in range(nt)], axis=1)


def _pack_bf16_pairs(x):
    half = x.shape[1] // 2
    bits = pltpu.bitcast(x.astype(F32), U32)
    return lax.shift_right_logical(bits[:, :half], U32(16)) | bits[:, half:]


def _unpack_bf16_pairs(w, dtype):
    lo = pltpu.bitcast(lax.shift_left(w, U32(16)), F32)
    hi = pltpu.bitcast(w & U32(0xFFFF0000), F32)
    return jnp.concatenate([lo, hi], axis=1).astype(dtype)


def _mixer_kernel(x_ref, xp_ref, xn_ref, gmix_ref, win_hbm, ws_ref, bias_ref, gsgu_ref, wconv_ref,
                  wout_hbm, gffn_ref, wr_ref, x1_ref, h2t_ref, lt_ref,
                  vg_ref, z_ref, acc_ref, win_ref, wout_ref, stage, wsems):
    ts, d = x_ref.shape[1], x_ref.shape[2]
    s = pl.program_id(1)
    ns = pl.num_programs(1)
    gw = d // N_GROUPS
    nc = ts // CHUNK

    @pl.when((pl.program_id(0) == 0) & (s == 0))
    def _():
        def slab_copy(j):
            src = win_hbm.at[:, pl.ds(j * d, d)] if j < N_BRANCH else wout_hbm
            return pltpu.make_async_copy(src, stage.at[j % 2], wsems.at[j % 2])

        slab_copy(0).start()
        for j in range(N_BRANCH + 1):
            if j < N_BRANCH:
                slab_copy(j + 1).start()
            slab_copy(j).wait()
            if j < N_BRANCH:
                win_ref[:, j * d:(j + 1) * d] = stage[j % 2].astype(BF16)
            else:
                wout_ref[...] = stage[j % 2].astype(BF16)

    x = x_ref[0]
    gmix = gmix_ref[...]
    h = _rms(x, gmix).astype(BF16)

    def proj(j, n=1):
        return jnp.dot(h, win_ref[:, j * d:(j + n) * d], preferred_element_type=F32)

    v_raw = proj(1)
    xh = jnp.concatenate([xp_ref[0], xn_ref[0]], axis=0)
    h_ext = jnp.concatenate([h, _rms(xh, gmix).astype(BF16)], axis=0)
    cx = jnp.dot(h_ext, win_ref[:, 3 * d:5 * d], preferred_element_type=F32)

    gv = _gelu_tanh(v_raw)
    for g in range(N_GROUPS):
        cs = slice(g * gw, (g + 1) * gw)
        blk = gv[:, cs]
        mu = jnp.mean(blk, axis=-1, keepdims=True)
        dv = blk - mu
        var = jnp.mean(dv * dv, axis=-1, keepdims=True)
        vg_ref[:, cs] = (dv * lax.rsqrt(var + EPS) * gsgu_ref[:, cs]).astype(BF16)
    for g in range(N_GROUPS):
        cs = slice(g * gw, (g + 1) * gw)
        vcat = jnp.concatenate([vg_ref[n * CHUNK:(n + 1) * CHUNK, cs] for n in range(nc)], axis=1)
        zg = jnp.dot(ws_ref[g].astype(BF16), vcat, preferred_element_type=F32)
        for n in range(nc):
            z_ref[n * CHUNK:(n + 1) * CHUNK, cs] = zg[:, n * gw:(n + 1) * gw]
    u = _gelu_tanh(proj(0))
    ga = _sigmoid(proj(5))
    for n in range(nc):
        rs = slice(n * CHUNK, (n + 1) * CHUNK)
        acc_ref[rs, :] = ga[rs] * (u[rs] * (z_ref[rs, :] + bias_ref[...]))

    z2 = cx[:ts, :d] * cx[:ts, d:]
    z2h = cx[ts:, :d] * cx[ts:, d:]
    prev = jnp.where(s > 0, z2h[SUBLANES - 1:SUBLANES, :], 0.0)
    nxt = jnp.where(s < ns - 1, z2h[SUBLANES:SUBLANES + 1, :], 0.0)
    row = lax.broadcasted_iota(I32, (ts, d), 0)
    zm1 = jnp.where(row == 0, prev, pltpu.roll(z2, 1, 0))
    zp1 = jnp.where(row == ts - 1, nxt, pltpu.roll(z2, ts - 1, 0))
    conv = wconv_ref[0:1, :] * zm1 + wconv_ref[1:2, :] * z2 + wconv_ref[2:3, :] * zp1
    cb = proj(2)
    gb = _sigmoid(proj(6))
    merged = acc_ref[...] + gb * (cb * conv)

    x1 = x + jnp.dot(merged.astype(BF16), wout_ref[...], preferred_element_type=F32)
    x1_ref[0] = x1

    h2 = _rms(x1, gffn_ref[...]).astype(BF16)
    lt_ref[...] = lax.dot_general(wr_ref[...].astype(BF16), h2, (((1,), (1,)), ((), ())),
                                  preferred_element_type=F32)
    _rows_to_tiles(h2t_ref, _pack_bf16_pairs(h2), ts)


def _mixer(x, g_mix, w_in, w_s, bias_full, g_sgu, w_conv, w_out, g_ffn, wr_t):
    b, s, d = x.shape
    ts = MIX_ROWS
    ns = s // ts
    t = b * s
    ntp = d // (2 * LANES)
    hb = ts // SUBLANES
    last_hb = s // SUBLANES - 1

    const = lambda *shape: pl.BlockSpec(shape, lambda bi, si: (0,) * len(shape))
    in_specs = [
        pl.BlockSpec((1, ts, d), lambda bi, si: (bi, si, 0)),
        pl.BlockSpec((1, SUBLANES, d), lambda bi, si: (bi, jnp.maximum(si * hb - 1, 0), 0)),
        pl.BlockSpec((1, SUBLANES, d), lambda bi, si: (bi, jnp.minimum((si + 1) * hb, last_hb), 0)),
        const(1, d),
        pl.BlockSpec(memory_space=pl.ANY),
        const(N_GROUPS, CHUNK, CHUNK),
        const(CHUNK, d),
        const(1, d),
        const(CONV_K, d),
        pl.BlockSpec(memory_space=pl.ANY),
        const(1, d),
        const(LANES, d),
    ]
    out_specs = [
        pl.BlockSpec((1, ts, d), lambda bi, si: (bi, si, 0)),
        pl.BlockSpec((ts * ntp, LANES), lambda bi, si: (bi * ns + si, 0)),
        pl.BlockSpec((LANES, ts), lambda bi, si: (0, bi * ns + si)),
    ]
    out_shape = [
        jax.ShapeDtypeStruct((b, s, d), F32),
        jax.ShapeDtypeStruct((t * ntp, LANES), U32),
        jax.ShapeDtypeStruct((LANES, t), F32),
    ]
    return pl.pallas_call(
        _mixer_kernel,
        grid=(b, ns),
        in_specs=in_specs,
        out_specs=out_specs,
        out_shape=out_shape,
        scratch_shapes=[pltpu.VMEM((ts, d), BF16), pltpu.VMEM((ts, d), F32), pltpu.VMEM((ts, d), F32),
                        pltpu.VMEM((d, N_BRANCH * d), BF16), pltpu.VMEM((d, d), BF16),
                        pltpu.VMEM((2, d, d), F32), pltpu.SemaphoreType.DMA((2,))],
        compiler_params=pltpu.CompilerParams(
            dimension_semantics=("arbitrary", "arbitrary"), vmem_limit_bytes=VMEM_LIMIT),
        name="mixer",
    )(x, x, x, g_mix, w_in, w_s, bias_full, g_sgu, w_conv, w_out, g_ffn, wr_t)


def _route_kernel(lt_ref, info_ref, gt_ref, cnt_ref, carry_ref):
    tb = lt_ref.shape[1]
    ne = N_GROUPS * EXPERTS_PER_GROUP

    @pl.when(pl.program_id(0) == 0)
    def _():
        carry_ref[...] = jnp.zeros_like(carry_ref)

    row8 = lax.broadcasted_iota(I32, (SUBLANES, tb), 0)
    gl = lt_ref[0:N_GROUPS, :]
    gmax = jnp.max(gl, axis=0, keepdims=True)
    gidx = jnp.min(jnp.where(gl == gmax, row8, N_GROUPS), axis=0, keepdims=True)
    pg = 1.0 / jnp.sum(jnp.exp(gl - gmax), axis=0, keepdims=True)

    sel = jnp.zeros((EXPERTS_PER_GROUP, tb), F32)
    for g in range(N_GROUPS):
        lo = N_GROUPS + g * EXPERTS_PER_GROUP
        sel = jnp.where(gidx == g, lt_ref[lo:lo + EXPERTS_PER_GROUP, :], sel)
    m1 = jnp.max(sel, axis=0, keepdims=True)
    i1 = jnp.min(jnp.where(sel == m1, row8, EXPERTS_PER_GROUP), axis=0, keepdims=True)
    sel2 = jnp.where(row8 == i1, -jnp.inf, sel)
    m2 = jnp.max(sel2, axis=0, keepdims=True)
    i2 = jnp.min(jnp.where(sel2 == m2, row8, EXPERTS_PER_GROUP), axis=0, keepdims=True)
    e2 = jnp.exp(m2 - m1)
    den = 1.0 + e2
    gate0 = pg * (1.0 / den)
    gate1 = pg * (e2 / den)
    eid0 = gidx * EXPERTS_PER_GROUP + i1
    eid1 = gidx * EXPERTS_PER_GROUP + i2

    rowe = lax.broadcasted_iota(I32, (ne, tb), 0)
    hit0 = rowe == eid0
    hit1 = rowe == eid1
    onehot = jnp.where(hit0 | hit1, 1.0, 0.0)
    sub = ROUTE_CHUNK
    before = (lax.broadcasted_iota(I32, (sub, sub), 0) < lax.broadcasted_iota(I32, (sub, sub), 1))
    before = jnp.where(before, 1.0, 0.0).astype(BF16)
    carry = carry_ref[:, 0:1]
    parts = []
    for c in range(tb // sub):
        part = onehot[:, c * sub:(c + 1) * sub]
        parts.append(jnp.dot(part.astype(BF16), before, preferred_element_type=F32) + carry)
        carry = carry + jnp.sum(part, axis=1, keepdims=True)
    base = jnp.concatenate(parts, axis=1)
    rank0 = jnp.sum(jnp.where(hit0, base, 0.0), axis=0, keepdims=True).astype(I32)
    rank1 = jnp.sum(jnp.where(hit1, base, 0.0), axis=0, keepdims=True).astype(I32)
    carry_ref[...] = jnp.broadcast_to(carry, carry_ref.shape)
    cnt_ref[...] = carry_ref[...].astype(I32)

    info_ref[...] = jnp.where(row8 == 0, eid0, jnp.where(row8 == 1, eid1,
                              jnp.where(row8 == 2, rank0, jnp.where(row8 == 3, rank1, 0))))
    rowl = lax.broadcasted_iota(I32, (LANES, tb), 0)
    gates = jnp.where(rowl == 0, gate0, jnp.where(rowl == 1, gate1, 0.0))
    gt_ref[...] = gates.T


def _route(logits_t):
    t = logits_t.shape[1]
    tb = ROUTE_TOKENS
    ne = N_GROUPS * EXPERTS_PER_GROUP
    return pl.pallas_call(
        _route_kernel,
        grid=(t // tb,),
        in_specs=[pl.BlockSpec((LANES, tb), lambda i: (0, i))],
        out_specs=[
            pl.BlockSpec((SUBLANES, tb), lambda i: (0, i)),
            pl.BlockSpec((tb, LANES), lambda i: (i, 0)),
            pl.BlockSpec((ne, LANES), lambda i: (0, 0)),
        ],
        out_shape=[
            jax.ShapeDtypeStruct((SUBLANES, t), I32),
            jax.ShapeDtypeStruct((t, LANES), F32),
            jax.ShapeDtypeStruct((ne, LANES), I32),
        ],
        scratch_shapes=[pltpu.VMEM((ne, LANES), F32)],
        compiler_params=pltpu.CompilerParams(dimension_semantics=("arbitrary",)),
        name="route",
    )(logits_t)


BLK_EXPERT, BLK_NEXT, BLK_USED, BLK_NEXT2, BLK_SEQ = range(5)
EXP_START, EXP_COUNT, EXP_PADDED = range(3)


class _TableRow:
    def __init__(self, ref, offset):
        self.ref, self.offset = ref, offset

    def __getitem__(self, j):
        return self.ref[self.offset + j]


def _plan_kernel(cnt_ref, exp_ref, blk_ref):
    ne = cnt_ref.shape[0]
    nbp = blk_ref.shape[1]
    shift = EXPERT_ROWS.bit_length() - 1
    pad_rows = lambda c: lax.shift_left(lax.shift_right_logical(c + (EXPERT_ROWS - 1), shift), shift)
    cnt = cnt_ref[...]
    padded = pad_rows(cnt)
    padded_lanes = pad_rows(cnt.astype(F32).T[:ne, :ne].astype(I32))
    e_sub = lax.broadcasted_iota(I32, (ne, ne), 0)
    e_lane = lax.broadcasted_iota(I32, (ne, ne), 1)
    pend = jnp.sum(jnp.where(e_lane <= e_sub, padded_lanes, 0), axis=1, keepdims=True)
    pstart = pend - padded[:, 0:1]

    lane = lax.broadcasted_iota(I32, (ne, LANES), 1)
    exp_ref[...] = jnp.where(lane == EXP_START, pstart, jnp.where(
        lane == EXP_COUNT, cnt, jnp.where(lane == EXP_PADDED, padded, 0)))

    first_row = lax.broadcasted_iota(I32, (ne, nbp), 1) * EXPERT_ROWS
    e_col = lax.broadcasted_iota(I32, (ne, nbp), 0)
    blk_e = jnp.minimum(jnp.sum(jnp.where(pend <= first_row, 1, 0), axis=0, keepdims=True), ne - 1)
    has_rows = padded[:, 0:1] > 0
    nxt_e = jnp.min(jnp.where((e_col > blk_e) & has_rows, e_col, ne), axis=0, keepdims=True)
    nxt2_e = jnp.min(jnp.where((e_col > nxt_e) & has_rows, e_col, ne), axis=0, keepdims=True)
    nxt_e = jnp.where(nxt_e == ne, -1, nxt_e)
    nxt2_e = jnp.where(nxt2_e == ne, -1, nxt2_e)
    n_used = lax.shift_right_logical(jnp.max(pend, axis=0, keepdims=True), shift)
    blk_seq = jnp.sum(jnp.where((e_col < blk_e) & has_rows, 1, 0), axis=0, keepdims=True)
    row8 = lax.broadcasted_iota(I32, (SUBLANES, nbp), 0)
    blk_ref[...] = jnp.where(row8 == BLK_EXPERT, blk_e, jnp.where(row8 == BLK_NEXT, nxt_e, jnp.where(
        row8 == BLK_USED, n_used, jnp.where(row8 == BLK_NEXT2, nxt2_e, jnp.where(
            row8 == BLK_SEQ, blk_seq, 0)))))


def _plan(counts, nb):
    ne = counts.shape[0]
    nbp = -(-nb // LANES) * LANES
    return pl.pallas_call(
        _plan_kernel,
        out_shape=[jax.ShapeDtypeStruct((ne, LANES), I32), jax.ShapeDtypeStruct((SUBLANES, nbp), I32)],
        name="plan",
    )(counts)


def _place_kernel(info_ref, exp_ref, *dest_refs):
    tb = info_ref.shape[1]
    ne = exp_ref.shape[0]
    rowe = lax.broadcasted_iota(I32, (ne, tb), 0)
    ps = exp_ref[:, EXP_START:EXP_START + 1]
    dest = []
    for k in range(TOP_K):
        start = jnp.sum(jnp.where(rowe == info_ref[k:k + 1, :], ps, 0), axis=0, keepdims=True)
        dest.append(start + info_ref[TOP_K + k:TOP_K + k + 1, :])
    for ref in dest_refs:
        n = ref.shape[2] // TOP_K
        for q in range(tb // n):
            ref[q] = jnp.concatenate([dk[:, q * n:(q + 1) * n] for dk in dest], axis=1)


def _place(info, exp_tbl, tile_tokens):
    t = info.shape[1]
    tb = PLACE_TOKENS
    ne = exp_tbl.shape[0]
    assert all(tb % n == 0 for n in tile_tokens)
    outs = pl.pallas_call(
        _place_kernel,
        grid=(t // tb,),
        in_specs=[pl.BlockSpec((SUBLANES, tb), lambda i: (0, i)),
                  pl.BlockSpec((ne, LANES), lambda i: (0, 0))],
        out_specs=[pl.BlockSpec((tb // n, 1, TOP_K * n), lambda i: (i, 0, 0)) for n in tile_tokens],
        out_shape=[jax.ShapeDtypeStruct((t // n, 1, TOP_K * n), I32) for n in tile_tokens],
        compiler_params=pltpu.CompilerParams(dimension_semantics=("arbitrary",)),
        name="place",
    )(info, exp_tbl)
    return [o.reshape(-1) for o in outs]


def _row_copy(src_ref, src_row, dst_ref, dst_row, nt, sem):
    first = lambda row: row * nt if isinstance(row, int) else pl.multiple_of(row * nt, nt)
    return pltpu.make_async_copy(
        src_ref.at[pl.ds(first(src_row), nt), :], dst_ref.at[pl.ds(first(dst_row), nt), :], sem)


def _rows_wait(ref, n_rows, nt, sem):
    pltpu.make_async_copy(ref.at[pl.ds(0, n_rows * nt), :], ref.at[pl.ds(0, n_rows * nt), :], sem).wait()


def _for_each_assignment(dest_ref, n_tok, start_copy, inline=False):
    def group(g, c):
        t0 = g * MOVE_UNROLL
        rows = [[dest_ref[k * n_tok + t0 + u] for k in range(TOP_K)] for u in range(MOVE_UNROLL)]
        for u in range(MOVE_UNROLL):
            for k in range(TOP_K):
                start_copy(k, t0 + u, rows[u][k])
        return c

    if inline:
        for g in range(n_tok // MOVE_UNROLL):
            group(g, 0)
    else:
        lax.fori_loop(0, n_tok // MOVE_UNROLL, group, 0)


def _dispatch_kernel(nt, nb, exp_ref, blk_ref, dest_ref, h2_ref, xs_ref, zbuf, sem, zsem):
    td = h2_ref.shape[0] // nt
    ne = exp_ref.shape[0] // LANES
    n_used = blk_ref[BLK_USED * (blk_ref.shape[0] // SUBLANES)]
    blk_rows = zbuf.shape[0]

    def zero_rows(start_not_wait):
        def fire(c):
            c.start() if start_not_wait else c.wait()

        def expert(e, carry):
            cnt = exp_ref[e * LANES + EXP_COUNT]
            row = exp_ref[e * LANES + EXP_START] + cnt
            n = exp_ref[e * LANES + EXP_PADDED] - cnt
            p = EXPERT_ROWS // 2
            while p >= 1:
                has = (n & p) != 0
                r, sz = row, p

                @pl.when(has)
                def _():
                    fire(pltpu.make_async_copy(
                        zbuf.at[pl.ds(0, sz * nt), :],
                        xs_ref.at[pl.ds(pl.multiple_of(r * nt, nt), sz * nt), :], zsem))

                row = row + jnp.where(has, p, 0)
                p //= 2
            return carry

        lax.fori_loop(0, ne, expert, 0)

        def tail(q, carry):
            fire(pltpu.make_async_copy(
                zbuf, xs_ref.at[pl.ds(pl.multiple_of(q * blk_rows, blk_rows), blk_rows), :], zsem))
            return carry

        lax.fori_loop(n_used, nb, tail, 0)

    @pl.when(pl.program_id(0) == 0)
    def _():
        zbuf[...] = jnp.zeros_like(zbuf)
        zero_rows(True)

    _for_each_assignment(
        dest_ref, td,
        lambda k, t, row: _row_copy(h2_ref, t, xs_ref, row, nt, sem).start(priority=k))
    _rows_wait(xs_ref, TOP_K * td, nt, sem)

    @pl.when(pl.program_id(0) == 0)
    def _():
        zero_rows(False)


def _dispatch(exp_flat, blk_flat, dest, h2t, n_rows):
    td = DISPATCH_TOKENS
    t = dest.shape[0] // TOP_K
    nt = h2t.shape[0] // t
    nb = n_rows // EXPERT_ROWS
    grid_spec = pltpu.PrefetchScalarGridSpec(
        num_scalar_prefetch=2,
        grid=(t // td,),
        in_specs=[
            pl.BlockSpec((TOP_K * td,), lambda i, *_: (i,), memory_space=pltpu.SMEM),
            pl.BlockSpec((td * nt, LANES), lambda i, *_: (i, 0)),
        ],
        out_specs=pl.BlockSpec(memory_space=pl.ANY),
        scratch_shapes=[pltpu.VMEM((EXPERT_ROWS * nt, LANES), h2t.dtype),
                        pltpu.SemaphoreType.DMA, pltpu.SemaphoreType.DMA],
    )
    return pl.pallas_call(
        functools.partial(_dispatch_kernel, nt, nb),
        grid_spec=grid_spec,
        out_shape=jax.ShapeDtypeStruct((n_rows * nt, LANES), h2t.dtype),
        compiler_params=pltpu.CompilerParams(dimension_semantics=("arbitrary",)),
        name="dispatch",
    )(exp_flat, blk_flat, dest, h2t)


def _experts_kernel(nb, blk_ref, xs_hbm, wg_hbm, wu_hbm, wd_hbm,
                    y_hbm, xbuf, ybuf, wg_st, wu_st, wd_st, wgu_b, wd_b, hid_s, wsems, isems, osems):
    ring = xbuf.shape[0]
    stages, d, de = wg_st.shape
    bs = EXPERT_ROWS
    ntp = xbuf.shape[1] // bs
    nbp = blk_ref.shape[0] // SUBLANES
    be_ref, nx_ref, nx2_ref, seq_ref = (
        _TableRow(blk_ref, r * nbp) for r in (BLK_EXPERT, BLK_NEXT, BLK_NEXT2, BLK_SEQ))
    nu = blk_ref[BLK_USED * nbp]

    def block(ref, q):
        rows = bs * ntp
        return ref.at[pl.ds(pl.multiple_of(q * rows, rows), rows), :]

    def in_copy(q):
        return pltpu.make_async_copy(block(xs_hbm, q), xbuf.at[q % ring], isems.at[q % ring])

    def out_copy(q):
        return pltpu.make_async_copy(ybuf.at[q % ring], block(y_hbm, q), osems.at[q % ring])

    def weight_copies(ex, slot):
        return (pltpu.make_async_copy(wg_hbm.at[ex], wg_st.at[slot], wsems.at[slot]),
                pltpu.make_async_copy(wu_hbm.at[ex], wu_st.at[slot], wsems.at[slot]),
                pltpu.make_async_copy(wd_hbm.at[ex], wd_st.at[slot], wsems.at[slot]))

    def start_weights(ex, slot):
        for c, prio in zip(weight_copies(ex, slot), WEIGHT_DMA_PRIORITIES):
            c.start(priority=prio)

    xbuf[...] = jnp.zeros_like(xbuf)
    hid_s[...] = jnp.zeros_like(hid_s)
    wgu_b[...] = jnp.zeros_like(wgu_b)
    wd_b[...] = jnp.zeros_like(wd_b)
    start_weights(be_ref[0], 0)

    @pl.when(nx_ref[0] >= 0)
    def _():
        start_weights(nx_ref[0], 1)

    for q in range(EXPERT_LOOKAHEAD):
        @pl.when(q < nu)
        def _():
            in_copy(q).start()

    seq_of = lambda j: seq_ref[jnp.clip(j, 0, nb - 1)]

    def load_expert_if_first(j):
        jc = jnp.clip(j, 0, nb - 1)
        e = be_ref[jc]

        @pl.when((j < nu) & ((j == 0) | (e != be_ref[jnp.maximum(jc - 1, 0)])))
        def _():
            seq = seq_ref[jc]
            slot = lax.rem(seq, stages)
            for c in weight_copies(e, slot):
                c.wait()
            nx2 = nx2_ref[jc]

            @pl.when(nx2 >= 0)
            def _():
                start_weights(nx2, lax.rem(seq + 2, stages))

            half = lax.rem(seq, 2)
            wgu_b[half, :, :de] = wg_st[slot].astype(BF16)
            wgu_b[half, :, de:] = wu_st[slot].astype(BF16)
            wd_b[lax.rem(seq, 3)] = wd_st[slot].astype(BF16)

    def up_proj(j):
        xb = _unpack_bf16_pairs(_tiles_to_rows(xbuf.at[j % ring], bs, ntp), BF16)
        gu = jnp.dot(xb, wgu_b[lax.rem(seq_of(j), 2)], preferred_element_type=F32)
        gate = gu[:, :de]
        return ((gate * _sigmoid(gate)) * gu[:, de:]).astype(BF16)

    def down_proj(hid, j):
        y = jnp.dot(hid, wd_b[lax.rem(seq_of(j), 3)], preferred_element_type=F32)
        _rows_to_tiles(ybuf.at[(j + ring) % ring], _pack_bf16_pairs(y.astype(BF16)), bs)

    def step(m, carry):
        j0 = 2 * m
        for j in (j0, j0 + 1):
            load_expert_if_first(j)
        for j in (j0, j0 + 1):
            @pl.when(j + EXPERT_LOOKAHEAD < nu)
            def _():
                in_copy(j + EXPERT_LOOKAHEAD).start()

            @pl.when(j < nu)
            def _():
                in_copy(j).wait()
        for j in (j0 - 1, j0):
            @pl.when(j >= ring)
            def _():
                out_copy(j - ring).wait()

        hid0 = up_proj(j0)
        down_proj(hid_s[...], j0 - 1)
        hid1 = up_proj(j0 + 1)
        down_proj(hid0, j0)
        hid_s[...] = hid1

        for j in (j0 - 1, j0):
            @pl.when((j >= 0) & (j < nu))
            def _():
                out_copy(j).start()

        return carry

    lax.fori_loop(0, nu // 2 + 1, step, 0)

    last_written = 2 * (nu // 2)
    for r in range(1, ring + 1):
        q = last_written - ring + r

        @pl.when((q >= 0) & (q < nu))
        def _():
            out_copy(q).wait()

    ybuf[0] = jnp.zeros(ybuf.shape[1:], ybuf.dtype)

    def zero_start(q, carry):
        pltpu.make_async_copy(ybuf.at[0], block(y_hbm, q), osems.at[0]).start()
        return carry

    def zero_wait(q, carry):
        pltpu.make_async_copy(ybuf.at[0], block(y_hbm, q), osems.at[0]).wait()
        return carry

    lax.fori_loop(nu, nb, zero_start, 0)
    lax.fori_loop(nu, nb, zero_wait, 0)


def _experts(blk_flat, xs, w_gate, w_up, w_down):
    ne, d, de = w_gate.shape
    in_rows = EXPERT_ROWS * (d // LANES) // 2
    nb = xs.shape[0] // in_rows
    ring = EXPERT_LOOKAHEAD + 2
    any_spec = pl.BlockSpec(memory_space=pl.ANY)
    grid_spec = pltpu.PrefetchScalarGridSpec(
        num_scalar_prefetch=1,
        grid=(1,),
        in_specs=[any_spec, any_spec, any_spec, any_spec],
        out_specs=any_spec,
        scratch_shapes=[pltpu.VMEM((ring, in_rows, LANES), U32), pltpu.VMEM((ring, in_rows, LANES), U32),
                        pltpu.VMEM((WEIGHT_STAGES, d, de), F32), pltpu.VMEM((WEIGHT_STAGES, d, de), F32),
                        pltpu.VMEM((WEIGHT_STAGES, de, d), F32),
                        pltpu.VMEM((2, d, 2 * de), BF16), pltpu.VMEM((3, de, d), BF16),
                        pltpu.VMEM((EXPERT_ROWS, de), BF16),
                        pltpu.SemaphoreType.DMA((WEIGHT_STAGES,)),
                        pltpu.SemaphoreType.DMA((ring,)), pltpu.SemaphoreType.DMA((ring,))],
    )
    return pl.pallas_call(
        functools.partial(_experts_kernel, nb),
        grid_spec=grid_spec,
        out_shape=jax.ShapeDtypeStruct((nb * in_rows, LANES), U32),
        compiler_params=pltpu.CompilerParams(
            dimension_semantics=("arbitrary",), vmem_limit_bytes=VMEM_LIMIT),
        name="experts",
    )(blk_flat, xs, w_gate, w_up, w_down)


def _combine_kernel(d0_ref, d1_ref, d2_ref, x1_ref, gt_ref, y_ref, gfin_ref, o_ref, ybuf, sems):
    tc, d = x1_ref.shape
    nt = ybuf.shape[2] // tc
    i = pl.program_id(0)
    n = pl.num_programs(0)
    slots = ybuf.shape[0]

    def issue_tile(dest_ref, slot, inline):
        _for_each_assignment(
            dest_ref, tc,
            lambda k, t, row: _row_copy(y_ref, row, ybuf.at[slot, k], t, nt,
                                        sems.at[slot]).start(priority=k),
            inline=inline)

    @pl.when(i == 0)
    def _():
        issue_tile(d0_ref, 0, False)
        issue_tile(d1_ref, 1, False)

    slot = i % slots
    _rows_wait(y_ref, TOP_K * tc, nt, sems.at[slot])
    issue_tile(d2_ref, (i + 2) % slots, True)
    y0 = _unpack_bf16_pairs(_tiles_to_rows(ybuf.at[slot, 0], tc, nt), F32)
    y1 = _unpack_bf16_pairs(_tiles_to_rows(ybuf.at[slot, 1], tc, nt), F32)
    gates = gt_ref[...]
    xo = x1_ref[...] + (gates[:, 0:1] * y0 + gates[:, 1:2] * y1)
    o_ref[...] = _rms(xo, gfin_ref[...])

    @pl.when(i == n - 1)
    def _():
        _rows_wait(y_ref, TOP_K * tc, nt, sems.at[(i + 1) % slots])
        _rows_wait(y_ref, TOP_K * tc, nt, sems.at[(i + 2) % slots])


def _combine(dest, x1, gates_t, y, g_final):
    t, d = x1.shape
    nt = d // LANES
    tc = COMBINE_TOKENS
    last = t // tc - 1
    assert last >= 1
    dest_spec = lambda ahead: pl.BlockSpec(
        (TOP_K * tc,), lambda i: (jnp.minimum(i + ahead, last),), memory_space=pltpu.SMEM)
    return pl.pallas_call(
        _combine_kernel,
        grid=(t // tc,),
        in_specs=[
            dest_spec(0), dest_spec(1), dest_spec(2),
            pl.BlockSpec((tc, d), lambda i: (i, 0)),
            pl.BlockSpec((tc, LANES), lambda i: (i, 0)),
            pl.BlockSpec(memory_space=pl.ANY),
            pl.BlockSpec((1, d), lambda i: (0, 0)),
        ],
        out_specs=pl.BlockSpec((tc, d), lambda i: (i, 0)),
        out_shape=jax.ShapeDtypeStruct((t, d), F32),
        scratch_shapes=[pltpu.VMEM((3, TOP_K, tc * nt // 2, LANES), U32),
                        pltpu.SemaphoreType.DMA((3,))],
        compiler_params=pltpu.CompilerParams(dimension_semantics=("arbitrary",)),
        name="combine",
    )(dest, dest, dest, x1, gates_t, y, g_final)


def _layer(x, g_mix, w_in, w_s, b_s, g_sgu, w_conv, w_out, g_ffn, w_rg, w_re, w_gate, w_up, w_down):
    b, s, d = x.shape
    t = b * s
    ne = w_gate.shape[0]
    bs = EXPERT_ROWS

    bias_full = jnp.repeat(b_s.T, d // N_GROUPS, axis=1)
    wr_t = jnp.concatenate([w_rg, w_re], axis=1).T
    wr_t = jnp.pad(wr_t, ((0, LANES - wr_t.shape[0]), (0, 0)))
    x1, h2t, logits_t = _mixer(
        x, g_mix.reshape(1, d), w_in, w_s, bias_full, g_sgu.reshape(1, d),
        w_conv, w_out, g_ffn.reshape(1, d), wr_t)

    info, gates_t, counts = _route(logits_t)

    n_rows = -(-(t * TOP_K + ne * (bs - 1)) // bs) * bs
    nb = n_rows // bs
    exp_tbl, blk_tbl = _plan(counts, nb)
    blk_flat = blk_tbl.reshape(-1)

    dest_dispatch, dest_combine = _place(info, exp_tbl, (DISPATCH_TOKENS, COMBINE_TOKENS))
    xs = _dispatch(exp_tbl.reshape(-1), blk_flat, dest_dispatch, h2t, n_rows)
    y = _experts(blk_flat, xs, w_gate, w_up, w_down)
    return x1.reshape(t, d), dest_combine, gates_t, y


def kernel(x, g_mix, w_in, w_s, b_s, g_sgu, w_conv, w_out, g_ffn, w_router_group, w_router_expert,
           w_gate, w_up, w_down, g_final):
    b, s, d = x.shape
    depth = g_mix.shape[0]
    assert depth == 1, "the final RMSNorm is fused into the last layer's combine"
    assert s % MIX_ROWS == 0 and MIX_ROWS % CHUNK == 0 and d % LANES == 0
    assert EXPERT_ROWS & (EXPERT_ROWS - 1) == 0, "block bookkeeping uses shifts"
    assert all((b * s) % n == 0 for n in (ROUTE_TOKENS, PLACE_TOKENS, DISPATCH_TOKENS, COMBINE_TOKENS))
    l = 0
    x1, dest, gates_t, y = _layer(
        x, g_mix[l], w_in[l], w_s[l], b_s[l], g_sgu[l], w_conv[l], w_out[l], g_ffn[l],
        w_router_group[l], w_router_expert[l], w_gate[l], w_up[l], w_down[l])
    out = _combine(dest, x1, gates_t, y, g_final.reshape(1, d))
    return out.reshape(b, s, d)
```

```python
import functools

import jax
import jax.numpy as jnp
from jax import lax
from jax.experimental import pallas as pl
from jax.experimental.pallas import tpu as pltpu

F32 = jnp.float32
BF16 = jnp.bfloat16
I32 = jnp.int32
U32 = jnp.uint32

EPS = 1e-6
LANES = 128
SUBLANES = 8
CHUNK = 128
N_GROUPS = 8
EXPERTS_PER_GROUP = 8
TOP_K = 2
CONV_K = 3
N_BRANCH = 7

MIX_ROWS = 512
ROUTE_TOKENS = 2048
ROUTE_CHUNK = 512
PLACE_TOKENS = 2048
DISPATCH_TOKENS = 2048
COMBINE_TOKENS = 512
MOVE_UNROLL = 8
WEIGHT_STAGES = 3
WEIGHT_DMA_PRIORITIES = (1, 1, 0)
EXPERT_ROWS = 256
EXPERT_LOOKAHEAD = 3
VMEM_LIMIT = 56 * 1024 * 1024


def _rms(x, g):
    return x * lax.rsqrt(jnp.mean(x * x, axis=-1, keepdims=True) + EPS) * g


def _sigmoid(x):
    return 0.5 * (1.0 + jnp.tanh(0.5 * x))


def _gelu_tanh(x):
    c = 0.7978845608028654
    return x * (0.5 * (1.0 + jnp.tanh(c * (x + 0.044715 * (x * x * x)))))


def _rows_to_tiles(dst_ref, val, rows):
    nt = val.shape[1] // LANES
    for c in range(nt):
        dst_ref[pl.ds(c, rows, stride=nt), :] = val[:, c * LANES:(c + 1) * LANES]


def _tiles_to_rows(src_ref, rows, nt):
    return jnp.concatenate([src_ref[pl.ds(c, rows, stride=nt), :] for c in range(nt)], axis=1)


def _pack_bf16_pairs(x):
    half = x.shape[1] // 2
    bits = pltpu.bitcast(x.astype(F32), U32)
    return lax.shift_right_logical(bits[:, :half], U32(16)) | bits[:, half:]


def _unpack_bf16_pairs(w, dtype):
    lo = pltpu.bitcast(lax.shift_left(w, U32(16)), F32)
    hi = pltpu.bitcast(w & U32(0xFFFF0000), F32)
    return jnp.concatenate([lo, hi], axis=1).astype(dtype)


def _mixer_kernel(x_ref, xp_ref, xn_ref, gmix_ref, win_hbm, ws_ref, bias_ref, gsgu_ref, wconv_ref,
                  wout_hbm, gffn_ref, wr_ref, x1_ref, h2t_ref, lt_ref,
                  vg_ref, z_ref, acc_ref, win_ref, wout_ref, stage, wsems):
    ts, d = x_ref.shape[1], x_ref.shape[2]
    s = pl.program_id(1)
    ns = pl.num_programs(1)
    gw = d // N_GROUPS
    nc = ts // CHUNK

    @pl.when((pl.program_id(0) == 0) & (s == 0))
    def _():
        def slab_copy(j):
            src = win_hbm.at[:, pl.ds(j * d, d)] if j < N_BRANCH else wout_hbm
            return pltpu.make_async_copy(src, stage.at[j % 2], wsems.at[j % 2])

        slab_copy(0).start()
        for j in range(N_BRANCH + 1):
            if j < N_BRANCH:
                slab_copy(j + 1).start()
            slab_copy(j).wait()
            if j < N_BRANCH:
                win_ref[:, j * d:(j + 1) * d] = stage[j % 2].astype(BF16)
            else:
                wout_ref[...] = stage[j % 2].astype(BF16)

    x = x_ref[0]
    gmix = gmix_ref[...]
    h = _rms(x, gmix).astype(BF16)

    def proj(j, n=1):
        return jnp.dot(h, win_ref[:, j * d:(j + n) * d], preferred_element_type=F32)

    v_raw = proj(1)
    xh = jnp.concatenate([xp_ref[0], xn_ref[0]], axis=0)
    h_ext = jnp.concatenate([h, _rms(xh, gmix).astype(BF16)], axis=0)
    cx = jnp.dot(h_ext, win_ref[:, 3 * d:5 * d], preferred_element_type=F32)

    gv = _gelu_tanh(v_raw)
    for g in range(N_GROUPS):
        cs = slice(g * gw, (g + 1) * gw)
        blk = gv[:, cs]
        mu = jnp.mean(blk, axis=-1, keepdims=True)
        dv = blk - mu
        var = jnp.mean(dv * dv, axis=-1, keepdims=True)
        vg_ref[:, cs] = (dv * lax.rsqrt(var + EPS) * gsgu_ref[:, cs]).astype(BF16)
    for g in range(N_GROUPS):
        cs = slice(g * gw, (g + 1) * gw)
        vcat = jnp.concatenate([vg_ref[n * CHUNK:(n + 1) * CHUNK, cs] for n in range(nc)], axis=1)
        zg = jnp.dot(ws_ref[g].astype(BF16), vcat, preferred_element_type=F32)
        for n in range(nc):
            z_ref[n * CHUNK:(n + 1) * CHUNK, cs] = zg[:, n * gw:(n + 1) * gw]
    u = _gelu_tanh(proj(0))
    ga = _sigmoid(proj(5))
    for n in range(nc):
        rs = slice(n * CHUNK, (n + 1) * CHUNK)
        acc_ref[rs, :] = ga[rs] * (u[rs] * (z_ref[rs, :] + bias_ref[...]))

    z2 = cx[:ts, :d] * cx[:ts, d:]
    z2h = cx[ts:, :d] * cx[ts:, d:]
    prev = jnp.where(s > 0, z2h[SUBLANES - 1:SUBLANES, :], 0.0)
    nxt = jnp.where(s < ns - 1, z2h[SUBLANES:SUBLANES + 1, :], 0.0)
    row = lax.broadcasted_iota(I32, (ts, d), 0)
    zm1 = jnp.where(row == 0, prev, pltpu.roll(z2, 1, 0))
    zp1 = jnp.where(row == ts - 1, nxt, pltpu.roll(z2, ts - 1, 0))
    conv = wconv_ref[0:1, :] * zm1 + wconv_ref[1:2, :] * z2 + wconv_ref[2:3, :] * zp1
    cb = proj(2)
    gb = _sigmoid(proj(6))
    merged = acc_ref[...] + gb * (cb * conv)

    x1 = x + jnp.dot(merged.astype(BF16), wout_ref[...], preferred_element_type=F32)
    x1_ref[0] = x1

    h2 = _rms(x1, gffn_ref[...]).astype(BF16)
    lt_ref[...] = lax.dot_general(wr_ref[...].astype(BF16), h2, (((1,), (1,)), ((), ())),
                                  preferred_element_type=F32)
    _rows_to_tiles(h2t_ref, _pack_bf16_pairs(h2), ts)


def _mixer(x, g_mix, w_in, w_s, bias_full, g_sgu, w_conv, w_out, g_ffn, wr_t):
    b, s, d = x.shape
    ts = MIX_ROWS
    ns = s // ts
    t = b * s
    ntp = d // (2 * LANES)
    hb = ts // SUBLANES
    last_hb = s // SUBLANES - 1

    const = lambda *shape: pl.BlockSpec(shape, lambda bi, si: (0,) * len(shape))
    in_specs = [
        pl.BlockSpec((1, ts, d), lambda bi, si: (bi, si, 0)),
        pl.BlockSpec((1, SUBLANES, d), lambda bi, si: (bi, jnp.maximum(si * hb - 1, 0), 0)),
        pl.BlockSpec((1, SUBLANES, d), lambda bi, si: (bi, jnp.minimum((si + 1) * hb, last_hb), 0)),
        const(1, d),
        pl.BlockSpec(memory_space=pl.ANY),
        const(N_GROUPS, CHUNK, CHUNK),
        const(CHUNK, d),
        const(1, d),
        const(CONV_K, d),
        pl.BlockSpec(memory_space=pl.ANY),
        const(1, d),
        const(LANES, d),
    ]
    out_specs = [
        pl.BlockSpec((1, ts, d), lambda bi, si: (bi, si, 0)),
        pl.BlockSpec((ts * ntp, LANES), lambda bi, si: (bi * ns + si, 0)),
        pl.BlockSpec((LANES, ts), lambda bi, si: (0, bi * ns + si)),
    ]
    out_shape = [
        jax.ShapeDtypeStruct((b, s, d), F32),
        jax.ShapeDtypeStruct((t * ntp, LANES), U32),
        jax.ShapeDtypeStruct((LANES, t), F32),
    ]
    return pl.pallas_call(
        _mixer_kernel,
        grid=(b, ns),
        in_specs=in_specs,
        out_specs=out_specs,
        out_shape=out_shape,
        scratch_shapes=[pltpu.VMEM((ts, d), BF16), pltpu.VMEM((ts, d), F32), pltpu.VMEM((ts, d), F32),
                        pltpu.VMEM((d, N_BRANCH * d), BF16), pltpu.VMEM((d, d), BF16),
                        pltpu.VMEM((2, d, d), F32), pltpu.SemaphoreType.DMA((2,))],
        compiler_params=pltpu.CompilerParams(
            dimension_semantics=("arbitrary", "arbitrary"), vmem_limit_bytes=VMEM_LIMIT),
        name="mixer",
    )(x, x, x, g_mix, w_in, w_s, bias_full, g_sgu, w_conv, w_out, g_ffn, wr_t)


def _route_kernel(lt_ref, info_ref, gt_ref, cnt_ref, carry_ref):
    tb = lt_ref.shape[1]
    ne = N_GROUPS * EXPERTS_PER_GROUP

    @pl.when(pl.program_id(0) == 0)
    def _():
        carry_ref[...] = jnp.zeros_like(carry_ref)

    row8 = lax.broadcasted_iota(I32, (SUBLANES, tb), 0)
    gl = lt_ref[0:N_GROUPS, :]
    gmax = jnp.max(gl, axis=0, keepdims=True)
    gidx = jnp.min(jnp.where(gl == gmax, row8, N_GROUPS), axis=0, keepdims=True)
    pg = 1.0 / jnp.sum(jnp.exp(gl - gmax), axis=0, keepdims=True)

    sel = jnp.zeros((EXPERTS_PER_GROUP, tb), F32)
    for g in range(N_GROUPS):
        lo = N_GROUPS + g * EXPERTS_PER_GROUP
        sel = jnp.where(gidx == g, lt_ref[lo:lo + EXPERTS_PER_GROUP, :], sel)
    m1 = jnp.max(sel, axis=0, keepdims=True)
    i1 = jnp.min(jnp.where(sel == m1, row8, EXPERTS_PER_GROUP), axis=0, keepdims=True)
    sel2 = jnp.where(row8 == i1, -jnp.inf, sel)
    m2 = jnp.max(sel2, axis=0, keepdims=True)
    i2 = jnp.min(jnp.where(sel2 == m2, row8, EXPERTS_PER_GROUP), axis=0, keepdims=True)
    e2 = jnp.exp(m2 - m1)
    den = 1.0 + e2
    gate0 = pg * (1.0 / den)
    gate1 = pg * (e2 / den)
    eid0 = gidx * EXPERTS_PER_GROUP + i1
    eid1 = gidx * EXPERTS_PER_GROUP + i2

    rowe = lax.broadcasted_iota(I32, (ne, tb), 0)
    hit0 = rowe == eid0
    hit1 = rowe == eid1
    onehot = jnp.where(hit0 | hit1, 1.0, 0.0)
    sub = ROUTE_CHUNK
    before = (lax.broadcasted_iota(I32, (sub, sub), 0) < lax.broadcasted_iota(I32, (sub, sub), 1))
    before = jnp.where(before, 1.0, 0.0).astype(BF16)
    carry = carry_ref[:, 0:1]
    parts = []
    for c in range(tb // sub):
        part = onehot[:, c * sub:(c + 1) * sub]
        parts.append(jnp.dot(part.astype(BF16), before, preferred_element_type=F32) + carry)
        carry = carry + jnp.sum(part, axis=1, keepdims=True)
    base = jnp.concatenate(parts, axis=1)
    rank0 = jnp.sum(jnp.where(hit0, base, 0.0), axis=0, keepdims=True).astype(I32)
    rank1 = jnp.sum(jnp.where(hit1, base, 0.0), axis=0, keepdims=True).astype(I32)
    carry_ref[...] = jnp.broadcast_to(carry, carry_ref.shape)
    cnt_ref[...] = carry_ref[...].astype(I32)

    info_ref[...] = jnp.where(row8 == 0, eid0, jnp.where(row8 == 1, eid1,
                              jnp.where(row8 == 2, rank0, jnp.where(row8 == 3, rank1, 0))))
    rowl = lax.broadcasted_iota(I32, (LANES, tb), 0)
    gates = jnp.where(rowl == 0, gate0, jnp.where(rowl == 1, gate1, 0.0))
    gt_ref[...] = gates.T


def _route(logits_t):
    t = logits_t.shape[1]
    tb = ROUTE_TOKENS
    ne = N_GROUPS * EXPERTS_PER_GROUP
    return pl.pallas_call(
        _route_kernel,
        grid=(t // tb,),
        in_specs=[pl.BlockSpec((LANES, tb), lambda i: (0, i))],
        out_specs=[
            pl.BlockSpec((SUBLANES, tb), lambda i: (0, i)),
            pl.BlockSpec((tb, LANES), lambda i: (i, 0)),
            pl.BlockSpec((ne, LANES), lambda i: (0, 0)),
        ],
        out_shape=[
            jax.ShapeDtypeStruct((SUBLANES, t), I32),
            jax.ShapeDtypeStruct((t, LANES), F32),
            jax.ShapeDtypeStruct((ne, LANES), I32),
        ],
        scratch_shapes=[pltpu.VMEM((ne, LANES), F32)],
        compiler_params=pltpu.CompilerParams(dimension_semantics=("arbitrary",)),
        name="route",
    )(logits_t)


BLK_EXPERT, BLK_NEXT, BLK_USED, BLK_NEXT2, BLK_SEQ = range(5)
EXP_START, EXP_COUNT, EXP_PADDED = range(3)


class _TableRow:
    def __init__(self, ref, offset):
        self.ref, self.offset = ref, offset

    def __getitem__(self, j):
        return self.ref[self.offset + j]


def _plan_kernel(cnt_ref, exp_ref, blk_ref):
    ne = cnt_ref.shape[0]
    nbp = blk_ref.shape[1]
    shift = EXPERT_ROWS.bit_length() - 1
    pad_rows = lambda c: lax.shift_left(lax.shift_right_logical(c + (EXPERT_ROWS - 1), shift), shift)
    cnt = cnt_ref[...]
    padded = pad_rows(cnt)
    padded_lanes = pad_rows(cnt.astype(F32).T[:ne, :ne].astype(I32))
    e_sub = lax.broadcasted_iota(I32, (ne, ne), 0)
    e_lane = lax.broadcasted_iota(I32, (ne, ne), 1)
    pend = jnp.sum(jnp.where(e_lane <= e_sub, padded_lanes, 0), axis=1, keepdims=True)
    pstart = pend - padded[:, 0:1]

    lane = lax.broadcasted_iota(I32, (ne, LANES), 1)
    exp_ref[...] = jnp.where(lane == EXP_START, pstart, jnp.where(
        lane == EXP_COUNT, cnt, jnp.where(lane == EXP_PADDED, padded, 0)))

    first_row = lax.broadcasted_iota(I32, (ne, nbp), 1) * EXPERT_ROWS
    e_col = lax.broadcasted_iota(I32, (ne, nbp), 0)
    blk_e = jnp.minimum(jnp.sum(jnp.where(pend <= first_row, 1, 0), axis=0, keepdims=True), ne - 1)
    has_rows = padded[:, 0:1] > 0
    nxt_e = jnp.min(jnp.where((e_col > blk_e) & has_rows, e_col, ne), axis=0, keepdims=True)
    nxt2_e = jnp.min(jnp.where((e_col > nxt_e) & has_rows, e_col, ne), axis=0, keepdims=True)
    nxt_e = jnp.where(nxt_e == ne, -1, nxt_e)
    nxt2_e = jnp.where(nxt2_e == ne, -1, nxt2_e)
    n_used = lax.shift_right_logical(jnp.max(pend, axis=0, keepdims=True), shift)
    blk_seq = jnp.sum(jnp.where((e_col < blk_e) & has_rows, 1, 0), axis=0, keepdims=True)
    row8 = lax.broadcasted_iota(I32, (SUBLANES, nbp), 0)
    blk_ref[...] = jnp.where(row8 == BLK_EXPERT, blk_e, jnp.where(row8 == BLK_NEXT, nxt_e, jnp.where(
        row8 == BLK_USED, n_used, jnp.where(row8 == BLK_NEXT2, nxt2_e, jnp.where(
            row8 == BLK_SEQ, blk_seq, 0)))))


def _plan(counts, nb):
    ne = counts.shape[0]
    nbp = -(-nb // LANES) * LANES
    return pl.pallas_call(
        _plan_kernel,
        out_shape=[jax.ShapeDtypeStruct((ne, LANES), I32), jax.ShapeDtypeStruct((SUBLANES, nbp), I32)],
        name="plan",
    )(counts)


def _place_kernel(info_ref, exp_ref, *dest_refs):
    tb = info_ref.shape[1]
    ne = exp_ref.shape[0]
    rowe = lax.broadcasted_iota(I32, (ne, tb), 0)
    ps = exp_ref[:, EXP_START:EXP_START + 1]
    dest = []
    for k in range(TOP_K):
        start = jnp.sum(jnp.where(rowe == info_ref[k:k + 1, :], ps, 0), axis=0, keepdims=True)
        dest.append(start + info_ref[TOP_K + k:TOP_K + k + 1, :])
    for ref in dest_refs:
        n = ref.shape[2] // TOP_K
        for q in range(tb // n):
            ref[q] = jnp.concatenate([dk[:, q * n:(q + 1) * n] for dk in dest], axis=1)


def _place(info, exp_tbl, tile_tokens):
    t = info.shape[1]
    tb = PLACE_TOKENS
    ne = exp_tbl.shape[0]
    assert all(tb % n == 0 for n in tile_tokens)
    outs = pl.pallas_call(
        _place_kernel,
        grid=(t // tb,),
        in_specs=[pl.BlockSpec((SUBLANES, tb), lambda i: (0, i)),
                  pl.BlockSpec((ne, LANES), lambda i: (0, 0))],
        out_specs=[pl.BlockSpec((tb // n, 1, TOP_K * n), lambda i: (i, 0, 0)) for n in tile_tokens],
        out_shape=[jax.ShapeDtypeStruct((t // n, 1, TOP_K * n), I32) for n in tile_tokens],
        compiler_params=pltpu.CompilerParams(dimension_semantics=("arbitrary",)),
        name="place",
    )(info, exp_tbl)
    return [o.reshape(-1) for o in outs]


def _row_copy(src_ref, src_row, dst_ref, dst_row, nt, sem):
    first = lambda row: row * nt if isinstance(row, int) else pl.multiple_of(row * nt, nt)
    return pltpu.make_async_copy(
        src_ref.at[pl.ds(first(src_row), nt), :], dst_ref.at[pl.ds(first(dst_row), nt), :], sem)


def _rows_wait(ref, n_rows, nt, sem):
    pltpu.make_async_copy(ref.at[pl.ds(0, n_rows * nt), :], ref.at[pl.ds(0, n_rows * nt), :], sem).wait()


def _for_each_assignment(dest_ref, n_tok, start_copy, inline=False):
    def group(g, c):
        t0 = g * MOVE_UNROLL
        rows = [[dest_ref[k * n_tok + t0 + u] for k in range(TOP_K)] for u in range(MOVE_UNROLL)]
        for u in range(MOVE_UNROLL):
            for k in range(TOP_K):
                start_copy(k, t0 + u, rows[u][k])
        return c

    if inline:
        for g in range(n_tok // MOVE_UNROLL):
            group(g, 0)
    else:
        lax.fori_loop(0, n_tok // MOVE_UNROLL, group, 0)


def _dispatch_kernel(nt, nb, exp_ref, blk_ref, dest_ref, h2_ref, xs_ref, zbuf, sem, zsem):
    td = h2_ref.shape[0] // nt
    ne = exp_ref.shape[0] // LANES
    n_used = blk_ref[BLK_USED * (blk_ref.shape[0] // SUBLANES)]
    blk_rows = zbuf.shape[0]

    def zero_rows(start_not_wait):
        def fire(c):
            c.start() if start_not_wait else c.wait()

        def expert(e, carry):
            cnt = exp_ref[e * LANES + EXP_COUNT]
            row = exp_ref[e * LANES + EXP_START] + cnt
            n = exp_ref[e * LANES + EXP_PADDED] - cnt
            p = EXPERT_ROWS // 2
            while p >= 1:
                has = (n & p) != 0
                r, sz = row, p

                @pl.when(has)
                def _():
                    fire(pltpu.make_async_copy(
                        zbuf.at[pl.ds(0, sz * nt), :],
                        xs_ref.at[pl.ds(pl.multiple_of(r * nt, nt), sz * nt), :], zsem))

                row = row + jnp.where(has, p, 0)
                p //= 2
            return carry

        lax.fori_loop(0, ne, expert, 0)

        def tail(q, carry):
            fire(pltpu.make_async_copy(
                zbuf, xs_ref.at[pl.ds(pl.multiple_of(q * blk_rows, blk_rows), blk_rows), :], zsem))
            return carry

        lax.fori_loop(n_used, nb, tail, 0)

    @pl.when(pl.program_id(0) == 0)
    def _():
        zbuf[...] = jnp.zeros_like(zbuf)
        zero_rows(True)

    _for_each_assignment(
        dest_ref, td,
        lambda k, t, row: _row_copy(h2_ref, t, xs_ref, row, nt, sem).start(priority=k))
    _rows_wait(xs_ref, TOP_K * td, nt, sem)

    @pl.when(pl.program_id(0) == 0)
    def _():
        zero_rows(False)


def _dispatch(exp_flat, blk_flat, dest, h2t, n_rows):
    td = DISPATCH_TOKENS
    t = dest.shape[0] // TOP_K
    nt = h2t.shape[0] // t
    nb = n_rows // EXPERT_ROWS
    grid_spec = pltpu.PrefetchScalarGridSpec(
        num_scalar_prefetch=2,
        grid=(t // td,),
        in_specs=[
            pl.BlockSpec((TOP_K * td,), lambda i, *_: (i,), memory_space=pltpu.SMEM),
            pl.BlockSpec((td * nt, LANES), lambda i, *_: (i, 0)),
        ],
        out_specs=pl.BlockSpec(memory_space=pl.ANY),
        scratch_shapes=[pltpu.VMEM((EXPERT_ROWS * nt, LANES), h2t.dtype),
                        pltpu.SemaphoreType.DMA, pltpu.SemaphoreType.DMA],
    )
    return pl.pallas_call(
        functools.partial(_dispatch_kernel, nt, nb),
        grid_spec=grid_spec,
        out_shape=jax.ShapeDtypeStruct((n_rows * nt, LANES), h2t.dtype),
        compiler_params=pltpu.CompilerParams(dimension_semantics=("arbitrary",)),
        name="dispatch",
    )(exp_flat, blk_flat, dest, h2t)


def _experts_kernel(nb, blk_ref, xs_hbm, wg_hbm, wu_hbm, wd_hbm,
                    y_hbm, xbuf, ybuf, wg_st, wu_st, wd_st, wgu_b, wd_b, hid_s, wsems, isems, osems):
    ring = xbuf.shape[0]
    stages, d, de = wg_st.shape
    bs = EXPERT_ROWS
    ntp = xbuf.shape[1] // bs
    nbp = blk_ref.shape[0] // SUBLANES
    be_ref, nx_ref, nx2_ref, seq_ref = (
        _TableRow(blk_ref, r * nbp) for r in (BLK_EXPERT, BLK_NEXT, BLK_NEXT2, BLK_SEQ))
    nu = blk_ref[BLK_USED * nbp]

    def block(ref, q):
        rows = bs * ntp
        return ref.at[pl.ds(pl.multiple_of(q * rows, rows), rows), :]

    def in_copy(q):
        return pltpu.make_async_copy(block(xs_hbm, q), xbuf.at[q % ring], isems.at[q % ring])

    def out_copy(q):
        return pltpu.make_async_copy(ybuf.at[q % ring], block(y_hbm, q), osems.at[q % ring])

    def weight_copies(ex, slot):
        return (pltpu.make_async_copy(wg_hbm.at[ex], wg_st.at[slot], wsems.at[slot]),
                pltpu.make_async_copy(wu_hbm.at[ex], wu_st.at[slot], wsems.at[slot]),
                pltpu.make_async_copy(wd_hbm.at[ex], wd_st.at[slot], wsems.at[slot]))

    def start_weights(ex, slot):
        for c, prio in zip(weight_copies(ex, slot), WEIGHT_DMA_PRIORITIES):
            c.start(priority=prio)

    xbuf[...] = jnp.zeros_like(xbuf)
    hid_s[...] = jnp.zeros_like(hid_s)
    wgu_b[...] = jnp.zeros_like(wgu_b)
    wd_b[...] = jnp.zeros_like(wd_b)
    start_weights(be_ref[0], 0)

    @pl.when(nx_ref[0] >= 0)
    def _():
        start_weights(nx_ref[0], 1)

    for q in range(EXPERT_LOOKAHEAD):
        @pl.when(q < nu)
        def _():
            in_copy(q).start()

    seq_of = lambda j: seq_ref[jnp.clip(j, 0, nb - 1)]

    def load_expert_if_first(j):
        jc = jnp.clip(j, 0, nb - 1)
        e = be_ref[jc]

        @pl.when((j < nu) & ((j == 0) | (e != be_ref[jnp.maximum(jc - 1, 0)])))
        def _():
            seq = seq_ref[jc]
            slot = lax.rem(seq, stages)
            for c in weight_copies(e, slot):
                c.wait()
            nx2 = nx2_ref[jc]

            @pl.when(nx2 >= 0)
            def _():
                start_weights(nx2, lax.rem(seq + 2, stages))

            half = lax.rem(seq, 2)
            wgu_b[half, :, :de] = wg_st[slot].astype(BF16)
            wgu_b[half, :, de:] = wu_st[slot].astype(BF16)
            wd_b[lax.rem(seq, 3)] = wd_st[slot].astype(BF16)

    def up_proj(j):
        xb = _unpack_bf16_pairs(_tiles_to_rows(xbuf.at[j % ring], bs, ntp), BF16)
        gu = jnp.dot(xb, wgu_b[lax.rem(seq_of(j), 2)], preferred_element_type=F32)
        gate = gu[:, :de]
        return ((gate * _sigmoid(gate)) * gu[:, de:]).astype(BF16)

    def down_proj(hid, j):
        y = jnp.dot(hid, wd_b[lax.rem(seq_of(j), 3)], preferred_element_type=F32)
        _rows_to_tiles(ybuf.at[(j + ring) % ring], _pack_bf16_pairs(y.astype(BF16)), bs)

    def step(m, carry):
        j0 = 2 * m
        for j in (j0, j0 + 1):
            load_expert_if_first(j)
        for j in (j0, j0 + 1):
            @pl.when(j + EXPERT_LOOKAHEAD < nu)
            def _():
                in_copy(j + EXPERT_LOOKAHEAD).start()

            @pl.when(j < nu)
            def _():
                in_copy(j).wait()
        for j in (j0 - 1, j0):
            @pl.when(j >= ring)
            def _():
                out_copy(j - ring).wait()

        hid0 = up_proj(j0)
        down_proj(hid_s[...], j0 - 1)
        hid1 = up_proj(j0 + 1)
        down_proj(hid0, j0)
        hid_s[...] = hid1

        for j in (j0 - 1, j0):
            @pl.when((j >= 0) & (j < nu))
            def _():
                out_copy(j).start()

        return carry

    lax.fori_loop(0, nu // 2 + 1, step, 0)

    last_written = 2 * (nu // 2)
    for r in range(1, ring + 1):
        q = last_written - ring + r

        @pl.when((q >= 0) & (q < nu))
        def _():
            out_copy(q).wait()

    ybuf[0] = jnp.zeros(ybuf.shape[1:], ybuf.dtype)

    def zero_start(q, carry):
        pltpu.make_async_copy(ybuf.at[0], block(y_hbm, q), osems.at[0]).start()
        return carry

    def zero_wait(q, carry):
        pltpu.make_async_copy(ybuf.at[0], block(y_hbm, q), osems.at[0]).wait()
        return carry

    lax.fori_loop(nu, nb, zero_start, 0)
    lax.fori_loop(nu, nb, zero_wait, 0)


def _experts(blk_flat, xs, w_gate, w_up, w_down):
    ne, d, de = w_gate.shape
    in_rows = EXPERT_ROWS * (d // LANES) // 2
    nb = xs.shape[0] // in_rows
    ring = EXPERT_LOOKAHEAD + 2
    any_spec = pl.BlockSpec(memory_space=pl.ANY)
    grid_spec = pltpu.PrefetchScalarGridSpec(
        num_scalar_prefetch=1,
        grid=(1,),
        in_specs=[any_spec, any_spec, any_spec, any_spec],
        out_specs=any_spec,
        scratch_shapes=[pltpu.VMEM((ring, in_rows, LANES), U32), pltpu.VMEM((ring, in_rows, LANES), U32),
                        pltpu.VMEM((WEIGHT_STAGES, d, de), F32), pltpu.VMEM((WEIGHT_STAGES, d, de), F32),
                        pltpu.VMEM((WEIGHT_STAGES, de, d), F32),
                        pltpu.VMEM((2, d, 2 * de), BF16), pltpu.VMEM((3, de, d), BF16),
                        pltpu.VMEM((EXPERT_ROWS, de), BF16),
                        pltpu.SemaphoreType.DMA((WEIGHT_STAGES,)),
                        pltpu.SemaphoreType.DMA((ring,)), pltpu.SemaphoreType.DMA((ring,))],
    )
    return pl.pallas_call(
        functools.partial(_experts_kernel, nb),
        grid_spec=grid_spec,
        out_shape=jax.ShapeDtypeStruct((nb * in_rows, LANES), U32),
        compiler_params=pltpu.CompilerParams(
            dimension_semantics=("arbitrary",), vmem_limit_bytes=VMEM_LIMIT),
        name="experts",
    )(blk_flat, xs, w_gate, w_up, w_down)


def _combine_kernel(d0_ref, d1_ref, d2_ref, x1_ref, gt_ref, y_ref, gfin_ref, o_ref, ybuf, sems):
    tc, d = x1_ref.shape
    nt = ybuf.shape[2] // tc
    i = pl.program_id(0)
    n = pl.num_programs(0)
    slots = ybuf.shape[0]

    def issue_tile(dest_ref, slot, inline):
        _for_each_assignment(
            dest_ref, tc,
            lambda k, t, row: _row_copy(y_ref, row, ybuf.at[slot, k], t, nt,
                                        sems.at[slot]).start(priority=k),
            inline=inline)

    @pl.when(i == 0)
    def _():
        issue_tile(d0_ref, 0, False)
        issue_tile(d1_ref, 1, False)

    slot = i % slots
    _rows_wait(y_ref, TOP_K * tc, nt, sems.at[slot])
    issue_tile(d2_ref, (i + 2) % slots, True)
    y0 = _unpack_bf16_pairs(_tiles_to_rows(ybuf.at[slot, 0], tc, nt), F32)
    y1 = _unpack_bf16_pairs(_tiles_to_rows(ybuf.at[slot, 1], tc, nt), F32)
    gates = gt_ref[...]
    xo = x1_ref[...] + (gates[:, 0:1] * y0 + gates[:, 1:2] * y1)
    o_ref[...] = _rms(xo, gfin_ref[...])

    @pl.when(i == n - 1)
    def _():
        _rows_wait(y_ref, TOP_K * tc, nt, sems.at[(i + 1) % slots])
        _rows_wait(y_ref, TOP_K * tc, nt, sems.at[(i + 2) % slots])


def _combine(dest, x1, gates_t, y, g_final):
    t, d = x1.shape
    nt = d // LANES
    tc = COMBINE_TOKENS
    last = t // tc - 1
    assert last >= 1
    dest_spec = lambda ahead: pl.BlockSpec(
        (TOP_K * tc,), lambda i: (jnp.minimum(i + ahead, last),), memory_space=pltpu.SMEM)
    return pl.pallas_call(
        _combine_kernel,
        grid=(t // tc,),
        in_specs=[
            dest_spec(0), dest_spec(1), dest_spec(2),
            pl.BlockSpec((tc, d), lambda i: (i, 0)),
            pl.BlockSpec((tc, LANES), lambda i: (i, 0)),
            pl.BlockSpec(memory_space=pl.ANY),
            pl.BlockSpec((1, d), lambda i: (0, 0)),
        ],
        out_specs=pl.BlockSpec((tc, d), lambda i: (i, 0)),
        out_shape=jax.ShapeDtypeStruct((t, d), F32),
        scratch_shapes=[pltpu.VMEM((3, TOP_K, tc * nt // 2, LANES), U32),
                        pltpu.SemaphoreType.DMA((3,))],
        compiler_params=pltpu.CompilerParams(dimension_semantics=("arbitrary",)),
        name="combine",
    )(dest, dest, dest, x1, gates_t, y, g_final)


def _layer(x, g_mix, w_in, w_s, b_s, g_sgu, w_conv, w_out, g_ffn, w_rg, w_re, w_gate, w_up, w_down):
    b, s, d = x.shape
    t = b * s
    ne = w_gate.shape[0]
    bs = EXPERT_ROWS

    bias_full = jnp.repeat(b_s.T, d // N_GROUPS, axis=1)
    wr_t = jnp.concatenate([w_rg, w_re], axis=1).T
    wr_t = jnp.pad(wr_t, ((0, LANES - wr_t.shape[0]), (0, 0)))
    x1, h2t, logits_t = _mixer(
        x, g_mix.reshape(1, d), w_in, w_s, bias_full, g_sgu.reshape(1, d),
        w_conv, w_out, g_ffn.reshape(1, d), wr_t)

    info, gates_t, counts = _route(logits_t)

    n_rows = -(-(t * TOP_K + ne * (bs - 1)) // bs) * bs
    nb = n_rows // bs
    exp_tbl, blk_tbl = _plan(counts, nb)
    blk_flat = blk_tbl.reshape(-1)

    dest_dispatch, dest_combine = _place(info, exp_tbl, (DISPATCH_TOKENS, COMBINE_TOKENS))
    xs = _dispatch(exp_tbl.reshape(-1), blk_flat, dest_dispatch, h2t, n_rows)
    y = _experts(blk_flat, xs, w_gate, w_up, w_down)
    return x1.reshape(t, d), dest_combine, gates_t, y


def kernel(x, g_mix, w_in, w_s, b_s, g_sgu, w_conv, w_out, g_ffn, w_router_group, w_router_expert,
           w_gate, w_up, w_down, g_final):
    b, s, d = x.shape
    depth = g_mix.shape[0]
    assert depth == 1, "the final RMSNorm is fused into the last layer's combine"
    assert s % MIX_ROWS == 0 and MIX_ROWS % CHUNK == 0 and d % LANES == 0
    assert EXPERT_ROWS & (EXPERT_ROWS - 1) == 0, "block bookkeeping uses shifts"
    assert all((b * s) % n == 0 for n in (ROUTE_TOKENS, PLACE_TOKENS, DISPATCH_TOKENS, COMBINE_TOKENS))
    l = 0
    x1, dest, gates_t, y = _layer(
        x, g_mix[l], w_in[l], w_s[l], b_s[l], g_sgu[l], w_conv[l], w_out[l], g_ffn[l],
        w_router_group[l], w_router_expert[l], w_gate[l], w_up[l], w_down[l])
    out = _combine(dest, x1, gates_t, y, g_final.reshape(1, d))
    return out.reshape(b, s, d)
```

```python
import functools

import jax
import jax.numpy as jnp
from jax import lax
from jax.experimental import pallas as pl
from jax.experimental.pallas import tpu as pltpu

F32 = jnp.float32
BF16 = jnp.bfloat16
I32 = jnp.int32
U32 = jnp.uint32

EPS = 1e-6
LANES = 128
SUBLANES = 8
CHUNK = 128
N_GROUPS = 8
EXPERTS_PER_GROUP = 8
TOP_K = 2
CONV_K = 3
N_BRANCH = 7

MIX_ROWS = 512
ROUTE_TOKENS = 2048
ROUTE_CHUNK = 512
PLACE_TOKENS = 2048
DISPATCH_TOKENS = 2048
COMBINE_TOKENS = 512
MOVE_UNROLL = 8
WEIGHT_STAGES = 3
WEIGHT_DMA_PRIORITIES = (1, 1, 0)
EXPERT_ROWS = 256
EXPERT_LOOKAHEAD = 3
VMEM_LIMIT = 56 * 1024 * 1024


def _rms(x, g):
    return x * lax.rsqrt(jnp.mean(x * x, axis=-1, keepdims=True) + EPS) * g


def _sigmoid(x):
    return 0.5 * (1.0 + jnp.tanh(0.5 * x))


def _gelu_tanh(x):
    c = 0.7978845608028654
    return x * (0.5 * (1.0 + jnp.tanh(c * (x + 0.044715 * (x * x * x)))))


def _rows_to_tiles(dst_ref, val, rows):
    nt = val.shape[1] // LANES
    for c in range(nt):
        dst_ref[pl.ds(c, rows, stride=nt), :] = val[:, c * LANES:(c + 1) * LANES]


def _tiles_to_rows(src_ref, rows, nt):
    return jnp.concatenate([src_ref[pl.ds(c, rows, stride=nt), :] for c in range(nt)], axis=1)


def _pack_bf16_pairs(x):
    half = x.shape[1] // 2
    bits = pltpu.bitcast(x.astype(F32), U32)
    return lax.shift_right_logical(bits[:, :half], U32(16)) | bits[:, half:]


def _unpack_bf16_pairs(w, dtype):
    lo = pltpu.bitcast(lax.shift_left(w, U32(16)), F32)
    hi = pltpu.bitcast(w & U32(0xFFFF0000), F32)
    return jnp.concatenate([lo, hi], axis=1).astype(dtype)


def _mixer_kernel(x_ref, xp_ref, xn_ref, gmix_ref, win_hbm, ws_ref, bias_ref, gsgu_ref, wconv_ref,
                  wout_hbm, gffn_ref, wr_ref, x1_ref, h2t_ref, lt_ref,
                  vg_ref, z_ref, acc_ref, win_ref, wout_ref, stage, wsems):
    ts, d = x_ref.shape[1], x_ref.shape[2]
    s = pl.program_id(1)
    ns = pl.num_programs(1)
    gw = d // N_GROUPS
    nc = ts // CHUNK

    @pl.when((pl.program_id(0) == 0) & (s == 0))
    def _():
        def slab_copy(j):
            src = win_hbm.at[:, pl.ds(j * d, d)] if j < N_BRANCH else wout_hbm
            return pltpu.make_async_copy(src, stage.at[j % 2], wsems.at[j % 2])

        slab_copy(0).start()
        for j in range(N_BRANCH + 1):
            if j < N_BRANCH:
                slab_copy(j + 1).start()
            slab_copy(j).wait()
            if j < N_BRANCH:
                win_ref[:, j * d:(j + 1) * d] = stage[j % 2].astype(BF16)
            else:
                wout_ref[...] = stage[j % 2].astype(BF16)

    x = x_ref[0]
    gmix = gmix_ref[...]
    h = _rms(x, gmix).astype(BF16)

    def proj(j, n=1):
        return jnp.dot(h, win_ref[:, j * d:(j + n) * d], preferred_element_type=F32)

    v_raw = proj(1)
    xh = jnp.concatenate([xp_ref[0], xn_ref[0]], axis=0)
    h_ext = jnp.concatenate([h, _rms(xh, gmix).astype(BF16)], axis=0)
    cx = jnp.dot(h_ext, win_ref[:, 3 * d:5 * d], preferred_element_type=F32)

    gv = _gelu_tanh(v_raw)
    for g in range(N_GROUPS):
        cs = slice(g * gw, (g + 1) * gw)
        blk = gv[:, cs]
        mu = jnp.mean(blk, axis=-1, keepdims=True)
        dv = blk - mu
        var = jnp.mean(dv * dv, axis=-1, keepdims=True)
        vg_ref[:, cs] = (dv * lax.rsqrt(var + EPS) * gsgu_ref[:, cs]).astype(BF16)
    for g in range(N_GROUPS):
        cs = slice(g * gw, (g + 1) * gw)
        vcat = jnp.concatenate([vg_ref[n * CHUNK:(n + 1) * CHUNK, cs] for n in range(nc)], axis=1)
        zg = jnp.dot(ws_ref[g].astype(BF16), vcat, preferred_element_type=F32)
        for n in range(nc):
            z_ref[n * CHUNK:(n + 1) * CHUNK, cs] = zg[:, n * gw:(n + 1) * gw]
    u = _gelu_tanh(proj(0))
    ga = _sigmoid(proj(5))
    for n in range(nc):
        rs = slice(n * CHUNK, (n + 1) * CHUNK)
        acc_ref[rs, :] = ga[rs] * (u[rs] * (z_ref[rs, :] + bias_ref[...]))

    z2 = cx[:ts, :d] * cx[:ts, d:]
    z2h = cx[ts:, :d] * cx[ts:, d:]
    prev = jnp.where(s > 0, z2h[SUBLANES - 1:SUBLANES, :], 0.0)
    nxt = jnp.where(s < ns - 1, z2h[SUBLANES:SUBLANES + 1, :], 0.0)
    row = lax.broadcasted_iota(I32, (ts, d), 0)
    zm1 = jnp.where(row == 0, prev, pltpu.roll(z2, 1, 0))
    zp1 = jnp.where(row == ts - 1, nxt, pltpu.roll(z2, ts - 1, 0))
    conv = wconv_ref[0:1, :] * zm1 + wconv_ref[1:2, :] * z2 + wconv_ref[2:3, :] * zp1
    cb = proj(2)
    gb = _sigmoid(proj(6))
    merged = acc_ref[...] + gb * (cb * conv)

    x1 = x + jnp.dot(merged.astype(BF16), wout_ref[...], preferred_element_type=F32)
    x1_ref[0] = x1

    h2 = _rms(x1, gffn_ref[...]).astype(BF16)
    lt_ref[...] = jnp.dot(h2, wr_ref[...].astype(BF16), preferred_element_type=F32)
    _rows_to_tiles(h2t_ref, _pack_bf16_pairs(h2), ts)


def _mixer(x, g_mix, w_in, w_s, bias_full, g_sgu, w_conv, w_out, g_ffn, wr_t):
    b, s, d = x.shape
    ts = MIX_ROWS
    ns = s // ts
    t = b * s
    ntp = d // (2 * LANES)
    hb = ts // SUBLANES
    last_hb = s // SUBLANES - 1

    const = lambda *shape: pl.BlockSpec(shape, lambda bi, si: (0,) * len(shape))
    in_specs = [
        pl.BlockSpec((1, ts, d), lambda bi, si: (bi, si, 0)),
        pl.BlockSpec((1, SUBLANES, d), lambda bi, si: (bi, jnp.maximum(si * hb - 1, 0), 0)),
        pl.BlockSpec((1, SUBLANES, d), lambda bi, si: (bi, jnp.minimum((si + 1) * hb, last_hb), 0)),
        const(1, d),
        pl.BlockSpec(memory_space=pl.ANY),
        const(N_GROUPS, CHUNK, CHUNK),
        const(CHUNK, d),
        const(1, d),
        const(CONV_K, d),
        pl.BlockSpec(memory_space=pl.ANY),
        const(1, d),
        const(d, LANES),
    ]
    out_specs = [
        pl.BlockSpec((1, ts, d), lambda bi, si: (bi, si, 0)),
        pl.BlockSpec((ts * ntp, LANES), lambda bi, si: (bi * ns + si, 0)),
        pl.BlockSpec((ts, LANES), lambda bi, si: (bi * ns + si, 0)),
    ]
    out_shape = [
        jax.ShapeDtypeStruct((b, s, d), F32),
        jax.ShapeDtypeStruct((t * ntp, LANES), U32),
        jax.ShapeDtypeStruct((t, LANES), F32),
    ]
    return pl.pallas_call(
        _mixer_kernel,
        grid=(b, ns),
        in_specs=in_specs,
        out_specs=out_specs,
        out_shape=out_shape,
        scratch_shapes=[pltpu.VMEM((ts, d), BF16), pltpu.VMEM((ts, d), F32), pltpu.VMEM((ts, d), F32),
                        pltpu.VMEM((d, N_BRANCH * d), BF16), pltpu.VMEM((d, d), BF16),
                        pltpu.VMEM((2, d, d), F32), pltpu.SemaphoreType.DMA((2,))],
        compiler_params=pltpu.CompilerParams(
            dimension_semantics=("arbitrary", "arbitrary"), vmem_limit_bytes=VMEM_LIMIT),
        name="mixer",
    )(x, x, x, g_mix, w_in, w_s, bias_full, g_sgu, w_conv, w_out, g_ffn, wr_t)


def _route_kernel(logits_ref, info_ref, gt_ref, cnt_ref, carry_ref):
    tb = logits_ref.shape[0]
    lt = logits_ref[...].T
    ne = N_GROUPS * EXPERTS_PER_GROUP

    @pl.when(pl.program_id(0) == 0)
    def _():
        carry_ref[...] = jnp.zeros_like(carry_ref)

    row8 = lax.broadcasted_iota(I32, (SUBLANES, tb), 0)
    gl = lt[0:N_GROUPS, :]
    gmax = jnp.max(gl, axis=0, keepdims=True)
    gidx = jnp.min(jnp.where(gl == gmax, row8, N_GROUPS), axis=0, keepdims=True)
    pg = 1.0 / jnp.sum(jnp.exp(gl - gmax), axis=0, keepdims=True)

    sel = jnp.zeros((EXPERTS_PER_GROUP, tb), F32)
    for g in range(N_GROUPS):
        lo = N_GROUPS + g * EXPERTS_PER_GROUP
        sel = jnp.where(gidx == g, lt[lo:lo + EXPERTS_PER_GROUP, :], sel)
    m1 = jnp.max(sel, axis=0, keepdims=True)
    i1 = jnp.min(jnp.where(sel == m1, row8, EXPERTS_PER_GROUP), axis=0, keepdims=True)
    sel2 = jnp.where(row8 == i1, -jnp.inf, sel)
    m2 = jnp.max(sel2, axis=0, keepdims=True)
    i2 = jnp.min(jnp.where(sel2 == m2, row8, EXPERTS_PER_GROUP), axis=0, keepdims=True)
    e2 = jnp.exp(m2 - m1)
    den = 1.0 + e2
    gate0 = pg * (1.0 / den)
    gate1 = pg * (e2 / den)
    eid0 = gidx * EXPERTS_PER_GROUP + i1
    eid1 = gidx * EXPERTS_PER_GROUP + i2

    rowe = lax.broadcasted_iota(I32, (ne, tb), 0)
    hit0 = rowe == eid0
    hit1 = rowe == eid1
    onehot = jnp.where(hit0 | hit1, 1.0, 0.0)
    sub = ROUTE_CHUNK
    before = (lax.broadcasted_iota(I32, (sub, sub), 0) < lax.broadcasted_iota(I32, (sub, sub), 1))
    before = jnp.where(before, 1.0, 0.0).astype(BF16)
    carry = carry_ref[:, 0:1]
    parts = []
    for c in range(tb // sub):
        part = onehot[:, c * sub:(c + 1) * sub]
        parts.append(jnp.dot(part.astype(BF16), before, preferred_element_type=F32) + carry)
        carry = carry + jnp.sum(part, axis=1, keepdims=True)
    base = jnp.concatenate(parts, axis=1)
    rank0 = jnp.sum(jnp.where(hit0, base, 0.0), axis=0, keepdims=True).astype(I32)
    rank1 = jnp.sum(jnp.where(hit1, base, 0.0), axis=0, keepdims=True).astype(I32)
    carry_ref[...] = jnp.broadcast_to(carry, carry_ref.shape)
    cnt_ref[...] = carry_ref[...].astype(I32)

    info_ref[...] = jnp.where(row8 == 0, eid0, jnp.where(row8 == 1, eid1,
                              jnp.where(row8 == 2, rank0, jnp.where(row8 == 3, rank1, 0))))
    rowl = lax.broadcasted_iota(I32, (LANES, tb), 0)
    gates = jnp.where(rowl == 0, gate0, jnp.where(rowl == 1, gate1, 0.0))
    gt_ref[...] = gates.T


def _route(logits):
    t = logits.shape[0]
    tb = ROUTE_TOKENS
    ne = N_GROUPS * EXPERTS_PER_GROUP
    return pl.pallas_call(
        _route_kernel,
        grid=(t // tb,),
        in_specs=[pl.BlockSpec((tb, LANES), lambda i: (i, 0))],
        out_specs=[
            pl.BlockSpec((SUBLANES, tb), lambda i: (0, i)),
            pl.BlockSpec((tb, LANES), lambda i: (i, 0)),
            pl.BlockSpec((ne, LANES), lambda i: (0, 0)),
        ],
        out_shape=[
            jax.ShapeDtypeStruct((SUBLANES, t), I32),
            jax.ShapeDtypeStruct((t, LANES), F32),
            jax.ShapeDtypeStruct((ne, LANES), I32),
        ],
        scratch_shapes=[pltpu.VMEM((ne, LANES), F32)],
        compiler_params=pltpu.CompilerParams(dimension_semantics=("arbitrary",)),
        name="route",
    )(logits)


BLK_EXPERT, BLK_NEXT, BLK_USED, BLK_NEXT2, BLK_SEQ = range(5)
EXP_START, EXP_COUNT, EXP_PADDED = range(3)


class _TableRow:
    def __init__(self, ref, offset):
        self.ref, self.offset = ref, offset

    def __getitem__(self, j):
        return self.ref[self.offset + j]


def _plan_kernel(cnt_ref, exp_ref, blk_ref):
    ne = cnt_ref.shape[0]
    nbp = blk_ref.shape[1]
    shift = EXPERT_ROWS.bit_length() - 1
    pad_rows = lambda c: lax.shift_left(lax.shift_right_logical(c + (EXPERT_ROWS - 1), shift), shift)
    cnt = cnt_ref[...]
    padded = pad_rows(cnt)
    padded_lanes = pad_rows(cnt.astype(F32).T[:ne, :ne].astype(I32))
    e_sub = lax.broadcasted_iota(I32, (ne, ne), 0)
    e_lane = lax.broadcasted_iota(I32, (ne, ne), 1)
    pend = jnp.sum(jnp.where(e_lane <= e_sub, padded_lanes, 0), axis=1, keepdims=True)
    pstart = pend - padded[:, 0:1]

    lane = lax.broadcasted_iota(I32, (ne, LANES), 1)
    exp_ref[...] = jnp.where(lane == EXP_START, pstart, jnp.where(
        lane == EXP_COUNT, cnt, jnp.where(lane == EXP_PADDED, padded, 0)))

    first_row = lax.broadcasted_iota(I32, (ne, nbp), 1) * EXPERT_ROWS
    e_col = lax.broadcasted_iota(I32, (ne, nbp), 0)
    blk_e = jnp.minimum(jnp.sum(jnp.where(pend <= first_row, 1, 0), axis=0, keepdims=True), ne - 1)
    has_rows = padded[:, 0:1] > 0
    nxt_e = jnp.min(jnp.where((e_col > blk_e) & has_rows, e_col, ne), axis=0, keepdims=True)
    nxt2_e = jnp.min(jnp.where((e_col > nxt_e) & has_rows, e_col, ne), axis=0, keepdims=True)
    nxt_e = jnp.where(nxt_e == ne, -1, nxt_e)
    nxt2_e = jnp.where(nxt2_e == ne, -1, nxt2_e)
    n_used = lax.shift_right_logical(jnp.max(pend, axis=0, keepdims=True), shift)
    blk_seq = jnp.sum(jnp.where((e_col < blk_e) & has_rows, 1, 0), axis=0, keepdims=True)
    row8 = lax.broadcasted_iota(I32, (SUBLANES, nbp), 0)
    blk_ref[...] = jnp.where(row8 == BLK_EXPERT, blk_e, jnp.where(row8 == BLK_NEXT, nxt_e, jnp.where(
        row8 == BLK_USED, n_used, jnp.where(row8 == BLK_NEXT2, nxt2_e, jnp.where(
            row8 == BLK_SEQ, blk_seq, 0)))))


def _plan(counts, nb):
    ne = counts.shape[0]
    nbp = -(-nb // LANES) * LANES
    return pl.pallas_call(
        _plan_kernel,
        out_shape=[jax.ShapeDtypeStruct((ne, LANES), I32), jax.ShapeDtypeStruct((SUBLANES, nbp), I32)],
        name="plan",
    )(counts)


def _place_kernel(info_ref, exp_ref, *dest_refs):
    tb = info_ref.shape[1]
    ne = exp_ref.shape[0]
    rowe = lax.broadcasted_iota(I32, (ne, tb), 0)
    ps = exp_ref[:, EXP_START:EXP_START + 1]
    dest = []
    for k in range(TOP_K):
        start = jnp.sum(jnp.where(rowe == info_ref[k:k + 1, :], ps, 0), axis=0, keepdims=True)
        dest.append(start + info_ref[TOP_K + k:TOP_K + k + 1, :])
    for ref in dest_refs:
        n = ref.shape[2] // TOP_K
        for q in range(tb // n):
            ref[q] = jnp.concatenate([dk[:, q * n:(q + 1) * n] for dk in dest], axis=1)


def _place(info, exp_tbl, tile_tokens):
    t = info.shape[1]
    tb = PLACE_TOKENS
    ne = exp_tbl.shape[0]
    assert all(tb % n == 0 for n in tile_tokens)
    outs = pl.pallas_call(
        _place_kernel,
        grid=(t // tb,),
        in_specs=[pl.BlockSpec((SUBLANES, tb), lambda i: (0, i)),
                  pl.BlockSpec((ne, LANES), lambda i: (0, 0))],
        out_specs=[pl.BlockSpec((tb // n, 1, TOP_K * n), lambda i: (i, 0, 0)) for n in tile_tokens],
        out_shape=[jax.ShapeDtypeStruct((t // n, 1, TOP_K * n), I32) for n in tile_tokens],
        compiler_params=pltpu.CompilerParams(dimension_semantics=("arbitrary",)),
        name="place",
    )(info, exp_tbl)
    return [o.reshape(-1) for o in outs]


def _row_copy(src_ref, src_row, dst_ref, dst_row, nt, sem):
    first = lambda row: row * nt if isinstance(row, int) else pl.multiple_of(row * nt, nt)
    return pltpu.make_async_copy(
        src_ref.at[pl.ds(first(src_row), nt), :], dst_ref.at[pl.ds(first(dst_row), nt), :], sem)


def _rows_wait(ref, n_rows, nt, sem):
    pltpu.make_async_copy(ref.at[pl.ds(0, n_rows * nt), :], ref.at[pl.ds(0, n_rows * nt), :], sem).wait()


def _for_each_assignment(dest_ref, n_tok, start_copy, inline=False):
    def group(g, c):
        t0 = g * MOVE_UNROLL
        rows = [[dest_ref[k * n_tok + t0 + u] for k in range(TOP_K)] for u in range(MOVE_UNROLL)]
        for u in range(MOVE_UNROLL):
            for k in range(TOP_K):
                start_copy(k, t0 + u, rows[u][k])
        return c

    if inline:
        for g in range(n_tok // MOVE_UNROLL):
            group(g, 0)
    else:
        lax.fori_loop(0, n_tok // MOVE_UNROLL, group, 0)


def _dispatch_kernel(nt, nb, exp_ref, blk_ref, dest_ref, h2_ref, xs_ref, zbuf, sem, zsem):
    td = h2_ref.shape[0] // nt
    ne = exp_ref.shape[0] // LANES
    n_used = blk_ref[BLK_USED * (blk_ref.shape[0] // SUBLANES)]
    blk_rows = zbuf.shape[0]

    def zero_rows(start_not_wait):
        def fire(c):
            c.start() if start_not_wait else c.wait()

        def expert(e, carry):
            cnt = exp_ref[e * LANES + EXP_COUNT]
            row = exp_ref[e * LANES + EXP_START] + cnt
            n = exp_ref[e * LANES + EXP_PADDED] - cnt
            p = EXPERT_ROWS // 2
            while p >= 1:
                has = (n & p) != 0
                r, sz = row, p

                @pl.when(has)
                def _():
                    fire(pltpu.make_async_copy(
                        zbuf.at[pl.ds(0, sz * nt), :],
                        xs_ref.at[pl.ds(pl.multiple_of(r * nt, nt), sz * nt), :], zsem))

                row = row + jnp.where(has, p, 0)
                p //= 2
            return carry

        lax.fori_loop(0, ne, expert, 0)

        def tail(q, carry):
            fire(pltpu.make_async_copy(
                zbuf, xs_ref.at[pl.ds(pl.multiple_of(q * blk_rows, blk_rows), blk_rows), :], zsem))
            return carry

        lax.fori_loop(n_used, nb, tail, 0)

    @pl.when(pl.program_id(0) == 0)
    def _():
        zbuf[...] = jnp.zeros_like(zbuf)
        zero_rows(True)

    _for_each_assignment(
        dest_ref, td,
        lambda k, t, row: _row_copy(h2_ref, t, xs_ref, row, nt, sem).start(priority=k))
    _rows_wait(xs_ref, TOP_K * td, nt, sem)

    @pl.when(pl.program_id(0) == 0)
    def _():
        zero_rows(False)


def _dispatch(exp_flat, blk_flat, dest, h2t, n_rows):
    td = DISPATCH_TOKENS
    t = dest.shape[0] // TOP_K
    nt = h2t.shape[0] // t
    nb = n_rows // EXPERT_ROWS
    grid_spec = pltpu.PrefetchScalarGridSpec(
        num_scalar_prefetch=2,
        grid=(t // td,),
        in_specs=[
            pl.BlockSpec((TOP_K * td,), lambda i, *_: (i,), memory_space=pltpu.SMEM),
            pl.BlockSpec((td * nt, LANES), lambda i, *_: (i, 0)),
        ],
        out_specs=pl.BlockSpec(memory_space=pl.ANY),
        scratch_shapes=[pltpu.VMEM((EXPERT_ROWS * nt, LANES), h2t.dtype),
                        pltpu.SemaphoreType.DMA, pltpu.SemaphoreType.DMA],
    )
    return pl.pallas_call(
        functools.partial(_dispatch_kernel, nt, nb),
        grid_spec=grid_spec,
        out_shape=jax.ShapeDtypeStruct((n_rows * nt, LANES), h2t.dtype),
        compiler_params=pltpu.CompilerParams(dimension_semantics=("arbitrary",)),
        name="dispatch",
    )(exp_flat, blk_flat, dest, h2t)


def _experts_kernel(nb, blk_ref, xs_hbm, wg_hbm, wu_hbm, wd_hbm,
                    y_hbm, xbuf, ybuf, wg_st, wu_st, wd_st, wgu_b, wd_b, hid_s, wsems, isems, osems):
    ring = xbuf.shape[0]
    stages, d, de = wg_st.shape
    bs = EXPERT_ROWS
    ntp = xbuf.shape[1] // bs
    nbp = blk_ref.shape[0] // SUBLANES
    be_ref, nx_ref, nx2_ref, seq_ref = (
        _TableRow(blk_ref, r * nbp) for r in (BLK_EXPERT, BLK_NEXT, BLK_NEXT2, BLK_SEQ))
    nu = blk_ref[BLK_USED * nbp]

    def block(ref, q):
        rows = bs * ntp
        return ref.at[pl.ds(pl.multiple_of(q * rows, rows), rows), :]

    def in_copy(q):
        return pltpu.make_async_copy(block(xs_hbm, q), xbuf.at[q % ring], isems.at[q % ring])

    def out_copy(q):
        return pltpu.make_async_copy(ybuf.at[q % ring], block(y_hbm, q), osems.at[q % ring])

    def weight_copies(ex, slot):
        return (pltpu.make_async_copy(wg_hbm.at[ex], wg_st.at[slot], wsems.at[slot]),
                pltpu.make_async_copy(wu_hbm.at[ex], wu_st.at[slot], wsems.at[slot]),
                pltpu.make_async_copy(wd_hbm.at[ex], wd_st.at[slot], wsems.at[slot]))

    def start_weights(ex, slot):
        for c, prio in zip(weight_copies(ex, slot), WEIGHT_DMA_PRIORITIES):
            c.start(priority=prio)

    xbuf[...] = jnp.zeros_like(xbuf)
    hid_s[...] = jnp.zeros_like(hid_s)
    wgu_b[...] = jnp.zeros_like(wgu_b)
    wd_b[...] = jnp.zeros_like(wd_b)
    start_weights(be_ref[0], 0)

    @pl.when(nx_ref[0] >= 0)
    def _():
        start_weights(nx_ref[0], 1)

    for q in range(EXPERT_LOOKAHEAD):
        @pl.when(q < nu)
        def _():
            in_copy(q).start()

    seq_of = lambda j: seq_ref[jnp.clip(j, 0, nb - 1)]

    def load_expert_if_first(j):
        jc = jnp.clip(j, 0, nb - 1)
        e = be_ref[jc]

        @pl.when((j < nu) & ((j == 0) | (e != be_ref[jnp.maximum(jc - 1, 0)])))
        def _():
            seq = seq_ref[jc]
            slot = lax.rem(seq, stages)
            for c in weight_copies(e, slot):
                c.wait()
            nx2 = nx2_ref[jc]

            @pl.when(nx2 >= 0)
            def _():
                start_weights(nx2, lax.rem(seq + 2, stages))

            half = lax.rem(seq, 2)
            wgu_b[half, :, :de] = wg_st[slot].astype(BF16)
            wgu_b[half, :, de:] = wu_st[slot].astype(BF16)
            wd_b[lax.rem(seq, 3)] = wd_st[slot].astype(BF16)

    def up_proj(j):
        xb = _unpack_bf16_pairs(_tiles_to_rows(xbuf.at[j % ring], bs, ntp), BF16)
        gu = jnp.dot(xb, wgu_b[lax.rem(seq_of(j), 2)], preferred_element_type=F32)
        gate = gu[:, :de]
        return ((gate * _sigmoid(gate)) * gu[:, de:]).astype(BF16)

    def down_proj(hid, j):
        y = jnp.dot(hid, wd_b[lax.rem(seq_of(j), 3)], preferred_element_type=F32)
        _rows_to_tiles(ybuf.at[(j + ring) % ring], _pack_bf16_pairs(y.astype(BF16)), bs)

    def step(m, carry):
        j0 = 2 * m
        for j in (j0, j0 + 1):
            load_expert_if_first(j)
        for j in (j0, j0 + 1):
            @pl.when(j + EXPERT_LOOKAHEAD < nu)
            def _():
                in_copy(j + EXPERT_LOOKAHEAD).start()

            @pl.when(j < nu)
            def _():
                in_copy(j).wait()
        for j in (j0 - 1, j0):
            @pl.when(j >= ring)
            def _():
                out_copy(j - ring).wait()

        hid0 = up_proj(j0)
        down_proj(hid_s[...], j0 - 1)
        hid1 = up_proj(j0 + 1)
        down_proj(hid0, j0)
        hid_s[...] = hid1

        for j in (j0 - 1, j0):
            @pl.when((j >= 0) & (j < nu))
            def _():
                out_copy(j).start()

        return carry

    lax.fori_loop(0, nu // 2 + 1, step, 0)

    last_written = 2 * (nu // 2)
    for r in range(1, ring + 1):
        q = last_written - ring + r

        @pl.when((q >= 0) & (q < nu))
        def _():
            out_copy(q).wait()

    ybuf[0] = jnp.zeros(ybuf.shape[1:], ybuf.dtype)

    def zero_start(q, carry):
        pltpu.make_async_copy(ybuf.at[0], block(y_hbm, q), osems.at[0]).start()
        return carry

    def zero_wait(q, carry):
        pltpu.make_async_copy(ybuf.at[0], block(y_hbm, q), osems.at[0]).wait()
        return carry

    lax.fori_loop(nu, nb, zero_start, 0)
    lax.fori_loop(nu, nb, zero_wait, 0)


def _experts(blk_flat, xs, w_gate, w_up, w_down):
    ne, d, de = w_gate.shape
    in_rows = EXPERT_ROWS * (d // LANES) // 2
    nb = xs.shape[0] // in_rows
    ring = EXPERT_LOOKAHEAD + 2
    any_spec = pl.BlockSpec(memory_space=pl.ANY)
    grid_spec = pltpu.PrefetchScalarGridSpec(
        num_scalar_prefetch=1,
        grid=(1,),
        in_specs=[any_spec, any_spec, any_spec, any_spec],
        out_specs=any_spec,
        scratch_shapes=[pltpu.VMEM((ring, in_rows, LANES), U32), pltpu.VMEM((ring, in_rows, LANES), U32),
                        pltpu.VMEM((WEIGHT_STAGES, d, de), F32), pltpu.VMEM((WEIGHT_STAGES, d, de), F32),
                        pltpu.VMEM((WEIGHT_STAGES, de, d), F32),
                        pltpu.VMEM((2, d, 2 * de), BF16), pltpu.VMEM((3, de, d), BF16),
                        pltpu.VMEM((EXPERT_ROWS, de), BF16),
                        pltpu.SemaphoreType.DMA((WEIGHT_STAGES,)),
                        pltpu.SemaphoreType.DMA((ring,)), pltpu.SemaphoreType.DMA((ring,))],
    )
    return pl.pallas_call(
        functools.partial(_experts_kernel, nb),
        grid_spec=grid_spec,
        out_shape=jax.ShapeDtypeStruct((nb * in_rows, LANES), U32),
        compiler_params=pltpu.CompilerParams(
            dimension_semantics=("arbitrary",), vmem_limit_bytes=VMEM_LIMIT),
        name="experts",
    )(blk_flat, xs, w_gate, w_up, w_down)


def _combine_kernel(d0_ref, d1_ref, d2_ref, x1_ref, gt_ref, y_ref, gfin_ref, o_ref, ybuf, sems):
    tc, d = x1_ref.shape
    nt = ybuf.shape[2] // tc
    i = pl.program_id(0)
    n = pl.num_programs(0)
    slots = ybuf.shape[0]

    def issue_tile(dest_ref, slot, inline):
        _for_each_assignment(
            dest_ref, tc,
            lambda k, t, row: _row_copy(y_ref, row, ybuf.at[slot, k], t, nt,
                                        sems.at[slot]).start(priority=k),
            inline=inline)

    @pl.when(i == 0)
    def _():
        issue_tile(d0_ref, 0, False)
        issue_tile(d1_ref, 1, False)

    slot = i % slots
    _rows_wait(y_ref, TOP_K * tc, nt, sems.at[slot])
    issue_tile(d2_ref, (i + 2) % slots, True)
    y0 = _unpack_bf16_pairs(_tiles_to_rows(ybuf.at[slot, 0], tc, nt), F32)
    y1 = _unpack_bf16_pairs(_tiles_to_rows(ybuf.at[slot, 1], tc, nt), F32)
    gates = gt_ref[...]
    xo = x1_ref[...] + (gates[:, 0:1] * y0 + gates[:, 1:2] * y1)
    o_ref[...] = _rms(xo, gfin_ref[...])

    @pl.when(i == n - 1)
    def _():
        _rows_wait(y_ref, TOP_K * tc, nt, sems.at[(i + 1) % slots])
        _rows_wait(y_ref, TOP_K * tc, nt, sems.at[(i + 2) % slots])


def _combine(dest, x1, gates_t, y, g_final):
    t, d = x1.shape
    nt = d // LANES
    tc = COMBINE_TOKENS
    last = t // tc - 1
    assert last >= 1
    dest_spec = lambda ahead: pl.BlockSpec(
        (TOP_K * tc,), lambda i: (jnp.minimum(i + ahead, last),), memory_space=pltpu.SMEM)
    return pl.pallas_call(
        _combine_kernel,
        grid=(t // tc,),
        in_specs=[
            dest_spec(0), dest_spec(1), dest_spec(2),
            pl.BlockSpec((tc, d), lambda i: (i, 0)),
            pl.BlockSpec((tc, LANES), lambda i: (i, 0)),
            pl.BlockSpec(memory_space=pl.ANY),
            pl.BlockSpec((1, d), lambda i: (0, 0)),
        ],
        out_specs=pl.BlockSpec((tc, d), lambda i: (i, 0)),
        out_shape=jax.ShapeDtypeStruct((t, d), F32),
        scratch_shapes=[pltpu.VMEM((3, TOP_K, tc * nt // 2, LANES), U32),
                        pltpu.SemaphoreType.DMA((3,))],
        compiler_params=pltpu.CompilerParams(dimension_semantics=("arbitrary",)),
        name="combine",
    )(dest, dest, dest, x1, gates_t, y, g_final)


def _layer(x, g_mix, w_in, w_s, b_s, g_sgu, w_conv, w_out, g_ffn, w_rg, w_re, w_gate, w_up, w_down):
    b, s, d = x.shape
    t = b * s
    ne = w_gate.shape[0]
    bs = EXPERT_ROWS

    bias_full = jnp.repeat(b_s.T, d // N_GROUPS, axis=1)
    wr_t = jnp.concatenate([w_rg, w_re], axis=1)
    wr_t = jnp.pad(wr_t, ((0, 0), (0, LANES - wr_t.shape[1])))
    x1, h2t, logits_t = _mixer(
        x, g_mix.reshape(1, d), w_in, w_s, bias_full, g_sgu.reshape(1, d),
        w_conv, w_out, g_ffn.reshape(1, d), wr_t)

    info, gates_t, counts = _route(logits_t)

    n_rows = -(-(t * TOP_K + ne * (bs - 1)) // bs) * bs
    nb = n_rows // bs
    exp_tbl, blk_tbl = _plan(counts, nb)
    blk_flat = blk_tbl.reshape(-1)

    dest_dispatch, dest_combine = _place(info, exp_tbl, (DISPATCH_TOKENS, COMBINE_TOKENS))
    xs = _dispatch(exp_tbl.reshape(-1), blk_flat, dest_dispatch, h2t, n_rows)
    y = _experts(blk_flat, xs, w_gate, w_up, w_down)
    return x1.reshape(t, d), dest_combine, gates_t, y


def kernel(x, g_mix, w_in, w_s, b_s, g_sgu, w_conv, w_out, g_ffn, w_router_group, w_router_expert,
           w_gate, w_up, w_down, g_final):
    b, s, d = x.shape
    depth = g_mix.shape[0]
    assert depth == 1, "the final RMSNorm is fused into the last layer's combine"
    assert s % MIX_ROWS == 0 and MIX_ROWS % CHUNK == 0 and d % LANES == 0
    assert EXPERT_ROWS & (EXPERT_ROWS - 1) == 0, "block bookkeeping uses shifts"
    assert all((b * s) % n == 0 for n in (ROUTE_TOKENS, PLACE_TOKENS, DISPATCH_TOKENS, COMBINE_TOKENS))
    l = 0
    x1, dest, gates_t, y = _layer(
        x, g_mix[l], w_in[l], w_s[l], b_s[l], g_sgu[l], w_conv[l], w_out[l], g_ffn[l],
        w_router_group[l], w_router_expert[l], w_gate[l], w_up[l], w_down[l])
    out = _combine(dest, x1, gates_t, y, g_final.reshape(1, d))
    return out.reshape(b, s, d)
```

```python
import functools

import jax
import jax.numpy as jnp
from jax import lax
from jax.experimental import pallas as pl
from jax.experimental.pallas import tpu as pltpu

F32 = jnp.float32
BF16 = jnp.bfloat16
I32 = jnp.int32
U32 = jnp.uint32

EPS = 1e-6
LANES = 128
SUBLANES = 8
CHUNK = 128
N_GROUPS = 8
EXPERTS_PER_GROUP = 8
TOP_K = 2
CONV_K = 3
N_BRANCH = 7

MIX_ROWS = 512
ROUTE_TOKENS = 2048
ROUTE_CHUNK = 512
PLACE_TOKENS = 2048
DISPATCH_TOKENS = 2048
COMBINE_TOKENS = 512
MOVE_UNROLL = 8
WEIGHT_STAGES = 3
WEIGHT_DMA_PRIORITIES = (1, 1, 0)
EXPERT_ROWS = 256
EXPERT_LOOKAHEAD = 3
VMEM_LIMIT = 56 * 1024 * 1024


def _rms(x, g):
    return x * lax.rsqrt(jnp.mean(x * x, axis=-1, keepdims=True) + EPS) * g


def _sigmoid(x):
    return 0.5 * (1.0 + jnp.tanh(0.5 * x))


def _gelu_tanh(x):
    c = 0.7978845608028654
    return x * (0.5 * (1.0 + jnp.tanh(c * (x + 0.044715 * (x * x * x)))))


def _rows_to_tiles(dst_ref, val, rows):
    nt = val.shape[1] // LANES
    for c in range(nt):
        dst_ref[pl.ds(c, rows, stride=nt), :] = val[:, c * LANES:(c + 1) * LANES]


def _tiles_to_rows(src_ref, rows, nt):
    return jnp.concatenate([src_ref[pl.ds(c, rows, stride=nt), :] for c in range(nt)], axis=1)


def _pack_bf16_pairs(x):
    half = x.shape[1] // 2
    bits = pltpu.bitcast(x.astype(F32), U32)
    return lax.shift_right_logical(bits[:, :half], U32(16)) | bits[:, half:]


def _unpack_bf16_pairs(w, dtype):
    lo = pltpu.bitcast(lax.shift_left(w, U32(16)), F32)
    hi = pltpu.bitcast(w & U32(0xFFFF0000), F32)
    return jnp.concatenate([lo, hi], axis=1).astype(dtype)


def _mixer_kernel(x_ref, xp_ref, xn_ref, gmix_ref, win_hbm, ws_ref, bias_ref, gsgu_ref, wconv_ref,
                  wout_hbm, gffn_ref, wr_ref, x1_ref, h2t_ref, lt_ref,
                  vg_ref, z_ref, acc_ref, win_ref, wout_ref, stage, wsems):
    ts, d = x_ref.shape[1], x_ref.shape[2]
    s = pl.program_id(1)
    ns = pl.num_programs(1)
    gw = d // N_GROUPS
    nc = ts // CHUNK

    @pl.when((pl.program_id(0) == 0) & (s == 0))
    def _():
        def slab_copy(j):
            src = win_hbm.at[:, pl.ds(j * d, d)] if j < N_BRANCH else wout_hbm
            return pltpu.make_async_copy(src, stage.at[j % 2], wsems.at[j % 2])

        slab_copy(0).start()
        for j in range(N_BRANCH + 1):
            if j < N_BRANCH:
                slab_copy(j + 1).start()
            slab_copy(j).wait()
            if j < N_BRANCH:
                win_ref[:, j * d:(j + 1) * d] = stage[j % 2].astype(BF16)
            else:
                wout_ref[...] = stage[j % 2].astype(BF16)

    x = x_ref[0]
    gmix = gmix_ref[...]
    h = _rms(x, gmix).astype(BF16)

    def proj(j, n=1):
        return jnp.dot(h, win_ref[:, j * d:(j + n) * d], preferred_element_type=F32)

    v_raw = proj(1)
    xh = jnp.concatenate([xp_ref[0], xn_ref[0]], axis=0)
    h_ext = jnp.concatenate([h, _rms(xh, gmix).astype(BF16)], axis=0)
    cx = jnp.dot(h_ext, win_ref[:, 3 * d:5 * d], preferred_element_type=F32)

    gv = _gelu_tanh(v_raw)
    for g in range(N_GROUPS):
        cs = slice(g * gw, (g + 1) * gw)
        blk = gv[:, cs]
        mu = jnp.mean(blk, axis=-1, keepdims=True)
        dv = blk - mu
        var = jnp.mean(dv * dv, axis=-1, keepdims=True)
        vg_ref[:, cs] = (dv * lax.rsqrt(var + EPS) * gsgu_ref[:, cs]).astype(BF16)
    for g in range(N_GROUPS):
        cs = slice(g * gw, (g + 1) * gw)
        vcat = jnp.concatenate([vg_ref[n * CHUNK:(n + 1) * CHUNK, cs] for n in range(nc)], axis=1)
        zg = jnp.dot(ws_ref[g].astype(BF16), vcat, preferred_element_type=F32)
        for n in range(nc):
            z_ref[n * CHUNK:(n + 1) * CHUNK, cs] = zg[:, n * gw:(n + 1) * gw]
    u = _gelu_tanh(proj(0))
    ga = _sigmoid(proj(5))
    for n in range(nc):
        rs = slice(n * CHUNK, (n + 1) * CHUNK)
        acc_ref[rs, :] = ga[rs] * (u[rs] * (z_ref[rs, :] + bias_ref[...]))

    z2 = cx[:ts, :d] * cx[:ts, d:]
    z2h = cx[ts:, :d] * cx[ts:, d:]
    prev = jnp.where(s > 0, z2h[SUBLANES - 1:SUBLANES, :], 0.0)
    nxt = jnp.where(s < ns - 1, z2h[SUBLANES:SUBLANES + 1, :], 0.0)
    row = lax.broadcasted_iota(I32, (ts, d), 0)
    zm1 = jnp.where(row == 0, prev, pltpu.roll(z2, 1, 0))
    zp1 = jnp.where(row == ts - 1, nxt, pltpu.roll(z2, ts - 1, 0))
    conv = wconv_ref[0:1, :] * zm1 + wconv_ref[1:2, :] * z2 + wconv_ref[2:3, :] * zp1
    cb = proj(2)
    gb = _sigmoid(proj(6))
    merged = acc_ref[...] + gb * (cb * conv)

    x1 = x + jnp.dot(merged.astype(BF16), wout_ref[...], preferred_element_type=F32)
    x1_ref[0] = x1

    h2 = _rms(x1, gffn_ref[...]).astype(BF16)
    lt_ref[...] = jnp.dot(h2, wr_ref[...].astype(BF16), preferred_element_type=F32)
    _rows_to_tiles(h2t_ref, _pack_bf16_pairs(h2), ts)


def _mixer(x, g_mix, w_in, w_s, bias_full, g_sgu, w_conv, w_out, g_ffn, wr_t):
    b, s, d = x.shape
    ts = MIX_ROWS
    ns = s // ts
    t = b * s
    ntp = d // (2 * LANES)
    hb = ts // SUBLANES
    last_hb = s // SUBLANES - 1

    const = lambda *shape: pl.BlockSpec(shape, lambda bi, si: (0,) * len(shape))
    in_specs = [
        pl.BlockSpec((1, ts, d), lambda bi, si: (bi, si, 0)),
        pl.BlockSpec((1, SUBLANES, d), lambda bi, si: (bi, jnp.maximum(si * hb - 1, 0), 0)),
        pl.BlockSpec((1, SUBLANES, d), lambda bi, si: (bi, jnp.minimum((si + 1) * hb, last_hb), 0)),
        const(1, d),
        pl.BlockSpec(memory_space=pl.ANY),
        const(N_GROUPS, CHUNK, CHUNK),
        const(CHUNK, d),
        const(1, d),
        const(CONV_K, d),
        pl.BlockSpec(memory_space=pl.ANY),
        const(1, d),
        const(d, LANES),
    ]
    out_specs = [
        pl.BlockSpec((1, ts, d), lambda bi, si: (bi, si, 0)),
        pl.BlockSpec((ts * ntp, LANES), lambda bi, si: (bi * ns + si, 0)),
        pl.BlockSpec((ts, LANES), lambda bi, si: (bi * ns + si, 0)),
    ]
    out_shape = [
        jax.ShapeDtypeStruct((b, s, d), F32),
        jax.ShapeDtypeStruct((t * ntp, LANES), U32),
        jax.ShapeDtypeStruct((t, LANES), F32),
    ]
    return pl.pallas_call(
        _mixer_kernel,
        grid=(b, ns),
        in_specs=in_specs,
        out_specs=out_specs,
        out_shape=out_shape,
        scratch_shapes=[pltpu.VMEM((ts, d), BF16), pltpu.VMEM((ts, d), F32), pltpu.VMEM((ts, d), F32),
                        pltpu.VMEM((d, N_BRANCH * d), BF16), pltpu.VMEM((d, d), BF16),
                        pltpu.VMEM((2, d, d), F32), pltpu.SemaphoreType.DMA((2,))],
        compiler_params=pltpu.CompilerParams(
            dimension_semantics=("arbitrary", "arbitrary"), vmem_limit_bytes=VMEM_LIMIT),
        name="mixer",
    )(x, x, x, g_mix, w_in, w_s, bias_full, g_sgu, w_conv, w_out, g_ffn, wr_t)


def _route_kernel(logits_ref, info_ref, gt_ref, cnt_ref, carry_ref):
    tb = logits_ref.shape[0]
    lt = logits_ref[...].T
    ne = N_GROUPS * EXPERTS_PER_GROUP

    @pl.when(pl.program_id(0) == 0)
    def _():
        carry_ref[...] = jnp.zeros_like(carry_ref)

    row8 = lax.broadcasted_iota(I32, (SUBLANES, tb), 0)
    gl = lt[0:N_GROUPS, :]
    gmax = jnp.max(gl, axis=0, keepdims=True)
    gidx = jnp.min(jnp.where(gl == gmax, row8, N_GROUPS), axis=0, keepdims=True)
    pg = 1.0 / jnp.sum(jnp.exp(gl - gmax), axis=0, keepdims=True)

    sel = jnp.zeros((EXPERTS_PER_GROUP, tb), F32)
    for g in range(N_GROUPS):
        lo = N_GROUPS + g * EXPERTS_PER_GROUP
        sel = jnp.where(gidx == g, lt[lo:lo + EXPERTS_PER_GROUP, :], sel)
    m1 = jnp.max(sel, axis=0, keepdims=True)
    i1 = jnp.min(jnp.where(sel == m1, row8, EXPERTS_PER_GROUP), axis=0, keepdims=True)
    sel2 = jnp.where(row8 == i1, -jnp.inf, sel)
    m2 = jnp.max(sel2, axis=0, keepdims=True)
    i2 = jnp.min(jnp.where(sel2 == m2, row8, EXPERTS_PER_GROUP), axis=0, keepdims=True)
    e2 = jnp.exp(m2 - m1)
    den = 1.0 + e2
    gate0 = pg * (1.0 / den)
    gate1 = pg * (e2 / den)
    eid0 = gidx * EXPERTS_PER_GROUP + i1
    eid1 = gidx * EXPERTS_PER_GROUP + i2

    rowe = lax.broadcasted_iota(I32, (ne, tb), 0)
    hit0 = rowe == eid0
    hit1 = rowe == eid1
    onehot = jnp.where(hit0 | hit1, 1.0, 0.0)
    sub = ROUTE_CHUNK
    before = (lax.broadcasted_iota(I32, (sub, sub), 0) < lax.broadcasted_iota(I32, (sub, sub), 1))
    before = jnp.where(before, 1.0, 0.0).astype(BF16)
    carry = carry_ref[:, 0:1]
    parts = []
    for c in range(tb // sub):
        part = onehot[:, c * sub:(c + 1) * sub]
        parts.append(jnp.dot(part.astype(BF16), before, preferred_element_type=F32) + carry)
        carry = carry + jnp.sum(part, axis=1, keepdims=True)
    base = jnp.concatenate(parts, axis=1)
    rank0 = jnp.sum(jnp.where(hit0, base, 0.0), axis=0, keepdims=True).astype(I32)
    rank1 = jnp.sum(jnp.where(hit1, base, 0.0), axis=0, keepdims=True).astype(I32)
    carry_ref[...] = jnp.broadcast_to(carry, carry_ref.shape)
    cnt_ref[...] = carry_ref[...].astype(I32)

    info_ref[...] = jnp.where(row8 == 0, eid0, jnp.where(row8 == 1, eid1,
                              jnp.where(row8 == 2, rank0, jnp.where(row8 == 3, rank1, 0))))
    rowl = lax.broadcasted_iota(I32, (LANES, tb), 0)
    gates = jnp.where(rowl == 0, gate0, jnp.where(rowl == 1, gate1, 0.0))
    gt_ref[...] = gates.T


def _route(logits):
    t = logits.shape[0]
    tb = ROUTE_TOKENS
    ne = N_GROUPS * EXPERTS_PER_GROUP
    return pl.pallas_call(
        _route_kernel,
        grid=(t // tb,),
        in_specs=[pl.BlockSpec((tb, LANES), lambda i: (i, 0))],
        out_specs=[
            pl.BlockSpec((SUBLANES, tb), lambda i: (0, i)),
            pl.BlockSpec((tb, LANES), lambda i: (i, 0)),
            pl.BlockSpec((ne, LANES), lambda i: (0, 0)),
        ],
        out_shape=[
            jax.ShapeDtypeStruct((SUBLANES, t), I32),
            jax.ShapeDtypeStruct((t, LANES), F32),
            jax.ShapeDtypeStruct((ne, LANES), I32),
        ],
        scratch_shapes=[pltpu.VMEM((ne, LANES), F32)],
        compiler_params=pltpu.CompilerParams(dimension_semantics=("arbitrary",)),
        name="route",
    )(logits)


BLK_EXPERT, BLK_NEXT, BLK_USED, BLK_NEXT2, BLK_SEQ = range(5)
EXP_START, EXP_COUNT, EXP_PADDED = range(3)


class _TableRow:
    def __init__(self, ref, offset):
        self.ref, self.offset = ref, offset

    def __getitem__(self, j):
        return self.ref[self.offset + j]


def _plan_kernel(cnt_ref, exp_ref, blk_ref):
    ne = cnt_ref.shape[0]
    nbp = blk_ref.shape[1]
    shift = EXPERT_ROWS.bit_length() - 1
    pad_rows = lambda c: lax.shift_left(lax.shift_right_logical(c + (EXPERT_ROWS - 1), shift), shift)
    cnt = cnt_ref[...]
    padded = pad_rows(cnt)
    padded_lanes = pad_rows(cnt.astype(F32).T[:ne, :ne].astype(I32))
    e_sub = lax.broadcasted_iota(I32, (ne, ne), 0)
    e_lane = lax.broadcasted_iota(I32, (ne, ne), 1)
    pend = jnp.sum(jnp.where(e_lane <= e_sub, padded_lanes, 0), axis=1, keepdims=True)
    pstart = pend - padded[:, 0:1]

    lane = lax.broadcasted_iota(I32, (ne, LANES), 1)
    exp_ref[...] = jnp.where(lane == EXP_START, pstart, jnp.where(
        lane == EXP_COUNT, cnt, jnp.where(lane == EXP_PADDED, padded, 0)))

    first_row = lax.broadcasted_iota(I32, (ne, nbp), 1) * EXPERT_ROWS
    e_col = lax.broadcasted_iota(I32, (ne, nbp), 0)
    blk_e = jnp.minimum(jnp.sum(jnp.where(pend <= first_row, 1, 0), axis=0, keepdims=True), ne - 1)
    has_rows = padded[:, 0:1] > 0
    nxt_e = jnp.min(jnp.where((e_col > blk_e) & has_rows, e_col, ne), axis=0, keepdims=True)
    nxt2_e = jnp.min(jnp.where((e_col > nxt_e) & has_rows, e_col, ne), axis=0, keepdims=True)
    nxt_e = jnp.where(nxt_e == ne, -1, nxt_e)
    nxt2_e = jnp.where(nxt2_e == ne, -1, nxt2_e)
    n_used = lax.shift_right_logical(jnp.max(pend, axis=0, keepdims=True), shift)
    blk_seq = jnp.sum(jnp.where((e_col < blk_e) & has_rows, 1, 0), axis=0, keepdims=True)
    row8 = lax.broadcasted_iota(I32, (SUBLANES, nbp), 0)
    blk_ref[...] = jnp.where(row8 == BLK_EXPERT, blk_e, jnp.where(row8 == BLK_NEXT, nxt_e, jnp.where(
        row8 == BLK_USED, n_used, jnp.where(row8 == BLK_NEXT2, nxt2_e, jnp.where(
            row8 == BLK_SEQ, blk_seq, 0)))))


def _plan(counts, nb):
    ne = counts.shape[0]
    nbp = -(-nb // LANES) * LANES
    return pl.pallas_call(
        _plan_kernel,
        out_shape=[jax.ShapeDtypeStruct((ne, LANES), I32), jax.ShapeDtypeStruct((SUBLANES, nbp), I32)],
        name="plan",
    )(counts)


def _place_kernel(info_ref, exp_ref, *dest_refs):
    tb = info_ref.shape[1]
    ne = exp_ref.shape[0]
    rowe = lax.broadcasted_iota(I32, (ne, tb), 0)
    ps = exp_ref[:, EXP_START:EXP_START + 1]
    dest = []
    for k in range(TOP_K):
        start = jnp.sum(jnp.where(rowe == info_ref[k:k + 1, :], ps, 0), axis=0, keepdims=True)
        dest.append(start + info_ref[TOP_K + k:TOP_K + k + 1, :])
    for ref in dest_refs:
        n = ref.shape[2] // TOP_K
        for q in range(tb // n):
            ref[q] = jnp.concatenate([dk[:, q * n:(q + 1) * n] for dk in dest], axis=1)


def _place(info, exp_tbl, tile_tokens):
    t = info.shape[1]
    tb = PLACE_TOKENS
    ne = exp_tbl.shape[0]
    assert all(tb % n == 0 for n in tile_tokens)
    outs = pl.pallas_call(
        _place_kernel,
        grid=(t // tb,),
        in_specs=[pl.BlockSpec((SUBLANES, tb), lambda i: (0, i)),
                  pl.BlockSpec((ne, LANES), lambda i: (0, 0))],
        out_specs=[pl.BlockSpec((tb // n, 1, TOP_K * n), lambda i: (i, 0, 0)) for n in tile_tokens],
        out_shape=[jax.ShapeDtypeStruct((t // n, 1, TOP_K * n), I32) for n in tile_tokens],
        compiler_params=pltpu.CompilerParams(dimension_semantics=("arbitrary",)),
        name="place",
    )(info, exp_tbl)
    return [o.reshape(-1) for o in outs]


def _row_copy(src_ref, src_row, dst_ref, dst_row, nt, sem):
    first = lambda row: row * nt if isinstance(row, int) else pl.multiple_of(row * nt, nt)
    return pltpu.make_async_copy(
        src_ref.at[pl.ds(first(src_row), nt), :], dst_ref.at[pl.ds(first(dst_row), nt), :], sem)


def _rows_wait(ref, n_rows, nt, sem):
    pltpu.make_async_copy(ref.at[pl.ds(0, n_rows * nt), :], ref.at[pl.ds(0, n_rows * nt), :], sem).wait()


def _for_each_assignment(dest_ref, n_tok, start_copy, inline=False):
    def group(g, c):
        t0 = g * MOVE_UNROLL
        rows = [[dest_ref[k * n_tok + t0 + u] for k in range(TOP_K)] for u in range(MOVE_UNROLL)]
        for u in range(MOVE_UNROLL):
            for k in range(TOP_K):
                start_copy(k, t0 + u, rows[u][k])
        return c

    if inline:
        for g in range(n_tok // MOVE_UNROLL):
            group(g, 0)
    else:
        lax.fori_loop(0, n_tok // MOVE_UNROLL, group, 0)


def _dispatch_kernel(nt, nb, exp_ref, blk_ref, dest_ref, h2_ref, xs_ref, zbuf, sem, zsem):
    td = h2_ref.shape[0] // nt
    ne = exp_ref.shape[0] // LANES
    n_used = blk_ref[BLK_USED * (blk_ref.shape[0] // SUBLANES)]
    blk_rows = zbuf.shape[0]

    def zero_rows(start_not_wait):
        def fire(c):
            c.start() if start_not_wait else c.wait()

        def expert(e, carry):
            cnt = exp_ref[e * LANES + EXP_COUNT]
            row = exp_ref[e * LANES + EXP_START] + cnt
            n = exp_ref[e * LANES + EXP_PADDED] - cnt
            p = EXPERT_ROWS // 2
            while p >= 1:
                has = (n & p) != 0
                r, sz = row, p

                @pl.when(has)
                def _():
                    fire(pltpu.make_async_copy(
                        zbuf.at[pl.ds(0, sz * nt), :],
                        xs_ref.at[pl.ds(pl.multiple_of(r * nt, nt), sz * nt), :], zsem))

                row = row + jnp.where(has, p, 0)
                p //= 2
            return carry

        lax.fori_loop(0, ne, expert, 0)

        def tail(q, carry):
            fire(pltpu.make_async_copy(
                zbuf, xs_ref.at[pl.ds(pl.multiple_of(q * blk_rows, blk_rows), blk_rows), :], zsem))
            return carry

        lax.fori_loop(n_used, nb, tail, 0)

    @pl.when(pl.program_id(0) == 0)
    def _():
        zbuf[...] = jnp.zeros_like(zbuf)
        zero_rows(True)

    _for_each_assignment(
        dest_ref, td,
        lambda k, t, row: _row_copy(h2_ref, t, xs_ref, row, nt, sem).start(priority=k))
    _rows_wait(xs_ref, TOP_K * td, nt, sem)

    @pl.when(pl.program_id(0) == 0)
    def _():
        zero_rows(False)


def _dispatch(exp_flat, blk_flat, dest, h2t, n_rows):
    td = DISPATCH_TOKENS
    t = dest.shape[0] // TOP_K
    nt = h2t.shape[0] // t
    nb = n_rows // EXPERT_ROWS
    grid_spec = pltpu.PrefetchScalarGridSpec(
        num_scalar_prefetch=2,
        grid=(t // td,),
        in_specs=[
            pl.BlockSpec((TOP_K * td,), lambda i, *_: (i,), memory_space=pltpu.SMEM),
            pl.BlockSpec((td * nt, LANES), lambda i, *_: (i, 0)),
        ],
        out_specs=pl.BlockSpec(memory_space=pl.ANY),
        scratch_shapes=[pltpu.VMEM((EXPERT_ROWS * nt, LANES), h2t.dtype),
                        pltpu.SemaphoreType.DMA, pltpu.SemaphoreType.DMA],
    )
    return pl.pallas_call(
        functools.partial(_dispatch_kernel, nt, nb),
        grid_spec=grid_spec,
        out_shape=jax.ShapeDtypeStruct((n_rows * nt, LANES), h2t.dtype),
        compiler_params=pltpu.CompilerParams(dimension_semantics=("arbitrary",)),
        name="dispatch",
    )(exp_flat, blk_flat, dest, h2t)


def _experts_kernel(nb, blk_ref, xs_hbm, wg_hbm, wu_hbm, wd_hbm,
                    y_hbm, xbuf, ybuf, zbuf, wg_st, wu_st, wd_st, wgu_b, wd_b, hid_s,
                    wsems, isems, osems, zsem):
    ring = xbuf.shape[0]
    stages, d, de = wg_st.shape
    bs = EXPERT_ROWS
    ntp = xbuf.shape[1] // bs
    nbp = blk_ref.shape[0] // SUBLANES
    be_ref, nx_ref, nx2_ref, seq_ref = (
        _TableRow(blk_ref, r * nbp) for r in (BLK_EXPERT, BLK_NEXT, BLK_NEXT2, BLK_SEQ))
    nu = blk_ref[BLK_USED * nbp]

    def block(ref, q):
        rows = bs * ntp
        return ref.at[pl.ds(pl.multiple_of(q * rows, rows), rows), :]

    def in_copy(q):
        return pltpu.make_async_copy(block(xs_hbm, q), xbuf.at[q % ring], isems.at[q % ring])

    def out_copy(q):
        return pltpu.make_async_copy(ybuf.at[q % ring], block(y_hbm, q), osems.at[q % ring])

    def zero_copy(q):
        return pltpu.make_async_copy(zbuf, block(y_hbm, q), zsem)

    def weight_copies(ex, slot):
        return (pltpu.make_async_copy(wg_hbm.at[ex], wg_st.at[slot], wsems.at[slot]),
                pltpu.make_async_copy(wu_hbm.at[ex], wu_st.at[slot], wsems.at[slot]),
                pltpu.make_async_copy(wd_hbm.at[ex], wd_st.at[slot], wsems.at[slot]))

    def start_weights(ex, slot):
        for c, prio in zip(weight_copies(ex, slot), WEIGHT_DMA_PRIORITIES):
            c.start(priority=prio)

    xbuf[...] = jnp.zeros_like(xbuf)
    zbuf[...] = jnp.zeros_like(zbuf)
    hid_s[...] = jnp.zeros_like(hid_s)
    wgu_b[...] = jnp.zeros_like(wgu_b)
    wd_b[...] = jnp.zeros_like(wd_b)
    start_weights(be_ref[0], 0)

    @pl.when(nx_ref[0] >= 0)
    def _():
        start_weights(nx_ref[0], 1)

    for q in range(EXPERT_LOOKAHEAD):
        @pl.when(q < nu)
        def _():
            in_copy(q).start()

    seq_of = lambda j: seq_ref[jnp.clip(j, 0, nb - 1)]

    def load_expert_if_first(j):
        jc = jnp.clip(j, 0, nb - 1)
        e = be_ref[jc]

        @pl.when((j < nu) & ((j == 0) | (e != be_ref[jnp.maximum(jc - 1, 0)])))
        def _():
            seq = seq_ref[jc]
            slot = lax.rem(seq, stages)
            for c in weight_copies(e, slot):
                c.wait()
            nx2 = nx2_ref[jc]

            @pl.when(nx2 >= 0)
            def _():
                start_weights(nx2, lax.rem(seq + 2, stages))

            half = lax.rem(seq, 2)
            wgu_b[half, :, :de] = wg_st[slot].astype(BF16)
            wgu_b[half, :, de:] = wu_st[slot].astype(BF16)
            wd_b[lax.rem(seq, 3)] = wd_st[slot].astype(BF16)

    def up_proj(j):
        xb = _unpack_bf16_pairs(_tiles_to_rows(xbuf.at[j % ring], bs, ntp), BF16)
        gu = jnp.dot(xb, wgu_b[lax.rem(seq_of(j), 2)], preferred_element_type=F32)
        gate = gu[:, :de]
        return ((gate * _sigmoid(gate)) * gu[:, de:]).astype(BF16)

    def down_proj(hid, j):
        y = jnp.dot(hid, wd_b[lax.rem(seq_of(j), 3)], preferred_element_type=F32)
        _rows_to_tiles(ybuf.at[(j + ring) % ring], _pack_bf16_pairs(y.astype(BF16)), bs)

    def step(m, carry):
        j0 = 2 * m
        for j in (j0, j0 + 1):
            load_expert_if_first(j)
        for j in (j0, j0 + 1):
            @pl.when(j + EXPERT_LOOKAHEAD < nu)
            def _():
                in_copy(j + EXPERT_LOOKAHEAD).start()

            @pl.when(j < nu)
            def _():
                in_copy(j).wait()
        for j in (j0 - 1, j0):
            @pl.when(j >= ring)
            def _():
                out_copy(j - ring).wait()

        @pl.when(nu + m < nb)
        def _():
            zero_copy(nu + m).start()

        hid0 = up_proj(j0)
        down_proj(hid_s[...], j0 - 1)
        hid1 = up_proj(j0 + 1)
        down_proj(hid0, j0)
        hid_s[...] = hid1

        for j in (j0 - 1, j0):
            @pl.when((j >= 0) & (j < nu))
            def _():
                out_copy(j).start()

        return carry

    n_steps = nu // 2 + 1
    lax.fori_loop(0, n_steps, step, 0)

    last_written = 2 * (nu // 2)
    for r in range(1, ring + 1):
        q = last_written - ring + r

        @pl.when((q >= 0) & (q < nu))
        def _():
            out_copy(q).wait()

    def zero_start(q, carry):
        zero_copy(q).start()
        return carry

    def zero_wait(q, carry):
        zero_copy(q).wait()
        return carry

    lax.fori_loop(nu + n_steps, nb, zero_start, 0)
    lax.fori_loop(nu, nb, zero_wait, 0)


def _experts(blk_flat, xs, w_gate, w_up, w_down):
    ne, d, de = w_gate.shape
    in_rows = EXPERT_ROWS * (d // LANES) // 2
    nb = xs.shape[0] // in_rows
    ring = EXPERT_LOOKAHEAD + 2
    any_spec = pl.BlockSpec(memory_space=pl.ANY)
    grid_spec = pltpu.PrefetchScalarGridSpec(
        num_scalar_prefetch=1,
        grid=(1,),
        in_specs=[any_spec, any_spec, any_spec, any_spec],
        out_specs=any_spec,
        scratch_shapes=[pltpu.VMEM((ring, in_rows, LANES), U32), pltpu.VMEM((ring, in_rows, LANES), U32),
                        pltpu.VMEM((in_rows, LANES), U32),
                        pltpu.VMEM((WEIGHT_STAGES, d, de), F32), pltpu.VMEM((WEIGHT_STAGES, d, de), F32),
                        pltpu.VMEM((WEIGHT_STAGES, de, d), F32),
                        pltpu.VMEM((2, d, 2 * de), BF16), pltpu.VMEM((3, de, d), BF16),
                        pltpu.VMEM((EXPERT_ROWS, de), BF16),
                        pltpu.SemaphoreType.DMA((WEIGHT_STAGES,)),
                        pltpu.SemaphoreType.DMA((ring,)), pltpu.SemaphoreType.DMA((ring,)),
                        pltpu.SemaphoreType.DMA],
    )
    return pl.pallas_call(
        functools.partial(_experts_kernel, nb),
        grid_spec=grid_spec,
        out_shape=jax.ShapeDtypeStruct((nb * in_rows, LANES), U32),
        compiler_params=pltpu.CompilerParams(
            dimension_semantics=("arbitrary",), vmem_limit_bytes=VMEM_LIMIT),
        name="experts",
    )(blk_flat, xs, w_gate, w_up, w_down)


def _combine_kernel(d0_ref, d1_ref, d2_ref, x1_ref, gt_ref, y_ref, gfin_ref, o_ref, ybuf, sems):
    tc, d = x1_ref.shape
    nt = ybuf.shape[2] // tc
    i = pl.program_id(0)
    n = pl.num_programs(0)
    slots = ybuf.shape[0]

    def issue_tile(dest_ref, slot, inline):
        _for_each_assignment(
            dest_ref, tc,
            lambda k, t, row: _row_copy(y_ref, row, ybuf.at[slot, k], t, nt,
                                        sems.at[slot]).start(priority=k),
            inline=inline)

    @pl.when(i == 0)
    def _():
        issue_tile(d0_ref, 0, False)
        issue_tile(d1_ref, 1, False)

    slot = i % slots
    _rows_wait(y_ref, TOP_K * tc, nt, sems.at[slot])
    issue_tile(d2_ref, (i + 2) % slots, True)
    y0 = _unpack_bf16_pairs(_tiles_to_rows(ybuf.at[slot, 0], tc, nt), F32)
    y1 = _unpack_bf16_pairs(_tiles_to_rows(ybuf.at[slot, 1], tc, nt), F32)
    gates = gt_ref[...]
    xo = x1_ref[...] + (gates[:, 0:1] * y0 + gates[:, 1:2] * y1)
    o_ref[...] = _rms(xo, gfin_ref[...])

    @pl.when(i == n - 1)
    def _():
        _rows_wait(y_ref, TOP_K * tc, nt, sems.at[(i + 1) % slots])
        _rows_wait(y_ref, TOP_K * tc, nt, sems.at[(i + 2) % slots])


def _combine(dest, x1, gates_t, y, g_final):
    t, d = x1.shape
    nt = d // LANES
    tc = COMBINE_TOKENS
    last = t // tc - 1
    assert last >= 1
    dest_spec = lambda ahead: pl.BlockSpec(
        (TOP_K * tc,), lambda i: (jnp.minimum(i + ahead, last),), memory_space=pltpu.SMEM)
    return pl.pallas_call(
        _combine_kernel,
        grid=(t // tc,),
        in_specs=[
            dest_spec(0), dest_spec(1), dest_spec(2),
            pl.BlockSpec((tc, d), lambda i: (i, 0)),
            pl.BlockSpec((tc, LANES), lambda i: (i, 0)),
            pl.BlockSpec(memory_space=pl.ANY),
            pl.BlockSpec((1, d), lambda i: (0, 0)),
        ],
        out_specs=pl.BlockSpec((tc, d), lambda i: (i, 0)),
        out_shape=jax.ShapeDtypeStruct((t, d), F32),
        scratch_shapes=[pltpu.VMEM((3, TOP_K, tc * nt // 2, LANES), U32),
                        pltpu.SemaphoreType.DMA((3,))],
        compiler_params=pltpu.CompilerParams(dimension_semantics=("arbitrary",)),
        name="combine",
    )(dest, dest, dest, x1, gates_t, y, g_final)


def _layer(x, g_mix, w_in, w_s, b_s, g_sgu, w_conv, w_out, g_ffn, w_rg, w_re, w_gate, w_up, w_down):
    b, s, d = x.shape
    t = b * s
    ne = w_gate.shape[0]
    bs = EXPERT_ROWS

    bias_full = jnp.repeat(b_s.T, d // N_GROUPS, axis=1)
    wr_t = jnp.concatenate([w_rg, w_re], axis=1)
    wr_t = jnp.pad(wr_t, ((0, 0), (0, LANES - wr_t.shape[1])))
    x1, h2t, logits_t = _mixer(
        x, g_mix.reshape(1, d), w_in, w_s, bias_full, g_sgu.reshape(1, d),
        w_conv, w_out, g_ffn.reshape(1, d), wr_t)

    info, gates_t, counts = _route(logits_t)

    n_rows = -(-(t * TOP_K + ne * (bs - 1)) // bs) * bs
    nb = n_rows // bs
    exp_tbl, blk_tbl = _plan(counts, nb)
    blk_flat = blk_tbl.reshape(-1)

    dest_dispatch, dest_combine = _place(info, exp_tbl, (DISPATCH_TOKENS, COMBINE_TOKENS))
    xs = _dispatch(exp_tbl.reshape(-1), blk_flat, dest_dispatch, h2t, n_rows)
    y = _experts(blk_flat, xs, w_gate, w_up, w_down)
    return x1.reshape(t, d), dest_combine, gates_t, y


def kernel(x, g_mix, w_in, w_s, b_s, g_sgu, w_conv, w_out, g_ffn, w_router_group, w_router_expert,
           w_gate, w_up, w_down, g_final):
    b, s, d = x.shape
    depth = g_mix.shape[0]
    assert depth == 1, "the final RMSNorm is fused into the last layer's combine"
    assert s % MIX_ROWS == 0 and MIX_ROWS % CHUNK == 0 and d % LANES == 0
    assert EXPERT_ROWS & (EXPERT_ROWS - 1) == 0, "block bookkeeping uses shifts"
    assert all((b * s) % n == 0 for n in (ROUTE_TOKENS, PLACE_TOKENS, DISPATCH_TOKENS, COMBINE_TOKENS))
    l = 0
    x1, dest, gates_t, y = _layer(
        x, g_mix[l], w_in[l], w_s[l], b_s[l], g_sgu[l], w_conv[l], w_out[l], g_ffn[l],
        w_router_group[l], w_router_expert[l], w_gate[l], w_up[l], w_down[l])
    out = _combine(dest, x1, gates_t, y, g_final.reshape(1, d))
    return out.reshape(b, s, d)
```

```python
import functools

import jax
import jax.numpy as jnp
from jax import lax
from jax.experimental import pallas as pl
from jax.experimental.pallas import tpu as pltpu

F32 = jnp.float32
BF16 = jnp.bfloat16
I32 = jnp.int32
U32 = jnp.uint32

EPS = 1e-6
LANES = 128
SUBLANES = 8
CHUNK = 128
N_GROUPS = 8
EXPERTS_PER_GROUP = 8
TOP_K = 2
CONV_K = 3
N_BRANCH = 7

MIX_ROWS = 512
ROUTE_TOKENS = 2048
ROUTE_CHUNK = 512
PLACE_TOKENS = 2048
DISPATCH_TOKENS = 2048
COMBINE_TOKENS = 512
MOVE_UNROLL = 8
WEIGHT_STAGES = 3
WEIGHT_DMA_PRIORITIES = (1, 1, 0)
EXPERT_ROWS = 256
EXPERT_LOOKAHEAD = 3
VMEM_LIMIT = 56 * 1024 * 1024


def _rms(x, g):
    return x * lax.rsqrt(jnp.mean(x * x, axis=-1, keepdims=True) + EPS) * g


def _sigmoid(x):
    return 0.5 * (1.0 + jnp.tanh(0.5 * x))


def _gelu_tanh(x):
    c = 0.7978845608028654
    return x * (0.5 * (1.0 + jnp.tanh(c * (x + 0.044715 * (x * x * x)))))


def _rows_to_tiles(dst_ref, val, rows):
    nt = val.shape[1] // LANES
    for c in range(nt):
        dst_ref[pl.ds(c, rows, stride=nt), :] = val[:, c * LANES:(c + 1) * LANES]


def _tiles_to_rows(src_ref, rows, nt):
    return jnp.concatenate([src_ref[pl.ds(c, rows, stride=nt), :] for c in range(nt)], axis=1)


def _pack_bf16_pairs(x):
    half = x.shape[1] // 2
    bits = pltpu.bitcast(x.astype(F32), U32)
    return lax.shift_right_logical(bits[:, :half], U32(16)) | bits[:, half:]


def _unpack_bf16_pairs(w, dtype):
    lo = pltpu.bitcast(lax.shift_left(w, U32(16)), F32)
    hi = pltpu.bitcast(w & U32(0xFFFF0000), F32)
    return jnp.concatenate([lo, hi], axis=1).astype(dtype)


def _mixer_kernel(x_ref, xp_ref, xn_ref, gmix_ref, win_hbm, ws_ref, bs_ref, gsgu_ref, wconv_ref,
                  wout_hbm, gffn_ref, wrg_ref, wre_ref, x1_ref, h2t_ref, lt_ref,
                  vg_ref, z_ref, acc_ref, win_ref, wout_ref, bias_ref, wr_ref, stage, wsems):
    ts, d = x_ref.shape[1], x_ref.shape[2]
    s = pl.program_id(1)
    ns = pl.num_programs(1)
    gw = d // N_GROUPS
    nc = ts // CHUNK
    nl = d // LANES

    @pl.when((pl.program_id(0) == 0) & (s == 0))
    def _():
        b_sq = jnp.concatenate([bs_ref[...], jnp.zeros((CHUNK - N_GROUPS, CHUNK), F32)], axis=0)
        b_t = b_sq.T
        for g in range(N_GROUPS):
            bias_ref[:, g * gw:(g + 1) * gw] = jnp.broadcast_to(b_t[:, g:g + 1], (CHUNK, gw))
        n_logits = wrg_ref.shape[0] + wre_ref.shape[0]
        wr_rows = jnp.concatenate(
            [wrg_ref[...], wre_ref[...], jnp.zeros((LANES - n_logits, d), F32)], axis=0)
        wr_ref[...] = wr_rows.T.astype(BF16)

        def slab_copy(j):
            src = win_hbm.at[:, pl.ds(j * d, d)] if j < N_BRANCH else wout_hbm
            return pltpu.make_async_copy(src, stage.at[j % 2], wsems.at[j % 2])

        slab_copy(0).start()
        for j in range(N_BRANCH + 1):
            if j < N_BRANCH:
                slab_copy(j + 1).start()
            slab_copy(j).wait()
            if j < N_BRANCH:
                win_ref[:, j * d:(j + 1) * d] = stage[j % 2].astype(BF16)
            else:
                wout_ref[...] = stage[j % 2].astype(BF16)

    x = x_ref[0]
    gmix = gmix_ref[...]
    h = _rms(x, gmix).astype(BF16)

    def proj(j, n=1):
        return jnp.dot(h, win_ref[:, j * d:(j + n) * d], preferred_element_type=F32)

    v_raw = proj(1)
    xh = jnp.concatenate([xp_ref[0], xn_ref[0]], axis=0)
    h_ext = jnp.concatenate([h, _rms(xh, gmix).astype(BF16)], axis=0)
    cx = jnp.dot(h_ext, win_ref[:, 3 * d:5 * d], preferred_element_type=F32)

    gv = _gelu_tanh(v_raw)
    for g in range(N_GROUPS):
        cs = slice(g * gw, (g + 1) * gw)
        blk = gv[:, cs]
        mu = jnp.mean(blk, axis=-1, keepdims=True)
        dv = blk - mu
        var = jnp.mean(dv * dv, axis=-1, keepdims=True)
        vg_ref[:, cs] = (dv * lax.rsqrt(var + EPS) * gsgu_ref[:, cs]).astype(BF16)
    for g in range(N_GROUPS):
        cs = slice(g * gw, (g + 1) * gw)
        vcat = jnp.concatenate([vg_ref[n * CHUNK:(n + 1) * CHUNK, cs] for n in range(nc)], axis=1)
        zg = jnp.dot(ws_ref[g].astype(BF16), vcat, preferred_element_type=F32)
        for n in range(nc):
            z_ref[n * CHUNK:(n + 1) * CHUNK, cs] = zg[:, n * gw:(n + 1) * gw]
    u = _gelu_tanh(proj(0))
    ga = _sigmoid(proj(5))
    for n in range(nc):
        rs = slice(n * CHUNK, (n + 1) * CHUNK)
        acc_ref[rs, :] = ga[rs] * (u[rs] * (z_ref[rs, :] + bias_ref[...]))

    z2 = cx[:ts, :d] * cx[:ts, d:]
    z2h = cx[ts:, :d] * cx[ts:, d:]
    prev = jnp.where(s > 0, z2h[SUBLANES - 1:SUBLANES, :], 0.0)
    nxt = jnp.where(s < ns - 1, z2h[SUBLANES:SUBLANES + 1, :], 0.0)
    row = lax.broadcasted_iota(I32, (ts, d), 0)
    zm1 = jnp.where(row == 0, prev, pltpu.roll(z2, 1, 0))
    zp1 = jnp.where(row == ts - 1, nxt, pltpu.roll(z2, ts - 1, 0))
    tap = lambda k: jnp.concatenate(
        [wconv_ref[k * nl + c:k * nl + c + 1, :] for c in range(nl)], axis=1)
    conv = tap(0) * zm1 + tap(1) * z2 + tap(2) * zp1
    cb = proj(2)
    gb = _sigmoid(proj(6))
    merged = acc_ref[...] + gb * (cb * conv)

    x1 = x + jnp.dot(merged.astype(BF16), wout_ref[...], preferred_element_type=F32)
    x1_ref[0] = x1

    h2 = _rms(x1, gffn_ref[...]).astype(BF16)
    lt_ref[...] = jnp.dot(h2, wr_ref[...], preferred_element_type=F32)
    _rows_to_tiles(h2t_ref, _pack_bf16_pairs(h2), ts)


def _mixer(x, g_mix, w_in, w_s, b_s, g_sgu, w_conv, w_out, g_ffn, wrg_t, wre_t):
    b, s, d = x.shape
    assert b_s.shape == (N_GROUPS, CHUNK) and N_GROUPS + wre_t.shape[0] <= LANES
    ts = MIX_ROWS
    ns = s // ts
    t = b * s
    ntp = d // (2 * LANES)
    hb = ts // SUBLANES
    last_hb = s // SUBLANES - 1

    const = lambda *shape: pl.BlockSpec(shape, lambda bi, si: (0,) * len(shape))
    in_specs = [
        pl.BlockSpec((1, ts, d), lambda bi, si: (bi, si, 0)),
        pl.BlockSpec((1, SUBLANES, d), lambda bi, si: (bi, jnp.maximum(si * hb - 1, 0), 0)),
        pl.BlockSpec((1, SUBLANES, d), lambda bi, si: (bi, jnp.minimum((si + 1) * hb, last_hb), 0)),
        const(1, d),
        pl.BlockSpec(memory_space=pl.ANY),
        const(N_GROUPS, CHUNK, CHUNK),
        const(N_GROUPS, CHUNK),
        const(1, d),
        const(*w_conv.shape),
        pl.BlockSpec(memory_space=pl.ANY),
        const(1, d),
        const(*wrg_t.shape),
        const(*wre_t.shape),
    ]
    out_specs = [
        pl.BlockSpec((1, ts, d), lambda bi, si: (bi, si, 0)),
        pl.BlockSpec((ts * ntp, LANES), lambda bi, si: (bi * ns + si, 0)),
        pl.BlockSpec((ts, LANES), lambda bi, si: (bi * ns + si, 0)),
    ]
    out_shape = [
        jax.ShapeDtypeStruct((b, s, d), F32),
        jax.ShapeDtypeStruct((t * ntp, LANES), U32),
        jax.ShapeDtypeStruct((t, LANES), F32),
    ]
    return pl.pallas_call(
        _mixer_kernel,
        grid=(b, ns),
        in_specs=in_specs,
        out_specs=out_specs,
        out_shape=out_shape,
        scratch_shapes=[pltpu.VMEM((ts, d), BF16), pltpu.VMEM((ts, d), F32), pltpu.VMEM((ts, d), F32),
                        pltpu.VMEM((d, N_BRANCH * d), BF16), pltpu.VMEM((d, d), BF16),
                        pltpu.VMEM((CHUNK, d), F32), pltpu.VMEM((d, LANES), BF16),
                        pltpu.VMEM((2, d, d), F32), pltpu.SemaphoreType.DMA((2,))],
        compiler_params=pltpu.CompilerParams(
            dimension_semantics=("arbitrary", "arbitrary"), vmem_limit_bytes=VMEM_LIMIT),
        name="mixer",
    )(x, x, x, g_mix, w_in, w_s, b_s, g_sgu, w_conv, w_out, g_ffn, wrg_t, wre_t)


def _route_kernel(logits_ref, info_ref, gt_ref, cnt_ref, carry_ref):
    tb = logits_ref.shape[0]
    lt = logits_ref[...].T
    ne = N_GROUPS * EXPERTS_PER_GROUP

    @pl.when(pl.program_id(0) == 0)
    def _():
        carry_ref[...] = jnp.zeros_like(carry_ref)

    row8 = lax.broadcasted_iota(I32, (SUBLANES, tb), 0)
    gl = lt[0:N_GROUPS, :]
    gmax = jnp.max(gl, axis=0, keepdims=True)
    gidx = jnp.min(jnp.where(gl == gmax, row8, N_GROUPS), axis=0, keepdims=True)
    pg = 1.0 / jnp.sum(jnp.exp(gl - gmax), axis=0, keepdims=True)

    sel = jnp.zeros((EXPERTS_PER_GROUP, tb), F32)
    for g in range(N_GROUPS):
        lo = N_GROUPS + g * EXPERTS_PER_GROUP
        sel = jnp.where(gidx == g, lt[lo:lo + EXPERTS_PER_GROUP, :], sel)
    m1 = jnp.max(sel, axis=0, keepdims=True)
    i1 = jnp.min(jnp.where(sel == m1, row8, EXPERTS_PER_GROUP), axis=0, keepdims=True)
    sel2 = jnp.where(row8 == i1, -jnp.inf, sel)
    m2 = jnp.max(sel2, axis=0, keepdims=True)
    i2 = jnp.min(jnp.where(sel2 == m2, row8, EXPERTS_PER_GROUP), axis=0, keepdims=True)
    e2 = jnp.exp(m2 - m1)
    den = 1.0 + e2
    gate0 = pg * (1.0 / den)
    gate1 = pg * (e2 / den)
    eid0 = gidx * EXPERTS_PER_GROUP + i1
    eid1 = gidx * EXPERTS_PER_GROUP + i2

    rowe = lax.broadcasted_iota(I32, (ne, tb), 0)
    hit0 = rowe == eid0
    hit1 = rowe == eid1
    onehot = jnp.where(hit0 | hit1, 1.0, 0.0)
    sub = ROUTE_CHUNK
    before = (lax.broadcasted_iota(I32, (sub, sub), 0) < lax.broadcasted_iota(I32, (sub, sub), 1))
    before = jnp.where(before, 1.0, 0.0).astype(BF16)
    carry = carry_ref[:, 0:1]
    parts = []
    for c in range(tb // sub):
        part = onehot[:, c * sub:(c + 1) * sub]
        parts.append(jnp.dot(part.astype(BF16), before, preferred_element_type=F32) + carry)
        carry = carry + jnp.sum(part, axis=1, keepdims=True)
    base = jnp.concatenate(parts, axis=1)
    rank0 = jnp.sum(jnp.where(hit0, base, 0.0), axis=0, keepdims=True).astype(I32)
    rank1 = jnp.sum(jnp.where(hit1, base, 0.0), axis=0, keepdims=True).astype(I32)
    carry_ref[...] = jnp.broadcast_to(carry, carry_ref.shape)
    cnt_ref[...] = carry_ref[...].astype(I32)

    info_ref[...] = jnp.where(row8 == 0, eid0, jnp.where(row8 == 1, eid1,
                              jnp.where(row8 == 2, rank0, jnp.where(row8 == 3, rank1, 0))))
    rowl = lax.broadcasted_iota(I32, (LANES, tb), 0)
    gates = jnp.where(rowl == 0, gate0, jnp.where(rowl == 1, gate1, 0.0))
    gt_ref[...] = gates.T


def _route(logits):
    t = logits.shape[0]
    tb = ROUTE_TOKENS
    ne = N_GROUPS * EXPERTS_PER_GROUP
    return pl.pallas_call(
        _route_kernel,
        grid=(t // tb,),
        in_specs=[pl.BlockSpec((tb, LANES), lambda i: (i, 0))],
        out_specs=[
            pl.BlockSpec((SUBLANES, tb), lambda i: (0, i)),
            pl.BlockSpec((tb, LANES), lambda i: (i, 0)),
            pl.BlockSpec((ne, LANES), lambda i: (0, 0)),
        ],
        out_shape=[
            jax.ShapeDtypeStruct((SUBLANES, t), I32),
            jax.ShapeDtypeStruct((t, LANES), F32),
            jax.ShapeDtypeStruct((ne, LANES), I32),
        ],
        scratch_shapes=[pltpu.VMEM((ne, LANES), F32)],
        compiler_params=pltpu.CompilerParams(dimension_semantics=("arbitrary",)),
        name="route",
    )(logits)


BLK_EXPERT, BLK_NEXT, BLK_USED, BLK_NEXT2, BLK_SEQ = range(5)
EXP_START, EXP_COUNT, EXP_PADDED = range(3)


class _TableRow:
    def __init__(self, ref, row):
        self.ref, self.row = ref, row

    def __getitem__(self, j):
        return self.ref[self.row, j]


def _plan_kernel(cnt_ref, exp_ref, blk_ref):
    ne = cnt_ref.shape[0]
    nbp = blk_ref.shape[1]
    shift = EXPERT_ROWS.bit_length() - 1
    pad_rows = lambda c: lax.shift_left(lax.shift_right_logical(c + (EXPERT_ROWS - 1), shift), shift)
    cnt = cnt_ref[...]
    padded = pad_rows(cnt)
    padded_lanes = pad_rows(cnt.astype(F32).T[:ne, :ne].astype(I32))
    e_sub = lax.broadcasted_iota(I32, (ne, ne), 0)
    e_lane = lax.broadcasted_iota(I32, (ne, ne), 1)
    pend = jnp.sum(jnp.where(e_lane <= e_sub, padded_lanes, 0), axis=1, keepdims=True)
    pstart = pend - padded[:, 0:1]

    lane = lax.broadcasted_iota(I32, (ne, LANES), 1)
    exp_ref[...] = jnp.where(lane == EXP_START, pstart, jnp.where(
        lane == EXP_COUNT, cnt, jnp.where(lane == EXP_PADDED, padded, 0)))

    first_row = lax.broadcasted_iota(I32, (ne, nbp), 1) * EXPERT_ROWS
    e_col = lax.broadcasted_iota(I32, (ne, nbp), 0)
    blk_e = jnp.minimum(jnp.sum(jnp.where(pend <= first_row, 1, 0), axis=0, keepdims=True), ne - 1)
    has_rows = padded[:, 0:1] > 0
    nxt_e = jnp.min(jnp.where((e_col > blk_e) & has_rows, e_col, ne), axis=0, keepdims=True)
    nxt2_e = jnp.min(jnp.where((e_col > nxt_e) & has_rows, e_col, ne), axis=0, keepdims=True)
    nxt_e = jnp.where(nxt_e == ne, -1, nxt_e)
    nxt2_e = jnp.where(nxt2_e == ne, -1, nxt2_e)
    n_used = lax.shift_right_logical(jnp.max(pend, axis=0, keepdims=True), shift)
    blk_seq = jnp.sum(jnp.where((e_col < blk_e) & has_rows, 1, 0), axis=0, keepdims=True)
    row8 = lax.broadcasted_iota(I32, (SUBLANES, nbp), 0)
    blk_ref[...] = jnp.where(row8 == BLK_EXPERT, blk_e, jnp.where(row8 == BLK_NEXT, nxt_e, jnp.where(
        row8 == BLK_USED, n_used, jnp.where(row8 == BLK_NEXT2, nxt2_e, jnp.where(
            row8 == BLK_SEQ, blk_seq, 0)))))


def _plan(counts, nb):
    ne = counts.shape[0]
    nbp = -(-nb // LANES) * LANES
    return pl.pallas_call(
        _plan_kernel,
        out_shape=[jax.ShapeDtypeStruct((ne, LANES), I32), jax.ShapeDtypeStruct((SUBLANES, nbp), I32)],
        name="plan",
    )(counts)


def _place_kernel(info_ref, exp_ref, *dest_refs):
    tb = info_ref.shape[1]
    ne = exp_ref.shape[0]
    rowe = lax.broadcasted_iota(I32, (ne, tb), 0)
    ps = exp_ref[:, EXP_START:EXP_START + 1]
    dest = []
    for k in range(TOP_K):
        start = jnp.sum(jnp.where(rowe == info_ref[k:k + 1, :], ps, 0), axis=0, keepdims=True)
        dest.append(start + info_ref[TOP_K + k:TOP_K + k + 1, :])
    for ref in dest_refs:
        n = ref.shape[2] // TOP_K
        for q in range(tb // n):
            ref[q] = jnp.concatenate([dk[:, q * n:(q + 1) * n] for dk in dest], axis=1)


def _place(info, exp_tbl, tile_tokens):
    t = info.shape[1]
    tb = PLACE_TOKENS
    ne = exp_tbl.shape[0]
    assert all(tb % n == 0 for n in tile_tokens)
    outs = pl.pallas_call(
        _place_kernel,
        grid=(t // tb,),
        in_specs=[pl.BlockSpec((SUBLANES, tb), lambda i: (0, i)),
                  pl.BlockSpec((ne, LANES), lambda i: (0, 0))],
        out_specs=[pl.BlockSpec((tb // n, 1, TOP_K * n), lambda i: (i, 0, 0)) for n in tile_tokens],
        out_shape=[jax.ShapeDtypeStruct((t // n, 1, TOP_K * n), I32) for n in tile_tokens],
        compiler_params=pltpu.CompilerParams(dimension_semantics=("arbitrary",)),
        name="place",
    )(info, exp_tbl)
    return [o.reshape(-1) for o in outs]


def _row_copy(src_ref, src_row, dst_ref, dst_row, nt, sem):
    first = lambda row: row * nt if isinstance(row, int) else pl.multiple_of(row * nt, nt)
    return pltpu.make_async_copy(
        src_ref.at[pl.ds(first(src_row), nt), :], dst_ref.at[pl.ds(first(dst_row), nt), :], sem)


def _rows_wait(ref, n_rows, nt, sem):
    pltpu.make_async_copy(ref.at[pl.ds(0, n_rows * nt), :], ref.at[pl.ds(0, n_rows * nt), :], sem).wait()


def _for_each_assignment(dest_ref, n_tok, start_copy, inline=False):
    def group(g, c):
        t0 = g * MOVE_UNROLL
        rows = [[dest_ref[k * n_tok + t0 + u] for k in range(TOP_K)] for u in range(MOVE_UNROLL)]
        for u in range(MOVE_UNROLL):
            for k in range(TOP_K):
                start_copy(k, t0 + u, rows[u][k])
        return c

    if inline:
        for g in range(n_tok // MOVE_UNROLL):
            group(g, 0)
    else:
        lax.fori_loop(0, n_tok // MOVE_UNROLL, group, 0)


def _dispatch_kernel(nt, nb, exp_ref, blk_ref, dest_ref, h2_ref, xs_ref, zbuf, sem, zsem):
    td = h2_ref.shape[0] // nt
    ne = exp_ref.shape[0] // LANES
    n_used = blk_ref[BLK_USED, 0]
    blk_rows = zbuf.shape[0]

    def zero_rows(start_not_wait):
        def fire(c):
            c.start() if start_not_wait else c.wait()

        def expert(e, carry):
            cnt = exp_ref[e * LANES + EXP_COUNT]
            row = exp_ref[e * LANES + EXP_START] + cnt
            n = exp_ref[e * LANES + EXP_PADDED] - cnt
            p = EXPERT_ROWS // 2
            while p >= 1:
                has = (n & p) != 0
                r, sz = row, p

                @pl.when(has)
                def _():
                    fire(pltpu.make_async_copy(
                        zbuf.at[pl.ds(0, sz * nt), :],
                        xs_ref.at[pl.ds(pl.multiple_of(r * nt, nt), sz * nt), :], zsem))

                row = row + jnp.where(has, p, 0)
                p //= 2
            return carry

        lax.fori_loop(0, ne, expert, 0)

        def tail(q, carry):
            fire(pltpu.make_async_copy(
                zbuf, xs_ref.at[pl.ds(pl.multiple_of(q * blk_rows, blk_rows), blk_rows), :], zsem))
            return carry

        lax.fori_loop(n_used, nb, tail, 0)

    @pl.when(pl.program_id(0) == 0)
    def _():
        zbuf[...] = jnp.zeros_like(zbuf)
        zero_rows(True)

    _for_each_assignment(
        dest_ref, td,
        lambda k, t, row: _row_copy(h2_ref, t, xs_ref, row, nt, sem).start(priority=k))
    _rows_wait(xs_ref, TOP_K * td, nt, sem)

    @pl.when(pl.program_id(0) == 0)
    def _():
        zero_rows(False)


def _dispatch(exp_flat, blk_tbl, dest, h2t, n_rows):
    td = DISPATCH_TOKENS
    t = dest.shape[0] // TOP_K
    nt = h2t.shape[0] // t
    nb = n_rows // EXPERT_ROWS
    grid_spec = pltpu.PrefetchScalarGridSpec(
        num_scalar_prefetch=2,
        grid=(t // td,),
        in_specs=[
            pl.BlockSpec((TOP_K * td,), lambda i, *_: (i,), memory_space=pltpu.SMEM),
            pl.BlockSpec((td * nt, LANES), lambda i, *_: (i, 0)),
        ],
        out_specs=pl.BlockSpec(memory_space=pl.ANY),
        scratch_shapes=[pltpu.VMEM((EXPERT_ROWS * nt, LANES), h2t.dtype),
                        pltpu.SemaphoreType.DMA, pltpu.SemaphoreType.DMA],
    )
    return pl.pallas_call(
        functools.partial(_dispatch_kernel, nt, nb),
        grid_spec=grid_spec,
        out_shape=jax.ShapeDtypeStruct((n_rows * nt, LANES), h2t.dtype),
        compiler_params=pltpu.CompilerParams(dimension_semantics=("arbitrary",)),
        name="dispatch",
    )(exp_flat, blk_tbl, dest, h2t)


def _experts_kernel(nb, blk_ref, xs_hbm, wg_hbm, wu_hbm, wd_hbm,
                    y_hbm, xbuf, ybuf, zbuf, wg_st, wu_st, wd_st, wgu_b, wd_b, hid_s,
                    wsems, isems, osems, zsem):
    ring = xbuf.shape[0]
    stages, d, de = wg_st.shape
    bs = EXPERT_ROWS
    ntp = xbuf.shape[1] // bs
    be_ref, nx_ref, nx2_ref, seq_ref = (
        _TableRow(blk_ref, r) for r in (BLK_EXPERT, BLK_NEXT, BLK_NEXT2, BLK_SEQ))
    nu = blk_ref[BLK_USED, 0]

    def block(ref, q):
        rows = bs * ntp
        return ref.at[pl.ds(pl.multiple_of(q * rows, rows), rows), :]

    def in_copy(q):
        return pltpu.make_async_copy(block(xs_hbm, q), xbuf.at[q % ring], isems.at[q % ring])

    def out_copy(q):
        return pltpu.make_async_copy(ybuf.at[q % ring], block(y_hbm, q), osems.at[q % ring])

    def zero_copy(q):
        return pltpu.make_async_copy(zbuf, block(y_hbm, q), zsem)

    def weight_copies(ex, slot):
        return (pltpu.make_async_copy(wg_hbm.at[ex], wg_st.at[slot], wsems.at[slot]),
                pltpu.make_async_copy(wu_hbm.at[ex], wu_st.at[slot], wsems.at[slot]),
                pltpu.make_async_copy(wd_hbm.at[ex], wd_st.at[slot], wsems.at[slot]))

    def start_weights(ex, slot):
        for c, prio in zip(weight_copies(ex, slot), WEIGHT_DMA_PRIORITIES):
            c.start(priority=prio)

    xbuf[...] = jnp.zeros_like(xbuf)
    zbuf[...] = jnp.zeros_like(zbuf)
    hid_s[...] = jnp.zeros_like(hid_s)
    wgu_b[...] = jnp.zeros_like(wgu_b)
    wd_b[...] = jnp.zeros_like(wd_b)
    start_weights(be_ref[0], 0)

    @pl.when(nx_ref[0] >= 0)
    def _():
        start_weights(nx_ref[0], 1)

    for q in range(EXPERT_LOOKAHEAD):
        @pl.when(q < nu)
        def _():
            in_copy(q).start()

    seq_of = lambda j: seq_ref[jnp.clip(j, 0, nb - 1)]

    def load_expert_if_first(j):
        jc = jnp.clip(j, 0, nb - 1)
        e = be_ref[jc]

        @pl.when((j < nu) & ((j == 0) | (e != be_ref[jnp.maximum(jc - 1, 0)])))
        def _():
            seq = seq_ref[jc]
            slot = lax.rem(seq, stages)
            for c in weight_copies(e, slot):
                c.wait()
            nx2 = nx2_ref[jc]

            @pl.when(nx2 >= 0)
            def _():
                start_weights(nx2, lax.rem(seq + 2, stages))

            half = lax.rem(seq, 2)
            wgu_b[half, :, :de] = wg_st[slot].astype(BF16)
            wgu_b[half, :, de:] = wu_st[slot].astype(BF16)
            wd_b[lax.rem(seq, 3)] = wd_st[slot].astype(BF16)

    def up_proj(j):
        xb = _unpack_bf16_pairs(_tiles_to_rows(xbuf.at[j % ring], bs, ntp), BF16)
        gu = jnp.dot(xb, wgu_b[lax.rem(seq_of(j), 2)], preferred_element_type=F32)
        gate = gu[:, :de]
        return ((gate * _sigmoid(gate)) * gu[:, de:]).astype(BF16)

    def down_proj(hid, j):
        y = jnp.dot(hid, wd_b[lax.rem(seq_of(j), 3)], preferred_element_type=F32)
        _rows_to_tiles(ybuf.at[(j + ring) % ring], _pack_bf16_pairs(y.astype(BF16)), bs)

    def step(m, carry):
        j0 = 2 * m
        for j in (j0, j0 + 1):
            load_expert_if_first(j)
        for j in (j0, j0 + 1):
            @pl.when(j + EXPERT_LOOKAHEAD < nu)
            def _():
                in_copy(j + EXPERT_LOOKAHEAD).start()

            @pl.when(j < nu)
            def _():
                in_copy(j).wait()
        for j in (j0 - 1, j0):
            @pl.when(j >= ring)
            def _():
                out_copy(j - ring).wait()

        @pl.when(nu + m < nb)
        def _():
            zero_copy(nu + m).start()

        hid0 = up_proj(j0)
        down_proj(hid_s[...], j0 - 1)
        hid1 = up_proj(j0 + 1)
        down_proj(hid0, j0)
        hid_s[...] = hid1

        for j in (j0 - 1, j0):
            @pl.when((j >= 0) & (j < nu))
            def _():
                out_copy(j).start()

        return carry

    n_steps = nu // 2 + 1
    lax.fori_loop(0, n_steps, step, 0)

    last_written = 2 * (nu // 2)
    for r in range(1, ring + 1):
        q = last_written - ring + r

        @pl.when((q >= 0) & (q < nu))
        def _():
            out_copy(q).wait()

    def zero_start(q, carry):
        zero_copy(q).start()
        return carry

    def zero_wait(q, carry):
        zero_copy(q).wait()
        return carry

    lax.fori_loop(nu + n_steps, nb, zero_start, 0)
    lax.fori_loop(nu, nb, zero_wait, 0)


def _experts(blk_tbl, xs, w_gate, w_up, w_down):
    ne, d, de = w_gate.shape
    in_rows = EXPERT_ROWS * (d // LANES) // 2
    nb = xs.shape[0] // in_rows
    ring = EXPERT_LOOKAHEAD + 2
    any_spec = pl.BlockSpec(memory_space=pl.ANY)
    grid_spec = pltpu.PrefetchScalarGridSpec(
        num_scalar_prefetch=1,
        grid=(1,),
        in_specs=[any_spec, any_spec, any_spec, any_spec],
        out_specs=any_spec,
        scratch_shapes=[pltpu.VMEM((ring, in_rows, LANES), U32), pltpu.VMEM((ring, in_rows, LANES), U32),
                        pltpu.VMEM((in_rows, LANES), U32),
                        pltpu.VMEM((WEIGHT_STAGES, d, de), F32), pltpu.VMEM((WEIGHT_STAGES, d, de), F32),
                        pltpu.VMEM((WEIGHT_STAGES, de, d), F32),
                        pltpu.VMEM((2, d, 2 * de), BF16), pltpu.VMEM((3, de, d), BF16),
                        pltpu.VMEM((EXPERT_ROWS, de), BF16),
                        pltpu.SemaphoreType.DMA((WEIGHT_STAGES,)),
                        pltpu.SemaphoreType.DMA((ring,)), pltpu.SemaphoreType.DMA((ring,)),
                        pltpu.SemaphoreType.DMA],
    )
    return pl.pallas_call(
        functools.partial(_experts_kernel, nb),
        grid_spec=grid_spec,
        out_shape=jax.ShapeDtypeStruct((nb * in_rows, LANES), U32),
        compiler_params=pltpu.CompilerParams(
            dimension_semantics=("arbitrary",), vmem_limit_bytes=VMEM_LIMIT),
        name="experts",
    )(blk_tbl, xs, w_gate, w_up, w_down)


def _combine_kernel(d0_ref, d1_ref, d2_ref, x1_ref, gt_ref, y_ref, gfin_ref, o_ref, ybuf, sems):
    tc, d = x1_ref.shape
    nt = ybuf.shape[2] // tc
    i = pl.program_id(0)
    n = pl.num_programs(0)
    slots = ybuf.shape[0]

    def issue_tile(dest_ref, slot, inline):
        _for_each_assignment(
            dest_ref, tc,
            lambda k, t, row: _row_copy(y_ref, row, ybuf.at[slot, k], t, nt,
                                        sems.at[slot]).start(priority=k),
            inline=inline)

    @pl.when(i == 0)
    def _():
        issue_tile(d0_ref, 0, False)
        issue_tile(d1_ref, 1, False)

    slot = i % slots
    _rows_wait(y_ref, TOP_K * tc, nt, sems.at[slot])
    issue_tile(d2_ref, (i + 2) % slots, True)
    y0 = _unpack_bf16_pairs(_tiles_to_rows(ybuf.at[slot, 0], tc, nt), F32)
    y1 = _unpack_bf16_pairs(_tiles_to_rows(ybuf.at[slot, 1], tc, nt), F32)
    gates = gt_ref[...]
    xo = x1_ref[...] + (gates[:, 0:1] * y0 + gates[:, 1:2] * y1)
    o_ref[...] = _rms(xo, gfin_ref[...])

    @pl.when(i == n - 1)
    def _():
        _rows_wait(y_ref, TOP_K * tc, nt, sems.at[(i + 1) % slots])
        _rows_wait(y_ref, TOP_K * tc, nt, sems.at[(i + 2) % slots])


def _combine(dest, x1, gates_t, y, g_final):
    t, d = x1.shape
    nt = d // LANES
    tc = COMBINE_TOKENS
    last = t // tc - 1
    assert last >= 1
    dest_spec = lambda ahead: pl.BlockSpec(
        (TOP_K * tc,), lambda i: (jnp.minimum(i + ahead, last),), memory_space=pltpu.SMEM)
    return pl.pallas_call(
        _combine_kernel,
        grid=(t // tc,),
        in_specs=[
            dest_spec(0), dest_spec(1), dest_spec(2),
            pl.BlockSpec((tc, d), lambda i: (i, 0)),
            pl.BlockSpec((tc, LANES), lambda i: (i, 0)),
            pl.BlockSpec(memory_space=pl.ANY),
            pl.BlockSpec((1, d), lambda i: (0, 0)),
        ],
        out_specs=pl.BlockSpec((tc, d), lambda i: (i, 0)),
        out_shape=jax.ShapeDtypeStruct((t, d), F32),
        scratch_shapes=[pltpu.VMEM((3, TOP_K, tc * nt // 2, LANES), U32),
                        pltpu.SemaphoreType.DMA((3,))],
        compiler_params=pltpu.CompilerParams(dimension_semantics=("arbitrary",)),
        name="combine",
    )(dest, dest, dest, x1, gates_t, y, g_final)


def _layer(x, g_mix, w_in, w_s, b_s, g_sgu, w_conv, w_out, g_ffn, w_rg, w_re, w_gate, w_up, w_down):
    b, s, d = x.shape
    t = b * s
    ne = w_gate.shape[0]
    bs = EXPERT_ROWS

    x1, h2t, logits_t = _mixer(
        x, g_mix.reshape(1, d), w_in, w_s, b_s, g_sgu.reshape(1, d),
        w_conv.reshape(-1, LANES), w_out, g_ffn.reshape(1, d), w_rg.T, w_re.T)

    info, gates_t, counts = _route(logits_t)

    n_rows = -(-(t * TOP_K + ne * (bs - 1)) // bs) * bs
    nb = n_rows // bs
    exp_tbl, blk_tbl = _plan(counts, nb)

    dest_dispatch, dest_combine = _place(info, exp_tbl, (DISPATCH_TOKENS, COMBINE_TOKENS))
    xs = _dispatch(exp_tbl.reshape(-1), blk_tbl, dest_dispatch, h2t, n_rows)
    y = _experts(blk_tbl, xs, w_gate, w_up, w_down)
    return x1.reshape(t, d), dest_combine, gates_t, y


def kernel(x, g_mix, w_in, w_s, b_s, g_sgu, w_conv, w_out, g_ffn, w_router_group, w_router_expert,
           w_gate, w_up, w_down, g_final):
    b, s, d = x.shape
    depth = g_mix.shape[0]
    assert depth == 1, "the final RMSNorm is fused into the last layer's combine"
    assert s % MIX_ROWS == 0 and MIX_ROWS % CHUNK == 0 and d % LANES == 0
    assert EXPERT_ROWS & (EXPERT_ROWS - 1) == 0, "block bookkeeping uses shifts"
    assert all((b * s) % n == 0 for n in (ROUTE_TOKENS, PLACE_TOKENS, DISPATCH_TOKENS, COMBINE_TOKENS))
    l = 0
    x1, dest, gates_t, y = _layer(
        x, g_mix[l], w_in[l], w_s[l], b_s[l], g_sgu[l], w_conv[l], w_out[l], g_ffn[l],
        w_router_group[l], w_router_expert[l], w_gate[l], w_up[l], w_down[l])
    out = _combine(dest, x1, gates_t, y, g_final.reshape(1, d))
    return out.reshape(b, s, d)
```

```python
import functools

import jax
import jax.numpy as jnp
from jax import lax
from jax.experimental import pallas as pl
from jax.experimental.pallas import tpu as pltpu

F32 = jnp.float32
BF16 = jnp.bfloat16
I32 = jnp.int32
U32 = jnp.uint32

EPS = 1e-6
LANES = 128
SUBLANES = 8
CHUNK = 128
N_GROUPS = 8
EXPERTS_PER_GROUP = 8
TOP_K = 2
CONV_K = 3
N_BRANCH = 7

MIX_ROWS = 512
ROUTE_TOKENS = 4096
ROUTE_CHUNK = 512
PLACE_TOKENS = 8192
DISPATCH_TOKENS = 4096
COMBINE_TOKENS = 512
MOVE_UNROLL = 8
WEIGHT_STAGES = 3
WEIGHT_DMA_PRIORITIES = (1, 1, 1)
EXPERT_ROWS = 256
EXPERT_LOOKAHEAD = 3
VMEM_LIMIT = 56 * 1024 * 1024


def _rms(x, g):
    return x * lax.rsqrt(jnp.mean(x * x, axis=-1, keepdims=True) + EPS) * g


def _sigmoid(x):
    return 0.5 * (1.0 + jnp.tanh(0.5 * x))


def _gelu_tanh(x):
    c = 0.7978845608028654
    return x * (0.5 * (1.0 + jnp.tanh(c * (x + 0.044715 * (x * x * x)))))


def _rows_to_tiles(dst_ref, val, rows):
    nt = val.shape[1] // LANES
    for c in range(nt):
        dst_ref[pl.ds(c, rows, stride=nt), :] = val[:, c * LANES:(c + 1) * LANES]


def _tiles_to_rows(src_ref, rows, nt):
    return jnp.concatenate([src_ref[pl.ds(c, rows, stride=nt), :] for c in range(nt)], axis=1)


def _pack_bf16_pairs(x):
    half = x.shape[1] // 2
    bits = pltpu.bitcast(x.astype(F32), U32)
    return lax.shift_right_logical(bits[:, :half], U32(16)) | bits[:, half:]


def _unpack_bf16_pairs(w, dtype):
    lo = pltpu.bitcast(lax.shift_left(w, U32(16)), F32)
    hi = pltpu.bitcast(w & U32(0xFFFF0000), F32)
    return jnp.concatenate([lo, hi], axis=1).astype(dtype)


def _mixer_kernel(x_ref, xp_ref, xn_ref, gmix_ref, win_hbm, ws_ref, bs_ref, gsgu_ref, wconv_ref,
                  wout_hbm, gffn_ref, wrg_ref, wre_ref, x1_ref, h2t_ref, lt_ref,
                  vg_ref, z_ref, acc_ref, win_ref, wout_ref, bias_ref, wr_ref, stage, wsems):
    ts, d = x_ref.shape[1], x_ref.shape[2]
    s = pl.program_id(1)
    ns = pl.num_programs(1)
    gw = d // N_GROUPS
    nc = ts // CHUNK
    nl = d // LANES

    @pl.when((pl.program_id(0) == 0) & (s == 0))
    def _():
        b_sq = jnp.concatenate([bs_ref[...], jnp.zeros((CHUNK - N_GROUPS, CHUNK), F32)], axis=0)
        b_t = b_sq.T
        for g in range(N_GROUPS):
            bias_ref[:, g * gw:(g + 1) * gw] = jnp.broadcast_to(b_t[:, g:g + 1], (CHUNK, gw))
        n_logits = wrg_ref.shape[0] + wre_ref.shape[0]
        wr_rows = jnp.concatenate(
            [wrg_ref[...], wre_ref[...], jnp.zeros((LANES - n_logits, d), F32)], axis=0)
        wr_ref[...] = wr_rows.T.astype(BF16)

        def slab_copy(j):
            src = win_hbm.at[:, pl.ds(j * d, d)] if j < N_BRANCH else wout_hbm
            return pltpu.make_async_copy(src, stage.at[j % 2], wsems.at[j % 2])

        slab_copy(0).start()
        for j in range(N_BRANCH + 1):
            if j < N_BRANCH:
                slab_copy(j + 1).start()
            slab_copy(j).wait()
            if j < N_BRANCH:
                win_ref[:, j * d:(j + 1) * d] = stage[j % 2].astype(BF16)
            else:
                wout_ref[...] = stage[j % 2].astype(BF16)

    x = x_ref[0]
    gmix = gmix_ref[...]
    h = _rms(x, gmix).astype(BF16)

    def proj(j, n=1):
        return jnp.dot(h, win_ref[:, j * d:(j + n) * d], preferred_element_type=F32)

    v_raw = proj(1)
    xh = jnp.concatenate([xp_ref[0], xn_ref[0]], axis=0)
    h_ext = jnp.concatenate([h, _rms(xh, gmix).astype(BF16)], axis=0)
    cx = jnp.dot(h_ext, win_ref[:, 3 * d:5 * d], preferred_element_type=F32)

    gv = _gelu_tanh(v_raw)
    for g in range(N_GROUPS):
        cs = slice(g * gw, (g + 1) * gw)
        blk = gv[:, cs]
        mu = jnp.mean(blk, axis=-1, keepdims=True)
        dv = blk - mu
        var = jnp.mean(dv * dv, axis=-1, keepdims=True)
        vg_ref[:, cs] = (dv * lax.rsqrt(var + EPS) * gsgu_ref[:, cs]).astype(BF16)
    for g in range(N_GROUPS):
        cs = slice(g * gw, (g + 1) * gw)
        vcat = jnp.concatenate([vg_ref[n * CHUNK:(n + 1) * CHUNK, cs] for n in range(nc)], axis=1)
        zg = jnp.dot(ws_ref[g].astype(BF16), vcat, preferred_element_type=F32)
        for n in range(nc):
            z_ref[n * CHUNK:(n + 1) * CHUNK, cs] = zg[:, n * gw:(n + 1) * gw]
    u = _gelu_tanh(proj(0))
    ga = _sigmoid(proj(5))
    for n in range(nc):
        rs = slice(n * CHUNK, (n + 1) * CHUNK)
        acc_ref[rs, :] = ga[rs] * (u[rs] * (z_ref[rs, :] + bias_ref[...]))

    z2 = cx[:ts, :d] * cx[:ts, d:]
    z2h = cx[ts:, :d] * cx[ts:, d:]
    prev = jnp.where(s > 0, z2h[SUBLANES - 1:SUBLANES, :], 0.0)
    nxt = jnp.where(s < ns - 1, z2h[SUBLANES:SUBLANES + 1, :], 0.0)
    row = lax.broadcasted_iota(I32, (ts, d), 0)
    zm1 = jnp.where(row == 0, prev, pltpu.roll(z2, 1, 0))
    zp1 = jnp.where(row == ts - 1, nxt, pltpu.roll(z2, ts - 1, 0))
    tap = lambda k: jnp.concatenate(
        [wconv_ref[k * nl + c:k * nl + c + 1, :] for c in range(nl)], axis=1)
    conv = tap(0) * zm1 + tap(1) * z2 + tap(2) * zp1
    cb = proj(2)
    gb = _sigmoid(proj(6))
    merged = acc_ref[...] + gb * (cb * conv)

    x1 = x + jnp.dot(merged.astype(BF16), wout_ref[...], preferred_element_type=F32)
    x1_ref[0] = x1

    h2 = _rms(x1, gffn_ref[...]).astype(BF16)
    lt_ref[...] = jnp.dot(h2, wr_ref[...], preferred_element_type=F32)
    _rows_to_tiles(h2t_ref, _pack_bf16_pairs(h2), ts)


def _mixer(x, g_mix, w_in, w_s, b_s, g_sgu, w_conv, w_out, g_ffn, wrg_t, wre_t):
    b, s, d = x.shape
    assert b_s.shape == (N_GROUPS, CHUNK) and N_GROUPS + wre_t.shape[0] <= LANES
    ts = MIX_ROWS
    ns = s // ts
    t = b * s
    ntp = d // (2 * LANES)
    hb = ts // SUBLANES
    last_hb = s // SUBLANES - 1

    const = lambda *shape: pl.BlockSpec(shape, lambda bi, si: (0,) * len(shape))
    in_specs = [
        pl.BlockSpec((1, ts, d), lambda bi, si: (bi, si, 0)),
        pl.BlockSpec((1, SUBLANES, d), lambda bi, si: (bi, jnp.maximum(si * hb - 1, 0), 0)),
        pl.BlockSpec((1, SUBLANES, d), lambda bi, si: (bi, jnp.minimum((si + 1) * hb, last_hb), 0)),
        const(1, d),
        pl.BlockSpec(memory_space=pl.ANY),
        const(N_GROUPS, CHUNK, CHUNK),
        const(N_GROUPS, CHUNK),
        const(1, d),
        const(*w_conv.shape),
        pl.BlockSpec(memory_space=pl.ANY),
        const(1, d),
        const(*wrg_t.shape),
        const(*wre_t.shape),
    ]
    out_specs = [
        pl.BlockSpec((1, ts, d), lambda bi, si: (bi, si, 0)),
        pl.BlockSpec((ts * ntp, LANES), lambda bi, si: (bi * ns + si, 0)),
        pl.BlockSpec((ts, LANES), lambda bi, si: (bi * ns + si, 0)),
    ]
    out_shape = [
        jax.ShapeDtypeStruct((b, s, d), F32),
        jax.ShapeDtypeStruct((t * ntp, LANES), U32),
        jax.ShapeDtypeStruct((t, LANES), F32),
    ]
    return pl.pallas_call(
        _mixer_kernel,
        grid=(b, ns),
        in_specs=in_specs,
        out_specs=out_specs,
        out_shape=out_shape,
        scratch_shapes=[pltpu.VMEM((ts, d), BF16), pltpu.VMEM((ts, d), F32), pltpu.VMEM((ts, d), F32),
                        pltpu.VMEM((d, N_BRANCH * d), BF16), pltpu.VMEM((d, d), BF16),
                        pltpu.VMEM((CHUNK, d), F32), pltpu.VMEM((d, LANES), BF16),
                        pltpu.VMEM((2, d, d), F32), pltpu.SemaphoreType.DMA((2,))],
        compiler_params=pltpu.CompilerParams(
            dimension_semantics=("arbitrary", "arbitrary"), vmem_limit_bytes=VMEM_LIMIT),
        name="mixer",
    )(x, x, x, g_mix, w_in, w_s, b_s, g_sgu, w_conv, w_out, g_ffn, wrg_t, wre_t)


def _route_kernel(logits_ref, info_ref, gt_ref, cnt_ref, carry_ref):
    tb = logits_ref.shape[0]
    lt = logits_ref[...].T
    ne = N_GROUPS * EXPERTS_PER_GROUP

    @pl.when(pl.program_id(0) == 0)
    def _():
        carry_ref[...] = jnp.zeros_like(carry_ref)

    row8 = lax.broadcasted_iota(I32, (SUBLANES, tb), 0)
    gl = lt[0:N_GROUPS, :]
    gmax = jnp.max(gl, axis=0, keepdims=True)
    gidx = jnp.min(jnp.where(gl == gmax, row8, N_GROUPS), axis=0, keepdims=True)
    pg = 1.0 / jnp.sum(jnp.exp(gl - gmax), axis=0, keepdims=True)

    sel = jnp.zeros((EXPERTS_PER_GROUP, tb), F32)
    for g in range(N_GROUPS):
        lo = N_GROUPS + g * EXPERTS_PER_GROUP
        sel = jnp.where(gidx == g, lt[lo:lo + EXPERTS_PER_GROUP, :], sel)
    m1 = jnp.max(sel, axis=0, keepdims=True)
    i1 = jnp.min(jnp.where(sel == m1, row8, EXPERTS_PER_GROUP), axis=0, keepdims=True)
    sel2 = jnp.where(row8 == i1, -jnp.inf, sel)
    m2 = jnp.max(sel2, axis=0, keepdims=True)
    i2 = jnp.min(jnp.where(sel2 == m2, row8, EXPERTS_PER_GROUP), axis=0, keepdims=True)
    e2 = jnp.exp(m2 - m1)
    den = 1.0 + e2
    gate0 = pg * (1.0 / den)
    gate1 = pg * (e2 / den)
    eid0 = gidx * EXPERTS_PER_GROUP + i1
    eid1 = gidx * EXPERTS_PER_GROUP + i2

    rowe = lax.broadcasted_iota(I32, (ne, tb), 0)
    hit0 = rowe == eid0
    hit1 = rowe == eid1
    onehot = jnp.where(hit0 | hit1, 1.0, 0.0)
    sub = ROUTE_CHUNK
    before = (lax.broadcasted_iota(I32, (sub, sub), 0) < lax.broadcasted_iota(I32, (sub, sub), 1))
    before = jnp.where(before, 1.0, 0.0).astype(BF16)
    carry = carry_ref[:, 0:1]
    parts = []
    for c in range(tb // sub):
        part = onehot[:, c * sub:(c + 1) * sub]
        parts.append(jnp.dot(part.astype(BF16), before, preferred_element_type=F32) + carry)
        carry = carry + jnp.sum(part, axis=1, keepdims=True)
    base = jnp.concatenate(parts, axis=1)
    rank0 = jnp.sum(jnp.where(hit0, base, 0.0), axis=0, keepdims=True).astype(I32)
    rank1 = jnp.sum(jnp.where(hit1, base, 0.0), axis=0, keepdims=True).astype(I32)
    carry_ref[...] = jnp.broadcast_to(carry, carry_ref.shape)
    cnt_ref[...] = carry_ref[...].astype(I32)

    info_ref[...] = jnp.where(row8 == 0, eid0, jnp.where(row8 == 1, eid1,
                              jnp.where(row8 == 2, rank0, jnp.where(row8 == 3, rank1, 0))))
    rowl = lax.broadcasted_iota(I32, (LANES, tb), 0)
    gates = jnp.where(rowl == 0, gate0, jnp.where(rowl == 1, gate1, 0.0))
    gt_ref[...] = gates.T


def _route(logits):
    t = logits.shape[0]
    tb = ROUTE_TOKENS
    ne = N_GROUPS * EXPERTS_PER_GROUP
    return pl.pallas_call(
        _route_kernel,
        grid=(t // tb,),
        in_specs=[pl.BlockSpec((tb, LANES), lambda i: (i, 0))],
        out_specs=[
            pl.BlockSpec((SUBLANES, tb), lambda i: (0, i)),
            pl.BlockSpec((tb, LANES), lambda i: (i, 0)),
            pl.BlockSpec((ne, LANES), lambda i: (0, 0)),
        ],
        out_shape=[
            jax.ShapeDtypeStruct((SUBLANES, t), I32),
            jax.ShapeDtypeStruct((t, LANES), F32),
            jax.ShapeDtypeStruct((ne, LANES), I32),
        ],
        scratch_shapes=[pltpu.VMEM((ne, LANES), F32)],
        compiler_params=pltpu.CompilerParams(dimension_semantics=("arbitrary",)),
        name="route",
    )(logits)


BLK_EXPERT, BLK_NEXT, BLK_USED, BLK_NEXT2, BLK_SEQ = range(5)
EXP_START, EXP_COUNT, EXP_PADDED = range(3)


class _TableRow:
    def __init__(self, ref, row):
        self.ref, self.row = ref, row

    def __getitem__(self, j):
        return self.ref[self.row, j]


def _plan_kernel(cnt_ref, exp_ref, blk_ref):
    ne = cnt_ref.shape[0]
    nbp = blk_ref.shape[1]
    shift = EXPERT_ROWS.bit_length() - 1
    pad_rows = lambda c: lax.shift_left(lax.shift_right_logical(c + (EXPERT_ROWS - 1), shift), shift)
    cnt = cnt_ref[...]
    padded = pad_rows(cnt)
    padded_lanes = pad_rows(cnt.astype(F32).T[:ne, :ne].astype(I32))
    e_sub = lax.broadcasted_iota(I32, (ne, ne), 0)
    e_lane = lax.broadcasted_iota(I32, (ne, ne), 1)
    pend = jnp.sum(jnp.where(e_lane <= e_sub, padded_lanes, 0), axis=1, keepdims=True)
    pstart = pend - padded[:, 0:1]

    lane = lax.broadcasted_iota(I32, (ne, LANES), 1)
    exp_ref[...] = jnp.where(lane == EXP_START, pstart, jnp.where(
        lane == EXP_COUNT, cnt, jnp.where(lane == EXP_PADDED, padded, 0)))

    first_row = lax.broadcasted_iota(I32, (ne, nbp), 1) * EXPERT_ROWS
    e_col = lax.broadcasted_iota(I32, (ne, nbp), 0)
    blk_e = jnp.minimum(jnp.sum(jnp.where(pend <= first_row, 1, 0), axis=0, keepdims=True), ne - 1)
    has_rows = padded[:, 0:1] > 0
    nxt_e = jnp.min(jnp.where((e_col > blk_e) & has_rows, e_col, ne), axis=0, keepdims=True)
    nxt2_e = jnp.min(jnp.where((e_col > nxt_e) & has_rows, e_col, ne), axis=0, keepdims=True)
    nxt_e = jnp.where(nxt_e == ne, -1, nxt_e)
    nxt2_e = jnp.where(nxt2_e == ne, -1, nxt2_e)
    n_used = lax.shift_right_logical(jnp.max(pend, axis=0, keepdims=True), shift)
    blk_seq = jnp.sum(jnp.where((e_col < blk_e) & has_rows, 1, 0), axis=0, keepdims=True)
    row8 = lax.broadcasted_iota(I32, (SUBLANES, nbp), 0)
    blk_ref[...] = jnp.where(row8 == BLK_EXPERT, blk_e, jnp.where(row8 == BLK_NEXT, nxt_e, jnp.where(
        row8 == BLK_USED, n_used, jnp.where(row8 == BLK_NEXT2, nxt2_e, jnp.where(
            row8 == BLK_SEQ, blk_seq, 0)))))


def _plan(counts, nb):
    ne = counts.shape[0]
    nbp = -(-nb // LANES) * LANES
    return pl.pallas_call(
        _plan_kernel,
        out_shape=[jax.ShapeDtypeStruct((ne, LANES), I32), jax.ShapeDtypeStruct((SUBLANES, nbp), I32)],
        name="plan",
    )(counts)


def _place_kernel(info_ref, exp_ref, *dest_refs):
    tb = info_ref.shape[1]
    ne = exp_ref.shape[0]
    rowe = lax.broadcasted_iota(I32, (ne, tb), 0)
    ps = exp_ref[:, EXP_START:EXP_START + 1]
    dest = []
    for k in range(TOP_K):
        start = jnp.sum(jnp.where(rowe == info_ref[k:k + 1, :], ps, 0), axis=0, keepdims=True)
        dest.append(start + info_ref[TOP_K + k:TOP_K + k + 1, :])
    for ref in dest_refs:
        n = ref.shape[2] // TOP_K
        for q in range(tb // n):
            ref[q] = jnp.concatenate([dk[:, q * n:(q + 1) * n] for dk in dest], axis=1)


def _place(info, exp_tbl, tile_tokens):
    t = info.shape[1]
    tb = PLACE_TOKENS
    ne = exp_tbl.shape[0]
    assert all(tb % n == 0 for n in tile_tokens)
    outs = pl.pallas_call(
        _place_kernel,
        grid=(t // tb,),
        in_specs=[pl.BlockSpec((SUBLANES, tb), lambda i: (0, i)),
                  pl.BlockSpec((ne, LANES), lambda i: (0, 0))],
        out_specs=[pl.BlockSpec((tb // n, 1, TOP_K * n), lambda i: (i, 0, 0)) for n in tile_tokens],
        out_shape=[jax.ShapeDtypeStruct((t // n, 1, TOP_K * n), I32) for n in tile_tokens],
        compiler_params=pltpu.CompilerParams(dimension_semantics=("arbitrary",)),
        name="place",
    )(info, exp_tbl)
    return [o.reshape(-1) for o in outs]


def _row_copy(src_ref, src_row, dst_ref, dst_row, nt, sem):
    first = lambda row: row * nt if isinstance(row, int) else pl.multiple_of(row * nt, nt)
    return pltpu.make_async_copy(
        src_ref.at[pl.ds(first(src_row), nt), :], dst_ref.at[pl.ds(first(dst_row), nt), :], sem)


def _rows_wait(ref, n_rows, nt, sem):
    pltpu.make_async_copy(ref.at[pl.ds(0, n_rows * nt), :], ref.at[pl.ds(0, n_rows * nt), :], sem).wait()


def _for_each_assignment(dest_ref, n_tok, start_copy, inline=False):
    def group(g, c):
        t0 = g * MOVE_UNROLL
        rows = [[dest_ref[k * n_tok + t0 + u] for k in range(TOP_K)] for u in range(MOVE_UNROLL)]
        for u in range(MOVE_UNROLL):
            for k in range(TOP_K):
                start_copy(k, t0 + u, rows[u][k])
        return c

    if inline:
        for g in range(n_tok // MOVE_UNROLL):
            group(g, 0)
    else:
        lax.fori_loop(0, n_tok // MOVE_UNROLL, group, 0)


def _dispatch_kernel(nt, nb, exp_ref, blk_ref, dest_ref, h2_ref, xs_ref, zbuf, sem, zsem):
    td = h2_ref.shape[0] // nt
    ne = exp_ref.shape[0] // LANES
    n_used = blk_ref[BLK_USED, 0]
    blk_rows = zbuf.shape[0]

    def zero_rows(start_not_wait):
        def fire(c):
            c.start() if start_not_wait else c.wait()

        def expert(e, carry):
            cnt = exp_ref[e * LANES + EXP_COUNT]
            row = exp_ref[e * LANES + EXP_START] + cnt
            n = exp_ref[e * LANES + EXP_PADDED] - cnt
            p = EXPERT_ROWS // 2
            while p >= 1:
                has = (n & p) != 0
                r, sz = row, p

                @pl.when(has)
                def _():
                    fire(pltpu.make_async_copy(
                        zbuf.at[pl.ds(0, sz * nt), :],
                        xs_ref.at[pl.ds(pl.multiple_of(r * nt, nt), sz * nt), :], zsem))

                row = row + jnp.where(has, p, 0)
                p //= 2
            return carry

        lax.fori_loop(0, ne, expert, 0)

        def tail(q, carry):
            fire(pltpu.make_async_copy(
                zbuf, xs_ref.at[pl.ds(pl.multiple_of(q * blk_rows, blk_rows), blk_rows), :], zsem))
            return carry

        lax.fori_loop(n_used, nb, tail, 0)

    @pl.when(pl.program_id(0) == 0)
    def _():
        zbuf[...] = jnp.zeros_like(zbuf)
        zero_rows(True)

    _for_each_assignment(
        dest_ref, td,
        lambda k, t, row: _row_copy(h2_ref, t, xs_ref, row, nt, sem).start(priority=k))
    _rows_wait(xs_ref, TOP_K * td, nt, sem)

    @pl.when(pl.program_id(0) == 0)
    def _():
        zero_rows(False)


def _dispatch(exp_flat, blk_tbl, dest, h2t, n_rows):
    td = DISPATCH_TOKENS
    t = dest.shape[0] // TOP_K
    nt = h2t.shape[0] // t
    nb = n_rows // EXPERT_ROWS
    grid_spec = pltpu.PrefetchScalarGridSpec(
        num_scalar_prefetch=2,
        grid=(t // td,),
        in_specs=[
            pl.BlockSpec((TOP_K * td,), lambda i, *_: (i,), memory_space=pltpu.SMEM),
            pl.BlockSpec((td * nt, LANES), lambda i, *_: (i, 0)),
        ],
        out_specs=pl.BlockSpec(memory_space=pl.ANY),
        scratch_shapes=[pltpu.VMEM((EXPERT_ROWS * nt, LANES), h2t.dtype),
                        pltpu.SemaphoreType.DMA, pltpu.SemaphoreType.DMA],
    )
    return pl.pallas_call(
        functools.partial(_dispatch_kernel, nt, nb),
        grid_spec=grid_spec,
        out_shape=jax.ShapeDtypeStruct((n_rows * nt, LANES), h2t.dtype),
        compiler_params=pltpu.CompilerParams(dimension_semantics=("arbitrary",)),
        name="dispatch",
    )(exp_flat, blk_tbl, dest, h2t)


def _experts_kernel(nb, blk_ref, xs_hbm, wg_hbm, wu_hbm, wd_hbm,
                    y_hbm, xbuf, ybuf, zbuf, wg_st, wu_st, wd_st, wgu_b, wd_b, hid_s,
                    wsems, isems, osems, zsem):
    ring = xbuf.shape[0]
    stages, d, de = wg_st.shape
    bs = EXPERT_ROWS
    ntp = xbuf.shape[1] // bs
    be_ref, nx_ref, nx2_ref, seq_ref = (
        _TableRow(blk_ref, r) for r in (BLK_EXPERT, BLK_NEXT, BLK_NEXT2, BLK_SEQ))
    nu = blk_ref[BLK_USED, 0]

    def block(ref, q):
        rows = bs * ntp
        return ref.at[pl.ds(pl.multiple_of(q * rows, rows), rows), :]

    def in_copy(q):
        return pltpu.make_async_copy(block(xs_hbm, q), xbuf.at[q % ring], isems.at[q % ring])

    def out_copy(q):
        return pltpu.make_async_copy(ybuf.at[q % ring], block(y_hbm, q), osems.at[q % ring])

    def zero_copy(q):
        return pltpu.make_async_copy(zbuf, block(y_hbm, q), zsem)

    def weight_copies(ex, slot):
        return (pltpu.make_async_copy(wg_hbm.at[ex], wg_st.at[slot], wsems.at[slot]),
                pltpu.make_async_copy(wu_hbm.at[ex], wu_st.at[slot], wsems.at[slot]),
                pltpu.make_async_copy(wd_hbm.at[ex], wd_st.at[slot], wsems.at[slot]))

    def start_weights(ex, slot):
        for c, prio in zip(weight_copies(ex, slot), WEIGHT_DMA_PRIORITIES):
            c.start(priority=prio)

    xbuf[...] = jnp.zeros_like(xbuf)
    zbuf[...] = jnp.zeros_like(zbuf)
    hid_s[...] = jnp.zeros_like(hid_s)
    wgu_b[...] = jnp.zeros_like(wgu_b)
    wd_b[...] = jnp.zeros_like(wd_b)
    start_weights(be_ref[0], 0)

    @pl.when(nx_ref[0] >= 0)
    def _():
        start_weights(nx_ref[0], 1)

    for q in range(EXPERT_LOOKAHEAD):
        @pl.when(q < nu)
        def _():
            in_copy(q).start()

    seq_of = lambda j: seq_ref[jnp.clip(j, 0, nb - 1)]

    def load_expert_if_first(j):
        jc = jnp.clip(j, 0, nb - 1)
        e = be_ref[jc]

        @pl.when((j < nu) & ((j == 0) | (e != be_ref[jnp.maximum(jc - 1, 0)])))
        def _():
            seq = seq_ref[jc]
            slot = lax.rem(seq, stages)
            for c in weight_copies(e, slot):
                c.wait()
            nx2 = nx2_ref[jc]

            @pl.when(nx2 >= 0)
            def _():
                start_weights(nx2, lax.rem(seq + 2, stages))

            half = lax.rem(seq, 2)
            wgu_b[half, :, :de] = wg_st[slot].astype(BF16)
            wgu_b[half, :, de:] = wu_st[slot].astype(BF16)
            wd_b[lax.rem(seq, 3)] = wd_st[slot].astype(BF16)

    def up_proj(j):
        xb = _unpack_bf16_pairs(_tiles_to_rows(xbuf.at[j % ring], bs, ntp), BF16)
        gu = jnp.dot(xb, wgu_b[lax.rem(seq_of(j), 2)], preferred_element_type=F32)
        gate = gu[:, :de]
        return ((gate * _sigmoid(gate)) * gu[:, de:]).astype(BF16)

    def down_proj(hid, j):
        y = jnp.dot(hid, wd_b[lax.rem(seq_of(j), 3)], preferred_element_type=F32)
        _rows_to_tiles(ybuf.at[(j + ring) % ring], _pack_bf16_pairs(y.astype(BF16)), bs)

    def step(m, carry):
        j0 = 2 * m
        for j in (j0, j0 + 1):
            load_expert_if_first(j)
        for j in (j0, j0 + 1):
            @pl.when(j + EXPERT_LOOKAHEAD < nu)
            def _():
                in_copy(j + EXPERT_LOOKAHEAD).start()

            @pl.when(j < nu)
            def _():
                in_copy(j).wait()
        for j in (j0 - 1, j0):
            @pl.when(j >= ring)
            def _():
                out_copy(j - ring).wait()

        @pl.when(nu + m < nb)
        def _():
            zero_copy(nu + m).start(priority=1)

        hid0 = up_proj(j0)
        down_proj(hid_s[...], j0 - 1)
        hid1 = up_proj(j0 + 1)
        down_proj(hid0, j0)
        hid_s[...] = hid1

        for j in (j0 - 1, j0):
            @pl.when((j >= 0) & (j < nu))
            def _():
                out_copy(j).start(priority=1)

        return carry

    n_steps = nu // 2 + 1
    lax.fori_loop(0, n_steps, step, 0)

    last_written = 2 * (nu // 2)
    for r in range(1, ring + 1):
        q = last_written - ring + r

        @pl.when((q >= 0) & (q < nu))
        def _():
            out_copy(q).wait()

    def zero_start(q, carry):
        zero_copy(q).start()
        return carry

    def zero_wait(q, carry):
        zero_copy(q).wait()
        return carry

    lax.fori_loop(nu + n_steps, nb, zero_start, 0)
    lax.fori_loop(nu, nb, zero_wait, 0)


def _experts(blk_tbl, xs, w_gate, w_up, w_down):
    ne, d, de = w_gate.shape
    in_rows = EXPERT_ROWS * (d // LANES) // 2
    nb = xs.shape[0] // in_rows
    ring = EXPERT_LOOKAHEAD + 2
    any_spec = pl.BlockSpec(memory_space=pl.ANY)
    grid_spec = pltpu.PrefetchScalarGridSpec(
        num_scalar_prefetch=1,
        grid=(1,),
        in_specs=[any_spec, any_spec, any_spec, any_spec],
        out_specs=any_spec,
        scratch_shapes=[pltpu.VMEM((ring, in_rows, LANES), U32), pltpu.VMEM((ring, in_rows, LANES), U32),
                        pltpu.VMEM((in_rows, LANES), U32),
                        pltpu.VMEM((WEIGHT_STAGES, d, de), F32), pltpu.VMEM((WEIGHT_STAGES, d, de), F32),
                        pltpu.VMEM((WEIGHT_STAGES, de, d), F32),
                        pltpu.VMEM((2, d, 2 * de), BF16), pltpu.VMEM((3, de, d), BF16),
                        pltpu.VMEM((EXPERT_ROWS, de), BF16),
                        pltpu.SemaphoreType.DMA((WEIGHT_STAGES,)),
                        pltpu.SemaphoreType.DMA((ring,)), pltpu.SemaphoreType.DMA((ring,)),
                        pltpu.SemaphoreType.DMA],
    )
    return pl.pallas_call(
        functools.partial(_experts_kernel, nb),
        grid_spec=grid_spec,
        out_shape=jax.ShapeDtypeStruct((nb * in_rows, LANES), U32),
        compiler_params=pltpu.CompilerParams(
            dimension_semantics=("arbitrary",), vmem_limit_bytes=VMEM_LIMIT),
        name="experts",
    )(blk_tbl, xs, w_gate, w_up, w_down)


def _combine_kernel(d0_ref, d1_ref, d2_ref, x1_ref, gt_ref, y_ref, gfin_ref, o_ref, ybuf, sems):
    tc, d = x1_ref.shape
    nt = ybuf.shape[2] // tc
    i = pl.program_id(0)
    n = pl.num_programs(0)
    slots = ybuf.shape[0]

    def issue_tile(dest_ref, slot, inline):
        _for_each_assignment(
            dest_ref, tc,
            lambda k, t, row: _row_copy(y_ref, row, ybuf.at[slot, k], t, nt,
                                        sems.at[slot]).start(priority=k),
            inline=inline)

    @pl.when(i == 0)
    def _():
        issue_tile(d0_ref, 0, False)
        issue_tile(d1_ref, 1, False)

    slot = i % slots
    _rows_wait(y_ref, TOP_K * tc, nt, sems.at[slot])
    issue_tile(d2_ref, (i + 2) % slots, True)
    y0 = _unpack_bf16_pairs(_tiles_to_rows(ybuf.at[slot, 0], tc, nt), F32)
    y1 = _unpack_bf16_pairs(_tiles_to_rows(ybuf.at[slot, 1], tc, nt), F32)
    gates = gt_ref[...]
    xo = x1_ref[...] + (gates[:, 0:1] * y0 + gates[:, 1:2] * y1)
    o_ref[...] = _rms(xo, gfin_ref[...])

    @pl.when(i == n - 1)
    def _():
        _rows_wait(y_ref, TOP_K * tc, nt, sems.at[(i + 1) % slots])
        _rows_wait(y_ref, TOP_K * tc, nt, sems.at[(i + 2) % slots])


def _combine(dest, x1, gates_t, y, g_final):
    t, d = x1.shape
    nt = d // LANES
    tc = COMBINE_TOKENS
    last = t // tc - 1
    assert last >= 1
    dest_spec = lambda ahead: pl.BlockSpec(
        (TOP_K * tc,), lambda i: (jnp.minimum(i + ahead, last),), memory_space=pltpu.SMEM)
    return pl.pallas_call(
        _combine_kernel,
        grid=(t // tc,),
        in_specs=[
            dest_spec(0), dest_spec(1), dest_spec(2),
            pl.BlockSpec((tc, d), lambda i: (i, 0)),
            pl.BlockSpec((tc, LANES), lambda i: (i, 0)),
            pl.BlockSpec(memory_space=pl.ANY),
            pl.BlockSpec((1, d), lambda i: (0, 0)),
        ],
        out_specs=pl.BlockSpec((tc, d), lambda i: (i, 0)),
        out_shape=jax.ShapeDtypeStruct((t, d), F32),
        scratch_shapes=[pltpu.VMEM((3, TOP_K, tc * nt // 2, LANES), U32),
                        pltpu.SemaphoreType.DMA((3,))],
        compiler_params=pltpu.CompilerParams(dimension_semantics=("arbitrary",)),
        name="combine",
    )(dest, dest, dest, x1, gates_t, y, g_final)


def _layer(x, g_mix, w_in, w_s, b_s, g_sgu, w_conv, w_out, g_ffn, w_rg, w_re, w_gate, w_up, w_down):
    b, s, d = x.shape
    t = b * s
    ne = w_gate.shape[0]
    bs = EXPERT_ROWS

    x1, h2t, logits_t = _mixer(
        x, g_mix.reshape(1, d), w_in, w_s, b_s, g_sgu.reshape(1, d),
        w_conv.reshape(-1, LANES), w_out, g_ffn.reshape(1, d), w_rg.T, w_re.T)

    info, gates_t, counts = _route(logits_t)

    n_rows = -(-(t * TOP_K + ne * (bs - 1)) // bs) * bs
    nb = n_rows // bs
    exp_tbl, blk_tbl = _plan(counts, nb)

    dest_dispatch, dest_combine = _place(info, exp_tbl, (DISPATCH_TOKENS, COMBINE_TOKENS))
    xs = _dispatch(exp_tbl.reshape(-1), blk_tbl, dest_dispatch, h2t, n_rows)
    y = _experts(blk_tbl, xs, w_gate, w_up, w_down)
    return x1.reshape(t, d), dest_combine, gates_t, y


def kernel(x, g_mix, w_in, w_s, b_s, g_sgu, w_conv, w_out, g_ffn, w_router_group, w_router_expert,
           w_gate, w_up, w_down, g_final):
    b, s, d = x.shape
    depth = g_mix.shape[0]
    assert depth == 1, "the final RMSNorm is fused into the last layer's combine"
    assert s % MIX_ROWS == 0 and MIX_ROWS % CHUNK == 0 and d % LANES == 0
    assert EXPERT_ROWS & (EXPERT_ROWS - 1) == 0, "block bookkeeping uses shifts"
    assert all((b * s) % n == 0 for n in (ROUTE_TOKENS, PLACE_TOKENS, DISPATCH_TOKENS, COMBINE_TOKENS))
    l = 0
    x1, dest, gates_t, y = _layer(
        x, g_mix[l], w_in[l], w_s[l], b_s[l], g_sgu[l], w_conv[l], w_out[l], g_ffn[l],
        w_router_group[l], w_router_expert[l], w_gate[l], w_up[l], w_down[l])
    out = _combine(dest, x1, gates_t, y, g_final.reshape(1, d))
    return out.reshape(b, s, d)
```

```python
import functools

import jax
import jax.numpy as jnp
from jax import lax
from jax.experimental import pallas as pl
from jax.experimental.pallas import tpu as pltpu

F32 = jnp.float32
BF16 = jnp.bfloat16
I32 = jnp.int32
U32 = jnp.uint32

EPS = 1e-6
LANES = 128
SUBLANES = 8
CHUNK = 128
N_GROUPS = 8
EXPERTS_PER_GROUP = 8
TOP_K = 2
CONV_K = 3
N_BRANCH = 7

MIX_ROWS = 512
ROUTE_TOKENS = 4096
ROUTE_CHUNK = 512
PLACE_TOKENS = 8192
DISPATCH_TOKENS = 4096
COMBINE_TOKENS = 512
MOVE_UNROLL = 8
WEIGHT_STAGES = 3
WEIGHT_DMA_PRIORITIES = (1, 1, 1)
EXPERT_ROWS = 256
EXPERT_LOOKAHEAD = 6
VMEM_LIMIT = 56 * 1024 * 1024


def _rms(x, g):
    return x * lax.rsqrt(jnp.mean(x * x, axis=-1, keepdims=True) + EPS) * g


def _sigmoid(x):
    return 0.5 * (1.0 + jnp.tanh(0.5 * x))


def _gelu_tanh(x):
    c = 0.7978845608028654
    return x * (0.5 * (1.0 + jnp.tanh(c * (x + 0.044715 * (x * x * x)))))


def _rows_to_tiles(dst_ref, val, rows):
    nt = val.shape[1] // LANES
    for c in range(nt):
        dst_ref[pl.ds(c, rows, stride=nt), :] = val[:, c * LANES:(c + 1) * LANES]


def _tiles_to_rows(src_ref, rows, nt):
    return jnp.concatenate([src_ref[pl.ds(c, rows, stride=nt), :] for c in range(nt)], axis=1)


def _pack_bf16_pairs(x):
    half = x.shape[1] // 2
    bits = pltpu.bitcast(x.astype(F32), U32)
    return lax.shift_right_logical(bits[:, :half], U32(16)) | bits[:, half:]


def _unpack_bf16_pairs(w, dtype):
    lo = pltpu.bitcast(lax.shift_left(w, U32(16)), F32)
    hi = pltpu.bitcast(w & U32(0xFFFF0000), F32)
    return jnp.concatenate([lo, hi], axis=1).astype(dtype)


def _mixer_kernel(x_ref, xp_ref, xn_ref, gmix_ref, win_hbm, ws_ref, bs_ref, gsgu_ref, wconv_ref,
                  wout_hbm, gffn_ref, wrg_ref, wre_ref, x1_ref, h2t_ref, lt_ref,
                  vg_ref, z_ref, acc_ref, win_ref, wout_ref, bias_ref, wr_ref, stage, wsems):
    ts, d = x_ref.shape[1], x_ref.shape[2]
    s = pl.program_id(1)
    ns = pl.num_programs(1)
    gw = d // N_GROUPS
    nc = ts // CHUNK
    nl = d // LANES

    @pl.when((pl.program_id(0) == 0) & (s == 0))
    def _():
        b_sq = jnp.concatenate([bs_ref[...], jnp.zeros((CHUNK - N_GROUPS, CHUNK), F32)], axis=0)
        b_t = b_sq.T
        for g in range(N_GROUPS):
            bias_ref[:, g * gw:(g + 1) * gw] = jnp.broadcast_to(b_t[:, g:g + 1], (CHUNK, gw))
        n_logits = wrg_ref.shape[0] + wre_ref.shape[0]
        wr_rows = jnp.concatenate(
            [wrg_ref[...], wre_ref[...], jnp.zeros((LANES - n_logits, d), F32)], axis=0)
        wr_ref[...] = wr_rows.T.astype(BF16)

        def slab_copy(j):
            src = win_hbm.at[:, pl.ds(j * d, d)] if j < N_BRANCH else wout_hbm
            return pltpu.make_async_copy(src, stage.at[j % 2], wsems.at[j % 2])

        slab_copy(0).start()
        for j in range(N_BRANCH + 1):
            if j < N_BRANCH:
                slab_copy(j + 1).start()
            slab_copy(j).wait()
            if j < N_BRANCH:
                win_ref[:, j * d:(j + 1) * d] = stage[j % 2].astype(BF16)
            else:
                wout_ref[...] = stage[j % 2].astype(BF16)

    x = x_ref[0]
    gmix = gmix_ref[...]
    h = _rms(x, gmix).astype(BF16)

    def proj(j, n=1):
        return jnp.dot(h, win_ref[:, j * d:(j + n) * d], preferred_element_type=F32)

    v_raw = proj(1)
    xh = jnp.concatenate([xp_ref[0], xn_ref[0]], axis=0)
    h_ext = jnp.concatenate([h, _rms(xh, gmix).astype(BF16)], axis=0)
    cx = jnp.dot(h_ext, win_ref[:, 3 * d:5 * d], preferred_element_type=F32)

    gv = _gelu_tanh(v_raw)
    for g in range(N_GROUPS):
        cs = slice(g * gw, (g + 1) * gw)
        blk = gv[:, cs]
        mu = jnp.mean(blk, axis=-1, keepdims=True)
        dv = blk - mu
        var = jnp.mean(dv * dv, axis=-1, keepdims=True)
        vg_ref[:, cs] = (dv * lax.rsqrt(var + EPS) * gsgu_ref[:, cs]).astype(BF16)
    for g in range(N_GROUPS):
        cs = slice(g * gw, (g + 1) * gw)
        vcat = jnp.concatenate([vg_ref[n * CHUNK:(n + 1) * CHUNK, cs] for n in range(nc)], axis=1)
        zg = jnp.dot(ws_ref[g].astype(BF16), vcat, preferred_element_type=F32)
        for n in range(nc):
            z_ref[n * CHUNK:(n + 1) * CHUNK, cs] = zg[:, n * gw:(n + 1) * gw]
    u = _gelu_tanh(proj(0))
    ga = _sigmoid(proj(5))
    for n in range(nc):
        rs = slice(n * CHUNK, (n + 1) * CHUNK)
        acc_ref[rs, :] = ga[rs] * (u[rs] * (z_ref[rs, :] + bias_ref[...]))

    z2 = cx[:ts, :d] * cx[:ts, d:]
    z2h = cx[ts:, :d] * cx[ts:, d:]
    prev = jnp.where(s > 0, z2h[SUBLANES - 1:SUBLANES, :], 0.0)
    nxt = jnp.where(s < ns - 1, z2h[SUBLANES:SUBLANES + 1, :], 0.0)
    row = lax.broadcasted_iota(I32, (ts, d), 0)
    zm1 = jnp.where(row == 0, prev, pltpu.roll(z2, 1, 0))
    zp1 = jnp.where(row == ts - 1, nxt, pltpu.roll(z2, ts - 1, 0))
    tap = lambda k: jnp.concatenate(
        [wconv_ref[k * nl + c:k * nl + c + 1, :] for c in range(nl)], axis=1)
    conv = tap(0) * zm1 + tap(1) * z2 + tap(2) * zp1
    cb = proj(2)
    gb = _sigmoid(proj(6))
    merged = acc_ref[...] + gb * (cb * conv)

    x1 = x + jnp.dot(merged.astype(BF16), wout_ref[...], preferred_element_type=F32)
    x1_ref[0] = x1

    h2 = _rms(x1, gffn_ref[...]).astype(BF16)
    lt_ref[...] = jnp.dot(h2, wr_ref[...], preferred_element_type=F32)
    _rows_to_tiles(h2t_ref, _pack_bf16_pairs(h2), ts)


def _mixer(x, g_mix, w_in, w_s, b_s, g_sgu, w_conv, w_out, g_ffn, wrg_t, wre_t):
    b, s, d = x.shape
    assert b_s.shape == (N_GROUPS, CHUNK) and N_GROUPS + wre_t.shape[0] <= LANES
    ts = MIX_ROWS
    ns = s // ts
    t = b * s
    ntp = d // (2 * LANES)
    hb = ts // SUBLANES
    last_hb = s // SUBLANES - 1

    const = lambda *shape: pl.BlockSpec(shape, lambda bi, si: (0,) * len(shape))
    in_specs = [
        pl.BlockSpec((1, ts, d), lambda bi, si: (bi, si, 0)),
        pl.BlockSpec((1, SUBLANES, d), lambda bi, si: (bi, jnp.maximum(si * hb - 1, 0), 0)),
        pl.BlockSpec((1, SUBLANES, d), lambda bi, si: (bi, jnp.minimum((si + 1) * hb, last_hb), 0)),
        const(1, d),
        pl.BlockSpec(memory_space=pl.ANY),
        const(N_GROUPS, CHUNK, CHUNK),
        const(N_GROUPS, CHUNK),
        const(1, d),
        const(*w_conv.shape),
        pl.BlockSpec(memory_space=pl.ANY),
        const(1, d),
        const(*wrg_t.shape),
        const(*wre_t.shape),
    ]
    out_specs = [
        pl.BlockSpec((1, ts, d), lambda bi, si: (bi, si, 0)),
        pl.BlockSpec((ts * ntp, LANES), lambda bi, si: (bi * ns + si, 0)),
        pl.BlockSpec((ts, LANES), lambda bi, si: (bi * ns + si, 0)),
    ]
    out_shape = [
        jax.ShapeDtypeStruct((b, s, d), F32),
        jax.ShapeDtypeStruct((t * ntp, LANES), U32),
        jax.ShapeDtypeStruct((t, LANES), F32),
    ]
    return pl.pallas_call(
        _mixer_kernel,
        grid=(b, ns),
        in_specs=in_specs,
        out_specs=out_specs,
        out_shape=out_shape,
        scratch_shapes=[pltpu.VMEM((ts, d), BF16), pltpu.VMEM((ts, d), F32), pltpu.VMEM((ts, d), F32),
                        pltpu.VMEM((d, N_BRANCH * d), BF16), pltpu.VMEM((d, d), BF16),
                        pltpu.VMEM((CHUNK, d), F32), pltpu.VMEM((d, LANES), BF16),
                        pltpu.VMEM((2, d, d), F32), pltpu.SemaphoreType.DMA((2,))],
        compiler_params=pltpu.CompilerParams(
            dimension_semantics=("arbitrary", "arbitrary"), vmem_limit_bytes=VMEM_LIMIT),
        name="mixer",
    )(x, x, x, g_mix, w_in, w_s, b_s, g_sgu, w_conv, w_out, g_ffn, wrg_t, wre_t)


def _route_kernel(logits_ref, info_ref, gt_ref, cnt_ref, carry_ref):
    tb = logits_ref.shape[0]
    lt = logits_ref[...].T
    ne = N_GROUPS * EXPERTS_PER_GROUP

    @pl.when(pl.program_id(0) == 0)
    def _():
        carry_ref[...] = jnp.zeros_like(carry_ref)

    row8 = lax.broadcasted_iota(I32, (SUBLANES, tb), 0)
    gl = lt[0:N_GROUPS, :]
    gmax = jnp.max(gl, axis=0, keepdims=True)
    gidx = jnp.min(jnp.where(gl == gmax, row8, N_GROUPS), axis=0, keepdims=True)
    pg = 1.0 / jnp.sum(jnp.exp(gl - gmax), axis=0, keepdims=True)

    sel = jnp.zeros((EXPERTS_PER_GROUP, tb), F32)
    for g in range(N_GROUPS):
        lo = N_GROUPS + g * EXPERTS_PER_GROUP
        sel = jnp.where(gidx == g, lt[lo:lo + EXPERTS_PER_GROUP, :], sel)
    m1 = jnp.max(sel, axis=0, keepdims=True)
    i1 = jnp.min(jnp.where(sel == m1, row8, EXPERTS_PER_GROUP), axis=0, keepdims=True)
    sel2 = jnp.where(row8 == i1, -jnp.inf, sel)
    m2 = jnp.max(sel2, axis=0, keepdims=True)
    i2 = jnp.min(jnp.where(sel2 == m2, row8, EXPERTS_PER_GROUP), axis=0, keepdims=True)
    e2 = jnp.exp(m2 - m1)
    den = 1.0 + e2
    gate0 = pg * (1.0 / den)
    gate1 = pg * (e2 / den)
    eid0 = gidx * EXPERTS_PER_GROUP + i1
    eid1 = gidx * EXPERTS_PER_GROUP + i2

    rowe = lax.broadcasted_iota(I32, (ne, tb), 0)
    hit0 = rowe == eid0
    hit1 = rowe == eid1
    onehot = jnp.where(hit0 | hit1, 1.0, 0.0)
    sub = ROUTE_CHUNK
    before = (lax.broadcasted_iota(I32, (sub, sub), 0) < lax.broadcasted_iota(I32, (sub, sub), 1))
    before = jnp.where(before, 1.0, 0.0).astype(BF16)
    carry = carry_ref[:, 0:1]
    parts = []
    for c in range(tb // sub):
        part = onehot[:, c * sub:(c + 1) * sub]
        parts.append(jnp.dot(part.astype(BF16), before, preferred_element_type=F32) + carry)
        carry = carry + jnp.sum(part, axis=1, keepdims=True)
    base = jnp.concatenate(parts, axis=1)
    rank0 = jnp.sum(jnp.where(hit0, base, 0.0), axis=0, keepdims=True).astype(I32)
    rank1 = jnp.sum(jnp.where(hit1, base, 0.0), axis=0, keepdims=True).astype(I32)
    carry_ref[...] = jnp.broadcast_to(carry, carry_ref.shape)
    cnt_ref[...] = carry_ref[...].astype(I32)

    info_ref[...] = jnp.where(row8 == 0, eid0, jnp.where(row8 == 1, eid1,
                              jnp.where(row8 == 2, rank0, jnp.where(row8 == 3, rank1, 0))))
    rowl = lax.broadcasted_iota(I32, (LANES, tb), 0)
    gates = jnp.where(rowl == 0, gate0, jnp.where(rowl == 1, gate1, 0.0))
    gt_ref[...] = gates.T


def _route(logits):
    t = logits.shape[0]
    tb = ROUTE_TOKENS
    ne = N_GROUPS * EXPERTS_PER_GROUP
    return pl.pallas_call(
        _route_kernel,
        grid=(t // tb,),
        in_specs=[pl.BlockSpec((tb, LANES), lambda i: (i, 0))],
        out_specs=[
            pl.BlockSpec((SUBLANES, tb), lambda i: (0, i)),
            pl.BlockSpec((tb, LANES), lambda i: (i, 0)),
            pl.BlockSpec((ne, LANES), lambda i: (0, 0)),
        ],
        out_shape=[
            jax.ShapeDtypeStruct((SUBLANES, t), I32),
            jax.ShapeDtypeStruct((t, LANES), F32),
            jax.ShapeDtypeStruct((ne, LANES), I32),
        ],
        scratch_shapes=[pltpu.VMEM((ne, LANES), F32)],
        compiler_params=pltpu.CompilerParams(dimension_semantics=("arbitrary",)),
        name="route",
    )(logits)


BLK_EXPERT, BLK_NEXT, BLK_USED, BLK_NEXT2, BLK_SEQ = range(5)
EXP_START, EXP_COUNT, EXP_PADDED = range(3)


class _TableRow:
    def __init__(self, ref, row):
        self.ref, self.row = ref, row

    def __getitem__(self, j):
        return self.ref[self.row, j]


def _plan_kernel(cnt_ref, exp_ref, blk_ref):
    ne = cnt_ref.shape[0]
    nbp = blk_ref.shape[1]
    shift = EXPERT_ROWS.bit_length() - 1
    pad_rows = lambda c: lax.shift_left(lax.shift_right_logical(c + (EXPERT_ROWS - 1), shift), shift)
    cnt = cnt_ref[...]
    padded = pad_rows(cnt)
    padded_lanes = pad_rows(cnt.astype(F32).T[:ne, :ne].astype(I32))
    e_sub = lax.broadcasted_iota(I32, (ne, ne), 0)
    e_lane = lax.broadcasted_iota(I32, (ne, ne), 1)
    pend = jnp.sum(jnp.where(e_lane <= e_sub, padded_lanes, 0), axis=1, keepdims=True)
    pstart = pend - padded[:, 0:1]

    lane = lax.broadcasted_iota(I32, (ne, LANES), 1)
    exp_ref[...] = jnp.where(lane == EXP_START, pstart, jnp.where(
        lane == EXP_COUNT, cnt, jnp.where(lane == EXP_PADDED, padded, 0)))

    first_row = lax.broadcasted_iota(I32, (ne, nbp), 1) * EXPERT_ROWS
    e_col = lax.broadcasted_iota(I32, (ne, nbp), 0)
    blk_e = jnp.minimum(jnp.sum(jnp.where(pend <= first_row, 1, 0), axis=0, keepdims=True), ne - 1)
    has_rows = padded[:, 0:1] > 0
    nxt_e = jnp.min(jnp.where((e_col > blk_e) & has_rows, e_col, ne), axis=0, keepdims=True)
    nxt2_e = jnp.min(jnp.where((e_col > nxt_e) & has_rows, e_col, ne), axis=0, keepdims=True)
    nxt_e = jnp.where(nxt_e == ne, -1, nxt_e)
    nxt2_e = jnp.where(nxt2_e == ne, -1, nxt2_e)
    n_used = lax.shift_right_logical(jnp.max(pend, axis=0, keepdims=True), shift)
    blk_seq = jnp.sum(jnp.where((e_col < blk_e) & has_rows, 1, 0), axis=0, keepdims=True)
    row8 = lax.broadcasted_iota(I32, (SUBLANES, nbp), 0)
    blk_ref[...] = jnp.where(row8 == BLK_EXPERT, blk_e, jnp.where(row8 == BLK_NEXT, nxt_e, jnp.where(
        row8 == BLK_USED, n_used, jnp.where(row8 == BLK_NEXT2, nxt2_e, jnp.where(
            row8 == BLK_SEQ, blk_seq, 0)))))


def _plan(counts, nb):
    ne = counts.shape[0]
    nbp = -(-nb // LANES) * LANES
    return pl.pallas_call(
        _plan_kernel,
        out_shape=[jax.ShapeDtypeStruct((ne, LANES), I32), jax.ShapeDtypeStruct((SUBLANES, nbp), I32)],
        name="plan",
    )(counts)


def _place_kernel(info_ref, exp_ref, *dest_refs):
    tb = info_ref.shape[1]
    ne = exp_ref.shape[0]
    rowe = lax.broadcasted_iota(I32, (ne, tb), 0)
    ps = exp_ref[:, EXP_START:EXP_START + 1]
    dest = []
    for k in range(TOP_K):
        start = jnp.sum(jnp.where(rowe == info_ref[k:k + 1, :], ps, 0), axis=0, keepdims=True)
        dest.append(start + info_ref[TOP_K + k:TOP_K + k + 1, :])
    for ref in dest_refs:
        n = ref.shape[2] // TOP_K
        for q in range(tb // n):
            ref[q] = jnp.concatenate([dk[:, q * n:(q + 1) * n] for dk in dest], axis=1)


def _place(info, exp_tbl, tile_tokens):
    t = info.shape[1]
    tb = PLACE_TOKENS
    ne = exp_tbl.shape[0]
    assert all(tb % n == 0 for n in tile_tokens)
    outs = pl.pallas_call(
        _place_kernel,
        grid=(t // tb,),
        in_specs=[pl.BlockSpec((SUBLANES, tb), lambda i: (0, i)),
                  pl.BlockSpec((ne, LANES), lambda i: (0, 0))],
        out_specs=[pl.BlockSpec((tb // n, 1, TOP_K * n), lambda i: (i, 0, 0)) for n in tile_tokens],
        out_shape=[jax.ShapeDtypeStruct((t // n, 1, TOP_K * n), I32) for n in tile_tokens],
        compiler_params=pltpu.CompilerParams(dimension_semantics=("arbitrary",)),
        name="place",
    )(info, exp_tbl)
    return [o.reshape(-1) for o in outs]


def _row_copy(src_ref, src_row, dst_ref, dst_row, nt, sem):
    first = lambda row: row * nt if isinstance(row, int) else pl.multiple_of(row * nt, nt)
    return pltpu.make_async_copy(
        src_ref.at[pl.ds(first(src_row), nt), :], dst_ref.at[pl.ds(first(dst_row), nt), :], sem)


def _rows_wait(ref, n_rows, nt, sem):
    pltpu.make_async_copy(ref.at[pl.ds(0, n_rows * nt), :], ref.at[pl.ds(0, n_rows * nt), :], sem).wait()


def _for_each_assignment(dest_ref, n_tok, start_copy, inline=False):
    def group(g, c):
        t0 = g * MOVE_UNROLL
        rows = [[dest_ref[k * n_tok + t0 + u] for k in range(TOP_K)] for u in range(MOVE_UNROLL)]
        for u in range(MOVE_UNROLL):
            for k in range(TOP_K):
                start_copy(k, t0 + u, rows[u][k])
        return c

    if inline:
        for g in range(n_tok // MOVE_UNROLL):
            group(g, 0)
    else:
        lax.fori_loop(0, n_tok // MOVE_UNROLL, group, 0)


def _dispatch_kernel(nt, nb, exp_ref, blk_ref, dest_ref, h2_ref, xs_ref, zbuf, sem, zsem):
    td = h2_ref.shape[0] // nt
    ne = exp_ref.shape[0] // LANES
    n_used = blk_ref[BLK_USED, 0]
    blk_rows = zbuf.shape[0]

    def zero_rows(start_not_wait):
        def fire(c):
            c.start() if start_not_wait else c.wait()

        def expert(e, carry):
            cnt = exp_ref[e * LANES + EXP_COUNT]
            row = exp_ref[e * LANES + EXP_START] + cnt
            n = exp_ref[e * LANES + EXP_PADDED] - cnt
            p = EXPERT_ROWS // 2
            while p >= 1:
                has = (n & p) != 0
                r, sz = row, p

                @pl.when(has)
                def _():
                    fire(pltpu.make_async_copy(
                        zbuf.at[pl.ds(0, sz * nt), :],
                        xs_ref.at[pl.ds(pl.multiple_of(r * nt, nt), sz * nt), :], zsem))

                row = row + jnp.where(has, p, 0)
                p //= 2
            return carry

        lax.fori_loop(0, ne, expert, 0)

        def tail(q, carry):
            fire(pltpu.make_async_copy(
                zbuf, xs_ref.at[pl.ds(pl.multiple_of(q * blk_rows, blk_rows), blk_rows), :], zsem))
            return carry

        lax.fori_loop(n_used, nb, tail, 0)

    @pl.when(pl.program_id(0) == 0)
    def _():
        zbuf[...] = jnp.zeros_like(zbuf)
        zero_rows(True)

    _for_each_assignment(
        dest_ref, td,
        lambda k, t, row: _row_copy(h2_ref, t, xs_ref, row, nt, sem).start(priority=k))
    _rows_wait(xs_ref, TOP_K * td, nt, sem)

    @pl.when(pl.program_id(0) == 0)
    def _():
        zero_rows(False)


def _dispatch(exp_flat, blk_tbl, dest, h2t, n_rows):
    td = DISPATCH_TOKENS
    t = dest.shape[0] // TOP_K
    nt = h2t.shape[0] // t
    nb = n_rows // EXPERT_ROWS
    grid_spec = pltpu.PrefetchScalarGridSpec(
        num_scalar_prefetch=2,
        grid=(t // td,),
        in_specs=[
            pl.BlockSpec((TOP_K * td,), lambda i, *_: (i,), memory_space=pltpu.SMEM),
            pl.BlockSpec((td * nt, LANES), lambda i, *_: (i, 0)),
        ],
        out_specs=pl.BlockSpec(memory_space=pl.ANY),
        scratch_shapes=[pltpu.VMEM((EXPERT_ROWS * nt, LANES), h2t.dtype),
                        pltpu.SemaphoreType.DMA, pltpu.SemaphoreType.DMA],
    )
    return pl.pallas_call(
        functools.partial(_dispatch_kernel, nt, nb),
        grid_spec=grid_spec,
        out_shape=jax.ShapeDtypeStruct((n_rows * nt, LANES), h2t.dtype),
        compiler_params=pltpu.CompilerParams(dimension_semantics=("arbitrary",)),
        name="dispatch",
    )(exp_flat, blk_tbl, dest, h2t)


def _experts_kernel(nb, blk_ref, xs_hbm, wg_hbm, wu_hbm, wd_hbm,
                    y_hbm, xbuf, ybuf, zbuf, wg_st, wu_st, wd_st, wgu_b, wd_b, hid_s,
                    wsems, isems, osems, zsem):
    ring = xbuf.shape[0]
    stages, d, de = wg_st.shape
    bs = EXPERT_ROWS
    ntp = xbuf.shape[1] // bs
    be_ref, nx_ref, nx2_ref, seq_ref = (
        _TableRow(blk_ref, r) for r in (BLK_EXPERT, BLK_NEXT, BLK_NEXT2, BLK_SEQ))
    nu = blk_ref[BLK_USED, 0]

    def block(ref, q):
        rows = bs * ntp
        return ref.at[pl.ds(pl.multiple_of(q * rows, rows), rows), :]

    def in_copy(q):
        return pltpu.make_async_copy(block(xs_hbm, q), xbuf.at[q % ring], isems.at[q % ring])

    def out_copy(q):
        return pltpu.make_async_copy(ybuf.at[q % ring], block(y_hbm, q), osems.at[q % ring])

    def zero_copy(q):
        return pltpu.make_async_copy(zbuf, block(y_hbm, q), zsem)

    def weight_copies(ex, slot):
        return (pltpu.make_async_copy(wg_hbm.at[ex], wg_st.at[slot], wsems.at[slot]),
                pltpu.make_async_copy(wu_hbm.at[ex], wu_st.at[slot], wsems.at[slot]),
                pltpu.make_async_copy(wd_hbm.at[ex], wd_st.at[slot], wsems.at[slot]))

    def start_weights(ex, slot):
        for c, prio in zip(weight_copies(ex, slot), WEIGHT_DMA_PRIORITIES):
            c.start(priority=prio)

    xbuf[...] = jnp.zeros_like(xbuf)
    zbuf[...] = jnp.zeros_like(zbuf)
    hid_s[...] = jnp.zeros_like(hid_s)
    wgu_b[...] = jnp.zeros_like(wgu_b)
    wd_b[...] = jnp.zeros_like(wd_b)
    start_weights(be_ref[0], 0)

    @pl.when(nx_ref[0] >= 0)
    def _():
        start_weights(nx_ref[0], 1)

    for q in range(EXPERT_LOOKAHEAD):
        @pl.when(q < nu)
        def _():
            in_copy(q).start()

    seq_of = lambda j: seq_ref[jnp.clip(j, 0, nb - 1)]

    def load_expert_if_first(j):
        jc = jnp.clip(j, 0, nb - 1)
        e = be_ref[jc]

        @pl.when((j < nu) & ((j == 0) | (e != be_ref[jnp.maximum(jc - 1, 0)])))
        def _():
            seq = seq_ref[jc]
            slot = lax.rem(seq, stages)
            for c in weight_copies(e, slot):
                c.wait()
            nx2 = nx2_ref[jc]

            @pl.when(nx2 >= 0)
            def _():
                start_weights(nx2, lax.rem(seq + 2, stages))

            half = lax.rem(seq, 2)
            wgu_b[half, :, :de] = wg_st[slot].astype(BF16)
            wgu_b[half, :, de:] = wu_st[slot].astype(BF16)
            wd_b[lax.rem(seq, 3)] = wd_st[slot].astype(BF16)

    def up_proj(j):
        xb = _unpack_bf16_pairs(_tiles_to_rows(xbuf.at[j % ring], bs, ntp), BF16)
        gu = jnp.dot(xb, wgu_b[lax.rem(seq_of(j), 2)], preferred_element_type=F32)
        gate = gu[:, :de]
        return ((gate * _sigmoid(gate)) * gu[:, de:]).astype(BF16)

    def down_proj(hid, j):
        y = jnp.dot(hid, wd_b[lax.rem(seq_of(j), 3)], preferred_element_type=F32)
        _rows_to_tiles(ybuf.at[(j + ring) % ring], _pack_bf16_pairs(y.astype(BF16)), bs)

    def step(m, carry):
        j0 = 2 * m
        for j in (j0, j0 + 1):
            load_expert_if_first(j)
        for j in (j0, j0 + 1):
            @pl.when(j + EXPERT_LOOKAHEAD < nu)
            def _():
                in_copy(j + EXPERT_LOOKAHEAD).start()

            @pl.when(j < nu)
            def _():
                in_copy(j).wait()
        for j in (j0 - 1, j0):
            @pl.when(j >= ring)
            def _():
                out_copy(j - ring).wait()

        @pl.when(nu + m < nb)
        def _():
            zero_copy(nu + m).start(priority=1)

        hid0 = up_proj(j0)
        down_proj(hid_s[...], j0 - 1)
        hid1 = up_proj(j0 + 1)
        down_proj(hid0, j0)
        hid_s[...] = hid1

        for j in (j0 - 1, j0):
            @pl.when((j >= 0) & (j < nu))
            def _():
                out_copy(j).start(priority=1)

        return carry

    n_steps = nu // 2 + 1
    lax.fori_loop(0, n_steps, step, 0)

    last_written = 2 * (nu // 2)
    for r in range(1, ring + 1):
        q = last_written - ring + r

        @pl.when((q >= 0) & (q < nu))
        def _():
            out_copy(q).wait()

    def zero_start(q, carry):
        zero_copy(q).start()
        return carry

    def zero_wait(q, carry):
        zero_copy(q).wait()
        return carry

    lax.fori_loop(nu + n_steps, nb, zero_start, 0)
    lax.fori_loop(nu, nb, zero_wait, 0)


def _experts(blk_tbl, xs, w_gate, w_up, w_down):
    ne, d, de = w_gate.shape
    in_rows = EXPERT_ROWS * (d // LANES) // 2
    nb = xs.shape[0] // in_rows
    ring = EXPERT_LOOKAHEAD + 2
    any_spec = pl.BlockSpec(memory_space=pl.ANY)
    grid_spec = pltpu.PrefetchScalarGridSpec(
        num_scalar_prefetch=1,
        grid=(1,),
        in_specs=[any_spec, any_spec, any_spec, any_spec],
        out_specs=any_spec,
        scratch_shapes=[pltpu.VMEM((ring, in_rows, LANES), U32), pltpu.VMEM((ring, in_rows, LANES), U32),
                        pltpu.VMEM((in_rows, LANES), U32),
                        pltpu.VMEM((WEIGHT_STAGES, d, de), F32), pltpu.VMEM((WEIGHT_STAGES, d, de), F32),
                        pltpu.VMEM((WEIGHT_STAGES, de, d), F32),
                        pltpu.VMEM((2, d, 2 * de), BF16), pltpu.VMEM((3, de, d), BF16),
                        pltpu.VMEM((EXPERT_ROWS, de), BF16),
                        pltpu.SemaphoreType.DMA((WEIGHT_STAGES,)),
                        pltpu.SemaphoreType.DMA((ring,)), pltpu.SemaphoreType.DMA((ring,)),
                        pltpu.SemaphoreType.DMA],
    )
    return pl.pallas_call(
        functools.partial(_experts_kernel, nb),
        grid_spec=grid_spec,
        out_shape=jax.ShapeDtypeStruct((nb * in_rows, LANES), U32),
        compiler_params=pltpu.CompilerParams(
            dimension_semantics=("arbitrary",), vmem_limit_bytes=VMEM_LIMIT),
        name="experts",
    )(blk_tbl, xs, w_gate, w_up, w_down)


def _combine_kernel(d0_ref, d1_ref, d2_ref, x1_ref, gt_ref, y_ref, gfin_ref, o_ref, ybuf, sems):
    tc, d = x1_ref.shape
    nt = ybuf.shape[2] // tc
    i = pl.program_id(0)
    n = pl.num_programs(0)
    slots = ybuf.shape[0]

    def issue_tile(dest_ref, slot, inline):
        _for_each_assignment(
            dest_ref, tc,
            lambda k, t, row: _row_copy(y_ref, row, ybuf.at[slot, k], t, nt,
                                        sems.at[slot]).start(priority=k),
            inline=inline)

    @pl.when(i == 0)
    def _():
        issue_tile(d0_ref, 0, False)
        issue_tile(d1_ref, 1, False)

    slot = i % slots
    _rows_wait(y_ref, TOP_K * tc, nt, sems.at[slot])
    issue_tile(d2_ref, (i + 2) % slots, True)
    y0 = _unpack_bf16_pairs(_tiles_to_rows(ybuf.at[slot, 0], tc, nt), F32)
    y1 = _unpack_bf16_pairs(_tiles_to_rows(ybuf.at[slot, 1], tc, nt), F32)
    gates = gt_ref[...]
    xo = x1_ref[...] + (gates[:, 0:1] * y0 + gates[:, 1:2] * y1)
    o_ref[...] = _rms(xo, gfin_ref[...])

    @pl.when(i == n - 1)
    def _():
        _rows_wait(y_ref, TOP_K * tc, nt, sems.at[(i + 1) % slots])
        _rows_wait(y_ref, TOP_K * tc, nt, sems.at[(i + 2) % slots])


def _combine(dest, x1, gates_t, y, g_final):
    t, d = x1.shape
    nt = d // LANES
    tc = COMBINE_TOKENS
    last = t // tc - 1
    assert last >= 1
    dest_spec = lambda ahead: pl.BlockSpec(
        (TOP_K * tc,), lambda i: (jnp.minimum(i + ahead, last),), memory_space=pltpu.SMEM)
    return pl.pallas_call(
        _combine_kernel,
        grid=(t // tc,),
        in_specs=[
            dest_spec(0), dest_spec(1), dest_spec(2),
            pl.BlockSpec((tc, d), lambda i: (i, 0)),
            pl.BlockSpec((tc, LANES), lambda i: (i, 0)),
            pl.BlockSpec(memory_space=pl.ANY),
            pl.BlockSpec((1, d), lambda i: (0, 0)),
        ],
        out_specs=pl.BlockSpec((tc, d), lambda i: (i, 0)),
        out_shape=jax.ShapeDtypeStruct((t, d), F32),
        scratch_shapes=[pltpu.VMEM((3, TOP_K, tc * nt // 2, LANES), U32),
                        pltpu.SemaphoreType.DMA((3,))],
        compiler_params=pltpu.CompilerParams(dimension_semantics=("arbitrary",)),
        name="combine",
    )(dest, dest, dest, x1, gates_t, y, g_final)


def _layer(x, g_mix, w_in, w_s, b_s, g_sgu, w_conv, w_out, g_ffn, w_rg, w_re, w_gate, w_up, w_down):
    b, s, d = x.shape
    t = b * s
    ne = w_gate.shape[0]
    bs = EXPERT_ROWS

    x1, h2t, logits_t = _mixer(
        x, g_mix.reshape(1, d), w_in, w_s, b_s, g_sgu.reshape(1, d),
        w_conv.reshape(-1, LANES), w_out, g_ffn.reshape(1, d), w_rg.T, w_re.T)

    info, gates_t, counts = _route(logits_t)

    n_rows = -(-(t * TOP_K + ne * (bs - 1)) // bs) * bs
    nb = n_rows // bs
    exp_tbl, blk_tbl = _plan(counts, nb)

    dest_dispatch, dest_combine = _place(info, exp_tbl, (DISPATCH_TOKENS, COMBINE_TOKENS))
    xs = _dispatch(exp_tbl.reshape(-1), blk_tbl, dest_dispatch, h2t, n_rows)
    y = _experts(blk_tbl, xs, w_gate, w_up, w_down)
    return x1.reshape(t, d), dest_combine, gates_t, y


def kernel(x, g_mix, w_in, w_s, b_s, g_sgu, w_conv, w_out, g_ffn, w_router_group, w_router_expert,
           w_gate, w_up, w_down, g_final):
    b, s, d = x.shape
    depth = g_mix.shape[0]
    assert depth == 1, "the final RMSNorm is fused into the last layer's combine"
    assert s % MIX_ROWS == 0 and MIX_ROWS % CHUNK == 0 and d % LANES == 0
    assert EXPERT_ROWS & (EXPERT_ROWS - 1) == 0, "block bookkeeping uses shifts"
    assert all((b * s) % n == 0 for n in (ROUTE_TOKENS, PLACE_TOKENS, DISPATCH_TOKENS, COMBINE_TOKENS))
    l = 0
    x1, dest, gates_t, y = _layer(
        x, g_mix[l], w_in[l], w_s[l], b_s[l], g_sgu[l], w_conv[l], w_out[l], g_ffn[l],
        w_router_group[l], w_router_expert[l], w_gate[l], w_up[l], w_down[l])
    out = _combine(dest, x1, gates_t, y, g_final.reshape(1, d))
    return out.reshape(b, s, d)
```

```python
import functools

import jax
import jax.numpy as jnp
from jax import lax
from jax.experimental import pallas as pl
from jax.experimental.pallas import tpu as pltpu

F32 = jnp.float32
BF16 = jnp.bfloat16
I32 = jnp.int32
U32 = jnp.uint32

EPS = 1e-6
LANES = 128
SUBLANES = 8
CHUNK = 128
N_GROUPS = 8
EXPERTS_PER_GROUP = 8
TOP_K = 2
CONV_K = 3
N_BRANCH = 7

MIX_ROWS = 512
ROUTE_TOKENS = 4096
ROUTE_CHUNK = 512
PLACE_TOKENS = 8192
DISPATCH_TOKENS = 4096
COMBINE_TOKENS = 512
MOVE_UNROLL = 8
WEIGHT_STAGES = 3
WEIGHT_DMA_PRIORITIES = (1, 1, 1)
EXPERT_ROWS = 256
EXPERT_LOOKAHEAD = 6
VMEM_LIMIT = 56 * 1024 * 1024


def _rms(x, g):
    return x * lax.rsqrt(jnp.mean(x * x, axis=-1, keepdims=True) + EPS) * g


def _sigmoid(x):
    return 0.5 * (1.0 + jnp.tanh(0.5 * x))


def _gelu_tanh(x):
    c = 0.7978845608028654
    return x * (0.5 * (1.0 + jnp.tanh(c * (x + 0.044715 * (x * x * x)))))


def _rows_to_tiles(dst_ref, val, rows):
    nt = val.shape[1] // LANES
    for c in range(nt):
        dst_ref[pl.ds(c, rows, stride=nt), :] = val[:, c * LANES:(c + 1) * LANES]


def _tiles_to_rows(src_ref, rows, nt):
    return jnp.concatenate([src_ref[pl.ds(c, rows, stride=nt), :] for c in range(nt)], axis=1)


def _pack_bf16_pairs(x):
    half = x.shape[1] // 2
    bits = pltpu.bitcast(x.astype(F32), U32)
    return lax.shift_right_logical(bits[:, :half], U32(16)) | bits[:, half:]


def _unpack_bf16_pairs(w, dtype):
    lo = pltpu.bitcast(lax.shift_left(w, U32(16)), F32)
    hi = pltpu.bitcast(w & U32(0xFFFF0000), F32)
    return jnp.concatenate([lo, hi], axis=1).astype(dtype)


def _mixer_kernel(x_ref, xp_ref, xn_ref, gmix_ref, win_hbm, ws_ref, bs_ref, gsgu_ref, wconv_ref,
                  wout_hbm, gffn_ref, wrg_ref, wre_ref, x1_ref, h2t_ref, lt_ref,
                  vg_ref, z_ref, acc_ref, win_ref, wout_ref, bias_ref, wr_ref, stage, wsems):
    ts, d = x_ref.shape[1], x_ref.shape[2]
    s = pl.program_id(1)
    ns = pl.num_programs(1)
    gw = d // N_GROUPS
    nc = ts // CHUNK
    nl = d // LANES

    @pl.when((pl.program_id(0) == 0) & (s == 0))
    def _():
        b_sq = jnp.concatenate([bs_ref[...], jnp.zeros((CHUNK - N_GROUPS, CHUNK), F32)], axis=0)
        b_t = b_sq.T
        for g in range(N_GROUPS):
            bias_ref[:, g * gw:(g + 1) * gw] = jnp.broadcast_to(b_t[:, g:g + 1], (CHUNK, gw))
        n_logits = wrg_ref.shape[0] + wre_ref.shape[0]
        wr_rows = jnp.concatenate(
            [wrg_ref[...], wre_ref[...], jnp.zeros((LANES - n_logits, d), F32)], axis=0)
        wr_ref[...] = wr_rows.T.astype(BF16)

        def slab_copy(j):
            src = win_hbm.at[:, pl.ds(j * d, d)] if j < N_BRANCH else wout_hbm
            return pltpu.make_async_copy(src, stage.at[j % 2], wsems.at[j % 2])

        slab_copy(0).start()
        for j in range(N_BRANCH + 1):
            if j < N_BRANCH:
                slab_copy(j + 1).start()
            slab_copy(j).wait()
            if j < N_BRANCH:
                win_ref[:, j * d:(j + 1) * d] = stage[j % 2].astype(BF16)
            else:
                wout_ref[...] = stage[j % 2].astype(BF16)

    x = x_ref[0]
    gmix = gmix_ref[...]
    h = _rms(x, gmix).astype(BF16)

    def proj(j, n=1):
        return jnp.dot(h, win_ref[:, j * d:(j + n) * d], preferred_element_type=F32)

    v_raw = proj(1)
    xh = jnp.concatenate([xp_ref[0], xn_ref[0]], axis=0)
    h_ext = jnp.concatenate([h, _rms(xh, gmix).astype(BF16)], axis=0)
    cx = jnp.dot(h_ext, win_ref[:, 3 * d:5 * d], preferred_element_type=F32)

    gv = _gelu_tanh(v_raw)
    for g in range(N_GROUPS):
        cs = slice(g * gw, (g + 1) * gw)
        blk = gv[:, cs]
        mu = jnp.mean(blk, axis=-1, keepdims=True)
        dv = blk - mu
        var = jnp.mean(dv * dv, axis=-1, keepdims=True)
        vg_ref[:, cs] = (dv * lax.rsqrt(var + EPS) * gsgu_ref[:, cs]).astype(BF16)
    for g in range(N_GROUPS):
        cs = slice(g * gw, (g + 1) * gw)
        vcat = jnp.concatenate([vg_ref[n * CHUNK:(n + 1) * CHUNK, cs] for n in range(nc)], axis=1)
        zg = jnp.dot(ws_ref[g].astype(BF16), vcat, preferred_element_type=F32)
        for n in range(nc):
            z_ref[n * CHUNK:(n + 1) * CHUNK, cs] = zg[:, n * gw:(n + 1) * gw]
    u = _gelu_tanh(proj(0))
    ga = _sigmoid(proj(5))
    for n in range(nc):
        rs = slice(n * CHUNK, (n + 1) * CHUNK)
        acc_ref[rs, :] = ga[rs] * (u[rs] * (z_ref[rs, :] + bias_ref[...]))

    z2 = cx[:ts, :d] * cx[:ts, d:]
    z2h = cx[ts:, :d] * cx[ts:, d:]
    prev = jnp.where(s > 0, z2h[SUBLANES - 1:SUBLANES, :], 0.0)
    nxt = jnp.where(s < ns - 1, z2h[SUBLANES:SUBLANES + 1, :], 0.0)
    row = lax.broadcasted_iota(I32, (ts, d), 0)
    zm1 = jnp.where(row == 0, prev, pltpu.roll(z2, 1, 0))
    zp1 = jnp.where(row == ts - 1, nxt, pltpu.roll(z2, ts - 1, 0))
    tap = lambda k: jnp.concatenate(
        [wconv_ref[k * nl + c:k * nl + c + 1, :] for c in range(nl)], axis=1)
    conv = tap(0) * zm1 + tap(1) * z2 + tap(2) * zp1
    cb = proj(2)
    gb = _sigmoid(proj(6))
    merged = acc_ref[...] + gb * (cb * conv)

    x1 = x + jnp.dot(merged.astype(BF16), wout_ref[...], preferred_element_type=F32)
    x1_ref[0] = x1

    h2 = _rms(x1, gffn_ref[...]).astype(BF16)
    lt_ref[...] = jnp.dot(h2, wr_ref[...], preferred_element_type=F32)
    _rows_to_tiles(h2t_ref, _pack_bf16_pairs(h2), ts)


def _mixer(x, g_mix, w_in, w_s, b_s, g_sgu, w_conv, w_out, g_ffn, wrg_t, wre_t):
    b, s, d = x.shape
    assert b_s.shape == (N_GROUPS, CHUNK) and N_GROUPS + wre_t.shape[0] <= LANES
    ts = MIX_ROWS
    ns = s // ts
    t = b * s
    ntp = d // (2 * LANES)
    hb = ts // SUBLANES
    last_hb = s // SUBLANES - 1

    const = lambda *shape: pl.BlockSpec(shape, lambda bi, si: (0,) * len(shape))
    in_specs = [
        pl.BlockSpec((1, ts, d), lambda bi, si: (bi, si, 0)),
        pl.BlockSpec((1, SUBLANES, d), lambda bi, si: (bi, jnp.maximum(si * hb - 1, 0), 0)),
        pl.BlockSpec((1, SUBLANES, d), lambda bi, si: (bi, jnp.minimum((si + 1) * hb, last_hb), 0)),
        const(1, d),
        pl.BlockSpec(memory_space=pl.ANY),
        const(N_GROUPS, CHUNK, CHUNK),
        const(N_GROUPS, CHUNK),
        const(1, d),
        const(*w_conv.shape),
        pl.BlockSpec(memory_space=pl.ANY),
        const(1, d),
        const(*wrg_t.shape),
        const(*wre_t.shape),
    ]
    out_specs = [
        pl.BlockSpec((1, ts, d), lambda bi, si: (bi, si, 0)),
        pl.BlockSpec((ts * ntp, LANES), lambda bi, si: (bi * ns + si, 0)),
        pl.BlockSpec((ts, LANES), lambda bi, si: (bi * ns + si, 0)),
    ]
    out_shape = [
        jax.ShapeDtypeStruct((b, s, d), F32),
        jax.ShapeDtypeStruct((t * ntp, LANES), U32),
        jax.ShapeDtypeStruct((t, LANES), F32),
    ]
    return pl.pallas_call(
        _mixer_kernel,
        grid=(b, ns),
        in_specs=in_specs,
        out_specs=out_specs,
        out_shape=out_shape,
        scratch_shapes=[pltpu.VMEM((ts, d), BF16), pltpu.VMEM((ts, d), F32), pltpu.VMEM((ts, d), F32),
                        pltpu.VMEM((d, N_BRANCH * d), BF16), pltpu.VMEM((d, d), BF16),
                        pltpu.VMEM((CHUNK, d), F32), pltpu.VMEM((d, LANES), BF16),
                        pltpu.VMEM((2, d, d), F32), pltpu.SemaphoreType.DMA((2,))],
        compiler_params=pltpu.CompilerParams(
            dimension_semantics=("arbitrary", "arbitrary"), vmem_limit_bytes=VMEM_LIMIT),
        name="mixer",
    )(x, x, x, g_mix, w_in, w_s, b_s, g_sgu, w_conv, w_out, g_ffn, wrg_t, wre_t)


def _route_kernel(logits_ref, info_ref, gt_ref, cnt_ref, carry_ref):
    tb = logits_ref.shape[0]
    lt = logits_ref[...].T
    ne = N_GROUPS * EXPERTS_PER_GROUP

    @pl.when(pl.program_id(0) == 0)
    def _():
        carry_ref[...] = jnp.zeros_like(carry_ref)

    row8 = lax.broadcasted_iota(I32, (SUBLANES, tb), 0)
    gl = lt[0:N_GROUPS, :]
    gmax = jnp.max(gl, axis=0, keepdims=True)
    gidx = jnp.min(jnp.where(gl == gmax, row8, N_GROUPS), axis=0, keepdims=True)
    pg = 1.0 / jnp.sum(jnp.exp(gl - gmax), axis=0, keepdims=True)

    sel = jnp.zeros((EXPERTS_PER_GROUP, tb), F32)
    for g in range(N_GROUPS):
        lo = N_GROUPS + g * EXPERTS_PER_GROUP
        sel = jnp.where(gidx == g, lt[lo:lo + EXPERTS_PER_GROUP, :], sel)
    m1 = jnp.max(sel, axis=0, keepdims=True)
    i1 = jnp.min(jnp.where(sel == m1, row8, EXPERTS_PER_GROUP), axis=0, keepdims=True)
    sel2 = jnp.where(row8 == i1, -jnp.inf, sel)
    m2 = jnp.max(sel2, axis=0, keepdims=True)
    i2 = jnp.min(jnp.where(sel2 == m2, row8, EXPERTS_PER_GROUP), axis=0, keepdims=True)
    e2 = jnp.exp(m2 - m1)
    den = 1.0 + e2
    gate0 = pg * (1.0 / den)
    gate1 = pg * (e2 / den)
    eid0 = gidx * EXPERTS_PER_GROUP + i1
    eid1 = gidx * EXPERTS_PER_GROUP + i2

    rowe = lax.broadcasted_iota(I32, (ne, tb), 0)
    hit0 = rowe == eid0
    hit1 = rowe == eid1
    onehot = jnp.where(hit0 | hit1, 1.0, 0.0)
    sub = ROUTE_CHUNK
    before = (lax.broadcasted_iota(I32, (sub, sub), 0) < lax.broadcasted_iota(I32, (sub, sub), 1))
    before = jnp.where(before, 1.0, 0.0).astype(BF16)
    carry = carry_ref[:, 0:1]
    parts = []
    for c in range(tb // sub):
        part = onehot[:, c * sub:(c + 1) * sub]
        parts.append(jnp.dot(part.astype(BF16), before, preferred_element_type=F32) + carry)
        carry = carry + jnp.sum(part, axis=1, keepdims=True)
    base = jnp.concatenate(parts, axis=1)
    rank0 = jnp.sum(jnp.where(hit0, base, 0.0), axis=0, keepdims=True).astype(I32)
    rank1 = jnp.sum(jnp.where(hit1, base, 0.0), axis=0, keepdims=True).astype(I32)
    carry_ref[...] = jnp.broadcast_to(carry, carry_ref.shape)
    cnt_ref[...] = carry_ref[...].astype(I32)

    info_ref[...] = jnp.where(row8 == 0, eid0, jnp.where(row8 == 1, eid1,
                              jnp.where(row8 == 2, rank0, jnp.where(row8 == 3, rank1, 0))))
    rowl = lax.broadcasted_iota(I32, (LANES, tb), 0)
    gates = jnp.where(rowl == 0, gate0, jnp.where(rowl == 1, gate1, 0.0))
    gt_ref[...] = gates.T


def _route(logits):
    t = logits.shape[0]
    tb = ROUTE_TOKENS
    ne = N_GROUPS * EXPERTS_PER_GROUP
    return pl.pallas_call(
        _route_kernel,
        grid=(t // tb,),
        in_specs=[pl.BlockSpec((tb, LANES), lambda i: (i, 0))],
        out_specs=[
            pl.BlockSpec((SUBLANES, tb), lambda i: (0, i)),
            pl.BlockSpec((tb, LANES), lambda i: (i, 0)),
            pl.BlockSpec((ne, LANES), lambda i: (0, 0)),
        ],
        out_shape=[
            jax.ShapeDtypeStruct((SUBLANES, t), I32),
            jax.ShapeDtypeStruct((t, LANES), F32),
            jax.ShapeDtypeStruct((ne, LANES), I32),
        ],
        scratch_shapes=[pltpu.VMEM((ne, LANES), F32)],
        compiler_params=pltpu.CompilerParams(dimension_semantics=("arbitrary",)),
        name="route",
    )(logits)


BLK_EXPERT, BLK_NEXT, BLK_USED, BLK_NEXT2, BLK_SEQ = range(5)
EXP_START, EXP_COUNT, EXP_PADDED = range(3)


class _TableRow:
    def __init__(self, ref, row):
        self.ref, self.row = ref, row

    def __getitem__(self, j):
        return self.ref[self.row, j]


def _plan_kernel(cnt_ref, exp_ref, blk_ref):
    ne = cnt_ref.shape[0]
    nbp = blk_ref.shape[1]
    shift = EXPERT_ROWS.bit_length() - 1
    pad_rows = lambda c: lax.shift_left(lax.shift_right_logical(c + (EXPERT_ROWS - 1), shift), shift)
    cnt = cnt_ref[...]
    padded = pad_rows(cnt)
    padded_lanes = pad_rows(cnt.astype(F32).T[:ne, :ne].astype(I32))
    e_sub = lax.broadcasted_iota(I32, (ne, ne), 0)
    e_lane = lax.broadcasted_iota(I32, (ne, ne), 1)
    pend = jnp.sum(jnp.where(e_lane <= e_sub, padded_lanes, 0), axis=1, keepdims=True)
    pstart = pend - padded[:, 0:1]

    lane = lax.broadcasted_iota(I32, (ne, LANES), 1)
    exp_ref[...] = jnp.where(lane == EXP_START, pstart, jnp.where(
        lane == EXP_COUNT, cnt, jnp.where(lane == EXP_PADDED, padded, 0)))

    first_row = lax.broadcasted_iota(I32, (ne, nbp), 1) * EXPERT_ROWS
    e_col = lax.broadcasted_iota(I32, (ne, nbp), 0)
    blk_e = jnp.minimum(jnp.sum(jnp.where(pend <= first_row, 1, 0), axis=0, keepdims=True), ne - 1)
    has_rows = padded[:, 0:1] > 0
    nxt_e = jnp.min(jnp.where((e_col > blk_e) & has_rows, e_col, ne), axis=0, keepdims=True)
    nxt2_e = jnp.min(jnp.where((e_col > nxt_e) & has_rows, e_col, ne), axis=0, keepdims=True)
    nxt_e = jnp.where(nxt_e == ne, -1, nxt_e)
    nxt2_e = jnp.where(nxt2_e == ne, -1, nxt2_e)
    n_used = lax.shift_right_logical(jnp.max(pend, axis=0, keepdims=True), shift)
    blk_seq = jnp.sum(jnp.where((e_col < blk_e) & has_rows, 1, 0), axis=0, keepdims=True)
    row8 = lax.broadcasted_iota(I32, (SUBLANES, nbp), 0)
    blk_ref[...] = jnp.where(row8 == BLK_EXPERT, blk_e, jnp.where(row8 == BLK_NEXT, nxt_e, jnp.where(
        row8 == BLK_USED, n_used, jnp.where(row8 == BLK_NEXT2, nxt2_e, jnp.where(
            row8 == BLK_SEQ, blk_seq, 0)))))


def _place_kernel(cnt_ref, info_ref, exp_ref, blk_ref, *dest_refs):
    @pl.when(pl.program_id(0) == 0)
    def _():
        _plan_kernel(cnt_ref, exp_ref, blk_ref)

    tb = info_ref.shape[1]
    ne = exp_ref.shape[0]
    rowe = lax.broadcasted_iota(I32, (ne, tb), 0)
    ps = exp_ref[:, EXP_START:EXP_START + 1]
    dest = []
    for k in range(TOP_K):
        start = jnp.sum(jnp.where(rowe == info_ref[k:k + 1, :], ps, 0), axis=0, keepdims=True)
        dest.append(start + info_ref[TOP_K + k:TOP_K + k + 1, :])
    for ref in dest_refs:
        n = ref.shape[2] // TOP_K
        for q in range(tb // n):
            ref[q] = jnp.concatenate([dk[:, q * n:(q + 1) * n] for dk in dest], axis=1)


def _place(info, counts, nb, tile_tokens):
    t = info.shape[1]
    tb = PLACE_TOKENS
    ne = counts.shape[0]
    nbp = -(-nb // LANES) * LANES
    assert all(tb % n == 0 for n in tile_tokens)
    const = lambda *shape: pl.BlockSpec(shape, lambda i: (0,) * len(shape))
    exp_tbl, blk_tbl, *outs = pl.pallas_call(
        _place_kernel,
        grid=(t // tb,),
        in_specs=[const(ne, LANES), pl.BlockSpec((SUBLANES, tb), lambda i: (0, i))],
        out_specs=[const(ne, LANES), const(SUBLANES, nbp)] + [
            pl.BlockSpec((tb // n, 1, TOP_K * n), lambda i: (i, 0, 0)) for n in tile_tokens],
        out_shape=[jax.ShapeDtypeStruct((ne, LANES), I32), jax.ShapeDtypeStruct((SUBLANES, nbp), I32)] + [
            jax.ShapeDtypeStruct((t // n, 1, TOP_K * n), I32) for n in tile_tokens],
        compiler_params=pltpu.CompilerParams(dimension_semantics=("arbitrary",)),
        name="place",
    )(counts, info)
    return exp_tbl, blk_tbl, [o.reshape(-1) for o in outs]


def _row_copy(src_ref, src_row, dst_ref, dst_row, nt, sem):
    first = lambda row: row * nt if isinstance(row, int) else pl.multiple_of(row * nt, nt)
    return pltpu.make_async_copy(
        src_ref.at[pl.ds(first(src_row), nt), :], dst_ref.at[pl.ds(first(dst_row), nt), :], sem)


def _rows_wait(ref, n_rows, nt, sem):
    pltpu.make_async_copy(ref.at[pl.ds(0, n_rows * nt), :], ref.at[pl.ds(0, n_rows * nt), :], sem).wait()


def _for_each_assignment(dest_ref, n_tok, start_copy, inline=False):
    def group(g, c):
        t0 = g * MOVE_UNROLL
        rows = [[dest_ref[k * n_tok + t0 + u] for k in range(TOP_K)] for u in range(MOVE_UNROLL)]
        for u in range(MOVE_UNROLL):
            for k in range(TOP_K):
                start_copy(k, t0 + u, rows[u][k])
        return c

    if inline:
        for g in range(n_tok // MOVE_UNROLL):
            group(g, 0)
    else:
        lax.fori_loop(0, n_tok // MOVE_UNROLL, group, 0)


def _dispatch_kernel(nt, nb, exp_ref, blk_ref, dest_ref, h2_ref, xs_ref, zbuf, sem, zsem):
    td = h2_ref.shape[0] // nt
    ne = exp_ref.shape[0] // LANES
    n_used = blk_ref[BLK_USED, 0]
    blk_rows = zbuf.shape[0]

    def zero_rows(start_not_wait):
        def fire(c):
            c.start() if start_not_wait else c.wait()

        def expert(e, carry):
            cnt = exp_ref[e * LANES + EXP_COUNT]
            row = exp_ref[e * LANES + EXP_START] + cnt
            n = exp_ref[e * LANES + EXP_PADDED] - cnt
            p = EXPERT_ROWS // 2
            while p >= 1:
                has = (n & p) != 0
                r, sz = row, p

                @pl.when(has)
                def _():
                    fire(pltpu.make_async_copy(
                        zbuf.at[pl.ds(0, sz * nt), :],
                        xs_ref.at[pl.ds(pl.multiple_of(r * nt, nt), sz * nt), :], zsem))

                row = row + jnp.where(has, p, 0)
                p //= 2
            return carry

        lax.fori_loop(0, ne, expert, 0)

        def tail(q, carry):
            fire(pltpu.make_async_copy(
                zbuf, xs_ref.at[pl.ds(pl.multiple_of(q * blk_rows, blk_rows), blk_rows), :], zsem))
            return carry

        lax.fori_loop(n_used, nb, tail, 0)

    @pl.when(pl.program_id(0) == 0)
    def _():
        zbuf[...] = jnp.zeros_like(zbuf)
        zero_rows(True)

    _for_each_assignment(
        dest_ref, td,
        lambda k, t, row: _row_copy(h2_ref, t, xs_ref, row, nt, sem).start(priority=k))
    _rows_wait(xs_ref, TOP_K * td, nt, sem)

    @pl.when(pl.program_id(0) == 0)
    def _():
        zero_rows(False)


def _dispatch(exp_flat, blk_tbl, dest, h2t, n_rows):
    td = DISPATCH_TOKENS
    t = dest.shape[0] // TOP_K
    nt = h2t.shape[0] // t
    nb = n_rows // EXPERT_ROWS
    grid_spec = pltpu.PrefetchScalarGridSpec(
        num_scalar_prefetch=2,
        grid=(t // td,),
        in_specs=[
            pl.BlockSpec((TOP_K * td,), lambda i, *_: (i,), memory_space=pltpu.SMEM),
            pl.BlockSpec((td * nt, LANES), lambda i, *_: (i, 0)),
        ],
        out_specs=pl.BlockSpec(memory_space=pl.ANY),
        scratch_shapes=[pltpu.VMEM((EXPERT_ROWS * nt, LANES), h2t.dtype),
                        pltpu.SemaphoreType.DMA, pltpu.SemaphoreType.DMA],
    )
    return pl.pallas_call(
        functools.partial(_dispatch_kernel, nt, nb),
        grid_spec=grid_spec,
        out_shape=jax.ShapeDtypeStruct((n_rows * nt, LANES), h2t.dtype),
        compiler_params=pltpu.CompilerParams(dimension_semantics=("arbitrary",)),
        name="dispatch",
    )(exp_flat, blk_tbl, dest, h2t)


def _experts_kernel(nb, blk_ref, xs_hbm, wg_hbm, wu_hbm, wd_hbm,
                    y_hbm, xbuf, ybuf, zbuf, wg_st, wu_st, wd_st, wgu_b, wd_b, hid_s,
                    wsems, isems, osems, zsem):
    ring = xbuf.shape[0]
    stages, d, de = wg_st.shape
    bs = EXPERT_ROWS
    ntp = xbuf.shape[1] // bs
    be_ref, nx_ref, nx2_ref, seq_ref = (
        _TableRow(blk_ref, r) for r in (BLK_EXPERT, BLK_NEXT, BLK_NEXT2, BLK_SEQ))
    nu = blk_ref[BLK_USED, 0]

    def block(ref, q):
        rows = bs * ntp
        return ref.at[pl.ds(pl.multiple_of(q * rows, rows), rows), :]

    def in_copy(q):
        return pltpu.make_async_copy(block(xs_hbm, q), xbuf.at[q % ring], isems.at[q % ring])

    def out_copy(q):
        return pltpu.make_async_copy(ybuf.at[q % ring], block(y_hbm, q), osems.at[q % ring])

    def zero_copy(q):
        return pltpu.make_async_copy(zbuf, block(y_hbm, q), zsem)

    def weight_copies(ex, slot):
        return (pltpu.make_async_copy(wg_hbm.at[ex], wg_st.at[slot], wsems.at[slot]),
                pltpu.make_async_copy(wu_hbm.at[ex], wu_st.at[slot], wsems.at[slot]),
                pltpu.make_async_copy(wd_hbm.at[ex], wd_st.at[slot], wsems.at[slot]))

    def start_weights(ex, slot):
        for c, prio in zip(weight_copies(ex, slot), WEIGHT_DMA_PRIORITIES):
            c.start(priority=prio)

    xbuf[...] = jnp.zeros_like(xbuf)
    zbuf[...] = jnp.zeros_like(zbuf)
    hid_s[...] = jnp.zeros_like(hid_s)
    wgu_b[...] = jnp.zeros_like(wgu_b)
    wd_b[...] = jnp.zeros_like(wd_b)
    start_weights(be_ref[0], 0)

    @pl.when(nx_ref[0] >= 0)
    def _():
        start_weights(nx_ref[0], 1)

    for q in range(EXPERT_LOOKAHEAD):
        @pl.when(q < nu)
        def _():
            in_copy(q).start()

    seq_of = lambda j: seq_ref[jnp.clip(j, 0, nb - 1)]

    def load_expert_if_first(j):
        jc = jnp.clip(j, 0, nb - 1)
        e = be_ref[jc]

        @pl.when((j < nu) & ((j == 0) | (e != be_ref[jnp.maximum(jc - 1, 0)])))
        def _():
            seq = seq_ref[jc]
            slot = lax.rem(seq, stages)
            for c in weight_copies(e, slot):
                c.wait()
            nx2 = nx2_ref[jc]

            @pl.when(nx2 >= 0)
            def _():
                start_weights(nx2, lax.rem(seq + 2, stages))

            half = lax.rem(seq, 2)
            wgu_b[half, :, :de] = wg_st[slot].astype(BF16)
            wgu_b[half, :, de:] = wu_st[slot].astype(BF16)
            wd_b[lax.rem(seq, 3)] = wd_st[slot].astype(BF16)

    def up_proj(j):
        xb = _unpack_bf16_pairs(_tiles_to_rows(xbuf.at[j % ring], bs, ntp), BF16)
        gu = jnp.dot(xb, wgu_b[lax.rem(seq_of(j), 2)], preferred_element_type=F32)
        gate = gu[:, :de]
        return ((gate * _sigmoid(gate)) * gu[:, de:]).astype(BF16)

    def down_proj(hid, j):
        y = jnp.dot(hid, wd_b[lax.rem(seq_of(j), 3)], preferred_element_type=F32)
        _rows_to_tiles(ybuf.at[(j + ring) % ring], _pack_bf16_pairs(y.astype(BF16)), bs)

    def step(m, carry):
        j0 = 2 * m
        for j in (j0, j0 + 1):
            load_expert_if_first(j)
        for j in (j0, j0 + 1):
            @pl.when(j + EXPERT_LOOKAHEAD < nu)
            def _():
                in_copy(j + EXPERT_LOOKAHEAD).start()

            @pl.when(j < nu)
            def _():
                in_copy(j).wait()
        for j in (j0 - 1, j0):
            @pl.when(j >= ring)
            def _():
                out_copy(j - ring).wait()

        @pl.when(nu + m < nb)
        def _():
            zero_copy(nu + m).start(priority=1)

        hid0 = up_proj(j0)
        down_proj(hid_s[...], j0 - 1)
        hid1 = up_proj(j0 + 1)
        down_proj(hid0, j0)
        hid_s[...] = hid1

        for j in (j0 - 1, j0):
            @pl.when((j >= 0) & (j < nu))
            def _():
                out_copy(j).start(priority=1)

        return carry

    n_steps = nu // 2 + 1
    lax.fori_loop(0, n_steps, step, 0)

    last_written = 2 * (nu // 2)
    for r in range(1, ring + 1):
        q = last_written - ring + r

        @pl.when((q >= 0) & (q < nu))
        def _():
            out_copy(q).wait()

    def zero_start(q, carry):
        zero_copy(q).start()
        return carry

    def zero_wait(q, carry):
        zero_copy(q).wait()
        return carry

    lax.fori_loop(nu + n_steps, nb, zero_start, 0)
    lax.fori_loop(nu, nb, zero_wait, 0)


def _experts(blk_tbl, xs, w_gate, w_up, w_down):
    ne, d, de = w_gate.shape
    in_rows = EXPERT_ROWS * (d // LANES) // 2
    nb = xs.shape[0] // in_rows
    ring = EXPERT_LOOKAHEAD + 2
    any_spec = pl.BlockSpec(memory_space=pl.ANY)
    grid_spec = pltpu.PrefetchScalarGridSpec(
        num_scalar_prefetch=1,
        grid=(1,),
        in_specs=[any_spec, any_spec, any_spec, any_spec],
        out_specs=any_spec,
        scratch_shapes=[pltpu.VMEM((ring, in_rows, LANES), U32), pltpu.VMEM((ring, in_rows, LANES), U32),
                        pltpu.VMEM((in_rows, LANES), U32),
                        pltpu.VMEM((WEIGHT_STAGES, d, de), F32), pltpu.VMEM((WEIGHT_STAGES, d, de), F32),
                        pltpu.VMEM((WEIGHT_STAGES, de, d), F32),
                        pltpu.VMEM((2, d, 2 * de), BF16), pltpu.VMEM((3, de, d), BF16),
                        pltpu.VMEM((EXPERT_ROWS, de), BF16),
                        pltpu.SemaphoreType.DMA((WEIGHT_STAGES,)),
                        pltpu.SemaphoreType.DMA((ring,)), pltpu.SemaphoreType.DMA((ring,)),
                        pltpu.SemaphoreType.DMA],
    )
    return pl.pallas_call(
        functools.partial(_experts_kernel, nb),
        grid_spec=grid_spec,
        out_shape=jax.ShapeDtypeStruct((nb * in_rows, LANES), U32),
        compiler_params=pltpu.CompilerParams(
            dimension_semantics=("arbitrary",), vmem_limit_bytes=VMEM_LIMIT),
        name="experts",
    )(blk_tbl, xs, w_gate, w_up, w_down)


def _combine_kernel(d0_ref, d1_ref, d2_ref, x1_ref, gt_ref, y_ref, gfin_ref, o_ref, ybuf, sems):
    tc, d = x1_ref.shape
    nt = ybuf.shape[2] // tc
    i = pl.program_id(0)
    n = pl.num_programs(0)
    slots = ybuf.shape[0]

    def issue_tile(dest_ref, slot, inline):
        _for_each_assignment(
            dest_ref, tc,
            lambda k, t, row: _row_copy(y_ref, row, ybuf.at[slot, k], t, nt,
                                        sems.at[slot]).start(priority=k),
            inline=inline)

    @pl.when(i == 0)
    def _():
        issue_tile(d0_ref, 0, False)
        issue_tile(d1_ref, 1, False)

    slot = i % slots
    _rows_wait(y_ref, TOP_K * tc, nt, sems.at[slot])
    issue_tile(d2_ref, (i + 2) % slots, True)
    y0 = _unpack_bf16_pairs(_tiles_to_rows(ybuf.at[slot, 0], tc, nt), F32)
    y1 = _unpack_bf16_pairs(_tiles_to_rows(ybuf.at[slot, 1], tc, nt), F32)
    gates = gt_ref[...]
    xo = x1_ref[...] + (gates[:, 0:1] * y0 + gates[:, 1:2] * y1)
    o_ref[...] = _rms(xo, gfin_ref[...])

    @pl.when(i == n - 1)
    def _():
        _rows_wait(y_ref, TOP_K * tc, nt, sems.at[(i + 1) % slots])
        _rows_wait(y_ref, TOP_K * tc, nt, sems.at[(i + 2) % slots])


def _combine(dest, x1, gates_t, y, g_final):
    t, d = x1.shape
    nt = d // LANES
    tc = COMBINE_TOKENS
    last = t // tc - 1
    assert last >= 1
    dest_spec = lambda ahead: pl.BlockSpec(
        (TOP_K * tc,), lambda i: (jnp.minimum(i + ahead, last),), memory_space=pltpu.SMEM)
    return pl.pallas_call(
        _combine_kernel,
        grid=(t // tc,),
        in_specs=[
            dest_spec(0), dest_spec(1), dest_spec(2),
            pl.BlockSpec((tc, d), lambda i: (i, 0)),
            pl.BlockSpec((tc, LANES), lambda i: (i, 0)),
            pl.BlockSpec(memory_space=pl.ANY),
            pl.BlockSpec((1, d), lambda i: (0, 0)),
        ],
        out_specs=pl.BlockSpec((tc, d), lambda i: (i, 0)),
        out_shape=jax.ShapeDtypeStruct((t, d), F32),
        scratch_shapes=[pltpu.VMEM((3, TOP_K, tc * nt // 2, LANES), U32),
                        pltpu.SemaphoreType.DMA((3,))],
        compiler_params=pltpu.CompilerParams(dimension_semantics=("arbitrary",)),
        name="combine",
    )(dest, dest, dest, x1, gates_t, y, g_final)


def _layer(x, g_mix, w_in, w_s, b_s, g_sgu, w_conv, w_out, g_ffn, w_rg, w_re, w_gate, w_up, w_down):
    b, s, d = x.shape
    t = b * s
    ne = w_gate.shape[0]
    bs = EXPERT_ROWS

    x1, h2t, logits_t = _mixer(
        x, g_mix.reshape(1, d), w_in, w_s, b_s, g_sgu.reshape(1, d),
        w_conv.reshape(-1, LANES), w_out, g_ffn.reshape(1, d), w_rg.T, w_re.T)

    info, gates_t, counts = _route(logits_t)

    n_rows = -(-(t * TOP_K + ne * (bs - 1)) // bs) * bs
    nb = n_rows // bs
    exp_tbl, blk_tbl, (dest_dispatch, dest_combine) = _place(
        info, counts, nb, (DISPATCH_TOKENS, COMBINE_TOKENS))
    xs = _dispatch(exp_tbl.reshape(-1), blk_tbl, dest_dispatch, h2t, n_rows)
    y = _experts(blk_tbl, xs, w_gate, w_up, w_down)
    return x1.reshape(t, d), dest_combine, gates_t, y


def kernel(x, g_mix, w_in, w_s, b_s, g_sgu, w_conv, w_out, g_ffn, w_router_group, w_router_expert,
           w_gate, w_up, w_down, g_final):
    b, s, d = x.shape
    depth = g_mix.shape[0]
    assert depth == 1, "the final RMSNorm is fused into the last layer's combine"
    assert s % MIX_ROWS == 0 and MIX_ROWS % CHUNK == 0 and d % LANES == 0
    assert EXPERT_ROWS & (EXPERT_ROWS - 1) == 0, "block bookkeeping uses shifts"
    assert all((b * s) % n == 0 for n in (ROUTE_TOKENS, PLACE_TOKENS, DISPATCH_TOKENS, COMBINE_TOKENS))
    l = 0
    x1, dest, gates_t, y = _layer(
        x, g_mix[l], w_in[l], w_s[l], b_s[l], g_sgu[l], w_conv[l], w_out[l], g_ffn[l],
        w_router_group[l], w_router_expert[l], w_gate[l], w_up[l], w_down[l])
    out = _combine(dest, x1, gates_t, y, g_final.reshape(1, d))
    return out.reshape(b, s, d)
```
